```python
import math, functools
import jax, jax.numpy as jnp
from jax import lax
import numpy as np

D_MODEL = 1024
BATCH = 2
SEQ = 8192
DEPTH = 1
DEC_BATCH = 128
DEC_SEQ = 1
PAST_LEN = 2048
PAGE_SIZE = 128

N_HEADS = 8
HEAD_DIM = 64
N_KV_HEADS = 2
GROUP = N_HEADS // N_KV_HEADS
BLOCK = 64
N_SELECT = 16
WINDOW = 512
Q_BLOCK = 128
NSA_WIDTH = N_HEADS * HEAD_DIM
KV_WIDTH = 2 * N_KV_HEADS * HEAD_DIM
FORCE_SCORE = 16.0
POOL_WINDOWS = (2, 4, 8, 16)
N_POOL_GROUPS = 4
POOL_WIDTH = 512
POOL_GROUP_DIM = POOL_WIDTH // N_POOL_GROUPS
POOL_HIST = 15
N_BRANCHES = 2
D_FF = 4 * D_MODEL
IN_WIDTH = NSA_WIDTH + 3 * KV_WIDTH + 3 * N_HEADS + POOL_WIDTH + N_BRANCHES * D_MODEL
EPS = 1e-6

kernel_name = "nsa_pool_gated_hybrid_step"


def rms_norm(x, g):
    x32 = x.astype(jnp.float32)
    y = x32 * lax.rsqrt(jnp.mean(x32 * x32, axis=-1, keepdims=True) + EPS)
    return (y * g.astype(jnp.float32)).astype(x.dtype)


def masked_softmax(s, mask):
    s = jnp.where(mask, s.astype(jnp.float32), -jnp.inf)
    m = jnp.max(s, axis=-1, keepdims=True)
    m = jnp.where(jnp.isfinite(m), m, 0.0)
    e = jnp.where(mask, jnp.exp(s - m), 0.0)
    return e / jnp.maximum(jnp.sum(e, axis=-1, keepdims=True), 1e-30)


def adaln(c, w_ada, b_ada):
    m = jnp.einsum('nd,de->ne', jax.nn.silu(c), w_ada) + b_ada
    return tuple(jnp.split(m[:, None, :], 6, axis=-1))


def project_in(h, w_in):
    n, t = h.shape[:2]
    z = jnp.einsum('ntd,de->nte', h, w_in)
    sizes = (NSA_WIDTH, KV_WIDTH, KV_WIDTH, KV_WIDTH, 3 * N_HEADS, POOL_WIDTH, N_BRANCHES * D_MODEL)
    cuts = [sum(sizes[:i + 1]) for i in range(len(sizes) - 1)]
    q, kv_c, kv_s, kv_w, g_nsa, u, g_merge = jnp.split(z, cuts, axis=-1)
    kv_shape = (n, t, 2, N_KV_HEADS, HEAD_DIM)
    return (q.reshape(n, t, N_KV_HEADS, GROUP, HEAD_DIM), kv_c.reshape(kv_shape), kv_s.reshape(kv_shape),
            kv_w.reshape(kv_shape), jax.nn.sigmoid(g_nsa).reshape(n, t, N_KV_HEADS, GROUP, 3), u,
            jax.nn.sigmoid(g_merge))


def compress_blocks(rows, w_cmp, pos_cmp):
    n, l = rows.shape[:2]
    blk = rows.reshape(n, l // BLOCK, BLOCK, 2, N_KV_HEADS, HEAD_DIM) + pos_cmp[:, :, None, :]
    return jnp.einsum('nbjshd,sjde->nbshe', blk, w_cmp)


def nsa_attend(q, q_pos, kv_cmp, kv_sel, kv_win, win_pos, gates):
    scale = HEAD_DIM ** -0.5
    n, t = q.shape[:2]
    nb = kv_cmp.shape[1]
    blk_ids = jnp.arange(nb)
    cur = q_pos // BLOCK
    s_c = jnp.einsum('nthgd,nbhd->nthgb', q, kv_cmp[:, :, 0]) * scale
    c_mask = (blk_ids[None, :] + 1) * BLOCK - 1 <= q_pos[:, None]
    p_c = masked_softmax(s_c, c_mask[None, :, None, None, :])
    o_c = jnp.einsum('nthgb,nbhd->nthgd', p_c.astype(q.dtype), kv_cmp[:, :, 1])
    imp = jnp.sum(p_c, axis=3)
    valid = blk_ids[None, :] <= cur[:, None]
    forced = (blk_ids[None, :] == 0) | (blk_ids[None, :] == cur[:, None]) | (blk_ids[None, :] == cur[:, None] - 1)
    score = jnp.where(valid[None, :, None, :], imp + FORCE_SCORE * forced[None, :, None, :].astype(jnp.float32), -1.0)
    top_val, top_idx = lax.top_k(score, min(N_SELECT, nb))
    n_sel = top_idx.shape[-1]
    n_idx = jnp.arange(n)[:, None, None, None]
    h_idx = jnp.arange(N_KV_HEADS)[None, None, :, None]
    kv_g = kv_sel[n_idx, h_idx, top_idx].reshape(n, t, N_KV_HEADS, n_sel * BLOCK, 2, HEAD_DIM)
    key_pos = (top_idx[..., None] * BLOCK + jnp.arange(BLOCK)).reshape(n, t, N_KV_HEADS, n_sel * BLOCK)
    ok = jnp.broadcast_to((top_val >= 0.0)[..., None], top_idx.shape + (BLOCK,)).reshape(n, t, N_KV_HEADS, n_sel * BLOCK)
    s_mask = ok & (key_pos <= q_pos[None, :, None, None])
    s_s = jnp.einsum('nthgd,nthkd->nthgk', q, kv_g[..., 0, :]) * scale
    p_s = masked_softmax(s_s, s_mask[:, :, :, None, :])
    o_s = jnp.einsum('nthgk,nthkd->nthgd', p_s.astype(q.dtype), kv_g[..., 1, :])
    delta = q_pos[:, None] - win_pos[None, :]
    w_mask = (delta >= 0) & (delta < WINDOW) & (win_pos[None, :] >= 0)
    s_w = jnp.einsum('nthgd,nshd->nthgs', q, kv_win[:, :, 0]) * scale
    p_w = masked_softmax(s_w, w_mask[None, :, None, None, :])
    o_w = jnp.einsum('nthgs,nshd->nthgd', p_w.astype(q.dtype), kv_win[:, :, 1])
    return gates[..., 0:1] * o_c + gates[..., 1:2] * o_s + gates[..., 2:3] * o_w


def nsa_prompt(q, kv_c_rows, kv_s_rows, kv_w_rows, gates, w_cmp, pos_cmp):
    n, s = q.shape[:2]
    kv_c = compress_blocks(kv_c_rows, w_cmp, pos_cmp)
    kv_s = kv_s_rows.reshape(n, s // BLOCK, BLOCK, 2, N_KV_HEADS, HEAD_DIM).transpose(0, 4, 1, 2, 3, 5)
    kv_w_pad = jnp.pad(kv_w_rows, ((0, 0), (WINDOW, 0), (0, 0), (0, 0), (0, 0)))

    def one_block(i):
        q0 = i * Q_BLOCK
        qb = lax.dynamic_slice_in_dim(q, q0, Q_BLOCK, axis=1)
        gb = lax.dynamic_slice_in_dim(gates, q0, Q_BLOCK, axis=1)
        kw = lax.dynamic_slice_in_dim(kv_w_pad, q0, WINDOW + Q_BLOCK, axis=1)
        q_pos = q0 + jnp.arange(Q_BLOCK)
        w_pos = q0 - WINDOW + jnp.arange(WINDOW + Q_BLOCK)
        return nsa_attend(qb, q_pos, kv_c, kv_s, kw, w_pos, gb)

    o = lax.map(one_block, jnp.arange(s // Q_BLOCK))
    o = o.transpose(1, 0, 2, 3, 4, 5).reshape(n, s, NSA_WIDTH)
    return o, (kv_c_rows, kv_s_rows, kv_w_rows[:, -min(WINDOW, s):])


def nsa_sample(q, kv_c_new, kv_s_new, kv_w_new, gates, cache_cmp_kv, cache_sel_kv, page_table, state_win_kv, w_cmp, pos_cmp):
    n, t = q.shape[:2]
    total = PAST_LEN + t
    nb = -(-total // BLOCK)
    pad = nb * BLOCK - total

    def full_rows(cache, new):
        past = cache[page_table].reshape(n, PAST_LEN, 2, N_KV_HEADS, HEAD_DIM)
        rows = jnp.concatenate([past, new], axis=1)
        return jnp.pad(rows, ((0, 0), (0, pad), (0, 0), (0, 0), (0, 0)))

    kv_c = compress_blocks(full_rows(cache_cmp_kv, kv_c_new), w_cmp, pos_cmp)
    kv_s = full_rows(cache_sel_kv, kv_s_new).reshape(n, nb, BLOCK, 2, N_KV_HEADS, HEAD_DIM).transpose(0, 4, 1, 2, 3, 5)
    win_buf = state_win_kv.shape[1]
    kw = jnp.concatenate([state_win_kv, kv_w_new], axis=1)
    q_pos = PAST_LEN + jnp.arange(t)
    w_pos = PAST_LEN - win_buf + jnp.arange(win_buf + t)
    o = nsa_attend(q, q_pos, kv_c, kv_s, kw, w_pos, gates)
    return o.reshape(n, t, NSA_WIDTH), (kv_c_new, kv_s_new, kw[:, t:])


def pool_mix(u, hist, first_pos, w_pool, pool_scale):
    n, t = u.shape[:2]
    ext = jnp.concatenate([hist, u], axis=1).astype(jnp.float32)
    cs = jnp.pad(jnp.cumsum(ext, axis=1), ((0, 0), (1, 0), (0, 0)))
    pos = first_pos + jnp.arange(t)
    end = cs[:, POOL_HIST + 1:]
    groups = []
    for gi, w in enumerate(POOL_WINDOWS):
        sl = slice(gi * POOL_GROUP_DIM, (gi + 1) * POOL_GROUP_DIM)
        start = cs[:, POOL_HIST + 1 - w: POOL_HIST + 1 - w + t, sl]
        cnt = jnp.minimum(pos + 1, w).astype(jnp.float32)[None, :, None]
        groups.append((end[..., sl] - start) / cnt)
    pooled = jnp.concatenate(groups, axis=-1).astype(u.dtype) - u
    pg = pooled.reshape(n, t, N_POOL_GROUPS, POOL_GROUP_DIM)
    mixed = jnp.einsum('ntgc,gcd->ntgd', pg, w_pool).reshape(n, t, POOL_WIDTH)
    return mixed * pool_scale


def run_layer(x, ada, layer_w, nsa_fn, pool_hist, first_pos):
    (g_pre_mix, g_post_mix, g_pre_mlp, g_post_mlp, w_in, w_pool, pool_scale,
     w_up_nsa, w_up_pool, w_o, w_ff1, w_ff2) = layer_w
    shift1, scale1, gate1, shift2, scale2, gate2 = ada
    h = rms_norm(x, g_pre_mix) * (1.0 + scale1) + shift1
    q, kv_c, kv_s, kv_w, g_nsa, u, g_merge = project_in(h, w_in)
    o_nsa, nsa_state = nsa_fn(q, kv_c, kv_s, kv_w, g_nsa)
    o_pool = pool_mix(u, pool_hist, first_pos, w_pool, pool_scale)
    new_pool = jnp.concatenate([pool_hist, u], axis=1)[:, -POOL_HIST:]
    g_a, g_b = jnp.split(g_merge, 2, axis=-1)
    m = g_a * jnp.einsum('ntc,cd->ntd', o_nsa, w_up_nsa) + g_b * jnp.einsum('ntc,cd->ntd', o_pool, w_up_pool)
    m = jnp.einsum('ntd,de->nte', m, w_o)
    x = x + gate1 * rms_norm(m, g_post_mix)
    h = rms_norm(x, g_pre_mlp) * (1.0 + scale2) + shift2
    f = jnp.einsum('ntf,fd->ntd', jnp.square(jax.nn.relu(jnp.einsum('ntd,df->ntf', h, w_ff1))), w_ff2)
    x = x + gate2 * rms_norm(f, g_post_mlp)
    return x, nsa_state + (new_pool,)


def setup_inputs(seed: int = 0) -> dict:
    key = jax.random.key(seed)
    ks = jax.random.split(key, 32)
    n_pages = PAST_LEN // PAGE_SIZE
    n_used = DEC_BATCH * n_pages
    n_phys = n_used + (n_used + 3) // 4
    win_buf = min(WINDOW, PAST_LEN)

    def nrm(k, shape, s):
        return jax.random.normal(k, shape, jnp.float32) * s

    page_table = jax.random.permutation(ks[0], n_phys)[:n_used].reshape(DEC_BATCH, n_pages).astype(jnp.int32)
    kv_page_shape = (DEPTH, n_phys, PAGE_SIZE, 2, N_KV_HEADS, HEAD_DIM)
    return {
        "x_prompt": nrm(ks[1], (BATCH, SEQ, D_MODEL), 1.0),
        "x_sample": nrm(ks[2], (DEC_BATCH, DEC_SEQ, D_MODEL), 1.0),
        "cache_cmp_kv": nrm(ks[3], kv_page_shape, 1.0),
        "cache_sel_kv": nrm(ks[4], kv_page_shape, 1.0),
        "state_win_kv": nrm(ks[5], (DEPTH, DEC_BATCH, win_buf, 2, N_KV_HEADS, HEAD_DIM), 1.0),
        "state_pool": nrm(ks[6], (DEPTH, DEC_BATCH, POOL_HIST, POOL_WIDTH), 1.0),
        "page_table": page_table,
        "c_prompt": nrm(ks[7], (BATCH, D_MODEL), 1.0),
        "c_sample": nrm(ks[8], (DEC_BATCH, D_MODEL), 1.0),
        "w_ada": nrm(ks[9], (DEPTH, D_MODEL, 6 * D_MODEL), D_MODEL ** -0.5),
        "b_ada": nrm(ks[10], (DEPTH, 6 * D_MODEL), 0.02),
        "g_pre_mix": 1.0 + nrm(ks[11], (DEPTH, D_MODEL), 0.05),
        "g_post_mix": 1.0 + nrm(ks[12], (DEPTH, D_MODEL), 0.05),
        "g_pre_mlp": 1.0 + nrm(ks[13], (DEPTH, D_MODEL), 0.05),
        "g_post_mlp": 1.0 + nrm(ks[14], (DEPTH, D_MODEL), 0.05),
        "w_in": nrm(ks[15], (DEPTH, D_MODEL, IN_WIDTH), D_MODEL ** -0.5),
        "w_cmp": nrm(ks[16], (DEPTH, 2, BLOCK, HEAD_DIM, HEAD_DIM), (BLOCK * HEAD_DIM) ** -0.5),
        "pos_cmp": nrm(ks[17], (DEPTH, BLOCK, 2, HEAD_DIM), 0.1),
        "w_pool": nrm(ks[18], (DEPTH, N_POOL_GROUPS, POOL_GROUP_DIM, POOL_GROUP_DIM), POOL_GROUP_DIM ** -0.5),
        "pool_scale": 1.0 + nrm(ks[19], (DEPTH, POOL_WIDTH), 0.1),
        "w_up_nsa": nrm(ks[20], (DEPTH, NSA_WIDTH, D_MODEL), NSA_WIDTH ** -0.5),
        "w_up_pool": nrm(ks[21], (DEPTH, POOL_WIDTH, D_MODEL), POOL_WIDTH ** -0.5),
        "w_o": nrm(ks[22], (DEPTH, D_MODEL, D_MODEL), D_MODEL ** -0.5),
        "w_ff1": nrm(ks[23], (DEPTH, D_MODEL, D_FF), D_MODEL ** -0.5),
        "w_ff2": nrm(ks[24], (DEPTH, D_FF, D_MODEL), D_FF ** -0.5),
    }


def reference(x_prompt, x_sample, cache_cmp_kv, cache_sel_kv, state_win_kv, state_pool, page_table,
              c_prompt, c_sample, w_ada, b_ada, g_pre_mix, g_post_mix, g_pre_mlp, g_post_mlp, w_in,
              w_cmp, pos_cmp, w_pool, pool_scale, w_up_nsa, w_up_pool, w_o, w_ff1, w_ff2):
    yp, ys = x_prompt, x_sample
    outs = [[] for _ in range(8)]
    for l in range(DEPTH):
        layer_w = (g_pre_mix[l], g_post_mix[l], g_pre_mlp[l], g_post_mlp[l], w_in[l], w_pool[l], pool_scale[l],
                   w_up_nsa[l], w_up_pool[l], w_o[l], w_ff1[l], w_ff2[l])
        ada_p = adaln(c_prompt, w_ada[l], b_ada[l])
        ada_s = adaln(c_sample, w_ada[l], b_ada[l])
        nsa_p = functools.partial(nsa_prompt, w_cmp=w_cmp[l], pos_cmp=pos_cmp[l])
        nsa_s = functools.partial(nsa_sample, cache_cmp_kv=cache_cmp_kv[l], cache_sel_kv=cache_sel_kv[l],
                                  page_table=page_table, state_win_kv=state_win_kv[l],
                                  w_cmp=w_cmp[l], pos_cmp=pos_cmp[l])
        hist_p = jnp.zeros((yp.shape[0], POOL_HIST, POOL_WIDTH), yp.dtype)
        yp, st_p = run_layer(yp, ada_p, layer_w, nsa_p, hist_p, 0)
        ys, st_s = run_layer(ys, ada_s, layer_w, nsa_s, state_pool[l], PAST_LEN)
        for i, a in enumerate(st_p + st_s):
            outs[i].append(a)
    new_cmp_kv_prompt = jnp.stack(outs[0], 0)
    new_sel_kv_prompt = jnp.stack(outs[1], 0)
    new_win_kv_prompt = jnp.stack(outs[2], 0)
    new_pool_prompt = jnp.stack(outs[3], 0)
    new_cmp_kv_sample = jnp.stack(outs[4], 0)
    new_sel_kv_sample = jnp.stack(outs[5], 0)
    new_win_kv_sample = jnp.stack(outs[6], 0)
    new_pool_sample = jnp.stack(outs[7], 0)
    return (yp, ys, new_cmp_kv_prompt, new_sel_kv_prompt, new_win_kv_prompt, new_pool_prompt,
            new_cmp_kv_sample, new_sel_kv_sample, new_win_kv_sample, new_pool_sample)
```

```python
import functools

import jax
import jax.numpy as jnp
from jax import lax
from jax.experimental import pallas as pl
from jax.experimental.pallas import tpu as pltpu

D_MODEL = 1024
N_HEADS = 8
HEAD_DIM = 64
N_KV_HEADS = 2
GROUP = N_HEADS // N_KV_HEADS
BLOCK = 64
N_SELECT = 16
WINDOW = 512
Q_TILE = 128
NSA_WIDTH = N_HEADS * HEAD_DIM
KV_WIDTH = 2 * N_KV_HEADS * HEAD_DIM
KV_HALF = N_KV_HEADS * HEAD_DIM
FORCE_SCORE = 16.0
POOL_WINDOWS = (2, 4, 8, 16)
POOL_WIDTH = 512
POOL_GROUP_DIM = 128
POOL_HIST = 15
D_FF = 4 * D_MODEL
EPS = 1e-6
PAGE_SIZE = 128
CMP_K = BLOCK * KV_WIDTH

BF = jnp.bfloat16
F32 = jnp.float32
MASK_BIG = 2.0 ** 100
NEG_INF = float("-inf")

TOKEN_TILE = 512
SEL_CHUNK = 512
VMEM_LIMIT = 56 * 1024 * 1024


def _cparams(*sem):
    return pltpu.CompilerParams(dimension_semantics=sem, vmem_limit_bytes=VMEM_LIMIT)


def _rms(x, g):
    return x * lax.rsqrt(jnp.mean(x * x, axis=-1, keepdims=True) + EPS) * g


def _dot(a, b):
    return jnp.dot(a, b, preferred_element_type=F32)


def _dot_nt(a, b):
    return lax.dot_general(a, b, (((1,), (1,)), ((), ())), preferred_element_type=F32)


def _ada_kernel(c_ref, w_ref, b_ref, o_ref):
    c = c_ref[...]
    a = (c * jax.nn.sigmoid(c)).astype(BF)
    o_ref[...] = _dot(a, w_ref[...].astype(BF)) + b_ref[...]


def _adaln(c_all, w_ada, b_ada):
    rows = c_all.shape[0]
    n_out = w_ada.shape[1]
    tn = 512
    return pl.pallas_call(
        _ada_kernel,
        out_shape=jax.ShapeDtypeStruct((rows, n_out), F32),
        grid=(n_out // tn,),
        in_specs=[pl.BlockSpec((rows, D_MODEL), lambda j: (0, 0)),
                  pl.BlockSpec((D_MODEL, tn), lambda j: (0, j)),
                  pl.BlockSpec((1, tn), lambda j: (0, j))],
        out_specs=pl.BlockSpec((rows, tn), lambda j: (0, j)),
        compiler_params=_cparams("arbitrary"),
        name="adaln",
    )(c_all, w_ada, b_ada)


def _proj_kernel(x_ref, sh_ref, sc_ref, g_ref, wn_ref, wt_ref,
                 kvc_ref, kvs_ref, kvw_ref, u_ref, qT_ref, kaug_ref, vsT_ref, kw_ref, vwT_ref, gT_ref,
                 *, tm, tpb):
    x = x_ref[...]
    h = _rms(x, g_ref[...]) * (1.0 + sc_ref[0]) + sh_ref[0]
    hb = h.astype(BF)
    zn = _dot(hb, wn_ref[...])
    zt = _dot_nt(wt_ref[...], hb)
    kvc_ref[...] = zn[:, 0:256]
    kvs = zn[:, 256:512]
    kvs_ref[...] = kvs
    kvw = zn[:, 512:768]
    kvw_ref[...] = kvw
    u_ref[...] = zn[:, 768:1280]
    qT_ref[0] = (zt[0:512] * (HEAD_DIM ** -0.5)).astype(BF)
    vsT_ref[0, 0] = zt[512:640].astype(BF)
    for c in range(tm // Q_TILE):
        vwT_ref[0, c] = zt[640:768, c * Q_TILE:(c + 1) * Q_TILE].astype(BF)
    gT_ref[0] = jax.nn.sigmoid(zt[768:800])
    t0 = (pl.program_id(0) % tpb) * tm
    blk = jnp.right_shift(t0 + lax.broadcasted_iota(jnp.int32, (tm, 128), 0), 6)
    lane = lax.broadcasted_iota(jnp.int32, (tm, 128), 1)
    kaug_ref[0, :, 0:128] = kvs[:, 0:128].astype(BF)
    kaug_ref[0, :, 128:256] = jnp.where(blk == lane, MASK_BIG, 0.0).astype(BF)
    kw_ref[0] = kvw[:, 0:128].astype(BF)


def _project_prompt(x2, shift, scale, g, wn, wt, n_batch, seq):
    tm = TOKEN_TILE
    tpb = seq // tm
    nt = n_batch * seq
    tok = lambda t: (t, 0)
    per_b = lambda t: (t // tpb, 0, 0)
    out_shape = (
        jax.ShapeDtypeStruct((nt, KV_WIDTH), F32),
        jax.ShapeDtypeStruct((nt, KV_WIDTH), F32),
        jax.ShapeDtypeStruct((nt, KV_WIDTH), F32),
        jax.ShapeDtypeStruct((nt, POOL_WIDTH), F32),
        jax.ShapeDtypeStruct((n_batch, NSA_WIDTH, seq), BF),
        jax.ShapeDtypeStruct((n_batch, seq, 256), BF),
        jax.ShapeDtypeStruct((n_batch, seq // SEL_CHUNK, KV_HALF, SEL_CHUNK), BF),
        jax.ShapeDtypeStruct((n_batch, seq, KV_HALF), BF),
        jax.ShapeDtypeStruct((n_batch, seq // Q_TILE, KV_HALF, Q_TILE), BF),
        jax.ShapeDtypeStruct((n_batch, 32, seq), F32),
    )
    out_specs = (
        pl.BlockSpec((tm, KV_WIDTH), tok),
        pl.BlockSpec((tm, KV_WIDTH), tok),
        pl.BlockSpec((tm, KV_WIDTH), tok),
        pl.BlockSpec((tm, POOL_WIDTH), tok),
        pl.BlockSpec((1, NSA_WIDTH, tm), lambda t: (t // tpb, 0, t % tpb)),
        pl.BlockSpec((1, tm, 256), lambda t: (t // tpb, t % tpb, 0)),
        pl.BlockSpec((1, 1, KV_HALF, SEL_CHUNK), lambda t: (t // tpb, t % tpb, 0, 0)),
        pl.BlockSpec((1, tm, KV_HALF), lambda t: (t // tpb, t % tpb, 0)),
        pl.BlockSpec((1, tm // Q_TILE, KV_HALF, Q_TILE), lambda t: (t // tpb, t % tpb, 0, 0)),
        pl.BlockSpec((1, 32, tm), lambda t: (t // tpb, 0, t % tpb)),
    )
    return pl.pallas_call(
        functools.partial(_proj_kernel, tm=tm, tpb=tpb),
        out_shape=out_shape,
        grid=(nt // tm,),
        in_specs=[pl.BlockSpec((tm, D_MODEL), tok),
                  pl.BlockSpec((1, 1, D_MODEL), per_b),
                  pl.BlockSpec((1, 1, D_MODEL), per_b),
                  pl.BlockSpec((1, D_MODEL), lambda t: (0, 0)),
                  pl.BlockSpec(wn.shape, lambda t: (0, 0)),
                  pl.BlockSpec(wt.shape, lambda t: (0, 0))],
        out_specs=out_specs,
        compiler_params=_cparams("arbitrary"),
        name="project_prompt",
    )(x2, shift, scale, g, wn, wt)


def _cmp_kernel(x_ref, pos_ref, w_ref, o_ref):
    @pl.when(pl.program_id(0) == 0)
    def _():
        o_ref[...] = jnp.zeros(o_ref.shape, F32)

    xb = (x_ref[...] + pos_ref[...]).astype(BF)
    o_ref[...] += _dot(xb, w_ref[...])


def _compress(xflat, pos_flat, w_full):
    m = xflat.shape[0]
    kc = 4096
    return pl.pallas_call(
        _cmp_kernel,
        out_shape=jax.ShapeDtypeStruct((m, KV_WIDTH), F32),
        grid=(CMP_K // kc,),
        in_specs=[pl.BlockSpec((m, kc), lambda k: (0, k)),
                  pl.BlockSpec((1, kc), lambda k: (0, k)),
                  pl.BlockSpec((kc, KV_WIDTH), lambda k: (k, 0))],
        out_specs=pl.BlockSpec((m, KV_WIDTH), lambda k: (0, 0)),
        compiler_params=_cparams("arbitrary"),
        name="compress_blocks",
    )(xflat, pos_flat, w_full)


def _scmp_kernel(pt_ref, *refs, n_pages_step, blocks_step):
    pages = refs[:n_pages_step]
    pos_ref, w_ref, o_ref, stack_ref = refs[n_pages_step:]
    rows_per_page = stack_ref.shape[0] // n_pages_step
    rows_per_block = rows_per_page // (PAGE_SIZE // BLOCK)
    for k in range(n_pages_step):
        stack_ref[k * rows_per_page:(k + 1) * rows_per_page, :] = pages[k][0]
    n_split = pos_ref.shape[0]
    rows_per_split = rows_per_block // n_split
    acc = jnp.zeros((blocks_step, KV_WIDTH), F32)
    for c in range(n_split):
        xc = jnp.concatenate(
            [stack_ref[pl.ds(c * rows_per_split + r, blocks_step, stride=rows_per_block), :]
             for r in range(rows_per_split)], axis=1) + pos_ref[c]
        acc = acc + _dot(xc.astype(BF), w_ref[c])
    o_ref[...] = acc


def _compress_paged(cache, page_table_flat, pos_flat, w_full, n_seq, n_pages):
    n_phys = cache.shape[0]
    n_split = 8
    row_w = CMP_K // n_split
    rows_per_page = PAGE_SIZE * KV_WIDTH // 128
    cache3 = cache.reshape(n_phys, rows_per_page, 128)
    seq_step = 4
    n_pages_step = seq_step * n_pages
    blocks_step = n_pages_step * (PAGE_SIZE // BLOCK)
    pos3 = pos_flat.reshape(n_split, 1, row_w)
    w3 = w_full.reshape(n_split, row_w, KV_WIDTH)

    def page_map(k):
        return lambda s, pt: (pt[s * n_pages_step + k], 0, 0)

    grid_spec = pltpu.PrefetchScalarGridSpec(
        num_scalar_prefetch=1,
        grid=(n_seq // seq_step,),
        in_specs=[pl.BlockSpec((1, rows_per_page, 128), page_map(k)) for k in range(n_pages_step)]
        + [pl.BlockSpec((n_split, 1, row_w), lambda s, pt: (0, 0, 0)),
           pl.BlockSpec((n_split, row_w, KV_WIDTH), lambda s, pt: (0, 0, 0))],
        out_specs=pl.BlockSpec((blocks_step, KV_WIDTH), lambda s, pt: (s, 0)),
        scratch_shapes=[pltpu.VMEM((n_pages_step * rows_per_page, 128), F32)],
    )
    return pl.pallas_call(
        functools.partial(_scmp_kernel, n_pages_step=n_pages_step, blocks_step=blocks_step),
        out_shape=jax.ShapeDtypeStruct((n_seq * n_pages * (PAGE_SIZE // BLOCK), KV_WIDTH), F32),
        grid_spec=grid_spec,
        compiler_params=_cparams("arbitrary"),
        name="compress_paged",
    )(page_table_flat, *([cache3] * n_pages_step), pos3, w3)


def _attn_kernel(qT_ref, kaug_ref, vsT_ref, kw_ref, vwT_ref, kc_ref, vcT_ref, gT_ref, o_ref,
                 qa_ref, sc_ref, m_ref, l_ref, acc_ref, outT_ref):
    i = pl.program_id(1)
    q0 = i * Q_TILE
    row = lax.broadcasted_iota(jnp.int32, (128, 128), 0)
    tok = lax.broadcasted_iota(jnp.int32, (128, 128), 1)
    qpos = q0 + tok
    cur = jnp.right_shift(qpos, 6)
    cmask = (row + 1) * BLOCK - 1 <= qpos
    valid = row <= cur
    forced = (row == 0) | (row == cur) | (row == cur - 1)
    key512 = lax.broadcasted_iota(jnp.int32, (SEL_CHUNK, 512), 0)
    qpos512 = q0 + jnp.bitwise_and(lax.broadcasted_iota(jnp.int32, (SEL_CHUNK, 512), 1), Q_TILE - 1)
    key128 = lax.broadcasted_iota(jnp.int32, (Q_TILE, 512), 0)
    tok128 = jnp.bitwise_and(lax.broadcasted_iota(jnp.int32, (Q_TILE, 512), 1), Q_TILE - 1)
    n_full = jnp.right_shift(i, 2)
    n_rank = jnp.right_shift(2 * i + 2 + 7, 3)

    def online_step(s, v_t):
        m_old = m_ref[...]
        m_new = jnp.maximum(m_old, jnp.max(s, axis=0, keepdims=True))
        alpha = jnp.exp(m_old - m_new)
        p = jnp.exp(s - m_new)
        l_ref[...] = alpha * l_ref[...] + jnp.sum(p, axis=0, keepdims=True)
        acc_ref[...] = alpha * acc_ref[...] + _dot(v_t, p.astype(BF))
        m_ref[...] = m_new

    def online_reset():
        m_ref[...] = jnp.full(m_ref.shape, NEG_INF, F32)
        l_ref[...] = jnp.zeros(l_ref.shape, F32)
        acc_ref[...] = jnp.zeros(acc_ref.shape, F32)

    for h in range(N_KV_HEADS):
        hs = slice(h * HEAD_DIM, (h + 1) * HEAD_DIM)
        qa_ref[...] = jnp.zeros(qa_ref.shape, BF)
        for g in range(GROUP):
            r0 = h * GROUP * HEAD_DIM + g * HEAD_DIM
            qa_ref[hs, g * Q_TILE:(g + 1) * Q_TILE] = qT_ref[0, r0:r0 + HEAD_DIM, :]
        qf = qa_ref[0:128, :]

        sc = _dot(kc_ref[0], qf)
        imp = jnp.zeros((128, Q_TILE), F32)
        p_parts = []
        for g in range(GROUP):
            s = jnp.where(cmask, sc[:, g * Q_TILE:(g + 1) * Q_TILE], NEG_INF)
            mx = jnp.max(s, axis=0, keepdims=True)
            mx = jnp.where(mx > NEG_INF, mx, 0.0)
            e = jnp.where(cmask, jnp.exp(s - mx), 0.0)
            p = e / jnp.maximum(jnp.sum(e, axis=0, keepdims=True), 1e-30)
            imp = imp + p
            p_parts.append(p)
        o_c = _dot(vcT_ref[0, hs, :], jnp.concatenate(p_parts, axis=1).astype(BF))

        score = jnp.where(valid, imp + FORCE_SCORE * forced.astype(F32), -1.0)
        sc_ref[...] = score

        def rank_body(kb, cnt):
            rows8 = sc_ref[pl.ds(pl.multiple_of(kb * 8, 8), 8), :]
            for sb in range(8):
                other = rows8[sb:sb + 1, :]
                ahead = (other > score) | ((other == score) & (kb * 8 + sb < row))
                cnt = cnt + jnp.where(ahead, 1.0, 0.0)
            return cnt

        cnt = lax.fori_loop(0, n_rank, rank_body, jnp.zeros((128, Q_TILE), F32))
        sel = (cnt < float(N_SELECT)) & (score >= 0.0)
        selm1 = jnp.where(sel, 0.0, -1.0).astype(BF)
        for g in range(GROUP):
            qa_ref[128:256, g * Q_TILE:(g + 1) * Q_TILE] = selm1

        online_reset()

        def sel_step(j, causal):
            kt = kaug_ref[0, pl.ds(pl.multiple_of(j * SEL_CHUNK, SEL_CHUNK), SEL_CHUNK), :]
            s = _dot(kt, qa_ref[...])
            if causal:
                s = jnp.where(j * SEL_CHUNK + key512 <= qpos512, s, -MASK_BIG)
            online_step(s, vsT_ref[0, j, hs, :])

        def sel_body(j, carry):
            sel_step(j, False)
            return carry

        lax.fori_loop(0, n_full, sel_body, 0)
        sel_step(n_full, True)
        o_s = acc_ref[...] / l_ref[...]

        online_reset()
        j0 = jnp.maximum(i - WINDOW // Q_TILE, 0)
        w0 = j0 * Q_TILE
        s = _dot(kw_ref[0, pl.ds(pl.multiple_of(w0, Q_TILE), WINDOW), :], qf)
        delta = qpos512 - (w0 + key512)
        s = jnp.where((delta >= 0) & (delta < WINDOW), s, -MASK_BIG)
        v_t = jnp.concatenate([vwT_ref[0, j0 + c, hs, :] for c in range(WINDOW // Q_TILE)], axis=1)
        online_step(s, v_t)
        s = _dot(kw_ref[0, pl.ds(pl.multiple_of(q0, Q_TILE), Q_TILE), :], qf)
        s = jnp.where((key128 <= tok128) & (i >= WINDOW // Q_TILE), s, -MASK_BIG)
        online_step(s, vwT_ref[0, i, hs, :])
        o_w = acc_ref[...] / l_ref[...]

        for g in range(GROUP):
            gs = slice(g * Q_TILE, (g + 1) * Q_TILE)
            gr = h * GROUP * 3 + g * 3
            og = (gT_ref[0, gr:gr + 1, :] * o_c[:, gs] + gT_ref[0, gr + 1:gr + 2, :] * o_s[:, gs]
                  + gT_ref[0, gr + 2:gr + 3, :] * o_w[:, gs])
            r0 = h * GROUP * HEAD_DIM + g * HEAD_DIM
            outT_ref[r0:r0 + HEAD_DIM, :] = og

    o_ref[0] = outT_ref[...].T.astype(BF)


def _attention_prompt(qT, kaug, vsT, kw, vwT, kc, vcT, gT, n_batch, seq):
    per_b3 = lambda n, i: (n, 0, 0)
    per_b4 = lambda n, i: (n, 0, 0, 0)
    return pl.pallas_call(
        _attn_kernel,
        out_shape=jax.ShapeDtypeStruct((n_batch, seq, NSA_WIDTH), BF),
        grid=(n_batch, seq // Q_TILE),
        in_specs=[pl.BlockSpec((1, NSA_WIDTH, Q_TILE), lambda n, i: (n, 0, i)),
                  pl.BlockSpec((1, seq, 256), per_b3),
                  pl.BlockSpec((1, seq // SEL_CHUNK, KV_HALF, SEL_CHUNK), per_b4),
                  pl.BlockSpec((1, seq, KV_HALF), per_b3),
                  pl.BlockSpec((1, seq // Q_TILE, KV_HALF, Q_TILE), per_b4),
                  pl.BlockSpec((1, 128, KV_HALF), per_b3),
                  pl.BlockSpec((1, KV_HALF, 128), per_b3),
                  pl.BlockSpec((1, 32, Q_TILE), lambda n, i: (n, 0, i))],
        out_specs=pl.BlockSpec((1, Q_TILE, NSA_WIDTH), lambda n, i: (n, i, 0)),
        scratch_shapes=[pltpu.VMEM((256, GROUP * Q_TILE), BF),
                        pltpu.VMEM((128, Q_TILE), F32),
                        pltpu.VMEM((1, GROUP * Q_TILE), F32),
                        pltpu.VMEM((1, GROUP * Q_TILE), F32),
                        pltpu.VMEM((HEAD_DIM, GROUP * Q_TILE), F32),
                        pltpu.VMEM((NSA_WIDTH, Q_TILE), F32)],
        compiler_params=_cparams("arbitrary", "arbitrary"),
        name="attention_prompt",
    )(qT, kaug, vsT, kw, vwT, kc, vcT, gT)


def _pool_kernel(u_ref, halo_ref, wp_ref, ps_ref, o_ref, ext_ref, *, tm, tpb):
    t = pl.program_id(0) % tpb
    ext_ref[0:16, :] = jnp.where(t == 0, 0.0, halo_ref[...])
    u = u_ref[...]
    ext_ref[16:16 + tm, :] = u
    pos = t * tm + lax.broadcasted_iota(jnp.int32, (tm, 1), 0)
    outs = []
    for gi, w in enumerate(POOL_WINDOWS):
        cs = slice(gi * POOL_GROUP_DIM, (gi + 1) * POOL_GROUP_DIM)
        acc = u[:, cs]
        for k in range(1, w):
            acc = acc + ext_ref[pl.ds(16 - k, tm), cs]
        cnt = jnp.minimum(pos + 1, w).astype(F32)
        pooled = acc / cnt - u[:, cs]
        outs.append(_dot(pooled.astype(BF), wp_ref[gi]))
    o_ref[...] = (jnp.concatenate(outs, axis=1) * ps_ref[...]).astype(BF)


def _pool_prompt(u, w_pool, pool_scale, n_batch, seq):
    tm = TOKEN_TILE
    tpb = seq // tm
    nt = n_batch * seq
    return pl.pallas_call(
        functools.partial(_pool_kernel, tm=tm, tpb=tpb),
        out_shape=jax.ShapeDtypeStruct((nt, POOL_WIDTH), BF),
        grid=(nt // tm,),
        in_specs=[pl.BlockSpec((tm, POOL_WIDTH), lambda t: (t, 0)),
                  pl.BlockSpec((16, POOL_WIDTH), lambda t: (jnp.maximum(t * (tm // 16) - 1, 0), 0)),
                  pl.BlockSpec(w_pool.shape, lambda t: (0, 0, 0)),
                  pl.BlockSpec((1, POOL_WIDTH), lambda t: (0, 0))],
        out_specs=pl.BlockSpec((tm, POOL_WIDTH), lambda t: (t, 0)),
        scratch_shapes=[pltpu.VMEM((tm + 16, POOL_WIDTH), F32)],
        compiler_params=_cparams("arbitrary"),
        name="pool_prompt",
    )(u, u, w_pool, pool_scale)


def _spool_kernel(u_ref, hist_ref, wp_ref, ps_ref, o_ref):
    u = u_ref[...]
    outs = []
    for gi, w in enumerate(POOL_WINDOWS):
        cs = slice(gi * POOL_GROUP_DIM, (gi + 1) * POOL_GROUP_DIM)
        acc = u[:, cs]
        for k in range(1, w):
            acc = acc + hist_ref[POOL_HIST - k, :, cs]
        pooled = acc / float(w) - u[:, cs]
        outs.append(_dot(pooled.astype(BF), wp_ref[gi]))
    o_ref[...] = (jnp.concatenate(outs, axis=1) * ps_ref[...]).astype(BF)


def _pool_sample(u, hist_t, w_pool, pool_scale):
    n = u.shape[0]
    return pl.pallas_call(
        _spool_kernel,
        out_shape=jax.ShapeDtypeStruct((n, POOL_WIDTH), BF),
        name="pool_sample",
    )(u, hist_t, w_pool, pool_scale)


def _merge_kernel(x_ref, sh_ref, sc_ref, gate_ref, gpre_ref, gpost_ref, onsa_ref, opool_ref,
                  wgm_ref, wun_ref, wup_ref, wo_ref, o_ref):
    x = x_ref[...]
    h = _rms(x, gpre_ref[...]) * (1.0 + sc_ref[0]) + sh_ref[0]
    gm = jax.nn.sigmoid(_dot(h.astype(BF), wgm_ref[...]))
    m = (gm[:, :D_MODEL] * _dot(onsa_ref[...], wun_ref[...])
         + gm[:, D_MODEL:] * _dot(opool_ref[...], wup_ref[...]))
    m = _dot(m.astype(BF), wo_ref[...])
    o_ref[...] = x + gate_ref[0] * _rms(m, gpost_ref[...])


def _mlp_kernel(x_ref, sh_ref, sc_ref, gate_ref, gpre_ref, gpost_ref, w1_ref, w2_ref, o_ref):
    x = x_ref[...]
    h = _rms(x, gpre_ref[...]) * (1.0 + sc_ref[0]) + sh_ref[0]
    hb = h.astype(BF)
    f = jnp.zeros(x.shape, F32)
    fc = 1024
    for c in range(D_FF // fc):
        a = jnp.maximum(_dot(hb, w1_ref[:, c * fc:(c + 1) * fc]), 0.0)
        f = f + _dot((a * a).astype(BF), w2_ref[c * fc:(c + 1) * fc, :])
    o_ref[...] = x + gate_ref[0] * _rms(f, gpost_ref[...])


def _token_call(kernel, name, x2, mods, tm, rows_per_mod, extra_tok, consts):
    nt = x2.shape[0]
    r = mods[0].shape[1]
    mod_spec = pl.BlockSpec((1, r, D_MODEL), lambda t: ((t * tm) // rows_per_mod, 0, 0))
    in_specs = [pl.BlockSpec((tm, D_MODEL), lambda t: (t, 0))] + [mod_spec] * len(mods)
    in_specs += [pl.BlockSpec((1, D_MODEL), lambda t: (0, 0))] * 2
    in_specs += [pl.BlockSpec((tm, a.shape[1]), lambda t: (t, 0)) for a in extra_tok]
    in_specs += [pl.BlockSpec(w.shape, lambda t: (0, 0), pipeline_mode=pl.Buffered(1)) for w in consts[2:]]
    return pl.pallas_call(
        kernel,
        out_shape=jax.ShapeDtypeStruct((nt, D_MODEL), F32),
        grid=(nt // tm,),
        in_specs=in_specs,
        out_specs=pl.BlockSpec((tm, D_MODEL), lambda t: (t, 0)),
        compiler_params=_cparams("arbitrary"),
        name=name,
    )(x2, *mods, consts[0], consts[1], *extra_tok, *consts[2:])


def _sproj_kernel(x_ref, sh_ref, sc_ref, g_ref, w_ref, z_ref, zs_ref):
    x = x_ref[...]
    h = _rms(x, g_ref[...]) * (1.0 + sc_ref[...]) + sh_ref[...]
    z = _dot(h.astype(BF), w_ref[...])
    z_ref[...] = z
    zs_ref[...] = jax.nn.sigmoid(z)


def _project_sample(x2, shift, scale, g, w):
    n = x2.shape[0]
    shp = jax.ShapeDtypeStruct((n, w.shape[1]), F32)
    return pl.pallas_call(_sproj_kernel, out_shape=(shp, shp), name="project_sample")(x2, shift, scale, g, w)


def _sattn_kernel(pt_ref, *refs, n_pages, past_len):
    pages = refs[:n_pages]
    (qb_ref, kvc_ref, win_ref, ksn_ref, kwn_ref, g_ref,
     o_ref, nwin_ref, kaug_ref, vall_ref) = refs[n_pages:]
    nb_past = past_len // BLOCK
    win_buf = win_ref.shape[1]

    @pl.when(pl.program_id(0) == 0)
    def _():
        blk = jnp.right_shift(lax.broadcasted_iota(jnp.int32, (past_len, 128), 0), 6)
        lane = lax.broadcasted_iota(jnp.int32, (past_len, 128), 1)
        kaug_ref[:, 128:256] = jnp.where(blk == lane, MASK_BIG, 0.0).astype(BF)

    for p in range(n_pages):
        pg = pages[p][0]
        kaug_ref[p * PAGE_SIZE:(p + 1) * PAGE_SIZE, 0:128] = pg[:, 0:128].astype(BF)
        vall_ref[p * PAGE_SIZE:(p + 1) * PAGE_SIZE, :] = pg[:, 128:256].astype(BF)

    qb = qb_ref[0].astype(BF)
    qf = qb.astype(F32)
    row8 = lax.broadcasted_iota(jnp.int32, (8, 128), 0)
    lane8 = lax.broadcasted_iota(jnp.int32, (8, 128), 1)

    def new_key_score(knew):
        return jnp.sum(qf * knew.astype(BF).astype(F32), axis=1, keepdims=True)

    def head_rows(v):
        return v

    kvc = kvc_ref[0]
    s_c = _dot_nt(qb, kvc[:, 0:128].astype(BF))
    cm = lane8 < nb_past
    s_c = jnp.where(cm, s_c, NEG_INF)
    mx = jnp.max(s_c, axis=1, keepdims=True)
    mx = jnp.where(mx > NEG_INF, mx, 0.0)
    e = jnp.where(cm, jnp.exp(s_c - mx), 0.0)
    p_c = e / jnp.maximum(jnp.sum(e, axis=1, keepdims=True), 1e-30)
    o_c = _dot(p_c.astype(BF), kvc[:, 128:256].astype(BF))

    top = row8 < GROUP
    imp_a = jnp.sum(jnp.where(top, p_c, 0.0), axis=0, keepdims=True)
    imp_b = jnp.sum(jnp.where(top, 0.0, p_c), axis=0, keepdims=True)
    imp = jnp.where(top, imp_a, imp_b)
    cur = nb_past
    forced = (lane8 == 0) | (lane8 == cur) | (lane8 == cur - 1)
    score = jnp.where(lane8 <= cur, imp + FORCE_SCORE * forced.astype(F32), -1.0)
    cnt = jnp.zeros((8, 128), F32)
    for bp in range(nb_past + 1):
        other = score[:, bp:bp + 1]
        ahead = (other > score) | ((other == score) & (bp < lane8))
        cnt = cnt + jnp.where(ahead, 1.0, 0.0)
    sel = (cnt < float(N_SELECT)) & (score >= 0.0)
    selm1 = jnp.where(sel, 0.0, -1.0).astype(BF)

    qaug = jnp.concatenate([qb, selm1], axis=1)
    s_s = _dot_nt(qaug, kaug_ref[...])
    s_n = new_key_score(ksn_ref[0][:, 0:128])
    m_s = jnp.maximum(jnp.max(s_s, axis=1, keepdims=True), s_n)
    e_s = jnp.exp(s_s - m_s)
    e_n = jnp.exp(s_n - m_s)
    l_s = jnp.sum(e_s, axis=1, keepdims=True) + e_n
    v_n = ksn_ref[0][:, 128:256].astype(BF).astype(F32)
    o_s = (_dot(e_s.astype(BF), vall_ref[...]) + e_n.astype(BF).astype(F32) * v_n) / l_s

    win = win_ref[0]
    s_w = _dot_nt(qb, win[:, 0:128].astype(BF))
    lane_w = lax.broadcasted_iota(jnp.int32, (8, win_buf), 1)
    s_w = jnp.where(lane_w >= win_buf + 1 - WINDOW, s_w, NEG_INF)
    s_n = new_key_score(kwn_ref[0][:, 0:128])
    m_w = jnp.maximum(jnp.max(s_w, axis=1, keepdims=True), s_n)
    e_w = jnp.exp(s_w - m_w)
    e_n = jnp.exp(s_n - m_w)
    l_w = jnp.sum(e_w, axis=1, keepdims=True) + e_n
    v_n = kwn_ref[0][:, 128:256].astype(BF).astype(F32)
    o_w = (_dot(e_w.astype(BF), win[:, 128:256].astype(BF)) + e_n.astype(BF).astype(F32) * v_n) / l_w

    g = g_ref[0]
    o_ref[0] = g[:, 0:1] * o_c + g[:, 1:2] * o_s + g[:, 2:3] * o_w
    nwin_ref[0, 0:win_buf - 1, :] = win_ref[0, 1:win_buf, :]
    nwin_ref[0, win_buf - 1:win_buf, :] = kwn_ref[0]


def _attention_sample(page_table_flat, cache_sel, qblk, kvc_pad, state_win, kvs_new, kvw_new, gates8,
                      n_pages, past_len):
    n_seq = qblk.shape[0]
    n_phys = cache_sel.shape[0]
    win_buf = state_win.shape[1]
    cache3 = cache_sel.reshape(n_phys, PAGE_SIZE, KV_WIDTH)
    per_seq = lambda n, pt: (n, 0, 0)

    def page_map(p):
        return lambda n, pt: (pt[n * n_pages + p], 0, 0)

    grid_spec = pltpu.PrefetchScalarGridSpec(
        num_scalar_prefetch=1,
        grid=(n_seq,),
        in_specs=[pl.BlockSpec((1, PAGE_SIZE, KV_WIDTH), page_map(p)) for p in range(n_pages)]
        + [pl.BlockSpec((1, 8, 128), per_seq),
           pl.BlockSpec((1, 128, KV_WIDTH), per_seq),
           pl.BlockSpec((1, win_buf, KV_WIDTH), per_seq),
           pl.BlockSpec((1, 1, KV_WIDTH), per_seq),
           pl.BlockSpec((1, 1, KV_WIDTH), per_seq),
           pl.BlockSpec((1, 8, 3), per_seq)],
        out_specs=(pl.BlockSpec((1, 8, 128), per_seq),
                   pl.BlockSpec((1, win_buf, KV_WIDTH), per_seq)),
        scratch_shapes=[pltpu.VMEM((past_len, 256), BF), pltpu.VMEM((past_len, KV_HALF), BF)],
    )
    return pl.pallas_call(
        functools.partial(_sattn_kernel, n_pages=n_pages, past_len=past_len),
        out_shape=(jax.ShapeDtypeStruct((n_seq, 8, 128), F32),
                   jax.ShapeDtypeStruct((n_seq, win_buf, KV_WIDTH), F32)),
        grid_spec=grid_spec,
        compiler_params=_cparams("arbitrary"),
        name="attention_sample",
    )(page_table_flat, *([cache3] * n_pages), qblk, kvc_pad, state_win, kvs_new, kvw_new, gates8)


def kernel(x_prompt, x_sample, cache_cmp_kv, cache_sel_kv, state_win_kv, state_pool, page_table, c_prompt, c_sample, w_ada, b_ada, g_pre_mix, g_post_mix, g_pre_mlp, g_post_mlp, w_in, w_cmp, pos_cmp, w_pool, pool_scale, w_up_nsa, w_up_pool, w_o, w_ff1, w_ff2):
    n_batch, seq, _ = x_prompt.shape
    n_seq = x_sample.shape[0]
    n_pages = page_table.shape[1]
    past_len = n_pages * PAGE_SIZE
    assert x_sample.shape[1] == 1 and w_ada.shape[0] == 1 and past_len % BLOCK == 0
    assert seq % TOKEN_TILE == 0 and seq // BLOCK == 128 and state_win_kv.shape[2] == WINDOW

    w = w_in[0]
    wn = jnp.concatenate([w[:, 512:1280], w[:, 1304:1816]], axis=1).astype(BF)
    wt = jnp.concatenate([w[:, 0:512], w[:, 896:1024], w[:, 1152:1280], w[:, 1280:1304],
                          jnp.zeros((D_MODEL, 8), F32)], axis=1).T.astype(BF)
    wgm = w[:, 1816:3864].astype(BF)
    ws = jnp.pad(w[:, :1816], ((0, 0), (0, 104))).astype(BF)
    eye = jnp.eye(N_KV_HEADS, dtype=F32)
    w_full = jnp.einsum('sjde,st,hk->jshdtke', w_cmp[0], eye, eye).reshape(CMP_K, KV_WIDTH).astype(BF)
    pos_flat = jnp.broadcast_to(pos_cmp[0][:, :, None, :], (BLOCK, 2, N_KV_HEADS, HEAD_DIM)).reshape(1, CMP_K)
    wp = w_pool[0].astype(BF)
    ps = pool_scale[0].reshape(1, POOL_WIDTH)
    wun, wup, wo = w_up_nsa[0].astype(BF), w_up_pool[0].astype(BF), w_o[0].astype(BF)
    w1, w2 = w_ff1[0].astype(BF), w_ff2[0].astype(BF)
    gpm, gqm = g_pre_mix[0].reshape(1, D_MODEL), g_post_mix[0].reshape(1, D_MODEL)
    gpf, gqf = g_pre_mlp[0].reshape(1, D_MODEL), g_post_mlp[0].reshape(1, D_MODEL)

    n_c = n_batch + n_seq
    c_all = jnp.pad(jnp.concatenate([c_prompt, c_sample], axis=0), ((0, (-n_c) % 8), (0, 0)))
    ada = _adaln(c_all, w_ada[0], b_ada[0].reshape(1, -1))
    ada_p = ada[:n_batch].reshape(n_batch, 6, 1, D_MODEL)
    ada_s = ada[n_batch:n_c].reshape(n_seq, 6, D_MODEL)
    mods_p = [ada_p[:, k] for k in range(6)]
    mods_s = [ada_s[:, k][None] for k in range(6)]

    xp = x_prompt.reshape(n_batch * seq, D_MODEL)
    kvc_p, kvs_p, kvw_p, u_p, qT, kaug, vsT, kw, vwT, gT = _project_prompt(
        xp, mods_p[0], mods_p[1], gpm, wn, wt, n_batch, seq)
    kvc_blk = _compress(kvc_p.reshape(n_batch * seq // BLOCK, CMP_K), pos_flat, w_full)
    kvc_blk = kvc_blk.reshape(n_batch, seq // BLOCK, KV_WIDTH)
    kc = kvc_blk[:, :, 0:128].astype(BF)
    vcT = jnp.swapaxes(kvc_blk[:, :, 128:256], 1, 2).astype(BF)
    onsa_p = _attention_prompt(qT, kaug, vsT, kw, vwT, kc, vcT, gT, n_batch, seq)
    opool_p = _pool_prompt(u_p, wp, ps, n_batch, seq)
    x1_p = _token_call(_merge_kernel, "merge_prompt", xp, [mods_p[0], mods_p[1], mods_p[2]], TOKEN_TILE, seq,
                       [onsa_p.reshape(n_batch * seq, NSA_WIDTH), opool_p], [gpm, gqm, wgm, wun, wup, wo])
    y_p = _token_call(_mlp_kernel, "mlp_prompt", x1_p, [mods_p[3], mods_p[4], mods_p[5]], 256, seq,
                      [], [gpf, gqf, w1, w2])

    xs = x_sample.reshape(n_seq, D_MODEL)
    z, zsig = _project_sample(xs, mods_s[0][0], mods_s[1][0], gpm, ws)
    q_s = z[:, 0:512] * (HEAD_DIM ** -0.5)
    kvc_n, kvs_n, kvw_n = z[:, 512:768], z[:, 768:1024], z[:, 1024:1280]
    gates_s = zsig[:, 1280:1304].reshape(n_seq, N_HEADS, 3)
    u_s = z[:, 1304:1816]
    q5 = q_s.reshape(n_seq, N_KV_HEADS, GROUP, 1, HEAD_DIM) * eye[None, :, None, :, None]
    qblk = q5.reshape(n_seq, N_HEADS, KV_HALF)

    pt_flat = page_table.reshape(-1)
    nb_past = past_len // BLOCK
    kvc_past = _compress_paged(cache_cmp_kv[0], pt_flat, pos_flat, w_full, n_seq, n_pages)
    x_last = jnp.pad(kvc_n, ((0, 0), (0, CMP_K - KV_WIDTH)))
    kvc_last = _compress(x_last, pos_flat, w_full)
    kvc_s = jnp.concatenate([kvc_past.reshape(n_seq, nb_past, KV_WIDTH), kvc_last[:, None, :]], axis=1)
    kvc_pad = jnp.pad(kvc_s, ((0, 0), (0, 128 - nb_past - 1), (0, 0)))
    win_buf = state_win_kv.shape[2]
    o8, new_win = _attention_sample(
        pt_flat, cache_sel_kv[0], qblk, kvc_pad, state_win_kv[0].reshape(n_seq, win_buf, KV_WIDTH),
        kvs_n[:, None, :], kvw_n[:, None, :], gates_s, n_pages, past_len)
    o5 = o8.reshape(n_seq, N_KV_HEADS, GROUP, N_KV_HEADS, HEAD_DIM)
    onsa_s = jnp.concatenate([o5[:, 0, :, 0, :], o5[:, 1, :, 1, :]], axis=1).reshape(n_seq, NSA_WIDTH).astype(BF)
    opool_s = _pool_sample(u_s, jnp.swapaxes(state_pool[0], 0, 1), wp, ps)
    x1_s = _token_call(_merge_kernel, "merge_sample", xs, [mods_s[0], mods_s[1], mods_s[2]], n_seq, n_seq,
                       [onsa_s, opool_s], [gpm, gqm, wgm, wun, wup, wo])
    y_s = _token_call(_mlp_kernel, "mlp_sample", x1_s, [mods_s[3], mods_s[4], mods_s[5]], n_seq, n_seq,
                      [], [gpf, gqf, w1, w2])

    kv6 = (2, N_KV_HEADS, HEAD_DIM)
    win_p = min(WINDOW, seq)
    return (
        y_p.reshape(n_batch, seq, D_MODEL),
        y_s.reshape(n_seq, 1, D_MODEL),
        kvc_p.reshape((1, n_batch, seq) + kv6),
        kvs_p.reshape((1, n_batch, seq) + kv6),
        kvw_p.reshape((n_batch, seq) + kv6)[None, :, seq - win_p:],
        u_p.reshape(n_batch, seq, POOL_WIDTH)[None, :, seq - POOL_HIST:],
        kvc_n.reshape((1, n_seq, 1) + kv6),
        kvs_n.reshape((1, n_seq, 1) + kv6),
        new_win.reshape((1, n_seq, win_buf) + kv6),
        jnp.concatenate([state_pool[0][:, 1:], u_s[:, None, :]], axis=1)[None],
    )
```

```python
import functools
import math

import jax
import jax.numpy as jnp
from jax import lax
from jax.experimental import pallas as pl
from jax.experimental.pallas import tpu as pltpu

D_MODEL = 1024
N_HEADS = 8
HEAD_DIM = 64
N_KV_HEADS = 2
GROUP = N_HEADS // N_KV_HEADS
BLOCK = 64
N_SELECT = 16
WINDOW = 512
Q_TILE = 128
NSA_WIDTH = N_HEADS * HEAD_DIM
KV_WIDTH = 2 * N_KV_HEADS * HEAD_DIM
KV_HALF = N_KV_HEADS * HEAD_DIM
FORCE_SCORE = 16.0
POOL_WINDOWS = (2, 4, 8, 16)
POOL_WIDTH = 512
POOL_GROUP_DIM = 128
POOL_HIST = 15
D_FF = 4 * D_MODEL
EPS = 1e-6
PAGE_SIZE = 128
CMP_K = BLOCK * KV_HALF
CMP_SPLIT = 4
CMP_GROUP = 128

BF = jnp.bfloat16
F32 = jnp.float32
I32 = jnp.int32
MASK_BIG = 2.0 ** 100
NEG_INF = float("-inf")
LOG2E = math.log2(math.e)

TOKEN_TILE = 512
SEL_CHUNK = 512
VMEM_LIMIT = 56 * 1024 * 1024


def _cparams(*sem):
    return pltpu.CompilerParams(dimension_semantics=sem, vmem_limit_bytes=VMEM_LIMIT)


def _rms(x, g):
    return x * lax.rsqrt(jnp.mean(x * x, axis=-1, keepdims=True) + EPS) * g


def _dot(a, b):
    return jnp.dot(a, b, preferred_element_type=F32)


def _dot_nt(a, b):
    return lax.dot_general(a, b, (((1,), (1,)), ((), ())), preferred_element_type=F32)


def _ada_kernel(c_ref, w_ref, b_ref, o_ref):
    c = c_ref[...]
    a = (c * jax.nn.sigmoid(c)).astype(BF)
    o_ref[...] = _dot(a, w_ref[...].astype(BF)) + b_ref[...]


def _adaln(c_all, w_ada, b_ada):
    rows = c_all.shape[0]
    n_out = w_ada.shape[1]
    tn = 512
    return pl.pallas_call(
        _ada_kernel,
        out_shape=jax.ShapeDtypeStruct((rows, n_out), F32),
        grid=(n_out // tn,),
        in_specs=[pl.BlockSpec((rows, D_MODEL), lambda j: (0, 0)),
                  pl.BlockSpec((D_MODEL, tn), lambda j: (0, j)),
                  pl.BlockSpec((1, tn), lambda j: (0, j))],
        out_specs=pl.BlockSpec((rows, tn), lambda j: (0, j)),
        compiler_params=_cparams("arbitrary"),
        name="adaln",
    )(c_all, w_ada, b_ada)


def _proj_kernel(x_ref, sh_ref, sc_ref, g_ref, wn_ref, wt_ref,
                 kvcT_ref, kvsT_ref, kvwT_ref, ca_ref, cb_ref, u_ref,
                 qT_ref, kaug_ref, vsT_ref, kw_ref, vwT_ref, gT_ref, *, tm, tpb):
    x = x_ref[...]
    h = _rms(x, g_ref[...]) * (1.0 + sc_ref[0]) + sh_ref[0]
    hb = h.astype(BF)
    zn = _dot(hb, wn_ref[...])
    zt = _dot_nt(wt_ref[...], hb)
    kvcT_ref[0] = zt[512:768]
    kvsT_ref[0] = zt[768:1024]
    kvwT_ref[0] = zt[1024:1280]
    ca_ref[...] = zn[:, 768:896]
    cb_ref[...] = zn[:, 896:1024]
    u_ref[...] = zn[:, 256:768]
    qT_ref[0] = (zt[0:512] * (HEAD_DIM ** -0.5 * LOG2E)).astype(BF)
    vsT_ref[0, 0] = zt[896:1024].astype(BF)
    for c in range(tm // Q_TILE):
        vwT_ref[0, c] = zt[1152:1280, c * Q_TILE:(c + 1) * Q_TILE].astype(BF)
    gT_ref[0] = jax.nn.sigmoid(zt[1280:1312])
    t0 = (pl.program_id(0) % tpb) * tm
    blk = jnp.right_shift(t0 + lax.broadcasted_iota(I32, (tm, 128), 0), 6)
    lane = lax.broadcasted_iota(I32, (tm, 128), 1)
    kaug_ref[0, :, 0:128] = zn[:, 0:128].astype(BF)
    kaug_ref[0, :, 128:256] = jnp.where(blk == lane, MASK_BIG, 0.0).astype(BF)
    kw_ref[0] = zn[:, 128:256].astype(BF)


def _project_prompt(x2, shift, scale, g, wn, wt, n_batch, seq):
    tm = TOKEN_TILE
    tpb = seq // tm
    nt = n_batch * seq
    tok = lambda t: (t, 0)
    per_b = lambda t: (t // tpb, 0, 0)
    featT = lambda t: (t // tpb, 0, t % tpb)
    rows3 = lambda t: (t // tpb, t % tpb, 0)
    rows4 = lambda t: (t // tpb, t % tpb, 0, 0)
    kvT = jax.ShapeDtypeStruct((n_batch, KV_WIDTH, seq), F32)
    out_shape = (
        kvT, kvT, kvT,
        jax.ShapeDtypeStruct((nt, KV_HALF), F32),
        jax.ShapeDtypeStruct((nt, KV_HALF), F32),
        jax.ShapeDtypeStruct((nt, POOL_WIDTH), F32),
        jax.ShapeDtypeStruct((n_batch, NSA_WIDTH, seq), BF),
        jax.ShapeDtypeStruct((n_batch, seq, 256), BF),
        jax.ShapeDtypeStruct((n_batch, seq // SEL_CHUNK, KV_HALF, SEL_CHUNK), BF),
        jax.ShapeDtypeStruct((n_batch, seq, KV_HALF), BF),
        jax.ShapeDtypeStruct((n_batch, seq // Q_TILE, KV_HALF, Q_TILE), BF),
        jax.ShapeDtypeStruct((n_batch, 32, seq), F32),
    )
    out_specs = (
        pl.BlockSpec((1, KV_WIDTH, tm), featT),
        pl.BlockSpec((1, KV_WIDTH, tm), featT),
        pl.BlockSpec((1, KV_WIDTH, tm), featT),
        pl.BlockSpec((tm, KV_HALF), tok),
        pl.BlockSpec((tm, KV_HALF), tok),
        pl.BlockSpec((tm, POOL_WIDTH), tok),
        pl.BlockSpec((1, NSA_WIDTH, tm), featT),
        pl.BlockSpec((1, tm, 256), rows3),
        pl.BlockSpec((1, 1, KV_HALF, SEL_CHUNK), rows4),
        pl.BlockSpec((1, tm, KV_HALF), rows3),
        pl.BlockSpec((1, tm // Q_TILE, KV_HALF, Q_TILE), rows4),
        pl.BlockSpec((1, 32, tm), featT),
    )
    return pl.pallas_call(
        functools.partial(_proj_kernel, tm=tm, tpb=tpb),
        out_shape=out_shape,
        grid=(nt // tm,),
        in_specs=[pl.BlockSpec((tm, D_MODEL), tok),
                  pl.BlockSpec((1, 1, D_MODEL), per_b),
                  pl.BlockSpec((1, 1, D_MODEL), per_b),
                  pl.BlockSpec((1, D_MODEL), lambda t: (0, 0)),
                  pl.BlockSpec(wn.shape, lambda t: (0, 0)),
                  pl.BlockSpec(wt.shape, lambda t: (0, 0))],
        out_specs=out_specs,
        compiler_params=_cparams("arbitrary"),
        name="project_prompt",
    )(x2, shift, scale, g, wn, wt)


def _compress_rows(src_refs, pos_ref, w_ref, o_ref, n_blocks):
    rows_per_slice = BLOCK // CMP_SPLIT
    for s in range(2):
        acc = jnp.zeros((n_blocks, KV_HALF), F32)
        for c in range(CMP_SPLIT):
            xc = jnp.concatenate(
                [src_refs[s][pl.ds(c * rows_per_slice + r, n_blocks, stride=BLOCK), :]
                 for r in range(rows_per_slice)], axis=1) + pos_ref[s, c]
            acc = acc + _dot(xc.astype(BF), w_ref[s, c])
        o_ref[:, s * KV_HALF:(s + 1) * KV_HALF] = acc


def _cmp_kernel(xa_ref, xb_ref, pos_ref, w_ref, o_ref):
    _compress_rows((xa_ref, xb_ref), pos_ref, w_ref, o_ref, CMP_GROUP)


def _compress(xa, xb, pos4, w4):
    m = xa.shape[0] // BLOCK
    rows = CMP_GROUP * BLOCK
    return pl.pallas_call(
        _cmp_kernel,
        out_shape=jax.ShapeDtypeStruct((m, KV_WIDTH), F32),
        grid=(m // CMP_GROUP,),
        in_specs=[pl.BlockSpec((rows, KV_HALF), lambda i: (i, 0)),
                  pl.BlockSpec((rows, KV_HALF), lambda i: (i, 0)),
                  pl.BlockSpec(pos4.shape, lambda i: (0, 0, 0, 0)),
                  pl.BlockSpec(w4.shape, lambda i: (0, 0, 0, 0))],
        out_specs=pl.BlockSpec((CMP_GROUP, KV_WIDTH), lambda i: (i, 0)),
        compiler_params=_cparams("arbitrary"),
        name="compress_blocks",
    )(xa, xb, pos4, w4)


def _scmp_kernel(pt_ref, *refs, n_pages_step):
    pages = refs[:n_pages_step]
    pos_ref, w_ref, o_ref, sa_ref, sb_ref = refs[n_pages_step:]
    for k in range(n_pages_step):
        pg = pages[k][0]
        sa_ref[k * PAGE_SIZE:(k + 1) * PAGE_SIZE, :] = pg[0:KV_HALF, :].T
        sb_ref[k * PAGE_SIZE:(k + 1) * PAGE_SIZE, :] = pg[KV_HALF:KV_WIDTH, :].T
    _compress_rows((sa_ref, sb_ref), pos_ref, w_ref, o_ref, CMP_GROUP)


def _compress_paged(cache_t, page_table_flat, pos4, w4, n_seq, n_pages):
    blocks_per_page = PAGE_SIZE // BLOCK
    n_pages_step = CMP_GROUP // blocks_per_page

    def page_map(k):
        return lambda s, pt: (pt[s * n_pages_step + k], 0, 0)

    grid_spec = pltpu.PrefetchScalarGridSpec(
        num_scalar_prefetch=1,
        grid=(n_seq * n_pages // n_pages_step,),
        in_specs=[pl.BlockSpec((1, KV_WIDTH, PAGE_SIZE), page_map(k)) for k in range(n_pages_step)]
        + [pl.BlockSpec(pos4.shape, lambda s, pt: (0, 0, 0, 0)),
           pl.BlockSpec(w4.shape, lambda s, pt: (0, 0, 0, 0))],
        out_specs=pl.BlockSpec((CMP_GROUP, KV_WIDTH), lambda s, pt: (s, 0)),
        scratch_shapes=[pltpu.VMEM((n_pages_step * PAGE_SIZE, KV_HALF), F32),
                        pltpu.VMEM((n_pages_step * PAGE_SIZE, KV_HALF), F32)],
    )
    return pl.pallas_call(
        functools.partial(_scmp_kernel, n_pages_step=n_pages_step),
        out_shape=jax.ShapeDtypeStruct((n_seq * n_pages * blocks_per_page, KV_WIDTH), F32),
        grid_spec=grid_spec,
        compiler_params=_cparams("arbitrary"),
        name="compress_paged",
    )(page_table_flat, *([cache_t] * n_pages_step), pos4, w4)


def _attn_kernel(qT_ref, kaug_ref, vsT_ref, kw_ref, vwT_ref, kc_ref, vcT_ref, gT_ref, o_ref,
                 qa_ref, sa_ref, sb_ref, m_ref, l_ref, acc_ref, outT_ref):
    i = pl.program_id(1)
    q0 = i * Q_TILE
    n_full = jnp.right_shift(i, 2)
    row = lax.broadcasted_iota(I32, (128, 128), 0)
    tok = lax.broadcasted_iota(I32, (128, 128), 1)
    qpos = q0 + tok
    cur = jnp.right_shift(qpos, 6)
    cmask = (row + 1) * BLOCK - 1 <= qpos
    valid = row <= cur
    forced = ((row == 0) | (row == cur) | (row == cur - 1)).astype(F32)
    key512 = lax.broadcasted_iota(I32, (SEL_CHUNK, 512), 0)
    qpos512 = q0 + jnp.bitwise_and(lax.broadcasted_iota(I32, (SEL_CHUNK, 512), 1), Q_TILE - 1)
    key128 = lax.broadcasted_iota(I32, (Q_TILE, 512), 0)
    tok128 = jnp.bitwise_and(lax.broadcasted_iota(I32, (Q_TILE, 512), 1), Q_TILE - 1)
    hs = [slice(h * HEAD_DIM, (h + 1) * HEAD_DIM) for h in range(N_KV_HEADS)]

    def online_step(st, s, v_t):
        m_old = m_ref[st]
        m_new = jnp.maximum(m_old, jnp.max(s, axis=0, keepdims=True))
        alpha = jnp.exp2(m_old - m_new)
        p = jnp.exp2(s - m_new)
        l_ref[st] = alpha * l_ref[st] + jnp.sum(p, axis=0, keepdims=True)
        acc_ref[st] = alpha * acc_ref[st] + _dot(v_t, p.astype(BF))
        m_ref[st] = m_new

    m_ref[...] = jnp.full(m_ref.shape, NEG_INF, F32)
    l_ref[...] = jnp.zeros(l_ref.shape, F32)
    acc_ref[...] = jnp.zeros(acc_ref.shape, F32)

    o_c = []
    scores = []
    for h in range(N_KV_HEADS):
        qa_ref[h] = jnp.zeros(qa_ref.shape[1:], BF)
        for g in range(GROUP):
            r0 = h * GROUP * HEAD_DIM + g * HEAD_DIM
            qa_ref[h, hs[h], g * Q_TILE:(g + 1) * Q_TILE] = qT_ref[0, r0:r0 + HEAD_DIM, :]
        sc = _dot(kc_ref[0], qa_ref[h, 0:128, :])
        imp = jnp.zeros((128, Q_TILE), F32)
        p_parts = []
        for g in range(GROUP):
            s = jnp.where(cmask, sc[:, g * Q_TILE:(g + 1) * Q_TILE], NEG_INF)
            mx = jnp.max(s, axis=0, keepdims=True)
            mx = jnp.where(mx > NEG_INF, mx, 0.0)
            e = jnp.where(cmask, jnp.exp2(s - mx), 0.0)
            p = e / jnp.maximum(jnp.sum(e, axis=0, keepdims=True), 1e-30)
            imp = imp + p
            p_parts.append(p)
        o_c.append(_dot(vcT_ref[0, hs[h], :], jnp.concatenate(p_parts, axis=1).astype(BF)))
        scores.append(jnp.where(valid, imp + FORCE_SCORE * forced, -1.0))

    bits = lax.bitcast_convert_type(jnp.concatenate(scores, axis=1), I32)

    def radix_body(k, thr):
        cand = jnp.bitwise_or(thr, jnp.left_shift(jnp.int32(1), 30 - k))
        cnt = jnp.sum(jnp.where(bits >= cand, 1, 0), axis=0, keepdims=True)
        return jnp.where(cnt >= N_SELECT, cand, thr)

    thr = lax.fori_loop(0, 31, radix_body, jnp.zeros((1, 2 * Q_TILE), I32))
    above = bits > thr
    ties = bits == thr
    n_above = jnp.sum(jnp.where(above, 1.0, 0.0), axis=0, keepdims=True)
    lower = jnp.where(tok < row, 1.0, 0.0).astype(BF)
    ties_before = _dot(lower, jnp.where(ties, 1.0, 0.0).astype(BF))
    sel = above | (ties & (ties_before < float(N_SELECT) - n_above))
    selm1 = jnp.where(sel, 0.0, -1.0).astype(BF)
    for h in range(N_KV_HEADS):
        for g in range(GROUP):
            qa_ref[h, 128:256, g * Q_TILE:(g + 1) * Q_TILE] = selm1[:, h * Q_TILE:(h + 1) * Q_TILE]

    def sel_scores(j, dst_ref):
        kt = kaug_ref[0, pl.ds(pl.multiple_of(j * SEL_CHUNK, SEL_CHUNK), SEL_CHUNK), :]
        for h in range(N_KV_HEADS):
            dst_ref[h] = _dot(kt, qa_ref[h])

    def sel_process(src_ref, j, causal):
        for h in range(N_KV_HEADS):
            s = src_ref[h]
            if causal:
                s = jnp.where(j * SEL_CHUNK + key512 <= qpos512, s, -MASK_BIG)
            online_step(h, s, vsT_ref[0, j, hs[h], :])

    sel_scores(0, sa_ref)

    def pair_body(t, carry):
        j = 2 * t
        sel_scores(j + 1, sb_ref)
        sel_process(sa_ref, j, False)
        sel_scores(j + 2, sa_ref)
        sel_process(sb_ref, j + 1, False)
        return carry

    n_pairs = jnp.right_shift(n_full, 1)
    lax.fori_loop(0, n_pairs, pair_body, 0)
    j_last = 2 * n_pairs

    @pl.when(jnp.bitwise_and(n_full, 1) == 1)
    def _():
        sel_scores(j_last + 1, sb_ref)
        sel_process(sa_ref, j_last, False)
        sel_process(sb_ref, j_last + 1, True)

    @pl.when(jnp.bitwise_and(n_full, 1) == 0)
    def _():
        sel_process(sa_ref, j_last, True)

    j0 = jnp.maximum(i - WINDOW // Q_TILE, 0)
    w0 = j0 * Q_TILE
    delta = qpos512 - (w0 + key512)
    wmask = (delta >= 0) & (delta < WINDOW)
    dmask = (key128 <= tok128) & (i >= WINDOW // Q_TILE)
    kw_a = kw_ref[0, pl.ds(pl.multiple_of(w0, Q_TILE), WINDOW), :]
    kw_b = kw_ref[0, pl.ds(pl.multiple_of(q0, Q_TILE), Q_TILE), :]
    for h in range(N_KV_HEADS):
        qf = qa_ref[h, 0:128, :]
        s = jnp.where(wmask, _dot(kw_a, qf), -MASK_BIG)
        v_t = jnp.concatenate([vwT_ref[0, j0 + c, hs[h], :] for c in range(WINDOW // Q_TILE)], axis=1)
        online_step(2 + h, s, v_t)
        s = jnp.where(dmask, _dot(kw_b, qf), -MASK_BIG)
        online_step(2 + h, s, vwT_ref[0, i, hs[h], :])

    for h in range(N_KV_HEADS):
        o_s = acc_ref[h] / l_ref[h]
        o_w = acc_ref[2 + h] / l_ref[2 + h]
        for g in range(GROUP):
            gs = slice(g * Q_TILE, (g + 1) * Q_TILE)
            gr = h * GROUP * 3 + g * 3
            og = (gT_ref[0, gr:gr + 1, :] * o_c[h][:, gs] + gT_ref[0, gr + 1:gr + 2, :] * o_s[:, gs]
                  + gT_ref[0, gr + 2:gr + 3, :] * o_w[:, gs])
            r0 = h * GROUP * HEAD_DIM + g * HEAD_DIM
            outT_ref[r0:r0 + HEAD_DIM, :] = og

    o_ref[0] = outT_ref[...].T.astype(BF)


def _attention_prompt(qT, kaug, vsT, kw, vwT, kc, vcT, gT, n_batch, seq):
    per_b3 = lambda n, i: (n, 0, 0)
    per_b4 = lambda n, i: (n, 0, 0, 0)
    rows = GROUP * Q_TILE
    return pl.pallas_call(
        _attn_kernel,
        out_shape=jax.ShapeDtypeStruct((n_batch, seq, NSA_WIDTH), BF),
        grid=(n_batch, seq // Q_TILE),
        in_specs=[pl.BlockSpec((1, NSA_WIDTH, Q_TILE), lambda n, i: (n, 0, i)),
                  pl.BlockSpec((1, seq, 256), per_b3),
                  pl.BlockSpec((1, seq // SEL_CHUNK, KV_HALF, SEL_CHUNK), per_b4),
                  pl.BlockSpec((1, seq, KV_HALF), per_b3),
                  pl.BlockSpec((1, seq // Q_TILE, KV_HALF, Q_TILE), per_b4),
                  pl.BlockSpec((1, 128, KV_HALF), per_b3),
                  pl.BlockSpec((1, KV_HALF, 128), per_b3),
                  pl.BlockSpec((1, 32, Q_TILE), lambda n, i: (n, 0, i))],
        out_specs=pl.BlockSpec((1, Q_TILE, NSA_WIDTH), lambda n, i: (n, i, 0)),
        scratch_shapes=[pltpu.VMEM((N_KV_HEADS, 256, rows), BF),
                        pltpu.VMEM((N_KV_HEADS, SEL_CHUNK, rows), F32),
                        pltpu.VMEM((N_KV_HEADS, SEL_CHUNK, rows), F32),
                        pltpu.VMEM((2 * N_KV_HEADS, 1, rows), F32),
                        pltpu.VMEM((2 * N_KV_HEADS, 1, rows), F32),
                        pltpu.VMEM((2 * N_KV_HEADS, HEAD_DIM, rows), F32),
                        pltpu.VMEM((NSA_WIDTH, Q_TILE), F32)],
        compiler_params=_cparams("arbitrary", "arbitrary"),
        name="attention_prompt",
    )(qT, kaug, vsT, kw, vwT, kc, vcT, gT)


def _pool_kernel(u_ref, halo_ref, wp_ref, ps_ref, o_ref, ext_ref, *, tm, tpb):
    t = pl.program_id(0) % tpb
    ext_ref[0:16, :] = jnp.where(t == 0, 0.0, halo_ref[...])
    u = u_ref[...]
    ext_ref[16:16 + tm, :] = u
    pos = t * tm + lax.broadcasted_iota(I32, (tm, 1), 0)
    outs = []
    for gi, w in enumerate(POOL_WINDOWS):
        cs = slice(gi * POOL_GROUP_DIM, (gi + 1) * POOL_GROUP_DIM)
        acc = u[:, cs]
        for k in range(1, w):
            acc = acc + ext_ref[pl.ds(16 - k, tm), cs]
        cnt = jnp.minimum(pos + 1, w).astype(F32)
        pooled = acc / cnt - u[:, cs]
        outs.append(_dot(pooled.astype(BF), wp_ref[gi]))
    o_ref[...] = (jnp.concatenate(outs, axis=1) * ps_ref[...]).astype(BF)


def _pool_prompt(u, w_pool, pool_scale, n_batch, seq):
    tm = TOKEN_TILE
    tpb = seq // tm
    nt = n_batch * seq
    return pl.pallas_call(
        functools.partial(_pool_kernel, tm=tm, tpb=tpb),
        out_shape=jax.ShapeDtypeStruct((nt, POOL_WIDTH), BF),
        grid=(nt // tm,),
        in_specs=[pl.BlockSpec((tm, POOL_WIDTH), lambda t: (t, 0)),
                  pl.BlockSpec((16, POOL_WIDTH), lambda t: (jnp.maximum(t * (tm // 16) - 1, 0), 0)),
                  pl.BlockSpec(w_pool.shape, lambda t: (0, 0, 0)),
                  pl.BlockSpec((1, POOL_WIDTH), lambda t: (0, 0))],
        out_specs=pl.BlockSpec((tm, POOL_WIDTH), lambda t: (t, 0)),
        scratch_shapes=[pltpu.VMEM((tm + 16, POOL_WIDTH), F32)],
        compiler_params=_cparams("arbitrary"),
        name="pool_prompt",
    )(u, u, w_pool, pool_scale)


def _spool_kernel(u_ref, hist_ref, wp_ref, ps_ref, o_ref):
    u = u_ref[...]
    outs = []
    for gi, w in enumerate(POOL_WINDOWS):
        cs = slice(gi * POOL_GROUP_DIM, (gi + 1) * POOL_GROUP_DIM)
        acc = u[:, cs]
        for k in range(1, w):
            acc = acc + hist_ref[POOL_HIST - k, :, cs]
        pooled = acc / float(w) - u[:, cs]
        outs.append(_dot(pooled.astype(BF), wp_ref[gi]))
    o_ref[...] = (jnp.concatenate(outs, axis=1) * ps_ref[...]).astype(BF)


def _pool_sample(u, hist_t, w_pool, pool_scale):
    n = u.shape[0]
    return pl.pallas_call(
        _spool_kernel,
        out_shape=jax.ShapeDtypeStruct((n, POOL_WIDTH), BF),
        name="pool_sample",
    )(u, hist_t, w_pool, pool_scale)


def _merge_kernel(x_ref, sh_ref, sc_ref, gate_ref, gpre_ref, gpost_ref, onsa_ref, opool_ref,
                  wgm_ref, wun_ref, wup_ref, wo_ref, o_ref):
    x = x_ref[...]
    h = _rms(x, gpre_ref[...]) * (1.0 + sc_ref[0]) + sh_ref[0]
    gm = jax.nn.sigmoid(_dot(h.astype(BF), wgm_ref[...]))
    m = (gm[:, :D_MODEL] * _dot(onsa_ref[...], wun_ref[...])
         + gm[:, D_MODEL:] * _dot(opool_ref[...], wup_ref[...]))
    m = _dot(m.astype(BF), wo_ref[...])
    o_ref[...] = x + gate_ref[0] * _rms(m, gpost_ref[...])


def _mlp_kernel(x_ref, sh_ref, sc_ref, gate_ref, gpre_ref, gpost_ref, w1_ref, w2_ref, o_ref):
    x = x_ref[...]
    h = _rms(x, gpre_ref[...]) * (1.0 + sc_ref[0]) + sh_ref[0]
    hb = h.astype(BF)
    f = jnp.zeros(x.shape, F32)
    fc = 1024
    for c in range(D_FF // fc):
        a = jnp.maximum(_dot(hb, w1_ref[:, c * fc:(c + 1) * fc]), 0.0)
        f = f + _dot((a * a).astype(BF), w2_ref[c * fc:(c + 1) * fc, :])
    o_ref[...] = x + gate_ref[0] * _rms(f, gpost_ref[...])


def _token_call(kernel, name, x2, mods, tm, rows_per_mod, extra_tok, consts):
    nt = x2.shape[0]
    r = mods[0].shape[1]
    mod_spec = pl.BlockSpec((1, r, D_MODEL), lambda t: ((t * tm) // rows_per_mod, 0, 0))
    in_specs = [pl.BlockSpec((tm, D_MODEL), lambda t: (t, 0))] + [mod_spec] * len(mods)
    in_specs += [pl.BlockSpec((1, D_MODEL), lambda t: (0, 0))] * 2
    in_specs += [pl.BlockSpec((tm, a.shape[1]), lambda t: (t, 0)) for a in extra_tok]
    in_specs += [pl.BlockSpec(w.shape, lambda t: (0, 0), pipeline_mode=pl.Buffered(1)) for w in consts[2:]]
    return pl.pallas_call(
        kernel,
        out_shape=jax.ShapeDtypeStruct((nt, D_MODEL), F32),
        grid=(nt // tm,),
        in_specs=in_specs,
        out_specs=pl.BlockSpec((tm, D_MODEL), lambda t: (t, 0)),
        compiler_params=_cparams("arbitrary"),
        name=name,
    )(x2, *mods, consts[0], consts[1], *extra_tok, *consts[2:])


def _sproj_kernel(x_ref, sh_ref, sc_ref, g_ref, wn_ref, wt_ref, z_ref, zs_ref, zT_ref):
    x = x_ref[...]
    h = _rms(x, g_ref[...]) * (1.0 + sc_ref[...]) + sh_ref[...]
    hb = h.astype(BF)
    z = _dot(hb, wn_ref[...])
    z_ref[...] = z
    zs_ref[...] = jax.nn.sigmoid(z)
    zT_ref[...] = _dot_nt(wt_ref[...], hb)


def _project_sample(x2, shift, scale, g, wn, wt):
    n = x2.shape[0]
    shp = jax.ShapeDtypeStruct((n, wn.shape[1]), F32)
    return pl.pallas_call(
        _sproj_kernel,
        out_shape=(shp, shp, jax.ShapeDtypeStruct((wt.shape[0], n), F32)),
        name="project_sample",
    )(x2, shift, scale, g, wn, wt)


def _sattn_kernel(pt_ref, *refs, n_pages, past_len):
    pages = refs[:n_pages]
    (qb_ref, kvc_ref, win_ref, ksn_ref, kwn_ref, kwnT_ref, g_ref,
     o_ref, nwin_ref, kaug_ref, vall_ref) = refs[n_pages:]
    nb_past = past_len // BLOCK
    win_buf = win_ref.shape[2]
    n = pl.program_id(0)

    @pl.when(n == 0)
    def _():
        blk = lax.broadcasted_iota(I32, (128, past_len), 0)
        key_blk = jnp.right_shift(lax.broadcasted_iota(I32, (128, past_len), 1), 6)
        kaug_ref[128:256, :] = jnp.where(blk == key_blk, MASK_BIG, 0.0).astype(BF)

    for p in range(n_pages):
        pg = pages[p][0]
        kaug_ref[0:128, p * PAGE_SIZE:(p + 1) * PAGE_SIZE] = pg[0:KV_HALF, :].astype(BF)
        vall_ref[:, p * PAGE_SIZE:(p + 1) * PAGE_SIZE] = pg[KV_HALF:KV_WIDTH, :].astype(BF)

    qb = qb_ref[0].astype(BF)
    qf = qb.astype(F32)
    row8 = lax.broadcasted_iota(I32, (8, 128), 0)
    lane8 = lax.broadcasted_iota(I32, (8, 128), 1)

    def new_key_score(knew):
        return jnp.sum(qf * knew.astype(BF).astype(F32), axis=1, keepdims=True)

    kvc = kvc_ref[0]
    s_c = _dot_nt(qb, kvc[:, 0:128].astype(BF))
    cm = lane8 < nb_past
    s_c = jnp.where(cm, s_c, NEG_INF)
    mx = jnp.max(s_c, axis=1, keepdims=True)
    mx = jnp.where(mx > NEG_INF, mx, 0.0)
    e = jnp.where(cm, jnp.exp(s_c - mx), 0.0)
    p_c = e / jnp.maximum(jnp.sum(e, axis=1, keepdims=True), 1e-30)
    o_c = _dot(p_c.astype(BF), kvc[:, 128:256].astype(BF))

    top = row8 < GROUP
    imp_a = jnp.sum(jnp.where(top, p_c, 0.0), axis=0, keepdims=True)
    imp_b = jnp.sum(jnp.where(top, 0.0, p_c), axis=0, keepdims=True)
    imp = jnp.where(top, imp_a, imp_b)
    cur = nb_past
    forced = (lane8 == 0) | (lane8 == cur) | (lane8 == cur - 1)
    score = jnp.where(lane8 <= cur, imp + FORCE_SCORE * forced.astype(F32), -1.0)
    cnt = jnp.zeros((8, 128), F32)
    for bp in range(nb_past + 1):
        other = score[:, bp:bp + 1]
        ahead = (other > score) | ((other == score) & (bp < lane8))
        cnt = cnt + jnp.where(ahead, 1.0, 0.0)
    sel = (cnt < float(N_SELECT)) & (score >= 0.0)
    selm1 = jnp.where(sel, 0.0, -1.0).astype(BF)

    qaug = jnp.concatenate([qb, selm1], axis=1)
    s_s = _dot(qaug, kaug_ref[...])
    ksn = ksn_ref[0]
    s_n = new_key_score(ksn[:, 0:128])
    m_s = jnp.maximum(jnp.max(s_s, axis=1, keepdims=True), s_n)
    e_s = jnp.exp(s_s - m_s)
    e_n = jnp.exp(s_n - m_s)
    l_s = jnp.sum(e_s, axis=1, keepdims=True) + e_n
    v_n = ksn[:, 128:256].astype(BF).astype(F32)
    o_s = (_dot_nt(e_s.astype(BF), vall_ref[...]) + e_n.astype(BF).astype(F32) * v_n) / l_s

    win = win_ref[0]
    s_w = _dot(qb, win[0:KV_HALF, :].astype(BF))
    lane_w = lax.broadcasted_iota(I32, (8, win_buf), 1)
    s_w = jnp.where(lane_w >= win_buf + 1 - WINDOW, s_w, NEG_INF)
    kwn = kwn_ref[0]
    s_n = new_key_score(kwn[:, 0:128])
    m_w = jnp.maximum(jnp.max(s_w, axis=1, keepdims=True), s_n)
    e_w = jnp.exp(s_w - m_w)
    e_n = jnp.exp(s_n - m_w)
    l_w = jnp.sum(e_w, axis=1, keepdims=True) + e_n
    v_n = kwn[:, 128:256].astype(BF).astype(F32)
    o_w = (_dot_nt(e_w.astype(BF), win[KV_HALF:KV_WIDTH, :].astype(BF)) + e_n.astype(BF).astype(F32) * v_n) / l_w

    g = g_ref[0]
    o_ref[0] = g[:, 0:1] * o_c + g[:, 1:2] * o_s + g[:, 2:3] * o_w

    seq_lane = lax.broadcasted_iota(I32, kwnT_ref.shape, 1)
    new_col = jnp.sum(jnp.where(seq_lane == n, kwnT_ref[...], 0.0), axis=1, keepdims=True)
    shifted = pltpu.roll(win, win_buf - 1, axis=1)
    row_lane = lax.broadcasted_iota(I32, (KV_WIDTH, win_buf), 1)
    nwin_ref[0] = jnp.where(row_lane == win_buf - 1, new_col, shifted)


def _attention_sample(page_table_flat, cache_t, qblk, kvc_pad, win_t, kvs_new, kvw_new, kvw_new_t, gates8,
                      n_pages, past_len):
    n_seq = qblk.shape[0]
    win_buf = win_t.shape[2]
    per_seq = lambda n, pt: (n, 0, 0)

    def page_map(p):
        return lambda n, pt: (pt[n * n_pages + p], 0, 0)

    grid_spec = pltpu.PrefetchScalarGridSpec(
        num_scalar_prefetch=1,
        grid=(n_seq,),
        in_specs=[pl.BlockSpec((1, KV_WIDTH, PAGE_SIZE), page_map(p)) for p in range(n_pages)]
        + [pl.BlockSpec((1, 8, 128), per_seq),
           pl.BlockSpec((1, 128, KV_WIDTH), per_seq),
           pl.BlockSpec((1, KV_WIDTH, win_buf), per_seq),
           pl.BlockSpec((1, 1, KV_WIDTH), per_seq),
           pl.BlockSpec((1, 1, KV_WIDTH), per_seq),
           pl.BlockSpec(kvw_new_t.shape, lambda n, pt: (0, 0)),
           pl.BlockSpec((1, 8, 3), per_seq)],
        out_specs=(pl.BlockSpec((1, 8, 128), per_seq),
                   pl.BlockSpec((1, KV_WIDTH, win_buf), per_seq)),
        scratch_shapes=[pltpu.VMEM((256, past_len), BF), pltpu.VMEM((KV_HALF, past_len), BF)],
    )
    return pl.pallas_call(
        functools.partial(_sattn_kernel, n_pages=n_pages, past_len=past_len),
        out_shape=(jax.ShapeDtypeStruct((n_seq, 8, 128), F32),
                   jax.ShapeDtypeStruct((n_seq, KV_WIDTH, win_buf), F32)),
        grid_spec=grid_spec,
        compiler_params=_cparams("arbitrary"),
        name="attention_sample",
    )(page_table_flat, *([cache_t] * n_pages), qblk, kvc_pad, win_t, kvs_new, kvw_new, kvw_new_t, gates8)


def _kv_rows_view(kv_t):
    n, _, t = kv_t.shape
    return jnp.transpose(kv_t.reshape(n, 2, N_KV_HEADS, HEAD_DIM, t), (0, 4, 1, 2, 3))


def _kv_feat_view(kv):
    n, t = kv.shape[:2]
    return jnp.transpose(kv, (0, 2, 3, 4, 1)).reshape(n, KV_WIDTH, t)


def kernel(x_prompt, x_sample, cache_cmp_kv, cache_sel_kv, state_win_kv, state_pool, page_table, c_prompt, c_sample, w_ada, b_ada, g_pre_mix, g_post_mix, g_pre_mlp, g_post_mlp, w_in, w_cmp, pos_cmp, w_pool, pool_scale, w_up_nsa, w_up_pool, w_o, w_ff1, w_ff2):
    n_batch, seq, _ = x_prompt.shape
    n_seq = x_sample.shape[0]
    n_pages = page_table.shape[1]
    past_len = n_pages * PAGE_SIZE
    nb_past = past_len // BLOCK
    assert x_sample.shape[1] == 1 and w_ada.shape[0] == 1 and past_len % BLOCK == 0
    assert seq % TOKEN_TILE == 0 and seq // BLOCK == 128 and state_win_kv.shape[2] == WINDOW
    assert (n_seq * nb_past) % CMP_GROUP == 0 and n_seq == CMP_GROUP

    w_t = w_in[0].T
    wt = jnp.pad(w_t[0:1304], ((0, 8), (0, 0))).astype(BF)
    wn = jnp.concatenate([w_t[768:896], w_t[1024:1152], w_t[1304:1816], w_t[512:768]], axis=0).T.astype(BF)
    ws = jnp.pad(w_t[0:1816], ((0, 104), (0, 0))).T.astype(BF)
    wgm = w_t[1816:3864].T.astype(BF)
    eye = jnp.eye(N_KV_HEADS, dtype=F32)
    w4 = jnp.einsum('sjde,hk->sjhdke', w_cmp[0], eye).reshape(2, CMP_SPLIT, CMP_K // CMP_SPLIT, KV_HALF).astype(BF)
    pos4 = jnp.broadcast_to(jnp.transpose(pos_cmp[0], (1, 0, 2))[:, :, None, :],
                            (2, BLOCK, N_KV_HEADS, HEAD_DIM)).reshape(2, CMP_SPLIT, 1, CMP_K // CMP_SPLIT)
    wp = w_pool[0].astype(BF)
    ps = pool_scale[0].reshape(1, POOL_WIDTH)
    wun, wup, wo = w_up_nsa[0].astype(BF), w_up_pool[0].astype(BF), w_o[0].astype(BF)
    w1, w2 = w_ff1[0].astype(BF), w_ff2[0].astype(BF)
    gpm, gqm = g_pre_mix[0].reshape(1, D_MODEL), g_post_mix[0].reshape(1, D_MODEL)
    gpf, gqf = g_pre_mlp[0].reshape(1, D_MODEL), g_post_mlp[0].reshape(1, D_MODEL)

    n_c = n_batch + n_seq
    c_all = jnp.pad(jnp.concatenate([c_prompt, c_sample], axis=0), ((0, (-n_c) % 8), (0, 0)))
    ada = _adaln(c_all, w_ada[0], b_ada[0].reshape(1, -1))
    ada_p = ada[:n_batch].reshape(n_batch, 6, 1, D_MODEL)
    ada_s = ada[n_batch:n_c].reshape(n_seq, 6, D_MODEL)
    mods_p = [ada_p[:, k] for k in range(6)]
    mods_s = [ada_s[:, k][None] for k in range(6)]

    xp = x_prompt.reshape(n_batch * seq, D_MODEL)
    kvcT, kvsT, kvwT, kvc_a, kvc_b, u_p, qT, kaug, vsT, kw, vwT, gT = _project_prompt(
        xp, mods_p[0], mods_p[1], gpm, wn, wt, n_batch, seq)
    kvc_blk = _compress(kvc_a, kvc_b, pos4, w4).reshape(n_batch, seq // BLOCK, KV_WIDTH)
    kc = kvc_blk[:, :, 0:128].astype(BF)
    vcT = jnp.swapaxes(kvc_blk[:, :, 128:256], 1, 2).astype(BF)
    onsa_p = _attention_prompt(qT, kaug, vsT, kw, vwT, kc, vcT, gT, n_batch, seq)
    opool_p = _pool_prompt(u_p, wp, ps, n_batch, seq)
    x1_p = _token_call(_merge_kernel, "merge_prompt", xp, [mods_p[0], mods_p[1], mods_p[2]], TOKEN_TILE, seq,
                       [onsa_p.reshape(n_batch * seq, NSA_WIDTH), opool_p], [gpm, gqm, wgm, wun, wup, wo])
    y_p = _token_call(_mlp_kernel, "mlp_prompt", x1_p, [mods_p[3], mods_p[4], mods_p[5]], 256, seq,
                      [], [gpf, gqf, w1, w2])

    xs = x_sample.reshape(n_seq, D_MODEL)
    z, zsig, zT = _project_sample(xs, mods_s[0][0], mods_s[1][0], gpm, ws, wt)
    q_s = z[:, 0:512] * (HEAD_DIM ** -0.5)
    kvc_n, kvs_n, kvw_n = z[:, 512:768], z[:, 768:1024], z[:, 1024:1280]
    gates_s = zsig[:, 1280:1304].reshape(n_seq, N_HEADS, 3)
    u_s = z[:, 1304:1816]
    q5 = q_s.reshape(n_seq, N_KV_HEADS, GROUP, 1, HEAD_DIM) * eye[None, :, None, :, None]
    qblk = q5.reshape(n_seq, N_HEADS, KV_HALF)

    pt_flat = page_table.reshape(-1)
    kvc_past = _compress_paged(_kv_feat_view(cache_cmp_kv[0]), pt_flat, pos4, w4, n_seq, n_pages)
    last_a = jnp.pad(kvc_n[:, None, 0:128], ((0, 0), (0, BLOCK - 1), (0, 0))).reshape(n_seq * BLOCK, KV_HALF)
    last_b = jnp.pad(kvc_n[:, None, 128:256], ((0, 0), (0, BLOCK - 1), (0, 0))).reshape(n_seq * BLOCK, KV_HALF)
    kvc_last = _compress(last_a, last_b, pos4, w4)
    kvc_s = jnp.concatenate([kvc_past.reshape(n_seq, nb_past, KV_WIDTH), kvc_last[:, None, :]], axis=1)
    kvc_pad = jnp.pad(kvc_s, ((0, 0), (0, 128 - nb_past - 1), (0, 0)))
    o8, new_win_t = _attention_sample(
        pt_flat, _kv_feat_view(cache_sel_kv[0]), qblk, kvc_pad, _kv_feat_view(state_win_kv[0]),
        kvs_n[:, None, :], kvw_n[:, None, :], zT[1024:1280], gates_s, n_pages, past_len)
    o5 = o8.reshape(n_seq, N_KV_HEADS, GROUP, N_KV_HEADS, HEAD_DIM)
    onsa_s = jnp.concatenate([o5[:, 0, :, 0, :], o5[:, 1, :, 1, :]], axis=1).reshape(n_seq, NSA_WIDTH).astype(BF)
    opool_s = _pool_sample(u_s, jnp.swapaxes(state_pool[0], 0, 1), wp, ps)
    x1_s = _token_call(_merge_kernel, "merge_sample", xs, [mods_s[0], mods_s[1], mods_s[2]], n_seq, n_seq,
                       [onsa_s, opool_s], [gpm, gqm, wgm, wun, wup, wo])
    y_s = _token_call(_mlp_kernel, "mlp_sample", x1_s, [mods_s[3], mods_s[4], mods_s[5]], n_seq, n_seq,
                      [], [gpf, gqf, w1, w2])

    win_p = min(WINDOW, seq)
    new_kv_s = lambda rows: _kv_rows_view(rows.reshape(1, KV_WIDTH, n_seq))[0][None, :, None]
    return (
        y_p.reshape(n_batch, seq, D_MODEL),
        y_s.reshape(n_seq, 1, D_MODEL),
        _kv_rows_view(kvcT)[None],
        _kv_rows_view(kvsT)[None],
        _kv_rows_view(kvwT[:, :, seq - win_p:])[None],
        u_p.reshape(n_batch, seq, POOL_WIDTH)[None, :, seq - POOL_HIST:],
        new_kv_s(zT[512:768]),
        new_kv_s(zT[768:1024]),
        _kv_rows_view(new_win_t)[None],
        jnp.concatenate([state_pool[0][:, 1:], u_s[:, None, :]], axis=1)[None],
    )
```

```python
import functools
import math

import jax
import jax.numpy as jnp
from jax import lax
from jax.experimental import pallas as pl
from jax.experimental.pallas import tpu as pltpu

D_MODEL = 1024
N_HEADS = 8
HEAD_DIM = 64
N_KV_HEADS = 2
GROUP = N_HEADS // N_KV_HEADS
BLOCK = 64
N_SELECT = 16
WINDOW = 512
Q_TILE = 128
NSA_WIDTH = N_HEADS * HEAD_DIM
KV_WIDTH = 2 * N_KV_HEADS * HEAD_DIM
KV_HALF = N_KV_HEADS * HEAD_DIM
FORCE_SCORE = 16.0
POOL_WINDOWS = (2, 4, 8, 16)
POOL_WIDTH = 512
POOL_GROUP_DIM = 128
POOL_HIST = 15
D_FF = 4 * D_MODEL
EPS = 1e-6
PAGE_SIZE = 128
CMP_K = BLOCK * KV_HALF
CMP_SPLIT = 4
CMP_GROUP = 128
CMP_PITCH = BLOCK + 4

BF = jnp.bfloat16
F32 = jnp.float32
I32 = jnp.int32
MASK_BIG = 2.0 ** 100
NEG_INF = float("-inf")
LOG2E = math.log2(math.e)

TOKEN_TILE = 512
SEL_CHUNK = 512
SAMPLE_SEQ_STEP = 4
VMEM_LIMIT = 56 * 1024 * 1024


def _cparams(*sem):
    return pltpu.CompilerParams(dimension_semantics=sem, vmem_limit_bytes=VMEM_LIMIT)


def _rms(x, g):
    return x * lax.rsqrt(jnp.mean(x * x, axis=-1, keepdims=True) + EPS) * g


def _dot(a, b):
    return jnp.dot(a, b, preferred_element_type=F32)


def _dot_nt(a, b):
    return lax.dot_general(a, b, (((1,), (1,)), ((), ())), preferred_element_type=F32)


def _ada_kernel(c_ref, w_ref, b_ref, o_ref):
    c = c_ref[...]
    a = (c * jax.nn.sigmoid(c)).astype(BF)
    o_ref[...] = _dot(a, w_ref[...].astype(BF)) + b_ref[...]


def _adaln(c_all, w_ada, b_ada):
    rows = c_all.shape[0]
    n_out = w_ada.shape[1]
    tn = 512
    return pl.pallas_call(
        _ada_kernel,
        out_shape=jax.ShapeDtypeStruct((rows, n_out), F32),
        grid=(n_out // tn,),
        in_specs=[pl.BlockSpec((rows, D_MODEL), lambda j: (0, 0)),
                  pl.BlockSpec((D_MODEL, tn), lambda j: (0, j)),
                  pl.BlockSpec((1, tn), lambda j: (0, j))],
        out_specs=pl.BlockSpec((rows, tn), lambda j: (0, j)),
        compiler_params=_cparams("arbitrary"),
        name="adaln",
    )(c_all, w_ada, b_ada)


def _proj_kernel(x_ref, sh_ref, sc_ref, g_ref, wn_ref, wt_ref,
                 kvcT_ref, kvsT_ref, kvwT_ref, ca_ref, cb_ref, u_ref,
                 qT_ref, kaug_ref, vsT_ref, kw_ref, vwT_ref, gT_ref, *, tm, tpb):
    x = x_ref[...]
    h = _rms(x, g_ref[...]) * (1.0 + sc_ref[0]) + sh_ref[0]
    hb = h.astype(BF)
    zn = _dot(hb, wn_ref[...])
    zt = _dot_nt(wt_ref[...], hb)
    kvcT_ref[0] = zt[512:768]
    kvsT_ref[0] = zt[768:1024]
    kvwT_ref[0] = zt[1024:1280]
    ca_ref[...] = zn[:, 768:896]
    cb_ref[...] = zn[:, 896:1024]
    u_ref[...] = zn[:, 256:768]
    qT_ref[0] = (zt[0:512] * (HEAD_DIM ** -0.5 * LOG2E)).astype(BF)
    vsT_ref[0, 0] = zt[896:1024].astype(BF)
    for c in range(tm // Q_TILE):
        vwT_ref[0, c] = zt[1152:1280, c * Q_TILE:(c + 1) * Q_TILE].astype(BF)
    gT_ref[0] = jax.nn.sigmoid(zt[1280:1312])
    t0 = (pl.program_id(0) % tpb) * tm
    blk = jnp.right_shift(t0 + lax.broadcasted_iota(I32, (tm, 128), 0), 6)
    lane = lax.broadcasted_iota(I32, (tm, 128), 1)
    kaug_ref[0, :, 0:128] = zn[:, 0:128].astype(BF)
    kaug_ref[0, :, 128:256] = jnp.where(blk == lane, MASK_BIG, 0.0).astype(BF)
    kw_ref[0] = zn[:, 128:256].astype(BF)


def _project_prompt(x2, shift, scale, g, wn, wt, n_batch, seq):
    tm = TOKEN_TILE
    tpb = seq // tm
    nt = n_batch * seq
    tok = lambda t: (t, 0)
    per_b = lambda t: (t // tpb, 0, 0)
    featT = lambda t: (t // tpb, 0, t % tpb)
    rows3 = lambda t: (t // tpb, t % tpb, 0)
    rows4 = lambda t: (t // tpb, t % tpb, 0, 0)
    kvT = jax.ShapeDtypeStruct((n_batch, KV_WIDTH, seq), F32)
    out_shape = (
        kvT, kvT, kvT,
        jax.ShapeDtypeStruct((nt, KV_HALF), F32),
        jax.ShapeDtypeStruct((nt, KV_HALF), F32),
        jax.ShapeDtypeStruct((nt, POOL_WIDTH), F32),
        jax.ShapeDtypeStruct((n_batch, NSA_WIDTH, seq), BF),
        jax.ShapeDtypeStruct((n_batch, seq, 256), BF),
        jax.ShapeDtypeStruct((n_batch, seq // SEL_CHUNK, KV_HALF, SEL_CHUNK), BF),
        jax.ShapeDtypeStruct((n_batch, seq, KV_HALF), BF),
        jax.ShapeDtypeStruct((n_batch, seq // Q_TILE, KV_HALF, Q_TILE), BF),
        jax.ShapeDtypeStruct((n_batch, 32, seq), F32),
    )
    out_specs = (
        pl.BlockSpec((1, KV_WIDTH, tm), featT),
        pl.BlockSpec((1, KV_WIDTH, tm), featT),
        pl.BlockSpec((1, KV_WIDTH, tm), featT),
        pl.BlockSpec((tm, KV_HALF), tok),
        pl.BlockSpec((tm, KV_HALF), tok),
        pl.BlockSpec((tm, POOL_WIDTH), tok),
        pl.BlockSpec((1, NSA_WIDTH, tm), featT),
        pl.BlockSpec((1, tm, 256), rows3),
        pl.BlockSpec((1, 1, KV_HALF, SEL_CHUNK), rows4),
        pl.BlockSpec((1, tm, KV_HALF), rows3),
        pl.BlockSpec((1, tm // Q_TILE, KV_HALF, Q_TILE), rows4),
        pl.BlockSpec((1, 32, tm), featT),
    )
    return pl.pallas_call(
        functools.partial(_proj_kernel, tm=tm, tpb=tpb),
        out_shape=out_shape,
        grid=(nt // tm,),
        in_specs=[pl.BlockSpec((tm, D_MODEL), tok),
                  pl.BlockSpec((1, 1, D_MODEL), per_b),
                  pl.BlockSpec((1, 1, D_MODEL), per_b),
                  pl.BlockSpec((1, D_MODEL), lambda t: (0, 0)),
                  pl.BlockSpec(wn.shape, lambda t: (0, 0)),
                  pl.BlockSpec(wt.shape, lambda t: (0, 0))],
        out_specs=out_specs,
        compiler_params=_cparams("arbitrary"),
        name="project_prompt",
    )(x2, shift, scale, g, wn, wt)


def _compress_rows(src_refs, pos_ref, w_ref, o_ref, n_blocks, pitch):
    rows_per_slice = BLOCK // CMP_SPLIT
    for s in range(2):
        acc = jnp.zeros((n_blocks, KV_HALF), F32)
        for c in range(CMP_SPLIT):
            xc = jnp.concatenate(
                [src_refs[s][pl.ds(c * rows_per_slice + r, n_blocks, stride=pitch), :]
                 for r in range(rows_per_slice)], axis=1) + pos_ref[s, c]
            acc = acc + _dot(xc.astype(BF), w_ref[s, c])
        o_ref[:, s * KV_HALF:(s + 1) * KV_HALF] = acc


def _cmp_kernel(xa_ref, xb_ref, pos_ref, w_ref, o_ref):
    _compress_rows((xa_ref, xb_ref), pos_ref, w_ref, o_ref, CMP_GROUP, BLOCK)


def _compress(xa, xb, pos4, w4):
    m = xa.shape[0] // BLOCK
    rows = CMP_GROUP * BLOCK
    return pl.pallas_call(
        _cmp_kernel,
        out_shape=jax.ShapeDtypeStruct((m, KV_WIDTH), F32),
        grid=(m // CMP_GROUP,),
        in_specs=[pl.BlockSpec((rows, KV_HALF), lambda i: (i, 0)),
                  pl.BlockSpec((rows, KV_HALF), lambda i: (i, 0)),
                  pl.BlockSpec(pos4.shape, lambda i: (0, 0, 0, 0)),
                  pl.BlockSpec(w4.shape, lambda i: (0, 0, 0, 0))],
        out_specs=pl.BlockSpec((CMP_GROUP, KV_WIDTH), lambda i: (i, 0)),
        compiler_params=_cparams("arbitrary"),
        name="compress_blocks",
    )(xa, xb, pos4, w4)


def _scmp_kernel(pt_ref, *refs, n_pages_step):
    pages = refs[:n_pages_step]
    pos_ref, w_ref, o_ref, sa_ref, sb_ref = refs[n_pages_step:]
    blocks_per_page = PAGE_SIZE // BLOCK
    for k in range(n_pages_step):
        pg = pages[k][0]
        for s, dst in enumerate((sa_ref, sb_ref)):
            rows = pg[s * KV_HALF:(s + 1) * KV_HALF, :].T
            for b in range(blocks_per_page):
                m = k * blocks_per_page + b
                dst[m * CMP_PITCH:m * CMP_PITCH + BLOCK, :] = rows[b * BLOCK:(b + 1) * BLOCK, :]
    _compress_rows((sa_ref, sb_ref), pos_ref, w_ref, o_ref, CMP_GROUP, CMP_PITCH)


def _compress_paged(cache_t, page_table_flat, pos4, w4, n_seq, n_pages):
    blocks_per_page = PAGE_SIZE // BLOCK
    n_pages_step = CMP_GROUP // blocks_per_page

    def page_map(k):
        return lambda s, pt: (pt[s * n_pages_step + k], 0, 0)

    grid_spec = pltpu.PrefetchScalarGridSpec(
        num_scalar_prefetch=1,
        grid=(n_seq * n_pages // n_pages_step,),
        in_specs=[pl.BlockSpec((1, KV_WIDTH, PAGE_SIZE), page_map(k)) for k in range(n_pages_step)]
        + [pl.BlockSpec(pos4.shape, lambda s, pt: (0, 0, 0, 0)),
           pl.BlockSpec(w4.shape, lambda s, pt: (0, 0, 0, 0))],
        out_specs=pl.BlockSpec((CMP_GROUP, KV_WIDTH), lambda s, pt: (s, 0)),
        scratch_shapes=[pltpu.VMEM((CMP_GROUP * CMP_PITCH, KV_HALF), F32),
                        pltpu.VMEM((CMP_GROUP * CMP_PITCH, KV_HALF), F32)],
    )
    return pl.pallas_call(
        functools.partial(_scmp_kernel, n_pages_step=n_pages_step),
        out_shape=jax.ShapeDtypeStruct((n_seq * n_pages * blocks_per_page, KV_WIDTH), F32),
        grid_spec=grid_spec,
        compiler_params=_cparams("arbitrary"),
        name="compress_paged",
    )(page_table_flat, *([cache_t] * n_pages_step), pos4, w4)


def _attn_kernel(qT_ref, kaug_ref, vsT_ref, kw_ref, vwT_ref, kc_ref, vcT_ref, gT_ref, o_ref,
                 qa_ref, sa_ref, sb_ref, m_ref, l_ref, acc_ref, outT_ref):
    i = pl.program_id(1)
    q0 = i * Q_TILE
    n_full = jnp.right_shift(i, 2)
    row = lax.broadcasted_iota(I32, (128, 128), 0)
    tok = lax.broadcasted_iota(I32, (128, 128), 1)
    qpos = q0 + tok
    cur = jnp.right_shift(qpos, 6)
    cmask = (row + 1) * BLOCK - 1 <= qpos
    valid = row <= cur
    forced = ((row == 0) | (row == cur) | (row == cur - 1)).astype(F32)
    key_c = lax.broadcasted_iota(I32, (SEL_CHUNK, Q_TILE), 0)
    qpos_c = q0 + lax.broadcasted_iota(I32, (SEL_CHUNK, Q_TILE), 1)
    hs = [slice(h * HEAD_DIM, (h + 1) * HEAD_DIM) for h in range(N_KV_HEADS)]

    def bias4(keep):
        b = jnp.where(keep, 0.0, -MASK_BIG)
        return jnp.concatenate([b] * GROUP, axis=1)

    def online_step(st, s, v_t):
        m_old = m_ref[st]
        m_new = jnp.maximum(m_old, jnp.max(s, axis=0, keepdims=True))
        alpha = jnp.exp2(m_old - m_new)
        p = jnp.exp2(s - m_new)
        l_ref[st] = alpha * l_ref[st] + jnp.sum(p, axis=0, keepdims=True)
        acc_ref[st] = alpha * acc_ref[st] + _dot(v_t, p.astype(BF))
        m_ref[st] = m_new

    m_ref[...] = jnp.full(m_ref.shape, NEG_INF, F32)
    l_ref[...] = jnp.zeros(l_ref.shape, F32)
    acc_ref[...] = jnp.zeros(acc_ref.shape, F32)

    o_c = []
    scores = []
    for h in range(N_KV_HEADS):
        qa_ref[h] = jnp.zeros(qa_ref.shape[1:], BF)
        for g in range(GROUP):
            r0 = h * GROUP * HEAD_DIM + g * HEAD_DIM
            qa_ref[h, hs[h], g * Q_TILE:(g + 1) * Q_TILE] = qT_ref[0, r0:r0 + HEAD_DIM, :]
        sc = _dot(kc_ref[0], qa_ref[h, 0:128, :])
        imp = jnp.zeros((128, Q_TILE), F32)
        p_parts = []
        for g in range(GROUP):
            s = jnp.where(cmask, sc[:, g * Q_TILE:(g + 1) * Q_TILE], NEG_INF)
            mx = jnp.max(s, axis=0, keepdims=True)
            mx = jnp.where(mx > NEG_INF, mx, 0.0)
            e = jnp.where(cmask, jnp.exp2(s - mx), 0.0)
            p = e / jnp.maximum(jnp.sum(e, axis=0, keepdims=True), 1e-30)
            imp = imp + p
            p_parts.append(p)
        o_c.append(_dot(vcT_ref[0, hs[h], :], jnp.concatenate(p_parts, axis=1).astype(BF)))
        scores.append(jnp.where(valid, imp + FORCE_SCORE * forced, -1.0))

    blk_f = lax.broadcasted_iota(I32, (128, 2 * Q_TILE), 0).astype(F32)

    def pick_body(k, work):
        best = jnp.max(work, axis=0, keepdims=True)
        first = jnp.min(jnp.where(work == best, blk_f, 128.0), axis=0, keepdims=True)
        return jnp.where((blk_f == first) & (best >= 0.0), -2.0, work)

    work = lax.fori_loop(0, N_SELECT, pick_body, jnp.concatenate(scores, axis=1))
    selm1 = jnp.where(work == -2.0, 0.0, -1.0).astype(BF)
    for h in range(N_KV_HEADS):
        for g in range(GROUP):
            qa_ref[h, 128:256, g * Q_TILE:(g + 1) * Q_TILE] = selm1[:, h * Q_TILE:(h + 1) * Q_TILE]

    def sel_scores(j, dst_ref):
        kt = kaug_ref[0, pl.ds(pl.multiple_of(j * SEL_CHUNK, SEL_CHUNK), SEL_CHUNK), :]
        for h in range(N_KV_HEADS):
            dst_ref[h] = _dot(kt, qa_ref[h])

    def sel_process(src_ref, j, causal):
        if causal:
            cbias = bias4(j * SEL_CHUNK + key_c <= qpos_c)
        for h in range(N_KV_HEADS):
            s = src_ref[h]
            if causal:
                s = s + cbias
            online_step(h, s, vsT_ref[0, j, hs[h], :])

    sel_scores(0, sa_ref)

    def pair_body(t, carry):
        j = 2 * t
        sel_scores(j + 1, sb_ref)
        sel_process(sa_ref, j, False)
        sel_scores(j + 2, sa_ref)
        sel_process(sb_ref, j + 1, False)
        return carry

    n_pairs = jnp.right_shift(n_full, 1)
    lax.fori_loop(0, n_pairs, pair_body, 0)
    j_last = 2 * n_pairs

    @pl.when(jnp.bitwise_and(n_full, 1) == 1)
    def _():
        sel_scores(j_last + 1, sb_ref)
        sel_process(sa_ref, j_last, False)
        sel_process(sb_ref, j_last + 1, True)

    @pl.when(jnp.bitwise_and(n_full, 1) == 0)
    def _():
        sel_process(sa_ref, j_last, True)

    j0 = jnp.maximum(i - WINDOW // Q_TILE, 0)
    w0 = j0 * Q_TILE
    delta = qpos_c - (w0 + key_c)
    wbias = bias4((delta >= 0) & (delta < WINDOW))
    dbias = bias4((row <= tok) & (i >= WINDOW // Q_TILE))
    kw_a = kw_ref[0, pl.ds(pl.multiple_of(w0, Q_TILE), WINDOW), :]
    kw_b = kw_ref[0, pl.ds(pl.multiple_of(q0, Q_TILE), Q_TILE), :]
    for h in range(N_KV_HEADS):
        qf = qa_ref[h, 0:128, :]
        v_t = jnp.concatenate([vwT_ref[0, j0 + c, hs[h], :] for c in range(WINDOW // Q_TILE)], axis=1)
        online_step(2 + h, _dot(kw_a, qf) + wbias, v_t)
        online_step(2 + h, _dot(kw_b, qf) + dbias, vwT_ref[0, i, hs[h], :])

    for h in range(N_KV_HEADS):
        o_s = acc_ref[h] / l_ref[h]
        o_w = acc_ref[2 + h] / l_ref[2 + h]
        for g in range(GROUP):
            gs = slice(g * Q_TILE, (g + 1) * Q_TILE)
            gr = h * GROUP * 3 + g * 3
            og = (gT_ref[0, gr:gr + 1, :] * o_c[h][:, gs] + gT_ref[0, gr + 1:gr + 2, :] * o_s[:, gs]
                  + gT_ref[0, gr + 2:gr + 3, :] * o_w[:, gs])
            r0 = h * GROUP * HEAD_DIM + g * HEAD_DIM
            outT_ref[r0:r0 + HEAD_DIM, :] = og

    o_ref[0] = outT_ref[...].T.astype(BF)


def _attention_prompt(qT, kaug, vsT, kw, vwT, kc, vcT, gT, n_batch, seq):
    per_b3 = lambda n, i: (n, 0, 0)
    per_b4 = lambda n, i: (n, 0, 0, 0)
    rows = GROUP * Q_TILE
    return pl.pallas_call(
        _attn_kernel,
        out_shape=jax.ShapeDtypeStruct((n_batch, seq, NSA_WIDTH), BF),
        grid=(n_batch, seq // Q_TILE),
        in_specs=[pl.BlockSpec((1, NSA_WIDTH, Q_TILE), lambda n, i: (n, 0, i)),
                  pl.BlockSpec((1, seq, 256), per_b3),
                  pl.BlockSpec((1, seq // SEL_CHUNK, KV_HALF, SEL_CHUNK), per_b4),
                  pl.BlockSpec((1, seq, KV_HALF), per_b3),
                  pl.BlockSpec((1, seq // Q_TILE, KV_HALF, Q_TILE), per_b4),
                  pl.BlockSpec((1, 128, KV_HALF), per_b3),
                  pl.BlockSpec((1, KV_HALF, 128), per_b3),
                  pl.BlockSpec((1, 32, Q_TILE), lambda n, i: (n, 0, i))],
        out_specs=pl.BlockSpec((1, Q_TILE, NSA_WIDTH), lambda n, i: (n, i, 0)),
        scratch_shapes=[pltpu.VMEM((N_KV_HEADS, 256, rows), BF),
                        pltpu.VMEM((N_KV_HEADS, SEL_CHUNK, rows), F32),
                        pltpu.VMEM((N_KV_HEADS, SEL_CHUNK, rows), F32),
                        pltpu.VMEM((2 * N_KV_HEADS, 1, rows), F32),
                        pltpu.VMEM((2 * N_KV_HEADS, 1, rows), F32),
                        pltpu.VMEM((2 * N_KV_HEADS, HEAD_DIM, rows), F32),
                        pltpu.VMEM((NSA_WIDTH, Q_TILE), F32)],
        compiler_params=_cparams("arbitrary", "arbitrary"),
        name="attention_prompt",
    )(qT, kaug, vsT, kw, vwT, kc, vcT, gT)


def _pool_kernel(u_ref, halo_ref, wp_ref, ps_ref, o_ref, ext_ref, *, tm, tpb):
    t = pl.program_id(0) % tpb
    ext_ref[0:16, :] = jnp.where(t == 0, 0.0, halo_ref[...])
    u = u_ref[...]
    ext_ref[16:16 + tm, :] = u
    pos = t * tm + lax.broadcasted_iota(I32, (tm, 1), 0)
    outs = []
    for gi, w in enumerate(POOL_WINDOWS):
        cs = slice(gi * POOL_GROUP_DIM, (gi + 1) * POOL_GROUP_DIM)
        acc = u[:, cs]
        for k in range(1, w):
            acc = acc + ext_ref[pl.ds(16 - k, tm), cs]
        cnt = jnp.minimum(pos + 1, w).astype(F32)
        pooled = acc / cnt - u[:, cs]
        outs.append(_dot(pooled.astype(BF), wp_ref[gi]))
    o_ref[...] = (jnp.concatenate(outs, axis=1) * ps_ref[...]).astype(BF)


def _pool_prompt(u, w_pool, pool_scale, n_batch, seq):
    tm = TOKEN_TILE
    tpb = seq // tm
    nt = n_batch * seq
    return pl.pallas_call(
        functools.partial(_pool_kernel, tm=tm, tpb=tpb),
        out_shape=jax.ShapeDtypeStruct((nt, POOL_WIDTH), BF),
        grid=(nt // tm,),
        in_specs=[pl.BlockSpec((tm, POOL_WIDTH), lambda t: (t, 0)),
                  pl.BlockSpec((16, POOL_WIDTH), lambda t: (jnp.maximum(t * (tm // 16) - 1, 0), 0)),
                  pl.BlockSpec(w_pool.shape, lambda t: (0, 0, 0)),
                  pl.BlockSpec((1, POOL_WIDTH), lambda t: (0, 0))],
        out_specs=pl.BlockSpec((tm, POOL_WIDTH), lambda t: (t, 0)),
        scratch_shapes=[pltpu.VMEM((tm + 16, POOL_WIDTH), F32)],
        compiler_params=_cparams("arbitrary"),
        name="pool_prompt",
    )(u, u, w_pool, pool_scale)


def _spool_kernel(u_ref, hist_ref, wp_ref, ps_ref, o_ref):
    u = u_ref[...]
    outs = []
    for gi, w in enumerate(POOL_WINDOWS):
        cs = slice(gi * POOL_GROUP_DIM, (gi + 1) * POOL_GROUP_DIM)
        acc = u[:, cs]
        for k in range(1, w):
            acc = acc + hist_ref[POOL_HIST - k, :, cs]
        pooled = acc / float(w) - u[:, cs]
        outs.append(_dot(pooled.astype(BF), wp_ref[gi]))
    o_ref[...] = (jnp.concatenate(outs, axis=1) * ps_ref[...]).astype(BF)


def _pool_sample(u, hist_t, w_pool, pool_scale):
    n = u.shape[0]
    return pl.pallas_call(
        _spool_kernel,
        out_shape=jax.ShapeDtypeStruct((n, POOL_WIDTH), BF),
        name="pool_sample",
    )(u, hist_t, w_pool, pool_scale)


def _merge_kernel(x_ref, sh_ref, sc_ref, gate_ref, gpre_ref, gpost_ref, onsa_ref, opool_ref,
                  wgm_ref, wun_ref, wup_ref, wo_ref, o_ref):
    x = x_ref[...]
    h = _rms(x, gpre_ref[...]) * (1.0 + sc_ref[0]) + sh_ref[0]
    gm = jax.nn.sigmoid(_dot(h.astype(BF), wgm_ref[...]))
    m = (gm[:, :D_MODEL] * _dot(onsa_ref[...], wun_ref[...])
         + gm[:, D_MODEL:] * _dot(opool_ref[...], wup_ref[...]))
    m = _dot(m.astype(BF), wo_ref[...])
    o_ref[...] = x + gate_ref[0] * _rms(m, gpost_ref[...])


def _mlp_kernel(x_ref, sh_ref, sc_ref, gate_ref, gpre_ref, gpost_ref, w1_ref, w2_ref, o_ref):
    x = x_ref[...]
    h = _rms(x, gpre_ref[...]) * (1.0 + sc_ref[0]) + sh_ref[0]
    hb = h.astype(BF)
    f = jnp.zeros(x.shape, F32)
    fc = 1024
    for c in range(D_FF // fc):
        a = jnp.maximum(_dot(hb, w1_ref[:, c * fc:(c + 1) * fc]), 0.0)
        f = f + _dot((a * a).astype(BF), w2_ref[c * fc:(c + 1) * fc, :])
    o_ref[...] = x + gate_ref[0] * _rms(f, gpost_ref[...])


def _token_call(kernel, name, x2, mods, tm, rows_per_mod, extra_tok, consts):
    nt = x2.shape[0]
    r = mods[0].shape[1]
    mod_spec = pl.BlockSpec((1, r, D_MODEL), lambda t: ((t * tm) // rows_per_mod, 0, 0))
    in_specs = [pl.BlockSpec((tm, D_MODEL), lambda t: (t, 0))] + [mod_spec] * len(mods)
    in_specs += [pl.BlockSpec((1, D_MODEL), lambda t: (0, 0))] * 2
    in_specs += [pl.BlockSpec((tm, a.shape[1]), lambda t: (t, 0)) for a in extra_tok]
    in_specs += [pl.BlockSpec(w.shape, lambda t: (0, 0), pipeline_mode=pl.Buffered(1)) for w in consts[2:]]
    return pl.pallas_call(
        kernel,
        out_shape=jax.ShapeDtypeStruct((nt, D_MODEL), F32),
        grid=(nt // tm,),
        in_specs=in_specs,
        out_specs=pl.BlockSpec((tm, D_MODEL), lambda t: (t, 0)),
        compiler_params=_cparams("arbitrary"),
        name=name,
    )(x2, *mods, consts[0], consts[1], *extra_tok, *consts[2:])


def _sproj_kernel(x_ref, sh_ref, sc_ref, g_ref, wn_ref, wt_ref, z_ref, zs_ref, zT_ref):
    x = x_ref[...]
    h = _rms(x, g_ref[...]) * (1.0 + sc_ref[...]) + sh_ref[...]
    hb = h.astype(BF)
    z = _dot(hb, wn_ref[...])
    z_ref[...] = z
    zs_ref[...] = jax.nn.sigmoid(z)
    zT_ref[...] = _dot_nt(wt_ref[...], hb)


def _project_sample(x2, shift, scale, g, wn, wt):
    n = x2.shape[0]
    shp = jax.ShapeDtypeStruct((n, wn.shape[1]), F32)
    return pl.pallas_call(
        _sproj_kernel,
        out_shape=(shp, shp, jax.ShapeDtypeStruct((wt.shape[0], n), F32)),
        name="project_sample",
    )(x2, shift, scale, g, wn, wt)


def _sattn_kernel(pt_ref, *refs, n_pages, past_len, seq_step):
    pages = refs[:seq_step * n_pages]
    (qb_ref, kvc_ref, win_ref, ksn_ref, kwn_ref, kwnT_ref, g_ref,
     o_ref, nwin_ref, kaug_ref, vall_ref) = refs[seq_step * n_pages:]

    @pl.when(pl.program_id(0) == 0)
    def _():
        blk = lax.broadcasted_iota(I32, (128, past_len), 0)
        key_blk = jnp.right_shift(lax.broadcasted_iota(I32, (128, past_len), 1), 6)
        onehot = jnp.where(blk == key_blk, MASK_BIG, 0.0).astype(BF)
        for q in range(seq_step):
            kaug_ref[q, 128:256, :] = onehot

    for q in range(seq_step):
        _sattn_one(q, pl.program_id(0) * seq_step + q, pages[q * n_pages:(q + 1) * n_pages],
                   qb_ref, kvc_ref, win_ref, ksn_ref, kwn_ref, kwnT_ref, g_ref, o_ref, nwin_ref, kaug_ref, vall_ref,
                   n_pages, past_len)


def _sattn_one(q, n, pages, qb_ref, kvc_ref, win_ref, ksn_ref, kwn_ref, kwnT_ref, g_ref, o_ref, nwin_ref,
               kaug_ref, vall_ref, n_pages, past_len):
    nb_past = past_len // BLOCK
    win_buf = win_ref.shape[2]
    for p in range(n_pages):
        pg = pages[p][0]
        kaug_ref[q, 0:128, p * PAGE_SIZE:(p + 1) * PAGE_SIZE] = pg[0:KV_HALF, :].astype(BF)
        vall_ref[q, :, p * PAGE_SIZE:(p + 1) * PAGE_SIZE] = pg[KV_HALF:KV_WIDTH, :].astype(BF)

    qb = qb_ref[q].astype(BF)
    qf = qb.astype(F32)
    row8 = lax.broadcasted_iota(I32, (8, 128), 0)
    lane8 = lax.broadcasted_iota(I32, (8, 128), 1)

    def new_key_score(knew):
        return jnp.sum(qf * knew.astype(BF).astype(F32), axis=1, keepdims=True)

    kvc = kvc_ref[q]
    s_c = _dot_nt(qb, kvc[:, 0:128].astype(BF))
    cm = lane8 < nb_past
    s_c = jnp.where(cm, s_c, NEG_INF)
    mx = jnp.max(s_c, axis=1, keepdims=True)
    mx = jnp.where(mx > NEG_INF, mx, 0.0)
    e = jnp.where(cm, jnp.exp(s_c - mx), 0.0)
    p_c = e / jnp.maximum(jnp.sum(e, axis=1, keepdims=True), 1e-30)
    o_c = _dot(p_c.astype(BF), kvc[:, 128:256].astype(BF))

    top = row8 < GROUP
    imp_a = jnp.sum(jnp.where(top, p_c, 0.0), axis=0, keepdims=True)
    imp_b = jnp.sum(jnp.where(top, 0.0, p_c), axis=0, keepdims=True)
    imp = jnp.where(top, imp_a, imp_b)
    cur = nb_past
    forced = (lane8 == 0) | (lane8 == cur) | (lane8 == cur - 1)
    score = jnp.where(lane8 <= cur, imp + FORCE_SCORE * forced.astype(F32), -1.0)
    cnt = jnp.zeros((8, 128), F32)
    for bp in range(nb_past + 1):
        other = score[:, bp:bp + 1]
        ahead = (other > score) | ((other == score) & (bp < lane8))
        cnt = cnt + jnp.where(ahead, 1.0, 0.0)
    sel = (cnt < float(N_SELECT)) & (score >= 0.0)
    selm1 = jnp.where(sel, 0.0, -1.0).astype(BF)

    qaug = jnp.concatenate([qb, selm1], axis=1)
    s_s = _dot(qaug, kaug_ref[q])
    ksn = ksn_ref[q]
    s_n = new_key_score(ksn[:, 0:128])
    m_s = jnp.maximum(jnp.max(s_s, axis=1, keepdims=True), s_n)
    e_s = jnp.exp(s_s - m_s)
    e_n = jnp.exp(s_n - m_s)
    l_s = jnp.sum(e_s, axis=1, keepdims=True) + e_n
    v_n = ksn[:, 128:256].astype(BF).astype(F32)
    o_s = (_dot_nt(e_s.astype(BF), vall_ref[q]) + e_n.astype(BF).astype(F32) * v_n) / l_s

    win = win_ref[q]
    s_w = _dot(qb, win[0:KV_HALF, :].astype(BF))
    lane_w = lax.broadcasted_iota(I32, (8, win_buf), 1)
    s_w = jnp.where(lane_w >= win_buf + 1 - WINDOW, s_w, NEG_INF)
    kwn = kwn_ref[q]
    s_n = new_key_score(kwn[:, 0:128])
    m_w = jnp.maximum(jnp.max(s_w, axis=1, keepdims=True), s_n)
    e_w = jnp.exp(s_w - m_w)
    e_n = jnp.exp(s_n - m_w)
    l_w = jnp.sum(e_w, axis=1, keepdims=True) + e_n
    v_n = kwn[:, 128:256].astype(BF).astype(F32)
    o_w = (_dot_nt(e_w.astype(BF), win[KV_HALF:KV_WIDTH, :].astype(BF)) + e_n.astype(BF).astype(F32) * v_n) / l_w

    g = g_ref[q]
    o_ref[q] = g[:, 0:1] * o_c + g[:, 1:2] * o_s + g[:, 2:3] * o_w

    seq_lane = lax.broadcasted_iota(I32, kwnT_ref.shape, 1)
    new_col = jnp.sum(jnp.where(seq_lane == n, kwnT_ref[...], 0.0), axis=1, keepdims=True)
    shifted = pltpu.roll(win, win_buf - 1, axis=1)
    row_lane = lax.broadcasted_iota(I32, (KV_WIDTH, win_buf), 1)
    nwin_ref[q] = jnp.where(row_lane == win_buf - 1, new_col, shifted)


def _attention_sample(page_table_flat, cache_t, qblk, kvc_pad, win_t, kvs_new, kvw_new, kvw_new_t, gates8,
                      n_pages, past_len):
    n_seq = qblk.shape[0]
    win_buf = win_t.shape[2]
    g = SAMPLE_SEQ_STEP
    per_step = lambda n, pt: (n, 0, 0)

    def page_map(k):
        return lambda n, pt: (pt[n * g * n_pages + k], 0, 0)

    grid_spec = pltpu.PrefetchScalarGridSpec(
        num_scalar_prefetch=1,
        grid=(n_seq // g,),
        in_specs=[pl.BlockSpec((1, KV_WIDTH, PAGE_SIZE), page_map(k)) for k in range(g * n_pages)]
        + [pl.BlockSpec((g, 8, 128), per_step),
           pl.BlockSpec((g, 128, KV_WIDTH), per_step),
           pl.BlockSpec((g, KV_WIDTH, win_buf), per_step),
           pl.BlockSpec((g, 1, KV_WIDTH), per_step),
           pl.BlockSpec((g, 1, KV_WIDTH), per_step),
           pl.BlockSpec(kvw_new_t.shape, lambda n, pt: (0, 0)),
           pl.BlockSpec((g, 8, 3), per_step)],
        out_specs=(pl.BlockSpec((g, 8, 128), per_step),
                   pl.BlockSpec((g, KV_WIDTH, win_buf), per_step)),
        scratch_shapes=[pltpu.VMEM((g, 256, past_len), BF), pltpu.VMEM((g, KV_HALF, past_len), BF)],
    )
    return pl.pallas_call(
        functools.partial(_sattn_kernel, n_pages=n_pages, past_len=past_len, seq_step=g),
        out_shape=(jax.ShapeDtypeStruct((n_seq, 8, 128), F32),
                   jax.ShapeDtypeStruct((n_seq, KV_WIDTH, win_buf), F32)),
        grid_spec=grid_spec,
        compiler_params=_cparams("arbitrary"),
        name="attention_sample",
    )(page_table_flat, *([cache_t] * (g * n_pages)), qblk, kvc_pad, win_t, kvs_new, kvw_new, kvw_new_t, gates8)


def _kv_rows_view(kv_t):
    n, _, t = kv_t.shape
    return jnp.transpose(kv_t.reshape(n, 2, N_KV_HEADS, HEAD_DIM, t), (0, 4, 1, 2, 3))


def _kv_feat_view(kv):
    n, t = kv.shape[:2]
    return jnp.transpose(kv, (0, 2, 3, 4, 1)).reshape(n, KV_WIDTH, t)


def kernel(x_prompt, x_sample, cache_cmp_kv, cache_sel_kv, state_win_kv, state_pool, page_table, c_prompt, c_sample, w_ada, b_ada, g_pre_mix, g_post_mix, g_pre_mlp, g_post_mlp, w_in, w_cmp, pos_cmp, w_pool, pool_scale, w_up_nsa, w_up_pool, w_o, w_ff1, w_ff2):
    n_batch, seq, _ = x_prompt.shape
    n_seq = x_sample.shape[0]
    n_pages = page_table.shape[1]
    past_len = n_pages * PAGE_SIZE
    nb_past = past_len // BLOCK
    assert x_sample.shape[1] == 1 and w_ada.shape[0] == 1 and past_len % BLOCK == 0
    assert seq % TOKEN_TILE == 0 and seq // BLOCK == 128 and state_win_kv.shape[2] == WINDOW
    assert (n_seq * nb_past) % CMP_GROUP == 0 and n_seq == CMP_GROUP

    w_t = w_in[0].T
    wt = jnp.pad(w_t[0:1304], ((0, 8), (0, 0))).astype(BF)
    wn = jnp.concatenate([w_t[768:896], w_t[1024:1152], w_t[1304:1816], w_t[512:768]], axis=0).T.astype(BF)
    ws = jnp.pad(w_t[0:1816], ((0, 104), (0, 0))).T.astype(BF)
    wgm = w_t[1816:3864].T.astype(BF)
    eye = jnp.eye(N_KV_HEADS, dtype=F32)
    w4 = jnp.einsum('sjde,hk->sjhdke', w_cmp[0], eye).reshape(2, CMP_SPLIT, CMP_K // CMP_SPLIT, KV_HALF).astype(BF)
    pos4 = jnp.broadcast_to(jnp.transpose(pos_cmp[0], (1, 0, 2))[:, :, None, :],
                            (2, BLOCK, N_KV_HEADS, HEAD_DIM)).reshape(2, CMP_SPLIT, 1, CMP_K // CMP_SPLIT)
    wp = w_pool[0].astype(BF)
    ps = pool_scale[0].reshape(1, POOL_WIDTH)
    wun, wup, wo = w_up_nsa[0].astype(BF), w_up_pool[0].astype(BF), w_o[0].astype(BF)
    w1, w2 = w_ff1[0].astype(BF), w_ff2[0].astype(BF)
    gpm, gqm = g_pre_mix[0].reshape(1, D_MODEL), g_post_mix[0].reshape(1, D_MODEL)
    gpf, gqf = g_pre_mlp[0].reshape(1, D_MODEL), g_post_mlp[0].reshape(1, D_MODEL)

    n_c = n_batch + n_seq
    c_all = jnp.pad(jnp.concatenate([c_prompt, c_sample], axis=0), ((0, (-n_c) % 8), (0, 0)))
    ada = _adaln(c_all, w_ada[0], b_ada[0].reshape(1, -1))
    ada_p = ada[:n_batch].reshape(n_batch, 6, 1, D_MODEL)
    ada_s = ada[n_batch:n_c].reshape(n_seq, 6, D_MODEL)
    mods_p = [ada_p[:, k] for k in range(6)]
    mods_s = [ada_s[:, k][None] for k in range(6)]

    xp = x_prompt.reshape(n_batch * seq, D_MODEL)
    kvcT, kvsT, kvwT, kvc_a, kvc_b, u_p, qT, kaug, vsT, kw, vwT, gT = _project_prompt(
        xp, mods_p[0], mods_p[1], gpm, wn, wt, n_batch, seq)
    kvc_blk = _compress(kvc_a, kvc_b, pos4, w4).reshape(n_batch, seq // BLOCK, KV_WIDTH)
    kc = kvc_blk[:, :, 0:128].astype(BF)
    vcT = jnp.swapaxes(kvc_blk[:, :, 128:256], 1, 2).astype(BF)
    onsa_p = _attention_prompt(qT, kaug, vsT, kw, vwT, kc, vcT, gT, n_batch, seq)
    opool_p = _pool_prompt(u_p, wp, ps, n_batch, seq)
    x1_p = _token_call(_merge_kernel, "merge_prompt", xp, [mods_p[0], mods_p[1], mods_p[2]], TOKEN_TILE, seq,
                       [onsa_p.reshape(n_batch * seq, NSA_WIDTH), opool_p], [gpm, gqm, wgm, wun, wup, wo])
    y_p = _token_call(_mlp_kernel, "mlp_prompt", x1_p, [mods_p[3], mods_p[4], mods_p[5]], 256, seq,
                      [], [gpf, gqf, w1, w2])

    xs = x_sample.reshape(n_seq, D_MODEL)
    z, zsig, zT = _project_sample(xs, mods_s[0][0], mods_s[1][0], gpm, ws, wt)
    q_s = z[:, 0:512] * (HEAD_DIM ** -0.5)
    kvc_n, kvs_n, kvw_n = z[:, 512:768], z[:, 768:1024], z[:, 1024:1280]
    gates_s = zsig[:, 1280:1304].reshape(n_seq, N_HEADS, 3)
    u_s = z[:, 1304:1816]
    q5 = q_s.reshape(n_seq, N_KV_HEADS, GROUP, 1, HEAD_DIM) * eye[None, :, None, :, None]
    qblk = q5.reshape(n_seq, N_HEADS, KV_HALF)

    pt_flat = page_table.reshape(-1)
    kvc_past = _compress_paged(_kv_feat_view(cache_cmp_kv[0]), pt_flat, pos4, w4, n_seq, n_pages)
    last_a = jnp.pad(kvc_n[:, None, 0:128], ((0, 0), (0, BLOCK - 1), (0, 0))).reshape(n_seq * BLOCK, KV_HALF)
    last_b = jnp.pad(kvc_n[:, None, 128:256], ((0, 0), (0, BLOCK - 1), (0, 0))).reshape(n_seq * BLOCK, KV_HALF)
    kvc_last = _compress(last_a, last_b, pos4, w4)
    kvc_s = jnp.concatenate([kvc_past.reshape(n_seq, nb_past, KV_WIDTH), kvc_last[:, None, :]], axis=1)
    kvc_pad = jnp.pad(kvc_s, ((0, 0), (0, 128 - nb_past - 1), (0, 0)))
    o8, new_win_t = _attention_sample(
        pt_flat, _kv_feat_view(cache_sel_kv[0]), qblk, kvc_pad, _kv_feat_view(state_win_kv[0]),
        kvs_n[:, None, :], kvw_n[:, None, :], zT[1024:1280], gates_s, n_pages, past_len)
    o5 = o8.reshape(n_seq, N_KV_HEADS, GROUP, N_KV_HEADS, HEAD_DIM)
    onsa_s = jnp.concatenate([o5[:, 0, :, 0, :], o5[:, 1, :, 1, :]], axis=1).reshape(n_seq, NSA_WIDTH).astype(BF)
    opool_s = _pool_sample(u_s, jnp.swapaxes(state_pool[0], 0, 1), wp, ps)
    x1_s = _token_call(_merge_kernel, "merge_sample", xs, [mods_s[0], mods_s[1], mods_s[2]], n_seq, n_seq,
                       [onsa_s, opool_s], [gpm, gqm, wgm, wun, wup, wo])
    y_s = _token_call(_mlp_kernel, "mlp_sample", x1_s, [mods_s[3], mods_s[4], mods_s[5]], n_seq, n_seq,
                      [], [gpf, gqf, w1, w2])

    win_p = min(WINDOW, seq)
    new_kv_s = lambda rows: _kv_rows_view(rows.reshape(1, KV_WIDTH, n_seq))[0][None, :, None]
    return (
        y_p.reshape(n_batch, seq, D_MODEL),
        y_s.reshape(n_seq, 1, D_MODEL),
        _kv_rows_view(kvcT)[None],
        _kv_rows_view(kvsT)[None],
        _kv_rows_view(kvwT[:, :, seq - win_p:])[None],
        u_p.reshape(n_batch, seq, POOL_WIDTH)[None, :, seq - POOL_HIST:],
        new_kv_s(zT[512:768]),
        new_kv_s(zT[768:1024]),
        _kv_rows_view(new_win_t)[None],
        jnp.concatenate([state_pool[0][:, 1:], u_s[:, None, :]], axis=1)[None],
    )
```

```python
import functools
import math

import jax
import jax.numpy as jnp
from jax import lax
from jax.experimental import pallas as pl
from jax.experimental.pallas import tpu as pltpu

D_MODEL = 1024
N_HEADS = 8
HEAD_DIM = 64
N_KV_HEADS = 2
GROUP = N_HEADS // N_KV_HEADS
BLOCK = 64
N_SELECT = 16
WINDOW = 512
Q_TILE = 128
NSA_WIDTH = N_HEADS * HEAD_DIM
KV_WIDTH = 2 * N_KV_HEADS * HEAD_DIM
KV_HALF = N_KV_HEADS * HEAD_DIM
FORCE_SCORE = 16.0
N_FORCED = 3
POOL_WINDOWS = (2, 4, 8, 16)
POOL_WIDTH = 512
POOL_GROUP_DIM = 128
POOL_HIST = 15
D_FF = 4 * D_MODEL
EPS = 1e-6
PAGE_SIZE = 128
V_ROWS = HEAD_DIM + 16
CMP_K = BLOCK * KV_HALF
CMP_SPLIT = 4
CMP_GROUP = 128
CMP_PITCH = BLOCK + 4

BF = jnp.bfloat16
F32 = jnp.float32
I32 = jnp.int32
MASK_BIG = 2.0 ** 100
NEG_INF = float("-inf")
LOG2E = math.log2(math.e)

TOKEN_TILE = 512
SEL_CHUNK = 512
SAMPLE_SEQ_STEP = 4
VMEM_LIMIT = 56 * 1024 * 1024


def _cparams(*sem):
    return pltpu.CompilerParams(dimension_semantics=sem, vmem_limit_bytes=VMEM_LIMIT)


def _rms(x, g):
    return x * lax.rsqrt(jnp.mean(x * x, axis=-1, keepdims=True) + EPS) * g


def _dot(a, b):
    return jnp.dot(a, b, preferred_element_type=F32)


def _dot_nt(a, b):
    return lax.dot_general(a, b, (((1,), (1,)), ((), ())), preferred_element_type=F32)


def _ada_kernel(c_ref, w_ref, b_ref, o_ref):
    c = c_ref[...]
    a = (c * jax.nn.sigmoid(c)).astype(BF)
    o_ref[...] = _dot(a, w_ref[...].astype(BF)) + b_ref[...]


def _adaln(c_all, w_ada, b_ada):
    rows = c_all.shape[0]
    n_out = w_ada.shape[1]
    tn = 512
    return pl.pallas_call(
        _ada_kernel,
        out_shape=jax.ShapeDtypeStruct((rows, n_out), F32),
        grid=(n_out // tn,),
        in_specs=[pl.BlockSpec((rows, D_MODEL), lambda j: (0, 0)),
                  pl.BlockSpec((D_MODEL, tn), lambda j: (0, j)),
                  pl.BlockSpec((1, tn), lambda j: (0, j))],
        out_specs=pl.BlockSpec((rows, tn), lambda j: (0, j)),
        compiler_params=_cparams("arbitrary"),
        name="adaln",
    )(c_all, w_ada, b_ada)


def _proj_kernel(x_ref, sh_ref, sc_ref, g_ref, wn_ref, wt_ref,
                 kvcT_ref, kvsT_ref, kvwT_ref, ca_ref, cb_ref, u_ref,
                 qT_ref, kaug_ref, vsT_ref, kw_ref, vwT_ref, gT_ref, *, tm, tpb):
    x = x_ref[...]
    h = _rms(x, g_ref[...]) * (1.0 + sc_ref[0]) + sh_ref[0]
    hb = h.astype(BF)
    zn = _dot(hb, wn_ref[...])
    zt = _dot_nt(wt_ref[...], hb)
    kvcT_ref[0] = zt[512:768]
    kvsT_ref[0] = zt[768:1024]
    kvwT_ref[0] = zt[1024:1280]
    ca_ref[...] = zn[:, 768:896]
    cb_ref[...] = zn[:, 896:1024]
    u_ref[...] = zn[:, 256:768]
    qT_ref[0] = (zt[0:512] * (HEAD_DIM ** -0.5 * LOG2E)).astype(BF)
    ones = jnp.ones((V_ROWS - HEAD_DIM, tm), BF)
    for h in range(N_KV_HEADS):
        vs_h = zt[896 + h * HEAD_DIM:896 + (h + 1) * HEAD_DIM].astype(BF)
        vw_h = zt[1152 + h * HEAD_DIM:1152 + (h + 1) * HEAD_DIM].astype(BF)
        vsT_ref[0, 0, h, 0:HEAD_DIM, :] = vs_h
        vsT_ref[0, 0, h, HEAD_DIM:V_ROWS, :] = ones
        for c in range(tm // Q_TILE):
            cs = slice(c * Q_TILE, (c + 1) * Q_TILE)
            vwT_ref[0, c, h, 0:HEAD_DIM, :] = vw_h[:, cs]
            vwT_ref[0, c, h, HEAD_DIM:V_ROWS, :] = ones[:, cs]
    gT_ref[0] = jax.nn.sigmoid(zt[1280:1312])
    t0 = (pl.program_id(0) % tpb) * tm
    blk = jnp.right_shift(t0 + lax.broadcasted_iota(I32, (tm, 128), 0), 6)
    lane = lax.broadcasted_iota(I32, (tm, 128), 1)
    kaug_ref[0, :, 0:128] = zn[:, 0:128].astype(BF)
    kaug_ref[0, :, 128:256] = jnp.where(blk == lane, MASK_BIG, 0.0).astype(BF)
    kw_ref[0] = zn[:, 128:256].astype(BF)


def _project_prompt(x2, shift, scale, g, wn, wt, n_batch, seq):
    tm = TOKEN_TILE
    tpb = seq // tm
    nt = n_batch * seq
    tok = lambda t: (t, 0)
    per_b = lambda t: (t // tpb, 0, 0)
    featT = lambda t: (t // tpb, 0, t % tpb)
    rows3 = lambda t: (t // tpb, t % tpb, 0)
    rows5 = lambda t: (t // tpb, t % tpb, 0, 0, 0)
    kvT = jax.ShapeDtypeStruct((n_batch, KV_WIDTH, seq), F32)
    out_shape = (
        kvT, kvT, kvT,
        jax.ShapeDtypeStruct((nt, KV_HALF), F32),
        jax.ShapeDtypeStruct((nt, KV_HALF), F32),
        jax.ShapeDtypeStruct((nt, POOL_WIDTH), F32),
        jax.ShapeDtypeStruct((n_batch, NSA_WIDTH, seq), BF),
        jax.ShapeDtypeStruct((n_batch, seq, 256), BF),
        jax.ShapeDtypeStruct((n_batch, seq // SEL_CHUNK, N_KV_HEADS, V_ROWS, SEL_CHUNK), BF),
        jax.ShapeDtypeStruct((n_batch, seq, KV_HALF), BF),
        jax.ShapeDtypeStruct((n_batch, seq // Q_TILE, N_KV_HEADS, V_ROWS, Q_TILE), BF),
        jax.ShapeDtypeStruct((n_batch, 32, seq), F32),
    )
    out_specs = (
        pl.BlockSpec((1, KV_WIDTH, tm), featT),
        pl.BlockSpec((1, KV_WIDTH, tm), featT),
        pl.BlockSpec((1, KV_WIDTH, tm), featT),
        pl.BlockSpec((tm, KV_HALF), tok),
        pl.BlockSpec((tm, KV_HALF), tok),
        pl.BlockSpec((tm, POOL_WIDTH), tok),
        pl.BlockSpec((1, NSA_WIDTH, tm), featT),
        pl.BlockSpec((1, tm, 256), rows3),
        pl.BlockSpec((1, 1, N_KV_HEADS, V_ROWS, SEL_CHUNK), rows5),
        pl.BlockSpec((1, tm, KV_HALF), rows3),
        pl.BlockSpec((1, tm // Q_TILE, N_KV_HEADS, V_ROWS, Q_TILE), rows5),
        pl.BlockSpec((1, 32, tm), featT),
    )
    return pl.pallas_call(
        functools.partial(_proj_kernel, tm=tm, tpb=tpb),
        out_shape=out_shape,
        grid=(nt // tm,),
        in_specs=[pl.BlockSpec((tm, D_MODEL), tok),
                  pl.BlockSpec((1, 1, D_MODEL), per_b),
                  pl.BlockSpec((1, 1, D_MODEL), per_b),
                  pl.BlockSpec((1, D_MODEL), lambda t: (0, 0)),
                  pl.BlockSpec(wn.shape, lambda t: (0, 0)),
                  pl.BlockSpec(wt.shape, lambda t: (0, 0))],
        out_specs=out_specs,
        compiler_params=_cparams("arbitrary"),
        name="project_prompt",
    )(x2, shift, scale, g, wn, wt)


def _compress_rows(src_refs, pos_ref, w_ref, o_ref, n_blocks, pitch):
    rows_per_slice = BLOCK // CMP_SPLIT
    for s in range(2):
        acc = jnp.zeros((n_blocks, KV_HALF), F32)
        for c in range(CMP_SPLIT):
            xc = jnp.concatenate(
                [src_refs[s][pl.ds(c * rows_per_slice + r, n_blocks, stride=pitch), :]
                 for r in range(rows_per_slice)], axis=1) + pos_ref[s, c]
            acc = acc + _dot(xc.astype(BF), w_ref[s, c])
        o_ref[:, s * KV_HALF:(s + 1) * KV_HALF] = acc


def _cmp_kernel(xa_ref, xb_ref, pos_ref, w_ref, o_ref):
    _compress_rows((xa_ref, xb_ref), pos_ref, w_ref, o_ref, CMP_GROUP, BLOCK)


def _compress(xa, xb, pos4, w4):
    m = xa.shape[0] // BLOCK
    rows = CMP_GROUP * BLOCK
    return pl.pallas_call(
        _cmp_kernel,
        out_shape=jax.ShapeDtypeStruct((m, KV_WIDTH), F32),
        grid=(m // CMP_GROUP,),
        in_specs=[pl.BlockSpec((rows, KV_HALF), lambda i: (i, 0)),
                  pl.BlockSpec((rows, KV_HALF), lambda i: (i, 0)),
                  pl.BlockSpec(pos4.shape, lambda i: (0, 0, 0, 0)),
                  pl.BlockSpec(w4.shape, lambda i: (0, 0, 0, 0))],
        out_specs=pl.BlockSpec((CMP_GROUP, KV_WIDTH), lambda i: (i, 0)),
        compiler_params=_cparams("arbitrary"),
        name="compress_blocks",
    )(xa, xb, pos4, w4)


def _scmp_kernel(pt_ref, *refs, n_pages_step):
    pages = refs[:n_pages_step]
    pos_ref, w_ref, o_ref, sa_ref, sb_ref = refs[n_pages_step:]
    blocks_per_page = PAGE_SIZE // BLOCK
    for k in range(n_pages_step):
        pg = pages[k][0]
        for s, dst in enumerate((sa_ref, sb_ref)):
            rows = pg[s * KV_HALF:(s + 1) * KV_HALF, :].T
            for b in range(blocks_per_page):
                m = k * blocks_per_page + b
                dst[m * CMP_PITCH:m * CMP_PITCH + BLOCK, :] = rows[b * BLOCK:(b + 1) * BLOCK, :]
    _compress_rows((sa_ref, sb_ref), pos_ref, w_ref, o_ref, CMP_GROUP, CMP_PITCH)


def _compress_paged(cache_t, page_table_flat, pos4, w4, n_seq, n_pages):
    blocks_per_page = PAGE_SIZE // BLOCK
    n_pages_step = CMP_GROUP // blocks_per_page

    def page_map(k):
        return lambda s, pt: (pt[s * n_pages_step + k], 0, 0)

    grid_spec = pltpu.PrefetchScalarGridSpec(
        num_scalar_prefetch=1,
        grid=(n_seq * n_pages // n_pages_step,),
        in_specs=[pl.BlockSpec((1, KV_WIDTH, PAGE_SIZE), page_map(k)) for k in range(n_pages_step)]
        + [pl.BlockSpec(pos4.shape, lambda s, pt: (0, 0, 0, 0)),
           pl.BlockSpec(w4.shape, lambda s, pt: (0, 0, 0, 0))],
        out_specs=pl.BlockSpec((CMP_GROUP, KV_WIDTH), lambda s, pt: (s, 0)),
        scratch_shapes=[pltpu.VMEM((CMP_GROUP * CMP_PITCH, KV_HALF), F32),
                        pltpu.VMEM((CMP_GROUP * CMP_PITCH, KV_HALF), F32)],
    )
    return pl.pallas_call(
        functools.partial(_scmp_kernel, n_pages_step=n_pages_step),
        out_shape=jax.ShapeDtypeStruct((n_seq * n_pages * blocks_per_page, KV_WIDTH), F32),
        grid_spec=grid_spec,
        compiler_params=_cparams("arbitrary"),
        name="compress_paged",
    )(page_table_flat, *([cache_t] * n_pages_step), pos4, w4)


def _attn_kernel(qT_ref, kaug_ref, vsT_ref, kw_ref, vwT_ref, kc_ref, vcT_ref, gT_ref, o_ref,
                 qa_ref, sa_ref, sb_ref, m_ref, acc_ref, outT_ref):
    i = pl.program_id(1)
    q0 = i * Q_TILE
    n_full = jnp.right_shift(i, 2)
    row = lax.broadcasted_iota(I32, (128, 128), 0)
    tok = lax.broadcasted_iota(I32, (128, 128), 1)
    qpos = q0 + tok
    cur = jnp.right_shift(qpos, 6)
    cmask = (row + 1) * BLOCK - 1 <= qpos
    valid = row <= cur
    forced = (row == 0) | (row == cur) | (row == cur - 1)
    key_c = lax.broadcasted_iota(I32, (SEL_CHUNK, Q_TILE), 0)
    qpos_c = q0 + lax.broadcasted_iota(I32, (SEL_CHUNK, Q_TILE), 1)
    hs = [slice(h * HEAD_DIM, (h + 1) * HEAD_DIM) for h in range(N_KV_HEADS)]

    def bias4(keep):
        b = jnp.where(keep, 0.0, -MASK_BIG)
        return jnp.concatenate([b] * GROUP, axis=1)

    def online_step(st, s, v_t):
        m_old = m_ref[st]
        m_new = jnp.maximum(m_old, jnp.max(s, axis=0, keepdims=True))
        alpha = jnp.exp2(m_old - m_new)
        p = jnp.exp2(s - m_new)
        acc_ref[st] = alpha * acc_ref[st] + _dot(v_t, p.astype(BF))
        m_ref[st] = m_new

    m_ref[...] = jnp.full(m_ref.shape, NEG_INF, F32)
    acc_ref[...] = jnp.zeros(acc_ref.shape, F32)

    o_c = []
    scores = []
    for h in range(N_KV_HEADS):
        qa_ref[h] = jnp.zeros(qa_ref.shape[1:], BF)
        for g in range(GROUP):
            r0 = h * GROUP * HEAD_DIM + g * HEAD_DIM
            qa_ref[h, hs[h], g * Q_TILE:(g + 1) * Q_TILE] = qT_ref[0, r0:r0 + HEAD_DIM, :]
        sc = _dot(kc_ref[0], qa_ref[h, 0:128, :])
        imp = jnp.zeros((128, Q_TILE), F32)
        p_parts = []
        for g in range(GROUP):
            s = jnp.where(cmask, sc[:, g * Q_TILE:(g + 1) * Q_TILE], NEG_INF)
            mx = jnp.max(s, axis=0, keepdims=True)
            mx = jnp.where(mx > NEG_INF, mx, 0.0)
            e = jnp.where(cmask, jnp.exp2(s - mx), 0.0)
            p = e / jnp.maximum(jnp.sum(e, axis=0, keepdims=True), 1e-30)
            imp = imp + p
            p_parts.append(p)
        o_c.append(_dot(vcT_ref[0, hs[h], :], jnp.concatenate(p_parts, axis=1).astype(BF)))
        scores.append(jnp.where(valid, jnp.where(forced, -2.0, imp), -1.0))

    blk_f = lax.broadcasted_iota(I32, (128, 2 * Q_TILE), 0).astype(F32)

    def pick_body(k, work):
        best = jnp.max(work, axis=0, keepdims=True)
        first = jnp.min(jnp.where(work == best, blk_f, 128.0), axis=0, keepdims=True)
        return jnp.where((blk_f == first) & (best >= 0.0), -2.0, work)

    work = lax.fori_loop(0, N_SELECT - N_FORCED, pick_body, jnp.concatenate(scores, axis=1))
    selm1 = jnp.where(work == -2.0, 0.0, -1.0).astype(BF)
    for h in range(N_KV_HEADS):
        for g in range(GROUP):
            qa_ref[h, 128:256, g * Q_TILE:(g + 1) * Q_TILE] = selm1[:, h * Q_TILE:(h + 1) * Q_TILE]

    def sel_scores(j, dst_ref):
        kt = kaug_ref[0, pl.ds(pl.multiple_of(j * SEL_CHUNK, SEL_CHUNK), SEL_CHUNK), :]
        for h in range(N_KV_HEADS):
            dst_ref[h] = _dot(kt, qa_ref[h])

    def sel_process(src_ref, j, causal):
        if causal:
            cbias = bias4(j * SEL_CHUNK + key_c <= qpos_c)
        for h in range(N_KV_HEADS):
            s = src_ref[h]
            if causal:
                s = s + cbias
            online_step(h, s, vsT_ref[0, j, h])

    sel_scores(0, sa_ref)

    def pair_body(t, carry):
        j = 2 * t
        sel_scores(j + 1, sb_ref)
        sel_process(sa_ref, j, False)
        sel_scores(j + 2, sa_ref)
        sel_process(sb_ref, j + 1, False)
        return carry

    n_pairs = jnp.right_shift(n_full, 1)
    lax.fori_loop(0, n_pairs, pair_body, 0)
    j_last = 2 * n_pairs

    @pl.when(jnp.bitwise_and(n_full, 1) == 1)
    def _():
        sel_scores(j_last + 1, sb_ref)
        sel_process(sa_ref, j_last, False)
        sel_process(sb_ref, j_last + 1, True)

    @pl.when(jnp.bitwise_and(n_full, 1) == 0)
    def _():
        sel_process(sa_ref, j_last, True)

    j0 = jnp.maximum(i - WINDOW // Q_TILE, 0)
    w0 = j0 * Q_TILE
    delta = qpos_c - (w0 + key_c)
    wbias = bias4((delta >= 0) & (delta < WINDOW))
    dbias = bias4((row <= tok) & (i >= WINDOW // Q_TILE))
    kw_a = kw_ref[0, pl.ds(pl.multiple_of(w0, Q_TILE), WINDOW), :]
    kw_b = kw_ref[0, pl.ds(pl.multiple_of(q0, Q_TILE), Q_TILE), :]
    for h in range(N_KV_HEADS):
        qf = qa_ref[h, 0:128, :]
        v_t = jnp.concatenate([vwT_ref[0, j0 + c, h] for c in range(WINDOW // Q_TILE)], axis=1)
        online_step(2 + h, _dot(kw_a, qf) + wbias, v_t)
        online_step(2 + h, _dot(kw_b, qf) + dbias, vwT_ref[0, i, h])

    for h in range(N_KV_HEADS):
        o_s = acc_ref[h, 0:HEAD_DIM, :] / acc_ref[h, HEAD_DIM:HEAD_DIM + 1, :]
        o_w = acc_ref[2 + h, 0:HEAD_DIM, :] / acc_ref[2 + h, HEAD_DIM:HEAD_DIM + 1, :]
        for g in range(GROUP):
            gs = slice(g * Q_TILE, (g + 1) * Q_TILE)
            gr = h * GROUP * 3 + g * 3
            og = (gT_ref[0, gr:gr + 1, :] * o_c[h][:, gs] + gT_ref[0, gr + 1:gr + 2, :] * o_s[:, gs]
                  + gT_ref[0, gr + 2:gr + 3, :] * o_w[:, gs])
            r0 = h * GROUP * HEAD_DIM + g * HEAD_DIM
            outT_ref[r0:r0 + HEAD_DIM, :] = og

    o_ref[0] = outT_ref[...].T.astype(BF)


def _attention_prompt(qT, kaug, vsT, kw, vwT, kc, vcT, gT, n_batch, seq):
    per_b3 = lambda n, i: (n, 0, 0)
    per_b5 = lambda n, i: (n, 0, 0, 0, 0)
    rows = GROUP * Q_TILE
    return pl.pallas_call(
        _attn_kernel,
        out_shape=jax.ShapeDtypeStruct((n_batch, seq, NSA_WIDTH), BF),
        grid=(n_batch, seq // Q_TILE),
        in_specs=[pl.BlockSpec((1, NSA_WIDTH, Q_TILE), lambda n, i: (n, 0, i)),
                  pl.BlockSpec((1, seq, 256), per_b3),
                  pl.BlockSpec((1, seq // SEL_CHUNK, N_KV_HEADS, V_ROWS, SEL_CHUNK), per_b5),
                  pl.BlockSpec((1, seq, KV_HALF), per_b3),
                  pl.BlockSpec((1, seq // Q_TILE, N_KV_HEADS, V_ROWS, Q_TILE), per_b5),
                  pl.BlockSpec((1, 128, KV_HALF), per_b3),
                  pl.BlockSpec((1, KV_HALF, 128), per_b3),
                  pl.BlockSpec((1, 32, Q_TILE), lambda n, i: (n, 0, i))],
        out_specs=pl.BlockSpec((1, Q_TILE, NSA_WIDTH), lambda n, i: (n, i, 0)),
        scratch_shapes=[pltpu.VMEM((N_KV_HEADS, 256, rows), BF),
                        pltpu.VMEM((N_KV_HEADS, SEL_CHUNK, rows), F32),
                        pltpu.VMEM((N_KV_HEADS, SEL_CHUNK, rows), F32),
                        pltpu.VMEM((2 * N_KV_HEADS, 1, rows), F32),
                        pltpu.VMEM((2 * N_KV_HEADS, V_ROWS, rows), F32),
                        pltpu.VMEM((NSA_WIDTH, Q_TILE), F32)],
        compiler_params=_cparams("arbitrary", "arbitrary"),
        name="attention_prompt",
    )(qT, kaug, vsT, kw, vwT, kc, vcT, gT)


def _pool_kernel(u_ref, halo_ref, wp_ref, ps_ref, o_ref, ext_ref, *, tm, tpb):
    t = pl.program_id(0) % tpb
    ext_ref[0:16, :] = jnp.where(t == 0, 0.0, halo_ref[...])
    u = u_ref[...]
    ext_ref[16:16 + tm, :] = u
    pos = t * tm + lax.broadcasted_iota(I32, (tm, 1), 0)
    outs = []
    for gi, w in enumerate(POOL_WINDOWS):
        cs = slice(gi * POOL_GROUP_DIM, (gi + 1) * POOL_GROUP_DIM)
        acc = u[:, cs]
        for k in range(1, w):
            acc = acc + ext_ref[pl.ds(16 - k, tm), cs]
        cnt = jnp.minimum(pos + 1, w).astype(F32)
        pooled = acc / cnt - u[:, cs]
        outs.append(_dot(pooled.astype(BF), wp_ref[gi]))
    o_ref[...] = (jnp.concatenate(outs, axis=1) * ps_ref[...]).astype(BF)


def _pool_prompt(u, w_pool, pool_scale, n_batch, seq):
    tm = TOKEN_TILE
    tpb = seq // tm
    nt = n_batch * seq
    return pl.pallas_call(
        functools.partial(_pool_kernel, tm=tm, tpb=tpb),
        out_shape=jax.ShapeDtypeStruct((nt, POOL_WIDTH), BF),
        grid=(nt // tm,),
        in_specs=[pl.BlockSpec((tm, POOL_WIDTH), lambda t: (t, 0)),
                  pl.BlockSpec((16, POOL_WIDTH), lambda t: (jnp.maximum(t * (tm // 16) - 1, 0), 0)),
                  pl.BlockSpec(w_pool.shape, lambda t: (0, 0, 0)),
                  pl.BlockSpec((1, POOL_WIDTH), lambda t: (0, 0))],
        out_specs=pl.BlockSpec((tm, POOL_WIDTH), lambda t: (t, 0)),
        scratch_shapes=[pltpu.VMEM((tm + 16, POOL_WIDTH), F32)],
        compiler_params=_cparams("arbitrary"),
        name="pool_prompt",
    )(u, u, w_pool, pool_scale)


def _spool_kernel(u_ref, hist_ref, wp_ref, ps_ref, o_ref):
    u = u_ref[...]
    outs = []
    for gi, w in enumerate(POOL_WINDOWS):
        cs = slice(gi * POOL_GROUP_DIM, (gi + 1) * POOL_GROUP_DIM)
        acc = u[:, cs]
        for k in range(1, w):
            acc = acc + hist_ref[POOL_HIST - k, :, cs]
        pooled = acc / float(w) - u[:, cs]
        outs.append(_dot(pooled.astype(BF), wp_ref[gi]))
    o_ref[...] = (jnp.concatenate(outs, axis=1) * ps_ref[...]).astype(BF)


def _pool_sample(u, hist_t, w_pool, pool_scale):
    n = u.shape[0]
    return pl.pallas_call(
        _spool_kernel,
        out_shape=jax.ShapeDtypeStruct((n, POOL_WIDTH), BF),
        name="pool_sample",
    )(u, hist_t, w_pool, pool_scale)


def _merge_kernel(x_ref, sh_ref, sc_ref, gate_ref, gpre_ref, gpost_ref, onsa_ref, opool_ref,
                  wgm_ref, wun_ref, wup_ref, wo_ref, o_ref):
    x = x_ref[...]
    h = _rms(x, gpre_ref[...]) * (1.0 + sc_ref[0]) + sh_ref[0]
    gm = jax.nn.sigmoid(_dot(h.astype(BF), wgm_ref[...]))
    m = (gm[:, :D_MODEL] * _dot(onsa_ref[...], wun_ref[...])
         + gm[:, D_MODEL:] * _dot(opool_ref[...], wup_ref[...]))
    m = _dot(m.astype(BF), wo_ref[...])
    o_ref[...] = x + gate_ref[0] * _rms(m, gpost_ref[...])


def _mlp_kernel(x_ref, sh_ref, sc_ref, gate_ref, gpre_ref, gpost_ref, w1_ref, w2_ref, o_ref):
    x = x_ref[...]
    h = _rms(x, gpre_ref[...]) * (1.0 + sc_ref[0]) + sh_ref[0]
    hb = h.astype(BF)
    f = jnp.zeros(x.shape, F32)
    fc = 1024
    for c in range(D_FF // fc):
        a = jnp.maximum(_dot(hb, w1_ref[:, c * fc:(c + 1) * fc]), 0.0)
        f = f + _dot((a * a).astype(BF), w2_ref[c * fc:(c + 1) * fc, :])
    o_ref[...] = x + gate_ref[0] * _rms(f, gpost_ref[...])


def _token_call(kernel, name, x2, mods, tm, rows_per_mod, extra_tok, consts):
    nt = x2.shape[0]
    r = mods[0].shape[1]
    mod_spec = pl.BlockSpec((1, r, D_MODEL), lambda t: ((t * tm) // rows_per_mod, 0, 0))
    in_specs = [pl.BlockSpec((tm, D_MODEL), lambda t: (t, 0))] + [mod_spec] * len(mods)
    in_specs += [pl.BlockSpec((1, D_MODEL), lambda t: (0, 0))] * 2
    in_specs += [pl.BlockSpec((tm, a.shape[1]), lambda t: (t, 0)) for a in extra_tok]
    in_specs += [pl.BlockSpec(w.shape, lambda t: (0, 0), pipeline_mode=pl.Buffered(1)) for w in consts[2:]]
    return pl.pallas_call(
        kernel,
        out_shape=jax.ShapeDtypeStruct((nt, D_MODEL), F32),
        grid=(nt // tm,),
        in_specs=in_specs,
        out_specs=pl.BlockSpec((tm, D_MODEL), lambda t: (t, 0)),
        compiler_params=_cparams("arbitrary"),
        name=name,
    )(x2, *mods, consts[0], consts[1], *extra_tok, *consts[2:])


def _sproj_kernel(x_ref, sh_ref, sc_ref, g_ref, wn_ref, wt_ref, z_ref, zs_ref, zT_ref):
    x = x_ref[...]
    h = _rms(x, g_ref[...]) * (1.0 + sc_ref[...]) + sh_ref[...]
    hb = h.astype(BF)
    z = _dot(hb, wn_ref[...])
    z_ref[...] = z
    zs_ref[...] = jax.nn.sigmoid(z)
    zT_ref[...] = _dot_nt(wt_ref[...], hb)


def _project_sample(x2, shift, scale, g, wn, wt):
    n = x2.shape[0]
    shp = jax.ShapeDtypeStruct((n, wn.shape[1]), F32)
    return pl.pallas_call(
        _sproj_kernel,
        out_shape=(shp, shp, jax.ShapeDtypeStruct((wt.shape[0], n), F32)),
        name="project_sample",
    )(x2, shift, scale, g, wn, wt)


def _sattn_kernel(pt_ref, *refs, n_pages, past_len, seq_step):
    pages = refs[:seq_step * n_pages]
    (qb_ref, kvc_ref, win_ref, ksn_ref, kwn_ref, kwnT_ref, g_ref,
     o_ref, nwin_ref, kaug_ref, vall_ref) = refs[seq_step * n_pages:]

    @pl.when(pl.program_id(0) == 0)
    def _():
        blk = lax.broadcasted_iota(I32, (128, past_len), 0)
        key_blk = jnp.right_shift(lax.broadcasted_iota(I32, (128, past_len), 1), 6)
        onehot = jnp.where(blk == key_blk, MASK_BIG, 0.0).astype(BF)
        for q in range(seq_step):
            kaug_ref[q, 128:256, :] = onehot

    for q in range(seq_step):
        for p in range(n_pages):
            pg = pages[q * n_pages + p][0]
            kaug_ref[q, 0:128, p * PAGE_SIZE:(p + 1) * PAGE_SIZE] = pg[0:KV_HALF, :].astype(BF)
            vall_ref[q, :, p * PAGE_SIZE:(p + 1) * PAGE_SIZE] = pg[KV_HALF:KV_WIDTH, :].astype(BF)

    nb_past = past_len // BLOCK
    win_buf = win_ref.shape[2]
    n_rows = 8 * seq_step
    per_seq = lambda f: jnp.concatenate([f(q) for q in range(seq_step)], axis=0)
    rows_of = lambda x, q: x[8 * q:8 * (q + 1)]
    qb = qb_ref[...].reshape(n_rows, KV_HALF).astype(BF)
    qf = qb.astype(F32)
    row = lax.broadcasted_iota(I32, (n_rows, 128), 0)
    lane = lax.broadcasted_iota(I32, (n_rows, 128), 1)

    def new_key(ref):
        return per_seq(lambda q: jnp.broadcast_to(ref[q], (8, KV_WIDTH))).astype(BF).astype(F32)

    s_c = per_seq(lambda q: _dot_nt(rows_of(qb, q), kvc_ref[q, :, 0:128].astype(BF)))
    cm = lane < nb_past
    s_c = jnp.where(cm, s_c, NEG_INF)
    mx = jnp.max(s_c, axis=1, keepdims=True)
    mx = jnp.where(mx > NEG_INF, mx, 0.0)
    e = jnp.where(cm, jnp.exp(s_c - mx), 0.0)
    p_c = e / jnp.maximum(jnp.sum(e, axis=1, keepdims=True), 1e-30)
    o_c = per_seq(lambda q: _dot(rows_of(p_c, q).astype(BF), kvc_ref[q, :, 128:256].astype(BF)))

    imp = jnp.zeros((n_rows, 128), F32)
    for grp in range(n_rows // GROUP):
        in_grp = jnp.right_shift(row, 2) == grp
        imp = jnp.where(in_grp, jnp.sum(jnp.where(in_grp, p_c, 0.0), axis=0, keepdims=True), imp)
    cur = nb_past
    forced = (lane == 0) | (lane == cur) | (lane == cur - 1)
    score = jnp.where(lane <= cur, imp + FORCE_SCORE * forced.astype(F32), -1.0)
    cnt = jnp.zeros((n_rows, 128), F32)
    for bp in range(nb_past + 1):
        other = score[:, bp:bp + 1]
        ahead = (other > score) | ((other == score) & (bp < lane))
        cnt = cnt + jnp.where(ahead, 1.0, 0.0)
    sel = (cnt < float(N_SELECT)) & (score >= 0.0)
    selm1 = jnp.where(sel, 0.0, -1.0).astype(BF)

    qaug = jnp.concatenate([qb, selm1], axis=1)
    s_s = per_seq(lambda q: _dot(rows_of(qaug, q), kaug_ref[q]))
    kv_n = new_key(ksn_ref)
    s_n = jnp.sum(qf * kv_n[:, 0:128], axis=1, keepdims=True)
    m_s = jnp.maximum(jnp.max(s_s, axis=1, keepdims=True), s_n)
    e_s = jnp.exp(s_s - m_s)
    e_n = jnp.exp(s_n - m_s)
    l_s = jnp.sum(e_s, axis=1, keepdims=True) + e_n
    pv = per_seq(lambda q: _dot_nt(rows_of(e_s, q).astype(BF), vall_ref[q]))
    o_s = (pv + e_n.astype(BF).astype(F32) * kv_n[:, 128:256]) / l_s

    s_w = per_seq(lambda q: _dot(rows_of(qb, q), win_ref[q, 0:KV_HALF, :].astype(BF)))
    lane_w = lax.broadcasted_iota(I32, (n_rows, win_buf), 1)
    s_w = jnp.where(lane_w >= win_buf + 1 - WINDOW, s_w, NEG_INF)
    kv_n = new_key(kwn_ref)
    s_n = jnp.sum(qf * kv_n[:, 0:128], axis=1, keepdims=True)
    m_w = jnp.maximum(jnp.max(s_w, axis=1, keepdims=True), s_n)
    e_w = jnp.exp(s_w - m_w)
    e_n = jnp.exp(s_n - m_w)
    l_w = jnp.sum(e_w, axis=1, keepdims=True) + e_n
    pv = per_seq(lambda q: _dot_nt(rows_of(e_w, q).astype(BF), win_ref[q, KV_HALF:KV_WIDTH, :].astype(BF)))
    o_w = (pv + e_n.astype(BF).astype(F32) * kv_n[:, 128:256]) / l_w

    g = g_ref[...].reshape(n_rows, 3)
    o_ref[...] = (g[:, 0:1] * o_c + g[:, 1:2] * o_s + g[:, 2:3] * o_w).reshape(seq_step, 8, KV_HALF)

    seq_lane = lax.broadcasted_iota(I32, kwnT_ref.shape, 1)
    row_lane = lax.broadcasted_iota(I32, (KV_WIDTH, win_buf), 1)
    for q in range(seq_step):
        n = pl.program_id(0) * seq_step + q
        new_col = jnp.sum(jnp.where(seq_lane == n, kwnT_ref[...], 0.0), axis=1, keepdims=True)
        shifted = pltpu.roll(win_ref[q], win_buf - 1, axis=1)
        nwin_ref[q] = jnp.where(row_lane == win_buf - 1, new_col, shifted)


def _attention_sample(page_table_flat, cache_t, qblk, kvc_pad, win_t, kvs_new, kvw_new, kvw_new_t, gates8,
                      n_pages, past_len):
    n_seq = qblk.shape[0]
    win_buf = win_t.shape[2]
    g = SAMPLE_SEQ_STEP
    per_step = lambda n, pt: (n, 0, 0)

    def page_map(k):
        return lambda n, pt: (pt[n * g * n_pages + k], 0, 0)

    grid_spec = pltpu.PrefetchScalarGridSpec(
        num_scalar_prefetch=1,
        grid=(n_seq // g,),
        in_specs=[pl.BlockSpec((1, KV_WIDTH, PAGE_SIZE), page_map(k)) for k in range(g * n_pages)]
        + [pl.BlockSpec((g, 8, 128), per_step),
           pl.BlockSpec((g, 128, KV_WIDTH), per_step),
           pl.BlockSpec((g, KV_WIDTH, win_buf), per_step),
           pl.BlockSpec((g, 1, KV_WIDTH), per_step),
           pl.BlockSpec((g, 1, KV_WIDTH), per_step),
           pl.BlockSpec(kvw_new_t.shape, lambda n, pt: (0, 0)),
           pl.BlockSpec((g, 8, 3), per_step)],
        out_specs=(pl.BlockSpec((g, 8, 128), per_step),
                   pl.BlockSpec((g, KV_WIDTH, win_buf), per_step)),
        scratch_shapes=[pltpu.VMEM((g, 256, past_len), BF), pltpu.VMEM((g, KV_HALF, past_len), BF)],
    )
    return pl.pallas_call(
        functools.partial(_sattn_kernel, n_pages=n_pages, past_len=past_len, seq_step=g),
        out_shape=(jax.ShapeDtypeStruct((n_seq, 8, 128), F32),
                   jax.ShapeDtypeStruct((n_seq, KV_WIDTH, win_buf), F32)),
        grid_spec=grid_spec,
        compiler_params=_cparams("arbitrary"),
        name="attention_sample",
    )(page_table_flat, *([cache_t] * (g * n_pages)), qblk, kvc_pad, win_t, kvs_new, kvw_new, kvw_new_t, gates8)


def _kv_rows_view(kv_t):
    n, _, t = kv_t.shape
    return jnp.transpose(kv_t.reshape(n, 2, N_KV_HEADS, HEAD_DIM, t), (0, 4, 1, 2, 3))


def _kv_feat_view(kv):
    n, t = kv.shape[:2]
    return jnp.transpose(kv, (0, 2, 3, 4, 1)).reshape(n, KV_WIDTH, t)


def kernel(x_prompt, x_sample, cache_cmp_kv, cache_sel_kv, state_win_kv, state_pool, page_table, c_prompt, c_sample, w_ada, b_ada, g_pre_mix, g_post_mix, g_pre_mlp, g_post_mlp, w_in, w_cmp, pos_cmp, w_pool, pool_scale, w_up_nsa, w_up_pool, w_o, w_ff1, w_ff2):
    n_batch, seq, _ = x_prompt.shape
    n_seq = x_sample.shape[0]
    n_pages = page_table.shape[1]
    past_len = n_pages * PAGE_SIZE
    nb_past = past_len // BLOCK
    assert x_sample.shape[1] == 1 and w_ada.shape[0] == 1 and past_len % BLOCK == 0
    assert seq % TOKEN_TILE == 0 and seq // BLOCK == 128 and state_win_kv.shape[2] == WINDOW
    assert (n_seq * nb_past) % CMP_GROUP == 0 and n_seq == CMP_GROUP
    assert FORCE_SCORE > GROUP

    w_t = w_in[0].T
    wt = jnp.pad(w_t[0:1304], ((0, 8), (0, 0))).astype(BF)
    wn = jnp.concatenate([w_t[768:896], w_t[1024:1152], w_t[1304:1816], w_t[512:768]], axis=0).T.astype(BF)
    ws = jnp.pad(w_t[0:1816], ((0, 104), (0, 0))).T.astype(BF)
    wgm = w_t[1816:3864].T.astype(BF)
    eye = jnp.eye(N_KV_HEADS, dtype=F32)
    wc = w_cmp[0].astype(BF)
    wz = jnp.zeros_like(wc)
    w4 = jnp.concatenate([jnp.concatenate([wc, wz], axis=3), jnp.concatenate([wz, wc], axis=3)], axis=2)
    w4 = w4.reshape(2, CMP_SPLIT, CMP_K // CMP_SPLIT, KV_HALF)
    pos4 = jnp.broadcast_to(jnp.transpose(pos_cmp[0], (1, 0, 2))[:, :, None, :],
                            (2, BLOCK, N_KV_HEADS, HEAD_DIM)).reshape(2, CMP_SPLIT, 1, CMP_K // CMP_SPLIT)
    wp = w_pool[0].astype(BF)
    ps = pool_scale[0].reshape(1, POOL_WIDTH)
    wun, wup, wo = w_up_nsa[0].astype(BF), w_up_pool[0].astype(BF), w_o[0].astype(BF)
    w1, w2 = w_ff1[0].astype(BF), w_ff2[0].astype(BF)
    gpm, gqm = g_pre_mix[0].reshape(1, D_MODEL), g_post_mix[0].reshape(1, D_MODEL)
    gpf, gqf = g_pre_mlp[0].reshape(1, D_MODEL), g_post_mlp[0].reshape(1, D_MODEL)

    n_c = n_batch + n_seq
    c_all = jnp.pad(jnp.concatenate([c_prompt, c_sample], axis=0), ((0, (-n_c) % 8), (0, 0)))
    ada = _adaln(c_all, w_ada[0], b_ada[0].reshape(1, -1))
    ada_p = ada[:n_batch].reshape(n_batch, 6, 1, D_MODEL)
    ada_s = ada[n_batch:n_c].reshape(n_seq, 6, D_MODEL)
    mods_p = [ada_p[:, k] for k in range(6)]
    mods_s = [ada_s[:, k][None] for k in range(6)]

    xp = x_prompt.reshape(n_batch * seq, D_MODEL)
    kvcT, kvsT, kvwT, kvc_a, kvc_b, u_p, qT, kaug, vsT, kw, vwT, gT = _project_prompt(
        xp, mods_p[0], mods_p[1], gpm, wn, wt, n_batch, seq)
    kvc_blk = _compress(kvc_a, kvc_b, pos4, w4).reshape(n_batch, seq // BLOCK, KV_WIDTH)
    kc = kvc_blk[:, :, 0:128].astype(BF)
    vcT = jnp.swapaxes(kvc_blk[:, :, 128:256], 1, 2).astype(BF)
    onsa_p = _attention_prompt(qT, kaug, vsT, kw, vwT, kc, vcT, gT, n_batch, seq)
    opool_p = _pool_prompt(u_p, wp, ps, n_batch, seq)
    x1_p = _token_call(_merge_kernel, "merge_prompt", xp, [mods_p[0], mods_p[1], mods_p[2]], TOKEN_TILE, seq,
                       [onsa_p.reshape(n_batch * seq, NSA_WIDTH), opool_p], [gpm, gqm, wgm, wun, wup, wo])
    y_p = _token_call(_mlp_kernel, "mlp_prompt", x1_p, [mods_p[3], mods_p[4], mods_p[5]], TOKEN_TILE, seq,
                      [], [gpf, gqf, w1, w2])

    xs = x_sample.reshape(n_seq, D_MODEL)
    z, zsig, zT = _project_sample(xs, mods_s[0][0], mods_s[1][0], gpm, ws, wt)
    q_s = z[:, 0:512] * (HEAD_DIM ** -0.5)
    kvc_n, kvs_n, kvw_n = z[:, 512:768], z[:, 768:1024], z[:, 1024:1280]
    gates_s = zsig[:, 1280:1304].reshape(n_seq, N_HEADS, 3)
    u_s = z[:, 1304:1816]
    q5 = q_s.reshape(n_seq, N_KV_HEADS, GROUP, 1, HEAD_DIM) * eye[None, :, None, :, None]
    qblk = q5.reshape(n_seq, N_HEADS, KV_HALF)

    pt_flat = page_table.reshape(-1)
    kvc_past = _compress_paged(_kv_feat_view(cache_cmp_kv[0]), pt_flat, pos4, w4, n_seq, n_pages)
    last_a = jnp.pad(kvc_n[:, None, 0:128], ((0, 0), (0, BLOCK - 1), (0, 0))).reshape(n_seq * BLOCK, KV_HALF)
    last_b = jnp.pad(kvc_n[:, None, 128:256], ((0, 0), (0, BLOCK - 1), (0, 0))).reshape(n_seq * BLOCK, KV_HALF)
    kvc_last = _compress(last_a, last_b, pos4, w4)
    kvc_s = jnp.concatenate([kvc_past.reshape(n_seq, nb_past, KV_WIDTH), kvc_last[:, None, :]], axis=1)
    kvc_pad = jnp.pad(kvc_s, ((0, 0), (0, 128 - nb_past - 1), (0, 0)))
    o8, new_win_t = _attention_sample(
        pt_flat, _kv_feat_view(cache_sel_kv[0]), qblk, kvc_pad, _kv_feat_view(state_win_kv[0]),
        kvs_n[:, None, :], kvw_n[:, None, :], zT[1024:1280], gates_s, n_pages, past_len)
    o5 = o8.reshape(n_seq, N_KV_HEADS, GROUP, N_KV_HEADS, HEAD_DIM)
    onsa_s = jnp.concatenate([o5[:, 0, :, 0, :], o5[:, 1, :, 1, :]], axis=1).reshape(n_seq, NSA_WIDTH).astype(BF)
    opool_s = _pool_sample(u_s, jnp.swapaxes(state_pool[0], 0, 1), wp, ps)
    x1_s = _token_call(_merge_kernel, "merge_sample", xs, [mods_s[0], mods_s[1], mods_s[2]], n_seq, n_seq,
                       [onsa_s, opool_s], [gpm, gqm, wgm, wun, wup, wo])
    y_s = _token_call(_mlp_kernel, "mlp_sample", x1_s, [mods_s[3], mods_s[4], mods_s[5]], n_seq, n_seq,
                      [], [gpf, gqf, w1, w2])

    win_p = min(WINDOW, seq)
    new_kv_s = lambda rows: _kv_rows_view(rows.reshape(1, KV_WIDTH, n_seq))[0][None, :, None]
    return (
        y_p.reshape(n_batch, seq, D_MODEL),
        y_s.reshape(n_seq, 1, D_MODEL),
        _kv_rows_view(kvcT)[None],
        _kv_rows_view(kvsT)[None],
        _kv_rows_view(kvwT[:, :, seq - win_p:])[None],
        u_p.reshape(n_batch, seq, POOL_WIDTH)[None, :, seq - POOL_HIST:],
        new_kv_s(zT[512:768]),
        new_kv_s(zT[768:1024]),
        _kv_rows_view(new_win_t)[None],
        jnp.concatenate([state_pool[0][:, 1:], u_s[:, None, :]], axis=1)[None],
    )
```

```python
import functools
import math

import jax
import jax.numpy as jnp
from jax import lax
from jax.experimental import pallas as pl
from jax.experimental.pallas import tpu as pltpu

D_MODEL = 1024
N_HEADS = 8
HEAD_DIM = 64
N_KV_HEADS = 2
GROUP = N_HEADS // N_KV_HEADS
BLOCK = 64
N_SELECT = 16
WINDOW = 512
Q_TILE = 128
NSA_WIDTH = N_HEADS * HEAD_DIM
KV_WIDTH = 2 * N_KV_HEADS * HEAD_DIM
KV_HALF = N_KV_HEADS * HEAD_DIM
FORCE_SCORE = 16.0
N_FORCED = 3
POOL_WINDOWS = (2, 4, 8, 16)
POOL_WIDTH = 512
POOL_GROUP_DIM = 128
POOL_HIST = 15
D_FF = 4 * D_MODEL
EPS = 1e-6
PAGE_SIZE = 128
V_ROWS = HEAD_DIM + 16
CMP_K = BLOCK * KV_HALF
CMP_SPLIT = 4
CMP_GROUP = 128
CMP_PITCH = BLOCK + 4

BF = jnp.bfloat16
F32 = jnp.float32
I32 = jnp.int32
MASK_BIG = 2.0 ** 100
NEG_INF = float("-inf")
LOG2E = math.log2(math.e)

TOKEN_TILE = 512
SEL_CHUNK = 512
SAMPLE_SEQ_STEP = 4
VMEM_LIMIT = 56 * 1024 * 1024


def _cparams(*sem):
    return pltpu.CompilerParams(dimension_semantics=sem, vmem_limit_bytes=VMEM_LIMIT)


def _rms(x, g):
    return x * lax.rsqrt(jnp.mean(x * x, axis=-1, keepdims=True) + EPS) * g


def _dot(a, b):
    return jnp.dot(a, b, preferred_element_type=F32)


def _dot_nt(a, b):
    return lax.dot_general(a, b, (((1,), (1,)), ((), ())), preferred_element_type=F32)


def _ada_kernel(c_ref, w_ref, b_ref, o_ref):
    c = c_ref[...]
    a = (c * jax.nn.sigmoid(c)).astype(BF)
    o_ref[...] = _dot(a, w_ref[...].astype(BF)) + b_ref[...]


def _adaln(c_all, w_ada, b_ada):
    rows = c_all.shape[0]
    n_out = w_ada.shape[1]
    tn = 512
    return pl.pallas_call(
        _ada_kernel,
        out_shape=jax.ShapeDtypeStruct((rows, n_out), F32),
        grid=(n_out // tn,),
        in_specs=[pl.BlockSpec((rows, D_MODEL), lambda j: (0, 0)),
                  pl.BlockSpec((D_MODEL, tn), lambda j: (0, j)),
                  pl.BlockSpec((1, tn), lambda j: (0, j))],
        out_specs=pl.BlockSpec((rows, tn), lambda j: (0, j)),
        compiler_params=_cparams("arbitrary"),
        name="adaln",
    )(c_all, w_ada, b_ada)


def _proj_kernel(x_ref, sh_ref, sc_ref, g_ref, wn_ref, wt_ref,
                 kvcT_ref, kvsT_ref, kvwT_ref, ca_ref, cb_ref, u_ref,
                 qT_ref, kaug_ref, vsT_ref, kw_ref, vwT_ref, gT_ref, *, tm, tpb):
    x = x_ref[...]
    h = _rms(x, g_ref[...]) * (1.0 + sc_ref[0]) + sh_ref[0]
    hb = h.astype(BF)
    zn = _dot(hb, wn_ref[...])
    zt = _dot_nt(wt_ref[...], hb)
    kvcT_ref[0] = zt[512:768]
    kvsT_ref[0] = zt[768:1024]
    kvwT_ref[0] = zt[1024:1280]
    ca_ref[...] = zn[:, 768:896]
    cb_ref[...] = zn[:, 896:1024]
    u_ref[...] = zn[:, 256:768]
    qT_ref[0] = (zt[0:512] * (HEAD_DIM ** -0.5 * LOG2E)).astype(BF)
    ones = jnp.ones((V_ROWS - HEAD_DIM, tm), BF)
    for h in range(N_KV_HEADS):
        vs_h = zt[896 + h * HEAD_DIM:896 + (h + 1) * HEAD_DIM].astype(BF)
        vw_h = zt[1152 + h * HEAD_DIM:1152 + (h + 1) * HEAD_DIM].astype(BF)
        vsT_ref[0, 0, h, 0:HEAD_DIM, :] = vs_h
        vsT_ref[0, 0, h, HEAD_DIM:V_ROWS, :] = ones
        for c in range(tm // Q_TILE):
            cs = slice(c * Q_TILE, (c + 1) * Q_TILE)
            vwT_ref[0, c, h, 0:HEAD_DIM, :] = vw_h[:, cs]
            vwT_ref[0, c, h, HEAD_DIM:V_ROWS, :] = ones[:, cs]
    gT_ref[0] = jax.nn.sigmoid(zt[1280:1312])
    t0 = (pl.program_id(0) % tpb) * tm
    blk = jnp.right_shift(t0 + lax.broadcasted_iota(I32, (tm, 128), 0), 6)
    lane = lax.broadcasted_iota(I32, (tm, 128), 1)
    kaug_ref[0, :, 0:128] = zn[:, 0:128].astype(BF)
    kaug_ref[0, :, 128:256] = jnp.where(blk == lane, MASK_BIG, 0.0).astype(BF)
    kw_ref[0] = zn[:, 128:256].astype(BF)


def _project_prompt(x2, shift, scale, g, wn, wt, n_batch, seq):
    tm = TOKEN_TILE
    tpb = seq // tm
    nt = n_batch * seq
    tok = lambda t: (t, 0)
    per_b = lambda t: (t // tpb, 0, 0)
    featT = lambda t: (t // tpb, 0, t % tpb)
    rows3 = lambda t: (t // tpb, t % tpb, 0)
    rows5 = lambda t: (t // tpb, t % tpb, 0, 0, 0)
    kvT = jax.ShapeDtypeStruct((n_batch, KV_WIDTH, seq), F32)
    out_shape = (
        kvT, kvT, kvT,
        jax.ShapeDtypeStruct((nt, KV_HALF), F32),
        jax.ShapeDtypeStruct((nt, KV_HALF), F32),
        jax.ShapeDtypeStruct((nt, POOL_WIDTH), F32),
        jax.ShapeDtypeStruct((n_batch, NSA_WIDTH, seq), BF),
        jax.ShapeDtypeStruct((n_batch, seq, 256), BF),
        jax.ShapeDtypeStruct((n_batch, seq // SEL_CHUNK, N_KV_HEADS, V_ROWS, SEL_CHUNK), BF),
        jax.ShapeDtypeStruct((n_batch, seq, KV_HALF), BF),
        jax.ShapeDtypeStruct((n_batch, seq // Q_TILE, N_KV_HEADS, V_ROWS, Q_TILE), BF),
        jax.ShapeDtypeStruct((n_batch, 32, seq), F32),
    )
    out_specs = (
        pl.BlockSpec((1, KV_WIDTH, tm), featT),
        pl.BlockSpec((1, KV_WIDTH, tm), featT),
        pl.BlockSpec((1, KV_WIDTH, tm), featT),
        pl.BlockSpec((tm, KV_HALF), tok),
        pl.BlockSpec((tm, KV_HALF), tok),
        pl.BlockSpec((tm, POOL_WIDTH), tok),
        pl.BlockSpec((1, NSA_WIDTH, tm), featT),
        pl.BlockSpec((1, tm, 256), rows3),
        pl.BlockSpec((1, 1, N_KV_HEADS, V_ROWS, SEL_CHUNK), rows5),
        pl.BlockSpec((1, tm, KV_HALF), rows3),
        pl.BlockSpec((1, tm // Q_TILE, N_KV_HEADS, V_ROWS, Q_TILE), rows5),
        pl.BlockSpec((1, 32, tm), featT),
    )
    return pl.pallas_call(
        functools.partial(_proj_kernel, tm=tm, tpb=tpb),
        out_shape=out_shape,
        grid=(nt // tm,),
        in_specs=[pl.BlockSpec((tm, D_MODEL), tok),
                  pl.BlockSpec((1, 1, D_MODEL), per_b),
                  pl.BlockSpec((1, 1, D_MODEL), per_b),
                  pl.BlockSpec((1, D_MODEL), lambda t: (0, 0)),
                  pl.BlockSpec(wn.shape, lambda t: (0, 0)),
                  pl.BlockSpec(wt.shape, lambda t: (0, 0))],
        out_specs=out_specs,
        compiler_params=_cparams("arbitrary"),
        name="project_prompt",
    )(x2, shift, scale, g, wn, wt)


def _compress_rows(src_refs, pos_ref, w_ref, o_ref, n_blocks, pitch):
    rows_per_slice = BLOCK // CMP_SPLIT
    for s in range(2):
        acc = jnp.zeros((n_blocks, KV_HALF), F32)
        for c in range(CMP_SPLIT):
            xc = jnp.concatenate(
                [src_refs[s][pl.ds(c * rows_per_slice + r, n_blocks, stride=pitch), :]
                 for r in range(rows_per_slice)], axis=1) + pos_ref[s, c]
            acc = acc + _dot(xc.astype(BF), w_ref[s, c])
        o_ref[:, s * KV_HALF:(s + 1) * KV_HALF] = acc


def _cmp_kernel(xa_ref, xb_ref, pos_ref, w_ref, o_ref):
    _compress_rows((xa_ref, xb_ref), pos_ref, w_ref, o_ref, CMP_GROUP, BLOCK)


def _compress(xa, xb, pos4, w4):
    m = xa.shape[0] // BLOCK
    rows = CMP_GROUP * BLOCK
    return pl.pallas_call(
        _cmp_kernel,
        out_shape=jax.ShapeDtypeStruct((m, KV_WIDTH), F32),
        grid=(m // CMP_GROUP,),
        in_specs=[pl.BlockSpec((rows, KV_HALF), lambda i: (i, 0)),
                  pl.BlockSpec((rows, KV_HALF), lambda i: (i, 0)),
                  pl.BlockSpec(pos4.shape, lambda i: (0, 0, 0, 0)),
                  pl.BlockSpec(w4.shape, lambda i: (0, 0, 0, 0))],
        out_specs=pl.BlockSpec((CMP_GROUP, KV_WIDTH), lambda i: (i, 0)),
        compiler_params=_cparams("arbitrary"),
        name="compress_blocks",
    )(xa, xb, pos4, w4)


def _scmp_kernel(pt_ref, *refs, n_pages_step):
    pages = refs[:n_pages_step]
    pos_ref, w_ref, o_ref, sa_ref, sb_ref = refs[n_pages_step:]
    blocks_per_page = PAGE_SIZE // BLOCK
    for k in range(n_pages_step):
        pg = pages[k][0]
        for s, dst in enumerate((sa_ref, sb_ref)):
            rows = pg[s * KV_HALF:(s + 1) * KV_HALF, :].T
            for b in range(blocks_per_page):
                m = k * blocks_per_page + b
                dst[m * CMP_PITCH:m * CMP_PITCH + BLOCK, :] = rows[b * BLOCK:(b + 1) * BLOCK, :]
    _compress_rows((sa_ref, sb_ref), pos_ref, w_ref, o_ref, CMP_GROUP, CMP_PITCH)


def _compress_paged(cache_t, page_table_flat, pos4, w4, n_seq, n_pages):
    blocks_per_page = PAGE_SIZE // BLOCK
    n_pages_step = CMP_GROUP // blocks_per_page

    def page_map(k):
        return lambda s, pt: (pt[s * n_pages_step + k], 0, 0)

    grid_spec = pltpu.PrefetchScalarGridSpec(
        num_scalar_prefetch=1,
        grid=(n_seq * n_pages // n_pages_step,),
        in_specs=[pl.BlockSpec((1, KV_WIDTH, PAGE_SIZE), page_map(k)) for k in range(n_pages_step)]
        + [pl.BlockSpec(pos4.shape, lambda s, pt: (0, 0, 0, 0)),
           pl.BlockSpec(w4.shape, lambda s, pt: (0, 0, 0, 0))],
        out_specs=pl.BlockSpec((CMP_GROUP, KV_WIDTH), lambda s, pt: (s, 0)),
        scratch_shapes=[pltpu.VMEM((CMP_GROUP * CMP_PITCH, KV_HALF), F32),
                        pltpu.VMEM((CMP_GROUP * CMP_PITCH, KV_HALF), F32)],
    )
    return pl.pallas_call(
        functools.partial(_scmp_kernel, n_pages_step=n_pages_step),
        out_shape=jax.ShapeDtypeStruct((n_seq * n_pages * blocks_per_page, KV_WIDTH), F32),
        grid_spec=grid_spec,
        compiler_params=_cparams("arbitrary"),
        name="compress_paged",
    )(page_table_flat, *([cache_t] * n_pages_step), pos4, w4)


def _attn_kernel(qT_ref, kaug_ref, vsT_ref, kw_ref, vwT_ref, kc_ref, vcT_ref, gT_ref, o_ref,
                 qa_ref, sa_ref, sb_ref, sc_ref, m_ref, acc_ref, outT_ref):
    i = pl.program_id(1)
    q0 = i * Q_TILE
    n_full = jnp.right_shift(i, 2)
    row = lax.broadcasted_iota(I32, (128, 128), 0)
    tok = lax.broadcasted_iota(I32, (128, 128), 1)
    qpos = q0 + tok
    cur = jnp.right_shift(qpos, 6)
    cmask = (row + 1) * BLOCK - 1 <= qpos
    valid = row <= cur
    forced = (row == 0) | (row == cur) | (row == cur - 1)
    key_c = lax.broadcasted_iota(I32, (SEL_CHUNK, Q_TILE), 0)
    qpos_c = q0 + lax.broadcasted_iota(I32, (SEL_CHUNK, Q_TILE), 1)
    hs = [slice(h * HEAD_DIM, (h + 1) * HEAD_DIM) for h in range(N_KV_HEADS)]

    def bias4(keep):
        b = jnp.where(keep, 0.0, -MASK_BIG)
        return jnp.concatenate([b] * GROUP, axis=1)

    def online_step(st, s, v_t):
        m_old = m_ref[st]
        m_new = jnp.maximum(m_old, jnp.max(s, axis=0, keepdims=True))
        alpha = jnp.exp2(m_old - m_new)
        p = jnp.exp2(s - m_new)
        acc_ref[st] = alpha * acc_ref[st] + _dot(v_t, p.astype(BF))
        m_ref[st] = m_new

    m_ref[...] = jnp.full(m_ref.shape, NEG_INF, F32)
    acc_ref[...] = jnp.zeros(acc_ref.shape, F32)

    o_c = []
    scores = []
    for h in range(N_KV_HEADS):
        qa_ref[h] = jnp.zeros(qa_ref.shape[1:], BF)
        for g in range(GROUP):
            r0 = h * GROUP * HEAD_DIM + g * HEAD_DIM
            qa_ref[h, hs[h], g * Q_TILE:(g + 1) * Q_TILE] = qT_ref[0, r0:r0 + HEAD_DIM, :]
        sc = _dot(kc_ref[0], qa_ref[h, 0:128, :])
        imp = jnp.zeros((128, Q_TILE), F32)
        p_parts = []
        for g in range(GROUP):
            s = jnp.where(cmask, sc[:, g * Q_TILE:(g + 1) * Q_TILE], NEG_INF)
            mx = jnp.max(s, axis=0, keepdims=True)
            mx = jnp.where(mx > NEG_INF, mx, 0.0)
            e = jnp.where(cmask, jnp.exp2(s - mx), 0.0)
            p = e / jnp.maximum(jnp.sum(e, axis=0, keepdims=True), 1e-30)
            imp = imp + p
            p_parts.append(p)
        o_c.append(_dot(vcT_ref[0, hs[h], :], jnp.concatenate(p_parts, axis=1).astype(BF)))
        scores.append(jnp.where(valid, jnp.where(forced, -2.0, imp), -1.0))

    blk_f = lax.broadcasted_iota(I32, (128, 2 * Q_TILE), 0).astype(F32)

    def pick_body(k, work):
        best = jnp.max(work, axis=0, keepdims=True)
        first = jnp.min(jnp.where(work == best, blk_f, 128.0), axis=0, keepdims=True)
        return jnp.where((blk_f == first) & (best >= 0.0), -2.0, work)

    work = lax.fori_loop(0, N_SELECT - N_FORCED, pick_body, jnp.concatenate(scores, axis=1))
    selm1 = jnp.where(work == -2.0, 0.0, -1.0).astype(BF)
    for h in range(N_KV_HEADS):
        for g in range(GROUP):
            qa_ref[h, 128:256, g * Q_TILE:(g + 1) * Q_TILE] = selm1[:, h * Q_TILE:(h + 1) * Q_TILE]

    last_chunk = kaug_ref.shape[1] // SEL_CHUNK - 1

    def sel_scores(j, dst_ref):
        j = jnp.minimum(j, last_chunk)
        kt = kaug_ref[0, pl.ds(pl.multiple_of(j * SEL_CHUNK, SEL_CHUNK), SEL_CHUNK), :]
        for h in range(N_KV_HEADS):
            dst_ref[h] = _dot(kt, qa_ref[h])

    def sel_process(src_ref, j, causal):
        if causal:
            cbias = bias4(j * SEL_CHUNK + key_c <= qpos_c)
        for h in range(N_KV_HEADS):
            s = src_ref[h]
            if causal:
                s = s + cbias
            online_step(h, s, vsT_ref[0, j, h])

    sel_scores(0, sa_ref)
    sel_scores(1, sb_ref)

    j0 = jnp.maximum(i - WINDOW // Q_TILE, 0)
    w0 = j0 * Q_TILE
    delta = qpos_c - (w0 + key_c)
    wbias = bias4((delta >= 0) & (delta < WINDOW))
    dbias = bias4((row <= tok) & (i >= WINDOW // Q_TILE))
    kw_a = kw_ref[0, pl.ds(pl.multiple_of(w0, Q_TILE), WINDOW), :]
    kw_b = kw_ref[0, pl.ds(pl.multiple_of(q0, Q_TILE), Q_TILE), :]
    for h in range(N_KV_HEADS):
        qf = qa_ref[h, 0:128, :]
        v_t = jnp.concatenate([vwT_ref[0, j0 + c, h] for c in range(WINDOW // Q_TILE)], axis=1)
        online_step(2 + h, _dot(kw_a, qf) + wbias, v_t)
        online_step(2 + h, _dot(kw_b, qf) + dbias, vwT_ref[0, i, h])

    def trio_body(t, carry):
        j = 3 * t
        sel_scores(j + 2, sc_ref)
        sel_process(sa_ref, j, False)
        sel_scores(j + 3, sa_ref)
        sel_process(sb_ref, j + 1, False)
        sel_scores(j + 4, sb_ref)
        sel_process(sc_ref, j + 2, False)
        return carry

    n_trios = lax.div(n_full, 3)
    lax.fori_loop(0, n_trios, trio_body, 0)
    j_last = 3 * n_trios
    n_left = n_full - j_last

    @pl.when(n_left == 0)
    def _():
        sel_process(sa_ref, j_last, True)

    @pl.when(n_left == 1)
    def _():
        sel_process(sa_ref, j_last, False)
        sel_process(sb_ref, j_last + 1, True)

    @pl.when(n_left == 2)
    def _():
        sel_scores(j_last + 2, sc_ref)
        sel_process(sa_ref, j_last, False)
        sel_process(sb_ref, j_last + 1, False)
        sel_process(sc_ref, j_last + 2, True)

    for h in range(N_KV_HEADS):
        o_s = acc_ref[h, 0:HEAD_DIM, :] / acc_ref[h, HEAD_DIM:HEAD_DIM + 1, :]
        o_w = acc_ref[2 + h, 0:HEAD_DIM, :] / acc_ref[2 + h, HEAD_DIM:HEAD_DIM + 1, :]
        for g in range(GROUP):
            gs = slice(g * Q_TILE, (g + 1) * Q_TILE)
            gr = h * GROUP * 3 + g * 3
            og = (gT_ref[0, gr:gr + 1, :] * o_c[h][:, gs] + gT_ref[0, gr + 1:gr + 2, :] * o_s[:, gs]
                  + gT_ref[0, gr + 2:gr + 3, :] * o_w[:, gs])
            r0 = h * GROUP * HEAD_DIM + g * HEAD_DIM
            outT_ref[r0:r0 + HEAD_DIM, :] = og

    o_ref[0] = outT_ref[...].T.astype(BF)


def _attention_prompt(qT, kaug, vsT, kw, vwT, kc, vcT, gT, n_batch, seq):
    per_b3 = lambda n, i: (n, 0, 0)
    per_b5 = lambda n, i: (n, 0, 0, 0, 0)
    rows = GROUP * Q_TILE
    return pl.pallas_call(
        _attn_kernel,
        out_shape=jax.ShapeDtypeStruct((n_batch, seq, NSA_WIDTH), BF),
        grid=(n_batch, seq // Q_TILE),
        in_specs=[pl.BlockSpec((1, NSA_WIDTH, Q_TILE), lambda n, i: (n, 0, i)),
                  pl.BlockSpec((1, seq, 256), per_b3),
                  pl.BlockSpec((1, seq // SEL_CHUNK, N_KV_HEADS, V_ROWS, SEL_CHUNK), per_b5),
                  pl.BlockSpec((1, seq, KV_HALF), per_b3),
                  pl.BlockSpec((1, seq // Q_TILE, N_KV_HEADS, V_ROWS, Q_TILE), per_b5),
                  pl.BlockSpec((1, 128, KV_HALF), per_b3),
                  pl.BlockSpec((1, KV_HALF, 128), per_b3),
                  pl.BlockSpec((1, 32, Q_TILE), lambda n, i: (n, 0, i))],
        out_specs=pl.BlockSpec((1, Q_TILE, NSA_WIDTH), lambda n, i: (n, i, 0)),
        scratch_shapes=[pltpu.VMEM((N_KV_HEADS, 256, rows), BF),
                        pltpu.VMEM((N_KV_HEADS, SEL_CHUNK, rows), F32),
                        pltpu.VMEM((N_KV_HEADS, SEL_CHUNK, rows), F32),
                        pltpu.VMEM((N_KV_HEADS, SEL_CHUNK, rows), F32),
                        pltpu.VMEM((2 * N_KV_HEADS, 1, rows), F32),
                        pltpu.VMEM((2 * N_KV_HEADS, V_ROWS, rows), F32),
                        pltpu.VMEM((NSA_WIDTH, Q_TILE), F32)],
        compiler_params=_cparams("arbitrary", "arbitrary"),
        name="attention_prompt",
    )(qT, kaug, vsT, kw, vwT, kc, vcT, gT)


def _pool_kernel(u_ref, halo_ref, wp_ref, ps_ref, o_ref, ext_ref, *, tm, tpb):
    t = pl.program_id(0) % tpb
    ext_ref[0:16, :] = jnp.where(t == 0, 0.0, halo_ref[...])
    u = u_ref[...]
    ext_ref[16:16 + tm, :] = u
    pos = t * tm + lax.broadcasted_iota(I32, (tm, 1), 0)
    outs = []
    for gi, w in enumerate(POOL_WINDOWS):
        cs = slice(gi * POOL_GROUP_DIM, (gi + 1) * POOL_GROUP_DIM)
        acc = u[:, cs]
        for k in range(1, w):
            acc = acc + ext_ref[pl.ds(16 - k, tm), cs]
        cnt = jnp.minimum(pos + 1, w).astype(F32)
        pooled = acc / cnt - u[:, cs]
        outs.append(_dot(pooled.astype(BF), wp_ref[gi]))
    o_ref[...] = (jnp.concatenate(outs, axis=1) * ps_ref[...]).astype(BF)


def _pool_prompt(u, w_pool, pool_scale, n_batch, seq):
    tm = TOKEN_TILE
    tpb = seq // tm
    nt = n_batch * seq
    return pl.pallas_call(
        functools.partial(_pool_kernel, tm=tm, tpb=tpb),
        out_shape=jax.ShapeDtypeStruct((nt, POOL_WIDTH), BF),
        grid=(nt // tm,),
        in_specs=[pl.BlockSpec((tm, POOL_WIDTH), lambda t: (t, 0)),
                  pl.BlockSpec((16, POOL_WIDTH), lambda t: (jnp.maximum(t * (tm // 16) - 1, 0), 0)),
                  pl.BlockSpec(w_pool.shape, lambda t: (0, 0, 0)),
                  pl.BlockSpec((1, POOL_WIDTH), lambda t: (0, 0))],
        out_specs=pl.BlockSpec((tm, POOL_WIDTH), lambda t: (t, 0)),
        scratch_shapes=[pltpu.VMEM((tm + 16, POOL_WIDTH), F32)],
        compiler_params=_cparams("arbitrary"),
        name="pool_prompt",
    )(u, u, w_pool, pool_scale)


def _spool_kernel(u_ref, hist_ref, wp_ref, ps_ref, o_ref):
    u = u_ref[...]
    outs = []
    for gi, w in enumerate(POOL_WINDOWS):
        cs = slice(gi * POOL_GROUP_DIM, (gi + 1) * POOL_GROUP_DIM)
        acc = u[:, cs]
        for k in range(1, w):
            acc = acc + hist_ref[POOL_HIST - k, :, cs]
        pooled = acc / float(w) - u[:, cs]
        outs.append(_dot(pooled.astype(BF), wp_ref[gi]))
    o_ref[...] = (jnp.concatenate(outs, axis=1) * ps_ref[...]).astype(BF)


def _pool_sample(u, hist_t, w_pool, pool_scale):
    n = u.shape[0]
    return pl.pallas_call(
        _spool_kernel,
        out_shape=jax.ShapeDtypeStruct((n, POOL_WIDTH), BF),
        name="pool_sample",
    )(u, hist_t, w_pool, pool_scale)


def _merge_kernel(x_ref, sh_ref, sc_ref, gate_ref, gpre_ref, gpost_ref, onsa_ref, opool_ref,
                  wgm_ref, wun_ref, wup_ref, wo_ref, o_ref):
    x = x_ref[...]
    h = _rms(x, gpre_ref[...]) * (1.0 + sc_ref[0]) + sh_ref[0]
    gm = jax.nn.sigmoid(_dot(h.astype(BF), wgm_ref[...]))
    m = (gm[:, :D_MODEL] * _dot(onsa_ref[...], wun_ref[...])
         + gm[:, D_MODEL:] * _dot(opool_ref[...], wup_ref[...]))
    m = _dot(m.astype(BF), wo_ref[...])
    o_ref[...] = x + gate_ref[0] * _rms(m, gpost_ref[...])


def _mlp_kernel(x_ref, sh_ref, sc_ref, gate_ref, gpre_ref, gpost_ref, w1_ref, w2_ref, o_ref):
    x = x_ref[...]
    h = _rms(x, gpre_ref[...]) * (1.0 + sc_ref[0]) + sh_ref[0]
    hb = h.astype(BF)
    f = jnp.zeros(x.shape, F32)
    fc = 1024
    for c in range(D_FF // fc):
        a = jnp.maximum(_dot(hb, w1_ref[:, c * fc:(c + 1) * fc]), 0.0)
        f = f + _dot((a * a).astype(BF), w2_ref[c * fc:(c + 1) * fc, :])
    o_ref[...] = x + gate_ref[0] * _rms(f, gpost_ref[...])


def _token_call(kernel, name, x2, mods, tm, rows_per_mod, extra_tok, consts):
    nt = x2.shape[0]
    r = mods[0].shape[1]
    mod_spec = pl.BlockSpec((1, r, D_MODEL), lambda t: ((t * tm) // rows_per_mod, 0, 0))
    in_specs = [pl.BlockSpec((tm, D_MODEL), lambda t: (t, 0))] + [mod_spec] * len(mods)
    in_specs += [pl.BlockSpec((1, D_MODEL), lambda t: (0, 0))] * 2
    in_specs += [pl.BlockSpec((tm, a.shape[1]), lambda t: (t, 0)) for a in extra_tok]
    in_specs += [pl.BlockSpec(w.shape, lambda t: (0, 0), pipeline_mode=pl.Buffered(1)) for w in consts[2:]]
    return pl.pallas_call(
        kernel,
        out_shape=jax.ShapeDtypeStruct((nt, D_MODEL), F32),
        grid=(nt // tm,),
        in_specs=in_specs,
        out_specs=pl.BlockSpec((tm, D_MODEL), lambda t: (t, 0)),
        compiler_params=_cparams("arbitrary"),
        name=name,
    )(x2, *mods, consts[0], consts[1], *extra_tok, *consts[2:])


def _sproj_kernel(x_ref, sh_ref, sc_ref, g_ref, wn_ref, wt_ref, z_ref, zs_ref, zT_ref):
    x = x_ref[...]
    h = _rms(x, g_ref[...]) * (1.0 + sc_ref[...]) + sh_ref[...]
    hb = h.astype(BF)
    z = _dot(hb, wn_ref[...])
    z_ref[...] = z
    zs_ref[...] = jax.nn.sigmoid(z)
    zT_ref[...] = _dot_nt(wt_ref[...], hb)


def _project_sample(x2, shift, scale, g, wn, wt):
    n = x2.shape[0]
    shp = jax.ShapeDtypeStruct((n, wn.shape[1]), F32)
    return pl.pallas_call(
        _sproj_kernel,
        out_shape=(shp, shp, jax.ShapeDtypeStruct((wt.shape[0], n), F32)),
        name="project_sample",
    )(x2, shift, scale, g, wn, wt)


def _sattn_kernel(pt_ref, *refs, n_pages, past_len, seq_step):
    pages = refs[:seq_step * n_pages]
    (qb_ref, kvc_ref, win_ref, ksn_ref, kwn_ref, kwnT_ref, g_ref,
     o_ref, nwin_ref, kaug_ref, vall_ref) = refs[seq_step * n_pages:]

    @pl.when(pl.program_id(0) == 0)
    def _():
        blk = lax.broadcasted_iota(I32, (128, past_len), 0)
        key_blk = jnp.right_shift(lax.broadcasted_iota(I32, (128, past_len), 1), 6)
        onehot = jnp.where(blk == key_blk, MASK_BIG, 0.0).astype(BF)
        for q in range(seq_step):
            kaug_ref[q, 128:256, :] = onehot

    for q in range(seq_step):
        for p in range(n_pages):
            pg = pages[q * n_pages + p][0]
            kaug_ref[q, 0:128, p * PAGE_SIZE:(p + 1) * PAGE_SIZE] = pg[0:KV_HALF, :].astype(BF)
            vall_ref[q, :, p * PAGE_SIZE:(p + 1) * PAGE_SIZE] = pg[KV_HALF:KV_WIDTH, :].astype(BF)

    nb_past = past_len // BLOCK
    win_buf = win_ref.shape[2]
    n_rows = 8 * seq_step
    per_seq = lambda f: jnp.concatenate([f(q) for q in range(seq_step)], axis=0)
    rows_of = lambda x, q: x[8 * q:8 * (q + 1)]
    qb = qb_ref[...].reshape(n_rows, KV_HALF).astype(BF)
    qf = qb.astype(F32)
    row = lax.broadcasted_iota(I32, (n_rows, 128), 0)
    lane = lax.broadcasted_iota(I32, (n_rows, 128), 1)

    def new_key(ref):
        return per_seq(lambda q: jnp.broadcast_to(ref[q], (8, KV_WIDTH))).astype(BF).astype(F32)

    s_c = per_seq(lambda q: _dot_nt(rows_of(qb, q), kvc_ref[q, :, 0:128].astype(BF)))
    cm = lane < nb_past
    s_c = jnp.where(cm, s_c, NEG_INF)
    mx = jnp.max(s_c, axis=1, keepdims=True)
    mx = jnp.where(mx > NEG_INF, mx, 0.0)
    e = jnp.where(cm, jnp.exp(s_c - mx), 0.0)
    p_c = e / jnp.maximum(jnp.sum(e, axis=1, keepdims=True), 1e-30)
    o_c = per_seq(lambda q: _dot(rows_of(p_c, q).astype(BF), kvc_ref[q, :, 128:256].astype(BF)))

    imp = jnp.zeros((n_rows, 128), F32)
    for grp in range(n_rows // GROUP):
        in_grp = jnp.right_shift(row, 2) == grp
        imp = jnp.where(in_grp, jnp.sum(jnp.where(in_grp, p_c, 0.0), axis=0, keepdims=True), imp)
    cur = nb_past
    forced = (lane == 0) | (lane == cur) | (lane == cur - 1)
    score = jnp.where(lane <= cur, imp + FORCE_SCORE * forced.astype(F32), -1.0)
    cnt = jnp.zeros((n_rows, 128), F32)
    for bp in range(nb_past + 1):
        other = score[:, bp:bp + 1]
        ahead = (other > score) | ((other == score) & (bp < lane))
        cnt = cnt + jnp.where(ahead, 1.0, 0.0)
    sel = (cnt < float(N_SELECT)) & (score >= 0.0)
    selm1 = jnp.where(sel, 0.0, -1.0).astype(BF)

    qaug = jnp.concatenate([qb, selm1], axis=1)
    s_s = per_seq(lambda q: _dot(rows_of(qaug, q), kaug_ref[q]))
    kv_n = new_key(ksn_ref)
    s_n = jnp.sum(qf * kv_n[:, 0:128], axis=1, keepdims=True)
    m_s = jnp.maximum(jnp.max(s_s, axis=1, keepdims=True), s_n)
    e_s = jnp.exp(s_s - m_s)
    e_n = jnp.exp(s_n - m_s)
    l_s = jnp.sum(e_s, axis=1, keepdims=True) + e_n
    pv = per_seq(lambda q: _dot_nt(rows_of(e_s, q).astype(BF), vall_ref[q]))
    o_s = (pv + e_n.astype(BF).astype(F32) * kv_n[:, 128:256]) / l_s

    s_w = per_seq(lambda q: _dot(rows_of(qb, q), win_ref[q, 0:KV_HALF, :].astype(BF)))
    lane_w = lax.broadcasted_iota(I32, (n_rows, win_buf), 1)
    s_w = jnp.where(lane_w >= win_buf + 1 - WINDOW, s_w, NEG_INF)
    kv_n = new_key(kwn_ref)
    s_n = jnp.sum(qf * kv_n[:, 0:128], axis=1, keepdims=True)
    m_w = jnp.maximum(jnp.max(s_w, axis=1, keepdims=True), s_n)
    e_w = jnp.exp(s_w - m_w)
    e_n = jnp.exp(s_n - m_w)
    l_w = jnp.sum(e_w, axis=1, keepdims=True) + e_n
    pv = per_seq(lambda q: _dot_nt(rows_of(e_w, q).astype(BF), win_ref[q, KV_HALF:KV_WIDTH, :].astype(BF)))
    o_w = (pv + e_n.astype(BF).astype(F32) * kv_n[:, 128:256]) / l_w

    g = g_ref[...].reshape(n_rows, 3)
    o_ref[...] = (g[:, 0:1] * o_c + g[:, 1:2] * o_s + g[:, 2:3] * o_w).reshape(seq_step, 8, KV_HALF)

    seq_lane = lax.broadcasted_iota(I32, kwnT_ref.shape, 1)
    row_lane = lax.broadcasted_iota(I32, (KV_WIDTH, win_buf), 1)
    for q in range(seq_step):
        n = pl.program_id(0) * seq_step + q
        new_col = jnp.sum(jnp.where(seq_lane == n, kwnT_ref[...], 0.0), axis=1, keepdims=True)
        shifted = pltpu.roll(win_ref[q], win_buf - 1, axis=1)
        nwin_ref[q] = jnp.where(row_lane == win_buf - 1, new_col, shifted)


def _attention_sample(page_table_flat, cache_t, qblk, kvc_pad, win_t, kvs_new, kvw_new, kvw_new_t, gates8,
                      n_pages, past_len):
    n_seq = qblk.shape[0]
    win_buf = win_t.shape[2]
    g = SAMPLE_SEQ_STEP
    per_step = lambda n, pt: (n, 0, 0)

    def page_map(k):
        return lambda n, pt: (pt[n * g * n_pages + k], 0, 0)

    grid_spec = pltpu.PrefetchScalarGridSpec(
        num_scalar_prefetch=1,
        grid=(n_seq // g,),
        in_specs=[pl.BlockSpec((1, KV_WIDTH, PAGE_SIZE), page_map(k)) for k in range(g * n_pages)]
        + [pl.BlockSpec((g, 8, 128), per_step),
           pl.BlockSpec((g, 128, KV_WIDTH), per_step),
           pl.BlockSpec((g, KV_WIDTH, win_buf), per_step),
           pl.BlockSpec((g, 1, KV_WIDTH), per_step),
           pl.BlockSpec((g, 1, KV_WIDTH), per_step),
           pl.BlockSpec(kvw_new_t.shape, lambda n, pt: (0, 0)),
           pl.BlockSpec((g, 8, 3), per_step)],
        out_specs=(pl.BlockSpec((g, 8, 128), per_step),
                   pl.BlockSpec((g, KV_WIDTH, win_buf), per_step)),
        scratch_shapes=[pltpu.VMEM((g, 256, past_len), BF), pltpu.VMEM((g, KV_HALF, past_len), BF)],
    )
    return pl.pallas_call(
        functools.partial(_sattn_kernel, n_pages=n_pages, past_len=past_len, seq_step=g),
        out_shape=(jax.ShapeDtypeStruct((n_seq, 8, 128), F32),
                   jax.ShapeDtypeStruct((n_seq, KV_WIDTH, win_buf), F32)),
        grid_spec=grid_spec,
        compiler_params=_cparams("arbitrary"),
        name="attention_sample",
    )(page_table_flat, *([cache_t] * (g * n_pages)), qblk, kvc_pad, win_t, kvs_new, kvw_new, kvw_new_t, gates8)


def _kv_rows_view(kv_t):
    n, _, t = kv_t.shape
    return jnp.transpose(kv_t.reshape(n, 2, N_KV_HEADS, HEAD_DIM, t), (0, 4, 1, 2, 3))


def _kv_feat_view(kv):
    n, t = kv.shape[:2]
    return jnp.transpose(kv, (0, 2, 3, 4, 1)).reshape(n, KV_WIDTH, t)


def kernel(x_prompt, x_sample, cache_cmp_kv, cache_sel_kv, state_win_kv, state_pool, page_table, c_prompt, c_sample, w_ada, b_ada, g_pre_mix, g_post_mix, g_pre_mlp, g_post_mlp, w_in, w_cmp, pos_cmp, w_pool, pool_scale, w_up_nsa, w_up_pool, w_o, w_ff1, w_ff2):
    n_batch, seq, _ = x_prompt.shape
    n_seq = x_sample.shape[0]
    n_pages = page_table.shape[1]
    past_len = n_pages * PAGE_SIZE
    nb_past = past_len // BLOCK
    assert x_sample.shape[1] == 1 and w_ada.shape[0] == 1 and past_len % BLOCK == 0
    assert seq % TOKEN_TILE == 0 and seq // BLOCK == 128 and state_win_kv.shape[2] == WINDOW
    assert (n_seq * nb_past) % CMP_GROUP == 0 and n_seq == CMP_GROUP
    assert FORCE_SCORE > GROUP

    w_t = w_in[0].T
    wt = jnp.pad(w_t[0:1304], ((0, 8), (0, 0))).astype(BF)
    wn = jnp.concatenate([w_t[768:896], w_t[1024:1152], w_t[1304:1816], w_t[512:768]], axis=0).T.astype(BF)
    ws = jnp.pad(w_t[0:1816], ((0, 104), (0, 0))).T.astype(BF)
    wgm = w_t[1816:3864].T.astype(BF)
    eye = jnp.eye(N_KV_HEADS, dtype=F32)
    wc = w_cmp[0].astype(BF)
    wz = jnp.zeros_like(wc)
    w4 = jnp.concatenate([jnp.concatenate([wc, wz], axis=3), jnp.concatenate([wz, wc], axis=3)], axis=2)
    w4 = w4.reshape(2, CMP_SPLIT, CMP_K // CMP_SPLIT, KV_HALF)
    pos4 = jnp.broadcast_to(jnp.transpose(pos_cmp[0], (1, 0, 2))[:, :, None, :],
                            (2, BLOCK, N_KV_HEADS, HEAD_DIM)).reshape(2, CMP_SPLIT, 1, CMP_K // CMP_SPLIT)
    wp = w_pool[0].astype(BF)
    ps = pool_scale[0].reshape(1, POOL_WIDTH)
    wun, wup, wo = w_up_nsa[0].astype(BF), w_up_pool[0].astype(BF), w_o[0].astype(BF)
    w1, w2 = w_ff1[0].astype(BF), w_ff2[0].astype(BF)
    gpm, gqm = g_pre_mix[0].reshape(1, D_MODEL), g_post_mix[0].reshape(1, D_MODEL)
    gpf, gqf = g_pre_mlp[0].reshape(1, D_MODEL), g_post_mlp[0].reshape(1, D_MODEL)

    n_c = n_batch + n_seq
    c_all = jnp.pad(jnp.concatenate([c_prompt, c_sample], axis=0), ((0, (-n_c) % 8), (0, 0)))
    ada = _adaln(c_all, w_ada[0], b_ada[0].reshape(1, -1))
    ada_p = ada[:n_batch].reshape(n_batch, 6, 1, D_MODEL)
    ada_s = ada[n_batch:n_c].reshape(n_seq, 6, D_MODEL)
    mods_p = [ada_p[:, k] for k in range(6)]
    mods_s = [ada_s[:, k][None] for k in range(6)]

    xp = x_prompt.reshape(n_batch * seq, D_MODEL)
    kvcT, kvsT, kvwT, kvc_a, kvc_b, u_p, qT, kaug, vsT, kw, vwT, gT = _project_prompt(
        xp, mods_p[0], mods_p[1], gpm, wn, wt, n_batch, seq)
    kvc_blk = _compress(kvc_a, kvc_b, pos4, w4).reshape(n_batch, seq // BLOCK, KV_WIDTH)
    kc = kvc_blk[:, :, 0:128].astype(BF)
    vcT = jnp.swapaxes(kvc_blk[:, :, 128:256], 1, 2).astype(BF)
    onsa_p = _attention_prompt(qT, kaug, vsT, kw, vwT, kc, vcT, gT, n_batch, seq)
    opool_p = _pool_prompt(u_p, wp, ps, n_batch, seq)
    x1_p = _token_call(_merge_kernel, "merge_prompt", xp, [mods_p[0], mods_p[1], mods_p[2]], TOKEN_TILE, seq,
                       [onsa_p.reshape(n_batch * seq, NSA_WIDTH), opool_p], [gpm, gqm, wgm, wun, wup, wo])
    y_p = _token_call(_mlp_kernel, "mlp_prompt", x1_p, [mods_p[3], mods_p[4], mods_p[5]], TOKEN_TILE, seq,
                      [], [gpf, gqf, w1, w2])

    xs = x_sample.reshape(n_seq, D_MODEL)
    z, zsig, zT = _project_sample(xs, mods_s[0][0], mods_s[1][0], gpm, ws, wt)
    q_s = z[:, 0:512] * (HEAD_DIM ** -0.5)
    kvc_n, kvs_n, kvw_n = z[:, 512:768], z[:, 768:1024], z[:, 1024:1280]
    gates_s = zsig[:, 1280:1304].reshape(n_seq, N_HEADS, 3)
    u_s = z[:, 1304:1816]
    q5 = q_s.reshape(n_seq, N_KV_HEADS, GROUP, 1, HEAD_DIM) * eye[None, :, None, :, None]
    qblk = q5.reshape(n_seq, N_HEADS, KV_HALF)

    pt_flat = page_table.reshape(-1)
    kvc_past = _compress_paged(_kv_feat_view(cache_cmp_kv[0]), pt_flat, pos4, w4, n_seq, n_pages)
    last_a = jnp.pad(kvc_n[:, None, 0:128], ((0, 0), (0, BLOCK - 1), (0, 0))).reshape(n_seq * BLOCK, KV_HALF)
    last_b = jnp.pad(kvc_n[:, None, 128:256], ((0, 0), (0, BLOCK - 1), (0, 0))).reshape(n_seq * BLOCK, KV_HALF)
    kvc_last = _compress(last_a, last_b, pos4, w4)
    kvc_s = jnp.concatenate([kvc_past.reshape(n_seq, nb_past, KV_WIDTH), kvc_last[:, None, :]], axis=1)
    kvc_pad = jnp.pad(kvc_s, ((0, 0), (0, 128 - nb_past - 1), (0, 0)))
    o8, new_win_t = _attention_sample(
        pt_flat, _kv_feat_view(cache_sel_kv[0]), qblk, kvc_pad, _kv_feat_view(state_win_kv[0]),
        kvs_n[:, None, :], kvw_n[:, None, :], zT[1024:1280], gates_s, n_pages, past_len)
    o5 = o8.reshape(n_seq, N_KV_HEADS, GROUP, N_KV_HEADS, HEAD_DIM)
    onsa_s = jnp.concatenate([o5[:, 0, :, 0, :], o5[:, 1, :, 1, :]], axis=1).reshape(n_seq, NSA_WIDTH).astype(BF)
    opool_s = _pool_sample(u_s, jnp.swapaxes(state_pool[0], 0, 1), wp, ps)
    x1_s = _token_call(_merge_kernel, "merge_sample", xs, [mods_s[0], mods_s[1], mods_s[2]], n_seq, n_seq,
                       [onsa_s, opool_s], [gpm, gqm, wgm, wun, wup, wo])
    y_s = _token_call(_mlp_kernel, "mlp_sample", x1_s, [mods_s[3], mods_s[4], mods_s[5]], n_seq, n_seq,
                      [], [gpf, gqf, w1, w2])

    win_p = min(WINDOW, seq)
    new_kv_s = lambda rows: _kv_rows_view(rows.reshape(1, KV_WIDTH, n_seq))[0][None, :, None]
    return (
        y_p.reshape(n_batch, seq, D_MODEL),
        y_s.reshape(n_seq, 1, D_MODEL),
        _kv_rows_view(kvcT)[None],
        _kv_rows_view(kvsT)[None],
        _kv_rows_view(kvwT[:, :, seq - win_p:])[None],
        u_p.reshape(n_batch, seq, POOL_WIDTH)[None, :, seq - POOL_HIST:],
        new_kv_s(zT[512:768]),
        new_kv_s(zT[768:1024]),
        _kv_rows_view(new_win_t)[None],
        jnp.concatenate([state_pool[0][:, 1:], u_s[:, None, :]], axis=1)[None],
    )
```

```python
import functools
import math

import jax
import jax.numpy as jnp
from jax import lax
from jax.experimental import pallas as pl
from jax.experimental.pallas import tpu as pltpu

D_MODEL = 1024
N_HEADS = 8
HEAD_DIM = 64
N_KV_HEADS = 2
GROUP = N_HEADS // N_KV_HEADS
BLOCK = 64
N_SELECT = 16
WINDOW = 512
Q_TILE = 256
WIN_CHUNK = 128
NSA_WIDTH = N_HEADS * HEAD_DIM
KV_WIDTH = 2 * N_KV_HEADS * HEAD_DIM
KV_HALF = N_KV_HEADS * HEAD_DIM
FORCE_SCORE = 16.0
N_FORCED = 3
POOL_WINDOWS = (2, 4, 8, 16)
POOL_WIDTH = 512
POOL_GROUP_DIM = 128
POOL_HIST = 15
D_FF = 4 * D_MODEL
EPS = 1e-6
PAGE_SIZE = 128
V_ROWS = HEAD_DIM + 16
CMP_K = BLOCK * KV_HALF
CMP_SPLIT = 4
CMP_GROUP = 128
CMP_PITCH = BLOCK + 4

BF = jnp.bfloat16
F32 = jnp.float32
I32 = jnp.int32
MASK_BIG = 2.0 ** 100
NEG_INF = float("-inf")
LOG2E = math.log2(math.e)

TOKEN_TILE = 512
PROJ_TILE = 256
SEL_CHUNK = 512
SAMPLE_SEQ_STEP = 4
VMEM_LIMIT = 56 * 1024 * 1024


def _cparams(*sem):
    return pltpu.CompilerParams(dimension_semantics=sem, vmem_limit_bytes=VMEM_LIMIT)


def _rms(x, g):
    return x * lax.rsqrt(jnp.mean(x * x, axis=-1, keepdims=True) + EPS) * g


def _dot(a, b):
    return jnp.dot(a, b, preferred_element_type=F32)


def _dot_nt(a, b):
    return lax.dot_general(a, b, (((1,), (1,)), ((), ())), preferred_element_type=F32)


def _ada_kernel(c_ref, w_ref, b_ref, o_ref):
    c = c_ref[...]
    a = (c * jax.nn.sigmoid(c)).astype(BF)
    o_ref[...] = _dot(a, w_ref[...].astype(BF)) + b_ref[...]


def _adaln(c_all, w_ada, b_ada):
    rows = c_all.shape[0]
    n_out = w_ada.shape[1]
    tn = 512
    return pl.pallas_call(
        _ada_kernel,
        out_shape=jax.ShapeDtypeStruct((rows, n_out), F32),
        grid=(n_out // tn,),
        in_specs=[pl.BlockSpec((rows, D_MODEL), lambda j: (0, 0)),
                  pl.BlockSpec((D_MODEL, tn), lambda j: (0, j)),
                  pl.BlockSpec((1, tn), lambda j: (0, j))],
        out_specs=pl.BlockSpec((rows, tn), lambda j: (0, j)),
        compiler_params=_cparams("arbitrary"),
        name="adaln",
    )(c_all, w_ada, b_ada)


def _proj_kernel(x_ref, sh_ref, sc_ref, g_ref, wn_ref, wt_ref,
                 kvcT_ref, kvsT_ref, kvwT_ref, ca_ref, cb_ref, u_ref,
                 qT_ref, kaug_ref, vsT_ref, kw_ref, vwT_ref, gT_ref, *, tm, tpb):
    x = x_ref[...]
    h = _rms(x, g_ref[...]) * (1.0 + sc_ref[0]) + sh_ref[0]
    hb = h.astype(BF)
    zn = _dot(hb, wn_ref[...])
    zt = _dot_nt(wt_ref[...], hb)
    kvcT_ref[0] = zt[512:768]
    kvsT_ref[0] = zt[768:1024]
    kvwT_ref[0] = zt[1024:1280]
    ca_ref[...] = zn[:, 768:896]
    cb_ref[...] = zn[:, 896:1024]
    u_ref[...] = zn[:, 256:768]
    qT_ref[0] = (zt[0:512] * (HEAD_DIM ** -0.5 * LOG2E)).astype(BF)
    ones = jnp.ones((V_ROWS - HEAD_DIM, tm), BF)
    for h in range(N_KV_HEADS):
        vs_h = zt[896 + h * HEAD_DIM:896 + (h + 1) * HEAD_DIM].astype(BF)
        vw_h = zt[1152 + h * HEAD_DIM:1152 + (h + 1) * HEAD_DIM].astype(BF)
        vsT_ref[0, 0, h, 0:HEAD_DIM, :] = vs_h
        vsT_ref[0, 0, h, HEAD_DIM:V_ROWS, :] = ones
        for c in range(tm // WIN_CHUNK):
            cs = slice(c * WIN_CHUNK, (c + 1) * WIN_CHUNK)
            vwT_ref[0, c, h, 0:HEAD_DIM, :] = vw_h[:, cs]
            vwT_ref[0, c, h, HEAD_DIM:V_ROWS, :] = ones[:, cs]
    gT_ref[0] = jax.nn.sigmoid(zt[1280:1312])
    t0 = (pl.program_id(0) % tpb) * tm
    blk = jnp.right_shift(t0 + lax.broadcasted_iota(I32, (tm, 128), 0), 6)
    lane = lax.broadcasted_iota(I32, (tm, 128), 1)
    kaug_ref[0, :, 0:128] = zn[:, 0:128].astype(BF)
    kaug_ref[0, :, 128:256] = jnp.where(blk == lane, MASK_BIG, 0.0).astype(BF)
    kw_ref[0] = zn[:, 128:256].astype(BF)


def _proj_cmp_kernel(pt_ref, *refs, n_pages_step, tm, tpb):
    proj_in = refs[:6]
    pages = refs[6:6 + n_pages_step]
    pos_ref, w_ref = refs[6 + n_pages_step:8 + n_pages_step]
    proj_out = refs[8 + n_pages_step:20 + n_pages_step]
    o_ref, sa_ref, sb_ref = refs[20 + n_pages_step:]
    _proj_kernel(*proj_in, *proj_out, tm=tm, tpb=tpb)
    blocks_per_page = PAGE_SIZE // BLOCK
    for k in range(n_pages_step):
        pg = pages[k][0]
        for s, dst in enumerate((sa_ref, sb_ref)):
            rows = pg[s * KV_HALF:(s + 1) * KV_HALF, :].T
            for b in range(blocks_per_page):
                m = k * blocks_per_page + b
                dst[m * CMP_PITCH:m * CMP_PITCH + BLOCK, :] = rows[b * BLOCK:(b + 1) * BLOCK, :]
    _compress_rows((sa_ref, sb_ref), pos_ref, w_ref, o_ref, n_pages_step * blocks_per_page, CMP_PITCH)


def _project_prompt_compress_cache(x2, shift, scale, g, wn, wt, n_batch, seq,
                                   cache_t, page_table_flat, pos4, w4):
    tm = PROJ_TILE
    tpb = seq // tm
    nt = n_batch * seq
    n_steps = nt // tm
    blocks_per_page = PAGE_SIZE // BLOCK
    n_pages_all = page_table_flat.shape[0]
    n_pages_step = n_pages_all // n_steps
    assert n_pages_step * n_steps == n_pages_all
    blocks_step = n_pages_step * blocks_per_page
    chunk_steps = SEL_CHUNK // tm
    tok = lambda t, pt: (t, 0)
    per_b = lambda t, pt: (t // tpb, 0, 0)
    featT = lambda t, pt: (t // tpb, 0, t % tpb)
    rows3 = lambda t, pt: (t // tpb, t % tpb, 0)
    rows5 = lambda t, pt: (t // tpb, t % tpb, 0, 0, 0)
    chunk5 = lambda t, pt: (t // tpb, (t % tpb) // chunk_steps, 0, 0, t % chunk_steps)
    const2 = lambda t, pt: (0, 0)
    const4 = lambda t, pt: (0, 0, 0, 0)

    def page_map(k):
        return lambda t, pt: (pt[t * n_pages_step + k], 0, 0)

    kvT = jax.ShapeDtypeStruct((n_batch, KV_WIDTH, seq), F32)
    out_shape = (
        kvT, kvT, kvT,
        jax.ShapeDtypeStruct((nt, KV_HALF), F32),
        jax.ShapeDtypeStruct((nt, KV_HALF), F32),
        jax.ShapeDtypeStruct((nt, POOL_WIDTH), F32),
        jax.ShapeDtypeStruct((n_batch, NSA_WIDTH, seq), BF),
        jax.ShapeDtypeStruct((n_batch, seq, 256), BF),
        jax.ShapeDtypeStruct((n_batch, seq // SEL_CHUNK, N_KV_HEADS, V_ROWS, SEL_CHUNK), BF),
        jax.ShapeDtypeStruct((n_batch, seq, KV_HALF), BF),
        jax.ShapeDtypeStruct((n_batch, seq // WIN_CHUNK, N_KV_HEADS, V_ROWS, WIN_CHUNK), BF),
        jax.ShapeDtypeStruct((n_batch, 32, seq), F32),
        jax.ShapeDtypeStruct((n_pages_all * blocks_per_page, KV_WIDTH), F32),
    )
    out_specs = (
        pl.BlockSpec((1, KV_WIDTH, tm), featT),
        pl.BlockSpec((1, KV_WIDTH, tm), featT),
        pl.BlockSpec((1, KV_WIDTH, tm), featT),
        pl.BlockSpec((tm, KV_HALF), tok),
        pl.BlockSpec((tm, KV_HALF), tok),
        pl.BlockSpec((tm, POOL_WIDTH), tok),
        pl.BlockSpec((1, NSA_WIDTH, tm), featT),
        pl.BlockSpec((1, tm, 256), rows3),
        pl.BlockSpec((1, 1, N_KV_HEADS, V_ROWS, tm), chunk5),
        pl.BlockSpec((1, tm, KV_HALF), rows3),
        pl.BlockSpec((1, tm // WIN_CHUNK, N_KV_HEADS, V_ROWS, WIN_CHUNK), rows5),
        pl.BlockSpec((1, 32, tm), featT),
        pl.BlockSpec((blocks_step, KV_WIDTH), tok),
    )
    grid_spec = pltpu.PrefetchScalarGridSpec(
        num_scalar_prefetch=1,
        grid=(n_steps,),
        in_specs=[pl.BlockSpec((tm, D_MODEL), tok),
                  pl.BlockSpec((1, 1, D_MODEL), per_b),
                  pl.BlockSpec((1, 1, D_MODEL), per_b),
                  pl.BlockSpec((1, D_MODEL), const2),
                  pl.BlockSpec(wn.shape, const2),
                  pl.BlockSpec(wt.shape, const2)]
        + [pl.BlockSpec((1, KV_WIDTH, PAGE_SIZE), page_map(k)) for k in range(n_pages_step)]
        + [pl.BlockSpec(pos4.shape, const4), pl.BlockSpec(w4.shape, const4)],
        out_specs=out_specs,
        scratch_shapes=[pltpu.VMEM((blocks_step * CMP_PITCH, KV_HALF), F32),
                        pltpu.VMEM((blocks_step * CMP_PITCH, KV_HALF), F32)],
    )
    return pl.pallas_call(
        functools.partial(_proj_cmp_kernel, n_pages_step=n_pages_step, tm=tm, tpb=tpb),
        out_shape=out_shape,
        grid_spec=grid_spec,
        compiler_params=_cparams("arbitrary"),
        name="project_prompt_compress_cache",
    )(page_table_flat, x2, shift, scale, g, wn, wt, *([cache_t] * n_pages_step), pos4, w4)


def _compress_rows(src_refs, pos_ref, w_ref, o_ref, n_blocks, pitch):
    rows_per_slice = BLOCK // CMP_SPLIT
    for s in range(2):
        acc = jnp.zeros((n_blocks, KV_HALF), F32)
        for c in range(CMP_SPLIT):
            xc = jnp.concatenate(
                [src_refs[s][pl.ds(c * rows_per_slice + r, n_blocks, stride=pitch), :]
                 for r in range(rows_per_slice)], axis=1) + pos_ref[s, c]
            acc = acc + _dot(xc.astype(BF), w_ref[s, c])
        o_ref[:, s * KV_HALF:(s + 1) * KV_HALF] = acc


def _cmp_kernel(xa_ref, xb_ref, pos_ref, w_ref, o_ref):
    _compress_rows((xa_ref, xb_ref), pos_ref, w_ref, o_ref, CMP_GROUP, BLOCK)


def _compress(xa, xb, pos4, w4):
    m = xa.shape[0] // BLOCK
    rows = CMP_GROUP * BLOCK
    return pl.pallas_call(
        _cmp_kernel,
        out_shape=jax.ShapeDtypeStruct((m, KV_WIDTH), F32),
        grid=(m // CMP_GROUP,),
        in_specs=[pl.BlockSpec((rows, KV_HALF), lambda i: (i, 0)),
                  pl.BlockSpec((rows, KV_HALF), lambda i: (i, 0)),
                  pl.BlockSpec(pos4.shape, lambda i: (0, 0, 0, 0)),
                  pl.BlockSpec(w4.shape, lambda i: (0, 0, 0, 0))],
        out_specs=pl.BlockSpec((CMP_GROUP, KV_WIDTH), lambda i: (i, 0)),
        compiler_params=_cparams("arbitrary"),
        name="compress_blocks",
    )(xa, xb, pos4, w4)


def _attn_kernel(qT_ref, kaug_ref, vsT_ref, kw_ref, vwT_ref, kc_ref, vcT_ref, gT_ref, o_ref,
                 qa_ref, sa_ref, sb_ref, sc_ref, m_ref, acc_ref, outT_ref):
    i = pl.program_id(1)
    q0 = i * Q_TILE
    n_full = lax.div(q0, SEL_CHUNK)
    n_blk = kc_ref.shape[1]
    row = lax.broadcasted_iota(I32, (n_blk, Q_TILE), 0)
    tok = lax.broadcasted_iota(I32, (n_blk, Q_TILE), 1)
    qpos = q0 + tok
    cur = jnp.right_shift(qpos, 6)
    cmask = (row + 1) * BLOCK - 1 <= qpos
    valid = row <= cur
    forced = (row == 0) | (row == cur) | (row == cur - 1)
    key_c = lax.broadcasted_iota(I32, (SEL_CHUNK, Q_TILE), 0)
    qpos_c = q0 + lax.broadcasted_iota(I32, (SEL_CHUNK, Q_TILE), 1)
    hs = [slice(h * HEAD_DIM, (h + 1) * HEAD_DIM) for h in range(N_KV_HEADS)]

    def bias4(keep):
        b = jnp.where(keep, 0.0, -MASK_BIG)
        return jnp.concatenate([b] * GROUP, axis=1)

    def online_step(st, s, v_t):
        m_old = m_ref[st]
        m_new = jnp.maximum(m_old, jnp.max(s, axis=0, keepdims=True))
        alpha = jnp.exp2(m_old - m_new)
        p = jnp.exp2(s - m_new)
        acc_ref[st] = alpha * acc_ref[st] + _dot(v_t, p.astype(BF))
        m_ref[st] = m_new

    m_ref[...] = jnp.full(m_ref.shape, NEG_INF, F32)
    acc_ref[...] = jnp.zeros(acc_ref.shape, F32)

    o_c = []
    scores = []
    for h in range(N_KV_HEADS):
        qa_ref[h] = jnp.zeros(qa_ref.shape[1:], BF)
        for g in range(GROUP):
            r0 = h * GROUP * HEAD_DIM + g * HEAD_DIM
            qa_ref[h, hs[h], g * Q_TILE:(g + 1) * Q_TILE] = qT_ref[0, r0:r0 + HEAD_DIM, :]
        sc = _dot(kc_ref[0], qa_ref[h, 0:KV_HALF, :])
        imp = jnp.zeros((n_blk, Q_TILE), F32)
        p_parts = []
        for g in range(GROUP):
            s = jnp.where(cmask, sc[:, g * Q_TILE:(g + 1) * Q_TILE], NEG_INF)
            mx = jnp.max(s, axis=0, keepdims=True)
            mx = jnp.where(mx > NEG_INF, mx, 0.0)
            e = jnp.where(cmask, jnp.exp2(s - mx), 0.0)
            p = e / jnp.maximum(jnp.sum(e, axis=0, keepdims=True), 1e-30)
            imp = imp + p
            p_parts.append(p)
        o_c.append(_dot(vcT_ref[0, hs[h], :], jnp.concatenate(p_parts, axis=1).astype(BF)))
        scores.append(jnp.where(valid, jnp.where(forced, -2.0, imp), -1.0))

    blk_f = lax.broadcasted_iota(I32, (n_blk, N_KV_HEADS * Q_TILE), 0).astype(F32)

    def pick_body(k, work):
        best = jnp.max(work, axis=0, keepdims=True)
        first = jnp.min(jnp.where(work == best, blk_f, float(n_blk)), axis=0, keepdims=True)
        return jnp.where((blk_f == first) & (best >= 0.0), -2.0, work)

    work = lax.fori_loop(0, N_SELECT - N_FORCED, pick_body, jnp.concatenate(scores, axis=1))
    selm1 = jnp.where(work == -2.0, 0.0, -1.0).astype(BF)
    for h in range(N_KV_HEADS):
        for g in range(GROUP):
            qa_ref[h, KV_HALF:KV_HALF + n_blk, g * Q_TILE:(g + 1) * Q_TILE] = selm1[:, h * Q_TILE:(h + 1) * Q_TILE]

    last_chunk = kaug_ref.shape[1] // SEL_CHUNK - 1

    def sel_scores(j, dst_ref):
        j = jnp.minimum(j, last_chunk)
        kt = kaug_ref[0, pl.ds(pl.multiple_of(j * SEL_CHUNK, SEL_CHUNK), SEL_CHUNK), :]
        for h in range(N_KV_HEADS):
            dst_ref[h] = _dot(kt, qa_ref[h])

    def sel_process(src_ref, j, causal):
        if causal:
            cbias = bias4(j * SEL_CHUNK + key_c <= qpos_c)
        for h in range(N_KV_HEADS):
            s = src_ref[h]
            if causal:
                s = s + cbias
            online_step(h, s, vsT_ref[0, j, h])

    sel_scores(0, sa_ref)
    sel_scores(1, sb_ref)

    w0 = jnp.maximum(q0 - WINDOW, 0)
    j0 = lax.div(w0, WIN_CHUNK)
    jq = lax.div(q0, WIN_CHUNK)
    delta = qpos_c - (w0 + key_c)
    wbias = bias4((delta >= 0) & (delta < WINDOW))
    key_d = lax.broadcasted_iota(I32, (Q_TILE, Q_TILE), 0)
    tok_d = lax.broadcasted_iota(I32, (Q_TILE, Q_TILE), 1)
    dbias = bias4((key_d <= tok_d) & (q0 >= WINDOW))
    kw_a = kw_ref[0, pl.ds(pl.multiple_of(w0, WIN_CHUNK), WINDOW), :]
    kw_b = kw_ref[0, pl.ds(pl.multiple_of(q0, Q_TILE), Q_TILE), :]
    for h in range(N_KV_HEADS):
        qf = qa_ref[h, 0:KV_HALF, :]
        v_a = jnp.concatenate([vwT_ref[0, j0 + c, h] for c in range(WINDOW // WIN_CHUNK)], axis=1)
        v_b = jnp.concatenate([vwT_ref[0, jq + c, h] for c in range(Q_TILE // WIN_CHUNK)], axis=1)
        online_step(2 + h, _dot(kw_a, qf) + wbias, v_a)
        online_step(2 + h, _dot(kw_b, qf) + dbias, v_b)

    def trio_body(t, carry):
        j = 3 * t
        sel_scores(j + 2, sc_ref)
        sel_process(sa_ref, j, False)
        sel_scores(j + 3, sa_ref)
        sel_process(sb_ref, j + 1, False)
        sel_scores(j + 4, sb_ref)
        sel_process(sc_ref, j + 2, False)
        return carry

    n_trios = lax.div(n_full, 3)
    lax.fori_loop(0, n_trios, trio_body, 0)
    j_last = 3 * n_trios
    n_left = n_full - j_last

    @pl.when(n_left == 0)
    def _():
        sel_process(sa_ref, j_last, True)

    @pl.when(n_left == 1)
    def _():
        sel_process(sa_ref, j_last, False)
        sel_process(sb_ref, j_last + 1, True)

    @pl.when(n_left == 2)
    def _():
        sel_scores(j_last + 2, sc_ref)
        sel_process(sa_ref, j_last, False)
        sel_process(sb_ref, j_last + 1, False)
        sel_process(sc_ref, j_last + 2, True)

    for h in range(N_KV_HEADS):
        o_s = acc_ref[h, 0:HEAD_DIM, :] / acc_ref[h, HEAD_DIM:HEAD_DIM + 1, :]
        o_w = acc_ref[2 + h, 0:HEAD_DIM, :] / acc_ref[2 + h, HEAD_DIM:HEAD_DIM + 1, :]
        for g in range(GROUP):
            gs = slice(g * Q_TILE, (g + 1) * Q_TILE)
            gr = h * GROUP * 3 + g * 3
            og = (gT_ref[0, gr:gr + 1, :] * o_c[h][:, gs] + gT_ref[0, gr + 1:gr + 2, :] * o_s[:, gs]
                  + gT_ref[0, gr + 2:gr + 3, :] * o_w[:, gs])
            r0 = h * GROUP * HEAD_DIM + g * HEAD_DIM
            outT_ref[r0:r0 + HEAD_DIM, :] = og

    o_ref[0] = outT_ref[...].T.astype(BF)


def _attention_prompt(qT, kaug, vsT, kw, vwT, kc, vcT, gT, n_batch, seq):
    per_b3 = lambda n, i: (n, 0, 0)
    per_b5 = lambda n, i: (n, 0, 0, 0, 0)
    rows = GROUP * Q_TILE
    return pl.pallas_call(
        _attn_kernel,
        out_shape=jax.ShapeDtypeStruct((n_batch, seq, NSA_WIDTH), BF),
        grid=(n_batch, seq // Q_TILE),
        in_specs=[pl.BlockSpec((1, NSA_WIDTH, Q_TILE), lambda n, i: (n, 0, i)),
                  pl.BlockSpec((1, seq, 256), per_b3),
                  pl.BlockSpec((1, seq // SEL_CHUNK, N_KV_HEADS, V_ROWS, SEL_CHUNK), per_b5),
                  pl.BlockSpec((1, seq, KV_HALF), per_b3),
                  pl.BlockSpec((1, seq // WIN_CHUNK, N_KV_HEADS, V_ROWS, WIN_CHUNK), per_b5),
                  pl.BlockSpec((1, seq // BLOCK, KV_HALF), per_b3),
                  pl.BlockSpec((1, KV_HALF, seq // BLOCK), per_b3),
                  pl.BlockSpec((1, 32, Q_TILE), lambda n, i: (n, 0, i))],
        out_specs=pl.BlockSpec((1, Q_TILE, NSA_WIDTH), lambda n, i: (n, i, 0)),
        scratch_shapes=[pltpu.VMEM((N_KV_HEADS, 256, rows), BF),
                        pltpu.VMEM((N_KV_HEADS, SEL_CHUNK, rows), F32),
                        pltpu.VMEM((N_KV_HEADS, SEL_CHUNK, rows), F32),
                        pltpu.VMEM((N_KV_HEADS, SEL_CHUNK, rows), F32),
                        pltpu.VMEM((2 * N_KV_HEADS, 1, rows), F32),
                        pltpu.VMEM((2 * N_KV_HEADS, V_ROWS, rows), F32),
                        pltpu.VMEM((NSA_WIDTH, Q_TILE), F32)],
        compiler_params=_cparams("arbitrary", "arbitrary"),
        name="attention_prompt",
    )(qT, kaug, vsT, kw, vwT, kc, vcT, gT)


def _pool_kernel(u_ref, halo_ref, wp_ref, ps_ref, o_ref, ext_ref, *, tm, tpb):
    t = pl.program_id(0) % tpb
    ext_ref[0:16, :] = jnp.where(t == 0, 0.0, halo_ref[...])
    u = u_ref[...]
    ext_ref[16:16 + tm, :] = u
    pos = t * tm + lax.broadcasted_iota(I32, (tm, 1), 0)
    outs = []
    for gi, w in enumerate(POOL_WINDOWS):
        cs = slice(gi * POOL_GROUP_DIM, (gi + 1) * POOL_GROUP_DIM)
        acc = u[:, cs]
        for k in range(1, w):
            acc = acc + ext_ref[pl.ds(16 - k, tm), cs]
        cnt = jnp.minimum(pos + 1, w).astype(F32)
        pooled = acc / cnt - u[:, cs]
        outs.append(_dot(pooled.astype(BF), wp_ref[gi]))
    o_ref[...] = (jnp.concatenate(outs, axis=1) * ps_ref[...]).astype(BF)


def _pool_prompt(u, w_pool, pool_scale, n_batch, seq):
    tm = TOKEN_TILE
    tpb = seq // tm
    nt = n_batch * seq
    return pl.pallas_call(
        functools.partial(_pool_kernel, tm=tm, tpb=tpb),
        out_shape=jax.ShapeDtypeStruct((nt, POOL_WIDTH), BF),
        grid=(nt // tm,),
        in_specs=[pl.BlockSpec((tm, POOL_WIDTH), lambda t: (t, 0)),
                  pl.BlockSpec((16, POOL_WIDTH), lambda t: (jnp.maximum(t * (tm // 16) - 1, 0), 0)),
                  pl.BlockSpec(w_pool.shape, lambda t: (0, 0, 0)),
                  pl.BlockSpec((1, POOL_WIDTH), lambda t: (0, 0))],
        out_specs=pl.BlockSpec((tm, POOL_WIDTH), lambda t: (t, 0)),
        scratch_shapes=[pltpu.VMEM((tm + 16, POOL_WIDTH), F32)],
        compiler_params=_cparams("arbitrary"),
        name="pool_prompt",
    )(u, u, w_pool, pool_scale)


def _spool_kernel(u_ref, hist_ref, wp_ref, ps_ref, o_ref):
    u = u_ref[...]
    outs = []
    for gi, w in enumerate(POOL_WINDOWS):
        cs = slice(gi * POOL_GROUP_DIM, (gi + 1) * POOL_GROUP_DIM)
        acc = u[:, cs]
        for k in range(1, w):
            acc = acc + hist_ref[POOL_HIST - k, :, cs]
        pooled = acc / float(w) - u[:, cs]
        outs.append(_dot(pooled.astype(BF), wp_ref[gi]))
    o_ref[...] = (jnp.concatenate(outs, axis=1) * ps_ref[...]).astype(BF)


def _pool_sample(u, hist_t, w_pool, pool_scale):
    n = u.shape[0]
    return pl.pallas_call(
        _spool_kernel,
        out_shape=jax.ShapeDtypeStruct((n, POOL_WIDTH), BF),
        name="pool_sample",
    )(u, hist_t, w_pool, pool_scale)


def _merge_kernel(x_ref, sh_ref, sc_ref, gate_ref, gpre_ref, gpost_ref, onsa_ref, opool_ref,
                  wgm_ref, wun_ref, wup_ref, wo_ref, o_ref):
    x = x_ref[...]
    h = _rms(x, gpre_ref[...]) * (1.0 + sc_ref[0]) + sh_ref[0]
    gm = jax.nn.sigmoid(_dot(h.astype(BF), wgm_ref[...]))
    m = (gm[:, :D_MODEL] * _dot(onsa_ref[...], wun_ref[...])
         + gm[:, D_MODEL:] * _dot(opool_ref[...], wup_ref[...]))
    m = _dot(m.astype(BF), wo_ref[...])
    o_ref[...] = x + gate_ref[0] * _rms(m, gpost_ref[...])


def _mlp_kernel(x_ref, sh_ref, sc_ref, gate_ref, gpre_ref, gpost_ref, w1_ref, w2_ref, o_ref):
    x = x_ref[...]
    h = _rms(x, gpre_ref[...]) * (1.0 + sc_ref[0]) + sh_ref[0]
    hb = h.astype(BF)
    f = jnp.zeros(x.shape, F32)
    fc = 1024
    for c in range(D_FF // fc):
        a = jnp.maximum(_dot(hb, w1_ref[:, c * fc:(c + 1) * fc]), 0.0)
        f = f + _dot((a * a).astype(BF), w2_ref[c * fc:(c + 1) * fc, :])
    o_ref[...] = x + gate_ref[0] * _rms(f, gpost_ref[...])


def _token_call(kernel, name, x2, mods, tm, rows_per_mod, extra_tok, consts):
    nt = x2.shape[0]
    r = mods[0].shape[1]
    mod_spec = pl.BlockSpec((1, r, D_MODEL), lambda t: ((t * tm) // rows_per_mod, 0, 0))
    in_specs = [pl.BlockSpec((tm, D_MODEL), lambda t: (t, 0))] + [mod_spec] * len(mods)
    in_specs += [pl.BlockSpec((1, D_MODEL), lambda t: (0, 0))] * 2
    in_specs += [pl.BlockSpec((tm, a.shape[1]), lambda t: (t, 0)) for a in extra_tok]
    in_specs += [pl.BlockSpec(w.shape, lambda t: (0, 0), pipeline_mode=pl.Buffered(1)) for w in consts[2:]]
    return pl.pallas_call(
        kernel,
        out_shape=jax.ShapeDtypeStruct((nt, D_MODEL), F32),
        grid=(nt // tm,),
        in_specs=in_specs,
        out_specs=pl.BlockSpec((tm, D_MODEL), lambda t: (t, 0)),
        compiler_params=_cparams("arbitrary"),
        name=name,
    )(x2, *mods, consts[0], consts[1], *extra_tok, *consts[2:])


def _sproj_kernel(x_ref, sh_ref, sc_ref, g_ref, wn_ref, wt_ref, z_ref, zs_ref, zT_ref):
    x = x_ref[...]
    h = _rms(x, g_ref[...]) * (1.0 + sc_ref[...]) + sh_ref[...]
    hb = h.astype(BF)
    z = _dot(hb, wn_ref[...])
    z_ref[...] = z
    zs_ref[...] = jax.nn.sigmoid(z)
    zT_ref[...] = _dot_nt(wt_ref[...], hb)


def _project_sample(x2, shift, scale, g, wn, wt):
    n = x2.shape[0]
    shp = jax.ShapeDtypeStruct((n, wn.shape[1]), F32)
    return pl.pallas_call(
        _sproj_kernel,
        out_shape=(shp, shp, jax.ShapeDtypeStruct((wt.shape[0], n), F32)),
        name="project_sample",
    )(x2, shift, scale, g, wn, wt)


def _sattn_init(kaug_ref, past_len, seq_step):
    @pl.when(pl.program_id(0) == 0)
    def _():
        blk = lax.broadcasted_iota(I32, (128, past_len), 0)
        key_blk = jnp.right_shift(lax.broadcasted_iota(I32, (128, past_len), 1), 6)
        onehot = jnp.where(blk == key_blk, MASK_BIG, 0.0).astype(BF)
        for q in range(seq_step):
            kaug_ref[q, 128:256, :] = onehot


def _sattn_main(*refs, n_pages, past_len, seq_step):
    pages = refs[:seq_step * n_pages]
    (qb_ref, kvc_ref, win_ref, ksn_ref, kwn_ref, kwnT_ref, g_ref,
     o_ref, nwin_ref, kaug_ref, vall_ref) = refs[seq_step * n_pages:]

    for q in range(seq_step):
        for p in range(n_pages):
            pg = pages[q * n_pages + p][0]
            kaug_ref[q, 0:128, p * PAGE_SIZE:(p + 1) * PAGE_SIZE] = pg[0:KV_HALF, :].astype(BF)
            vall_ref[q, :, p * PAGE_SIZE:(p + 1) * PAGE_SIZE] = pg[KV_HALF:KV_WIDTH, :].astype(BF)

    nb_past = past_len // BLOCK
    win_buf = win_ref.shape[2]
    n_rows = 8 * seq_step
    per_seq = lambda f: jnp.concatenate([f(q) for q in range(seq_step)], axis=0)
    rows_of = lambda x, q: x[8 * q:8 * (q + 1)]
    qb = qb_ref[...].reshape(n_rows, KV_HALF).astype(BF)
    qf = qb.astype(F32)
    row = lax.broadcasted_iota(I32, (n_rows, 128), 0)
    lane = lax.broadcasted_iota(I32, (n_rows, 128), 1)

    def new_key(ref):
        return per_seq(lambda q: jnp.broadcast_to(ref[q], (8, KV_WIDTH))).astype(BF).astype(F32)

    s_c = per_seq(lambda q: _dot_nt(rows_of(qb, q), kvc_ref[q, :, 0:128].astype(BF)))
    cm = lane < nb_past
    s_c = jnp.where(cm, s_c, NEG_INF)
    mx = jnp.max(s_c, axis=1, keepdims=True)
    mx = jnp.where(mx > NEG_INF, mx, 0.0)
    e = jnp.where(cm, jnp.exp(s_c - mx), 0.0)
    p_c = e / jnp.maximum(jnp.sum(e, axis=1, keepdims=True), 1e-30)
    o_c = per_seq(lambda q: _dot(rows_of(p_c, q).astype(BF), kvc_ref[q, :, 128:256].astype(BF)))

    imp = jnp.zeros((n_rows, 128), F32)
    for grp in range(n_rows // GROUP):
        in_grp = jnp.right_shift(row, 2) == grp
        imp = jnp.where(in_grp, jnp.sum(jnp.where(in_grp, p_c, 0.0), axis=0, keepdims=True), imp)
    cur = nb_past
    forced = (lane == 0) | (lane == cur) | (lane == cur - 1)
    score = jnp.where(lane <= cur, imp + FORCE_SCORE * forced.astype(F32), -1.0)
    cnt = jnp.zeros((n_rows, 128), F32)
    for bp in range(nb_past + 1):
        other = score[:, bp:bp + 1]
        ahead = (other > score) | ((other == score) & (bp < lane))
        cnt = cnt + jnp.where(ahead, 1.0, 0.0)
    sel = (cnt < float(N_SELECT)) & (score >= 0.0)
    selm1 = jnp.where(sel, 0.0, -1.0).astype(BF)

    qaug = jnp.concatenate([qb, selm1], axis=1)
    s_s = per_seq(lambda q: _dot(rows_of(qaug, q), kaug_ref[q]))
    kv_n = new_key(ksn_ref)
    s_n = jnp.sum(qf * kv_n[:, 0:128], axis=1, keepdims=True)
    m_s = jnp.maximum(jnp.max(s_s, axis=1, keepdims=True), s_n)
    e_s = jnp.exp(s_s - m_s)
    e_n = jnp.exp(s_n - m_s)
    l_s = jnp.sum(e_s, axis=1, keepdims=True) + e_n
    pv = per_seq(lambda q: _dot_nt(rows_of(e_s, q).astype(BF), vall_ref[q]))
    o_s = (pv + e_n.astype(BF).astype(F32) * kv_n[:, 128:256]) / l_s

    s_w = per_seq(lambda q: _dot(rows_of(qb, q), win_ref[q, 0:KV_HALF, :].astype(BF)))
    lane_w = lax.broadcasted_iota(I32, (n_rows, win_buf), 1)
    s_w = jnp.where(lane_w >= win_buf + 1 - WINDOW, s_w, NEG_INF)
    kv_n = new_key(kwn_ref)
    s_n = jnp.sum(qf * kv_n[:, 0:128], axis=1, keepdims=True)
    m_w = jnp.maximum(jnp.max(s_w, axis=1, keepdims=True), s_n)
    e_w = jnp.exp(s_w - m_w)
    e_n = jnp.exp(s_n - m_w)
    l_w = jnp.sum(e_w, axis=1, keepdims=True) + e_n
    pv = per_seq(lambda q: _dot_nt(rows_of(e_w, q).astype(BF), win_ref[q, KV_HALF:KV_WIDTH, :].astype(BF)))
    o_w = (pv + e_n.astype(BF).astype(F32) * kv_n[:, 128:256]) / l_w

    g = g_ref[...].reshape(n_rows, 3)
    o_ref[...] = (g[:, 0:1] * o_c + g[:, 1:2] * o_s + g[:, 2:3] * o_w).reshape(seq_step, 8, KV_HALF)

    seq_lane = lax.broadcasted_iota(I32, kwnT_ref.shape, 1)
    row_lane = lax.broadcasted_iota(I32, (KV_WIDTH, win_buf), 1)
    for q in range(seq_step):
        n = pl.program_id(0) * seq_step + q
        new_col = jnp.sum(jnp.where(seq_lane == n, kwnT_ref[...], 0.0), axis=1, keepdims=True)
        shifted = pltpu.roll(win_ref[q], win_buf - 1, axis=1)
        nwin_ref[q] = jnp.where(row_lane == win_buf - 1, new_col, shifted)


def _sattn_kernel(pt_ref, *refs, n_pages, past_len, seq_step):
    _sattn_init(refs[-2], past_len, seq_step)
    _sattn_main(*refs, n_pages=n_pages, past_len=past_len, seq_step=seq_step)


def _attention_sample(page_table_flat, cache_t, qblk, kvc_pad, win_t, kvs_new, kvw_new, kvw_new_t, gates8,
                      n_pages, past_len):
    n_seq = qblk.shape[0]
    win_buf = win_t.shape[2]
    g = SAMPLE_SEQ_STEP
    const2 = lambda t, pt: (0, 0)
    per_step = lambda t, pt: (t, 0, 0)

    def page_map(k):
        return lambda t, pt: (pt[t * g * n_pages + k], 0, 0)

    grid_spec = pltpu.PrefetchScalarGridSpec(
        num_scalar_prefetch=1,
        grid=(n_seq // g,),
        in_specs=[pl.BlockSpec((1, KV_WIDTH, PAGE_SIZE), page_map(k)) for k in range(g * n_pages)]
        + [pl.BlockSpec((g, 8, 128), per_step),
           pl.BlockSpec((g, 128, KV_WIDTH), per_step),
           pl.BlockSpec((g, KV_WIDTH, win_buf), per_step),
           pl.BlockSpec((g, 1, KV_WIDTH), per_step),
           pl.BlockSpec((g, 1, KV_WIDTH), per_step),
           pl.BlockSpec(kvw_new_t.shape, const2),
           pl.BlockSpec((g, 8, 3), per_step)],
        out_specs=(pl.BlockSpec((g, 8, 128), per_step),
                   pl.BlockSpec((g, KV_WIDTH, win_buf), per_step)),
        scratch_shapes=[pltpu.VMEM((g, 256, past_len), BF), pltpu.VMEM((g, KV_HALF, past_len), BF)],
    )
    return pl.pallas_call(
        functools.partial(_sattn_kernel, n_pages=n_pages, past_len=past_len, seq_step=g),
        out_shape=(jax.ShapeDtypeStruct((n_seq, 8, 128), F32),
                   jax.ShapeDtypeStruct((n_seq, KV_WIDTH, win_buf), F32)),
        grid_spec=grid_spec,
        compiler_params=_cparams("arbitrary"),
        name="attention_sample",
    )(page_table_flat, *([cache_t] * (g * n_pages)), qblk, kvc_pad, win_t, kvs_new, kvw_new, kvw_new_t, gates8)


def _kv_rows_view(kv_t):
    n, _, t = kv_t.shape
    return jnp.transpose(kv_t.reshape(n, 2, N_KV_HEADS, HEAD_DIM, t), (0, 4, 1, 2, 3))


def _kv_feat_view(kv):
    n, t = kv.shape[:2]
    return jnp.transpose(kv, (0, 2, 3, 4, 1)).reshape(n, KV_WIDTH, t)


def kernel(x_prompt, x_sample, cache_cmp_kv, cache_sel_kv, state_win_kv, state_pool, page_table, c_prompt, c_sample, w_ada, b_ada, g_pre_mix, g_post_mix, g_pre_mlp, g_post_mlp, w_in, w_cmp, pos_cmp, w_pool, pool_scale, w_up_nsa, w_up_pool, w_o, w_ff1, w_ff2):
    n_batch, seq, _ = x_prompt.shape
    n_seq = x_sample.shape[0]
    n_pages = page_table.shape[1]
    past_len = n_pages * PAGE_SIZE
    nb_past = past_len // BLOCK
    assert x_sample.shape[1] == 1 and w_ada.shape[0] == 1 and past_len % BLOCK == 0
    assert seq % TOKEN_TILE == 0 and seq // BLOCK == 128 and state_win_kv.shape[2] == WINDOW
    assert (n_seq * nb_past) % CMP_GROUP == 0 and n_seq == CMP_GROUP
    assert FORCE_SCORE > GROUP

    w_t = w_in[0].T
    wt = jnp.pad(w_t[0:1304], ((0, 8), (0, 0))).astype(BF)
    wn = jnp.concatenate([w_t[768:896], w_t[1024:1152], w_t[1304:1816], w_t[512:768]], axis=0).T.astype(BF)
    ws = jnp.pad(w_t[0:1816], ((0, 104), (0, 0))).T.astype(BF)
    wgm = w_t[1816:3864].T.astype(BF)
    eye = jnp.eye(N_KV_HEADS, dtype=F32)
    wc = w_cmp[0].astype(BF)
    wz = jnp.zeros_like(wc)
    w4 = jnp.concatenate([jnp.concatenate([wc, wz], axis=3), jnp.concatenate([wz, wc], axis=3)], axis=2)
    w4 = w4.reshape(2, CMP_SPLIT, CMP_K // CMP_SPLIT, KV_HALF)
    pos4 = jnp.broadcast_to(jnp.transpose(pos_cmp[0], (1, 0, 2))[:, :, None, :],
                            (2, BLOCK, N_KV_HEADS, HEAD_DIM)).reshape(2, CMP_SPLIT, 1, CMP_K // CMP_SPLIT)
    wp = w_pool[0].astype(BF)
    ps = pool_scale[0].reshape(1, POOL_WIDTH)
    wun, wup, wo = w_up_nsa[0].astype(BF), w_up_pool[0].astype(BF), w_o[0].astype(BF)
    w1, w2 = w_ff1[0].astype(BF), w_ff2[0].astype(BF)
    gpm, gqm = g_pre_mix[0].reshape(1, D_MODEL), g_post_mix[0].reshape(1, D_MODEL)
    gpf, gqf = g_pre_mlp[0].reshape(1, D_MODEL), g_post_mlp[0].reshape(1, D_MODEL)

    n_c = n_batch + n_seq
    c_all = jnp.pad(jnp.concatenate([c_prompt, c_sample], axis=0), ((0, (-n_c) % 8), (0, 0)))
    ada = _adaln(c_all, w_ada[0], b_ada[0].reshape(1, -1))
    ada_p = ada[:n_batch].reshape(n_batch, 6, 1, D_MODEL)
    ada_s = ada[n_batch:n_c].reshape(n_seq, 6, D_MODEL)
    mods_p = [ada_p[:, k] for k in range(6)]
    mods_s = [ada_s[:, k][None] for k in range(6)]

    xp = x_prompt.reshape(n_batch * seq, D_MODEL)
    pt_flat = page_table.reshape(-1)
    (kvcT, kvsT, kvwT, kvc_a, kvc_b, u_p, qT, kaug, vsT, kw, vwT, gT,
     kvc_past) = _project_prompt_compress_cache(xp, mods_p[0], mods_p[1], gpm, wn, wt, n_batch, seq,
                                                _kv_feat_view(cache_cmp_kv[0]), pt_flat, pos4, w4)
    kvc_blk = _compress(kvc_a, kvc_b, pos4, w4).reshape(n_batch, seq // BLOCK, KV_WIDTH)
    kc = kvc_blk[:, :, 0:128].astype(BF)
    vcT = jnp.swapaxes(kvc_blk[:, :, 128:256], 1, 2).astype(BF)
    onsa_p = _attention_prompt(qT, kaug, vsT, kw, vwT, kc, vcT, gT, n_batch, seq)
    opool_p = _pool_prompt(u_p, wp, ps, n_batch, seq)
    x1_p = _token_call(_merge_kernel, "merge_prompt", xp, [mods_p[0], mods_p[1], mods_p[2]], TOKEN_TILE, seq,
                       [onsa_p.reshape(n_batch * seq, NSA_WIDTH), opool_p], [gpm, gqm, wgm, wun, wup, wo])
    y_p = _token_call(_mlp_kernel, "mlp_prompt", x1_p, [mods_p[3], mods_p[4], mods_p[5]], TOKEN_TILE, seq,
                      [], [gpf, gqf, w1, w2])

    xs = x_sample.reshape(n_seq, D_MODEL)
    z, zsig, zT = _project_sample(xs, mods_s[0][0], mods_s[1][0], gpm, ws, wt)
    q_s = z[:, 0:512] * (HEAD_DIM ** -0.5)
    kvc_n, kvs_n, kvw_n = z[:, 512:768], z[:, 768:1024], z[:, 1024:1280]
    gates_s = zsig[:, 1280:1304].reshape(n_seq, N_HEADS, 3)
    u_s = z[:, 1304:1816]
    q5 = q_s.reshape(n_seq, N_KV_HEADS, GROUP, 1, HEAD_DIM) * eye[None, :, None, :, None]
    qblk = q5.reshape(n_seq, N_HEADS, KV_HALF)

    last_a = jnp.pad(kvc_n[:, None, 0:128], ((0, 0), (0, BLOCK - 1), (0, 0))).reshape(n_seq * BLOCK, KV_HALF)
    last_b = jnp.pad(kvc_n[:, None, 128:256], ((0, 0), (0, BLOCK - 1), (0, 0))).reshape(n_seq * BLOCK, KV_HALF)
    kvc_last = _compress(last_a, last_b, pos4, w4)
    kvc_s = jnp.concatenate([kvc_past.reshape(n_seq, nb_past, KV_WIDTH), kvc_last[:, None, :]], axis=1)
    kvc_pad = jnp.pad(kvc_s, ((0, 0), (0, 128 - nb_past - 1), (0, 0)))
    o8, new_win_t = _attention_sample(
        pt_flat, _kv_feat_view(cache_sel_kv[0]), qblk, kvc_pad, _kv_feat_view(state_win_kv[0]),
        kvs_n[:, None, :], kvw_n[:, None, :], zT[1024:1280], gates_s, n_pages, past_len)
    o5 = o8.reshape(n_seq, N_KV_HEADS, GROUP, N_KV_HEADS, HEAD_DIM)
    onsa_s = jnp.concatenate([o5[:, 0, :, 0, :], o5[:, 1, :, 1, :]], axis=1).reshape(n_seq, NSA_WIDTH).astype(BF)
    opool_s = _pool_sample(u_s, jnp.swapaxes(state_pool[0], 0, 1), wp, ps)
    x1_s = _token_call(_merge_kernel, "merge_sample", xs, [mods_s[0], mods_s[1], mods_s[2]], n_seq, n_seq,
                       [onsa_s, opool_s], [gpm, gqm, wgm, wun, wup, wo])
    y_s = _token_call(_mlp_kernel, "mlp_sample", x1_s, [mods_s[3], mods_s[4], mods_s[5]], n_seq, n_seq,
                      [], [gpf, gqf, w1, w2])

    win_p = min(WINDOW, seq)
    new_kv_s = lambda rows: _kv_rows_view(rows.reshape(1, KV_WIDTH, n_seq))[0][None, :, None]
    return (
        y_p.reshape(n_batch, seq, D_MODEL),
        y_s.reshape(n_seq, 1, D_MODEL),
        _kv_rows_view(kvcT)[None],
        _kv_rows_view(kvsT)[None],
        _kv_rows_view(kvwT[:, :, seq - win_p:])[None],
        u_p.reshape(n_batch, seq, POOL_WIDTH)[None, :, seq - POOL_HIST:],
        new_kv_s(zT[512:768]),
        new_kv_s(zT[768:1024]),
        _kv_rows_view(new_win_t)[None],
        jnp.concatenate([state_pool[0][:, 1:], u_s[:, None, :]], axis=1)[None],
    )
```

```python
import functools
import math

import jax
import jax.numpy as jnp
from jax import lax
from jax.experimental import pallas as pl
from jax.experimental.pallas import tpu as pltpu

D_MODEL = 1024
N_HEADS = 8
HEAD_DIM = 64
N_KV_HEADS = 2
GROUP = N_HEADS // N_KV_HEADS
BLOCK = 64
N_SELECT = 16
WINDOW = 512
Q_TILE = 256
WIN_CHUNK = 128
NSA_WIDTH = N_HEADS * HEAD_DIM
KV_WIDTH = 2 * N_KV_HEADS * HEAD_DIM
KV_HALF = N_KV_HEADS * HEAD_DIM
FORCE_SCORE = 16.0
N_FORCED = 3
POOL_WINDOWS = (2, 4, 8, 16)
POOL_WIDTH = 512
POOL_GROUP_DIM = 128
POOL_HIST = 15
D_FF = 4 * D_MODEL
EPS = 1e-6
PAGE_SIZE = 128
V_ROWS = HEAD_DIM + 16
CMP_K = BLOCK * KV_HALF
CMP_SPLIT = 4
CMP_GROUP = 128
CMP_PITCH = BLOCK + 4

BF = jnp.bfloat16
F32 = jnp.float32
I32 = jnp.int32
MASK_BIG = 2.0 ** 100
NEG_INF = float("-inf")
LOG2E = math.log2(math.e)

TOKEN_TILE = 512
PROJ_TILE = 256
SEL_CHUNK = 512
SAMPLE_SEQ_STEP = 4
VMEM_LIMIT = 56 * 1024 * 1024


def _cparams(*sem):
    return pltpu.CompilerParams(dimension_semantics=sem, vmem_limit_bytes=VMEM_LIMIT)


def _rms(x, g):
    return x * lax.rsqrt(jnp.mean(x * x, axis=-1, keepdims=True) + EPS) * g


def _dot(a, b):
    return jnp.dot(a, b, preferred_element_type=F32)


def _dot_nt(a, b):
    return lax.dot_general(a, b, (((1,), (1,)), ((), ())), preferred_element_type=F32)


def _ada_kernel(c_ref, w_ref, b_ref, o_ref):
    c = c_ref[...]
    a = (c * jax.nn.sigmoid(c)).astype(BF)
    o_ref[...] = _dot(a, w_ref[...].astype(BF)) + b_ref[...]


def _adaln(c_all, w_ada, b_ada):
    rows = c_all.shape[0]
    n_out = w_ada.shape[1]
    tn = 512
    return pl.pallas_call(
        _ada_kernel,
        out_shape=jax.ShapeDtypeStruct((rows, n_out), F32),
        grid=(n_out // tn,),
        in_specs=[pl.BlockSpec((rows, D_MODEL), lambda j: (0, 0)),
                  pl.BlockSpec((D_MODEL, tn), lambda j: (0, j)),
                  pl.BlockSpec((1, tn), lambda j: (0, j))],
        out_specs=pl.BlockSpec((rows, tn), lambda j: (0, j)),
        compiler_params=_cparams("arbitrary"),
        name="adaln",
    )(c_all, w_ada, b_ada)


def _proj_kernel(x_ref, sh_ref, sc_ref, g_ref, wn_ref, wt_ref,
                 kvcT_ref, kvsT_ref, kvwT_ref, ca_ref, cb_ref, u_ref,
                 qT_ref, kaug_ref, vsT_ref, kw_ref, vwT_ref, gT_ref, *, tm, tpb):
    x = x_ref[...]
    h = _rms(x, g_ref[...]) * (1.0 + sc_ref[0]) + sh_ref[0]
    hb = h.astype(BF)
    zn = _dot(hb, wn_ref[...])
    zt = _dot_nt(wt_ref[...], hb)
    kvcT_ref[0] = zt[512:768]
    kvsT_ref[0] = zt[768:1024]
    kvwT_ref[0] = zt[1024:1280]
    ca_ref[...] = zn[:, 768:896]
    cb_ref[...] = zn[:, 896:1024]
    u_ref[...] = zn[:, 256:768]
    qT_ref[0] = (zt[0:512] * (HEAD_DIM ** -0.5 * LOG2E)).astype(BF)
    ones = jnp.ones((V_ROWS - HEAD_DIM, tm), BF)
    for h in range(N_KV_HEADS):
        vs_h = zt[896 + h * HEAD_DIM:896 + (h + 1) * HEAD_DIM].astype(BF)
        vw_h = zt[1152 + h * HEAD_DIM:1152 + (h + 1) * HEAD_DIM].astype(BF)
        vsT_ref[0, 0, h, 0:HEAD_DIM, :] = vs_h
        vsT_ref[0, 0, h, HEAD_DIM:V_ROWS, :] = ones
        for c in range(tm // WIN_CHUNK):
            cs = slice(c * WIN_CHUNK, (c + 1) * WIN_CHUNK)
            vwT_ref[0, c, h, 0:HEAD_DIM, :] = vw_h[:, cs]
            vwT_ref[0, c, h, HEAD_DIM:V_ROWS, :] = ones[:, cs]
    gT_ref[0] = jax.nn.sigmoid(zt[1280:1312])
    t0 = (pl.program_id(0) % tpb) * tm
    blk = jnp.right_shift(t0 + lax.broadcasted_iota(I32, (tm, 128), 0), 6)
    lane = lax.broadcasted_iota(I32, (tm, 128), 1)
    kaug_ref[0, :, 0:128] = zn[:, 0:128].astype(BF)
    kaug_ref[0, :, 128:256] = jnp.where(blk == lane, MASK_BIG, 0.0).astype(BF)
    kw_ref[0] = zn[:, 128:256].astype(BF)


def _proj_cmp_kernel(pt_ref, *refs, n_pages_step, tm, tpb):
    proj_in = refs[:6]
    pages = refs[6:6 + n_pages_step]
    pos_ref, w_ref = refs[6 + n_pages_step:8 + n_pages_step]
    proj_out = refs[8 + n_pages_step:20 + n_pages_step]
    o_ref, sa_ref, sb_ref = refs[20 + n_pages_step:]
    _proj_kernel(*proj_in, *proj_out, tm=tm, tpb=tpb)
    blocks_per_page = PAGE_SIZE // BLOCK
    for k in range(n_pages_step):
        pg = pages[k][0]
        for s, dst in enumerate((sa_ref, sb_ref)):
            rows = pg[s * KV_HALF:(s + 1) * KV_HALF, :].T
            for b in range(blocks_per_page):
                m = k * blocks_per_page + b
                dst[m * CMP_PITCH:m * CMP_PITCH + BLOCK, :] = rows[b * BLOCK:(b + 1) * BLOCK, :]
    _compress_rows((sa_ref, sb_ref), pos_ref, w_ref, o_ref, n_pages_step * blocks_per_page, CMP_PITCH)


def _project_prompt_compress_cache(x2, shift, scale, g, wn, wt, n_batch, seq,
                                   cache_t, page_table_flat, pos4, w4):
    tm = PROJ_TILE
    tpb = seq // tm
    nt = n_batch * seq
    n_steps = nt // tm
    blocks_per_page = PAGE_SIZE // BLOCK
    n_pages_all = page_table_flat.shape[0]
    n_pages_step = n_pages_all // n_steps
    assert n_pages_step * n_steps == n_pages_all
    blocks_step = n_pages_step * blocks_per_page
    chunk_steps = SEL_CHUNK // tm
    tok = lambda t, pt: (t, 0)
    per_b = lambda t, pt: (t // tpb, 0, 0)
    featT = lambda t, pt: (t // tpb, 0, t % tpb)
    rows3 = lambda t, pt: (t // tpb, t % tpb, 0)
    rows5 = lambda t, pt: (t // tpb, t % tpb, 0, 0, 0)
    chunk5 = lambda t, pt: (t // tpb, (t % tpb) // chunk_steps, 0, 0, t % chunk_steps)
    const2 = lambda t, pt: (0, 0)
    const4 = lambda t, pt: (0, 0, 0, 0)

    def page_map(k):
        return lambda t, pt: (pt[t * n_pages_step + k], 0, 0)

    kvT = jax.ShapeDtypeStruct((n_batch, KV_WIDTH, seq), F32)
    out_shape = (
        kvT, kvT, kvT,
        jax.ShapeDtypeStruct((nt, KV_HALF), F32),
        jax.ShapeDtypeStruct((nt, KV_HALF), F32),
        jax.ShapeDtypeStruct((nt, POOL_WIDTH), F32),
        jax.ShapeDtypeStruct((n_batch, NSA_WIDTH, seq), BF),
        jax.ShapeDtypeStruct((n_batch, seq, 256), BF),
        jax.ShapeDtypeStruct((n_batch, seq // SEL_CHUNK, N_KV_HEADS, V_ROWS, SEL_CHUNK), BF),
        jax.ShapeDtypeStruct((n_batch, seq, KV_HALF), BF),
        jax.ShapeDtypeStruct((n_batch, seq // WIN_CHUNK, N_KV_HEADS, V_ROWS, WIN_CHUNK), BF),
        jax.ShapeDtypeStruct((n_batch, 32, seq), F32),
        jax.ShapeDtypeStruct((n_pages_all * blocks_per_page, KV_WIDTH), F32),
    )
    out_specs = (
        pl.BlockSpec((1, KV_WIDTH, tm), featT),
        pl.BlockSpec((1, KV_WIDTH, tm), featT),
        pl.BlockSpec((1, KV_WIDTH, tm), featT),
        pl.BlockSpec((tm, KV_HALF), tok),
        pl.BlockSpec((tm, KV_HALF), tok),
        pl.BlockSpec((tm, POOL_WIDTH), tok),
        pl.BlockSpec((1, NSA_WIDTH, tm), featT),
        pl.BlockSpec((1, tm, 256), rows3),
        pl.BlockSpec((1, 1, N_KV_HEADS, V_ROWS, tm), chunk5),
        pl.BlockSpec((1, tm, KV_HALF), rows3),
        pl.BlockSpec((1, tm // WIN_CHUNK, N_KV_HEADS, V_ROWS, WIN_CHUNK), rows5),
        pl.BlockSpec((1, 32, tm), featT),
        pl.BlockSpec((blocks_step, KV_WIDTH), tok),
    )
    grid_spec = pltpu.PrefetchScalarGridSpec(
        num_scalar_prefetch=1,
        grid=(n_steps,),
        in_specs=[pl.BlockSpec((tm, D_MODEL), tok),
                  pl.BlockSpec((1, 1, D_MODEL), per_b),
                  pl.BlockSpec((1, 1, D_MODEL), per_b),
                  pl.BlockSpec((1, D_MODEL), const2),
                  pl.BlockSpec(wn.shape, const2),
                  pl.BlockSpec(wt.shape, const2)]
        + [pl.BlockSpec((1, KV_WIDTH, PAGE_SIZE), page_map(k)) for k in range(n_pages_step)]
        + [pl.BlockSpec(pos4.shape, const4), pl.BlockSpec(w4.shape, const4)],
        out_specs=out_specs,
        scratch_shapes=[pltpu.VMEM((blocks_step * CMP_PITCH, KV_HALF), F32),
                        pltpu.VMEM((blocks_step * CMP_PITCH, KV_HALF), F32)],
    )
    return pl.pallas_call(
        functools.partial(_proj_cmp_kernel, n_pages_step=n_pages_step, tm=tm, tpb=tpb),
        out_shape=out_shape,
        grid_spec=grid_spec,
        compiler_params=_cparams("arbitrary"),
        name="project_prompt_compress_cache",
    )(page_table_flat, x2, shift, scale, g, wn, wt, *([cache_t] * n_pages_step), pos4, w4)


def _compress_rows(src_refs, pos_ref, w_ref, o_ref, n_blocks, pitch):
    rows_per_slice = BLOCK // CMP_SPLIT
    for s in range(2):
        acc = jnp.zeros((n_blocks, KV_HALF), F32)
        for c in range(CMP_SPLIT):
            xc = jnp.concatenate(
                [src_refs[s][pl.ds(c * rows_per_slice + r, n_blocks, stride=pitch), :]
                 for r in range(rows_per_slice)], axis=1) + pos_ref[s, c]
            acc = acc + _dot(xc.astype(BF), w_ref[s, c])
        o_ref[:, s * KV_HALF:(s + 1) * KV_HALF] = acc


def _cmp_kernel(xa_ref, xb_ref, pos_ref, w_ref, o_ref):
    _compress_rows((xa_ref, xb_ref), pos_ref, w_ref, o_ref, CMP_GROUP, BLOCK)


def _compress(xa, xb, pos4, w4):
    m = xa.shape[0] // BLOCK
    rows = CMP_GROUP * BLOCK
    return pl.pallas_call(
        _cmp_kernel,
        out_shape=jax.ShapeDtypeStruct((m, KV_WIDTH), F32),
        grid=(m // CMP_GROUP,),
        in_specs=[pl.BlockSpec((rows, KV_HALF), lambda i: (i, 0)),
                  pl.BlockSpec((rows, KV_HALF), lambda i: (i, 0)),
                  pl.BlockSpec(pos4.shape, lambda i: (0, 0, 0, 0)),
                  pl.BlockSpec(w4.shape, lambda i: (0, 0, 0, 0))],
        out_specs=pl.BlockSpec((CMP_GROUP, KV_WIDTH), lambda i: (i, 0)),
        compiler_params=_cparams("arbitrary"),
        name="compress_blocks",
    )(xa, xb, pos4, w4)


def _attn_kernel(qT_ref, kaug_ref, vsT_ref, kw_ref, vwT_ref, kc_ref, vcT_ref, gT_ref, o_ref,
                 qa_ref, sa_ref, sb_ref, sc_ref, ma_ref, mb_ref, mc_ref, m_ref, acc_ref, outT_ref):
    i = pl.program_id(1)
    q0 = i * Q_TILE
    n_full = lax.div(q0, SEL_CHUNK)
    n_blk = kc_ref.shape[1]
    row = lax.broadcasted_iota(I32, (n_blk, Q_TILE), 0)
    tok = lax.broadcasted_iota(I32, (n_blk, Q_TILE), 1)
    qpos = q0 + tok
    cur = jnp.right_shift(qpos, 6)
    cmask = (row + 1) * BLOCK - 1 <= qpos
    valid = row <= cur
    forced = (row == 0) | (row == cur) | (row == cur - 1)
    key_c = lax.broadcasted_iota(I32, (SEL_CHUNK, Q_TILE), 0)
    qpos_c = q0 + lax.broadcasted_iota(I32, (SEL_CHUNK, Q_TILE), 1)
    hs = [slice(h * HEAD_DIM, (h + 1) * HEAD_DIM) for h in range(N_KV_HEADS)]

    def bias4(keep):
        b = jnp.where(keep, 0.0, -MASK_BIG)
        return jnp.concatenate([b] * GROUP, axis=1)

    def online_step(st, s, v_t, s_max=None):
        m_old = m_ref[st]
        m_new = jnp.maximum(m_old, jnp.max(s, axis=0, keepdims=True) if s_max is None else s_max)
        alpha = jnp.exp2(m_old - m_new)
        p = jnp.exp2(s - m_new)
        acc_ref[st] = alpha * acc_ref[st] + _dot(v_t, p.astype(BF))
        m_ref[st] = m_new

    m_ref[...] = jnp.full(m_ref.shape, NEG_INF, F32)
    acc_ref[...] = jnp.zeros(acc_ref.shape, F32)

    o_c = []
    scores = []
    for h in range(N_KV_HEADS):
        qa_ref[h] = jnp.zeros(qa_ref.shape[1:], BF)
        for g in range(GROUP):
            r0 = h * GROUP * HEAD_DIM + g * HEAD_DIM
            qa_ref[h, hs[h], g * Q_TILE:(g + 1) * Q_TILE] = qT_ref[0, r0:r0 + HEAD_DIM, :]
        sc = _dot(kc_ref[0], qa_ref[h, 0:KV_HALF, :])
        imp = jnp.zeros((n_blk, Q_TILE), F32)
        p_parts = []
        for g in range(GROUP):
            s = jnp.where(cmask, sc[:, g * Q_TILE:(g + 1) * Q_TILE], NEG_INF)
            mx = jnp.max(s, axis=0, keepdims=True)
            mx = jnp.where(mx > NEG_INF, mx, 0.0)
            e = jnp.where(cmask, jnp.exp2(s - mx), 0.0)
            p = e / jnp.maximum(jnp.sum(e, axis=0, keepdims=True), 1e-30)
            imp = imp + p
            p_parts.append(p)
        o_c.append(_dot(vcT_ref[0, hs[h], :], jnp.concatenate(p_parts, axis=1).astype(BF)))
        scores.append(jnp.where(valid, jnp.where(forced, -2.0, imp), -1.0))

    blk_f = lax.broadcasted_iota(I32, (n_blk, N_KV_HEADS * Q_TILE), 0).astype(F32)

    def pick_body(k, work):
        best = jnp.max(work, axis=0, keepdims=True)
        first = jnp.min(jnp.where(work == best, blk_f, float(n_blk)), axis=0, keepdims=True)
        return jnp.where((blk_f == first) & (best >= 0.0), -2.0, work)

    work = lax.fori_loop(0, N_SELECT - N_FORCED, pick_body, jnp.concatenate(scores, axis=1))
    selm1 = jnp.where(work == -2.0, 0.0, -1.0).astype(BF)
    for h in range(N_KV_HEADS):
        for g in range(GROUP):
            qa_ref[h, KV_HALF:KV_HALF + n_blk, g * Q_TILE:(g + 1) * Q_TILE] = selm1[:, h * Q_TILE:(h + 1) * Q_TILE]

    last_chunk = kaug_ref.shape[1] // SEL_CHUNK - 1

    def sel_scores(j, buf):
        dst_ref, max_ref = buf
        j = jnp.minimum(j, last_chunk)
        kt = kaug_ref[0, pl.ds(pl.multiple_of(j * SEL_CHUNK, SEL_CHUNK), SEL_CHUNK), :]
        for h in range(N_KV_HEADS):
            s = _dot(kt, qa_ref[h])
            dst_ref[h] = s
            max_ref[h] = jnp.max(s, axis=0, keepdims=True)

    def sel_process(buf, j, causal):
        src_ref, max_ref = buf
        if causal:
            cbias = bias4(j * SEL_CHUNK + key_c <= qpos_c)
        for h in range(N_KV_HEADS):
            if causal:
                online_step(h, src_ref[h] + cbias, vsT_ref[0, j, h])
            else:
                online_step(h, src_ref[h], vsT_ref[0, j, h], max_ref[h])

    buf_a, buf_b, buf_c = (sa_ref, ma_ref), (sb_ref, mb_ref), (sc_ref, mc_ref)
    sel_scores(0, buf_a)
    sel_scores(1, buf_b)

    w0 = jnp.maximum(q0 - WINDOW, 0)
    j0 = lax.div(w0, WIN_CHUNK)
    jq = lax.div(q0, WIN_CHUNK)
    delta = qpos_c - (w0 + key_c)
    wbias = bias4((delta >= 0) & (delta < WINDOW))
    key_d = lax.broadcasted_iota(I32, (Q_TILE, Q_TILE), 0)
    tok_d = lax.broadcasted_iota(I32, (Q_TILE, Q_TILE), 1)
    dbias = bias4((key_d <= tok_d) & (q0 >= WINDOW))
    kw_a = kw_ref[0, pl.ds(pl.multiple_of(w0, WIN_CHUNK), WINDOW), :]
    kw_b = kw_ref[0, pl.ds(pl.multiple_of(q0, Q_TILE), Q_TILE), :]
    for h in range(N_KV_HEADS):
        qf = qa_ref[h, 0:KV_HALF, :]
        v_a = jnp.concatenate([vwT_ref[0, j0 + c, h] for c in range(WINDOW // WIN_CHUNK)], axis=1)
        v_b = jnp.concatenate([vwT_ref[0, jq + c, h] for c in range(Q_TILE // WIN_CHUNK)], axis=1)
        online_step(2 + h, _dot(kw_a, qf) + wbias, v_a)
        online_step(2 + h, _dot(kw_b, qf) + dbias, v_b)

    def trio_body(t, carry):
        j = 3 * t
        sel_scores(j + 2, buf_c)
        sel_process(buf_a, j, False)
        sel_scores(j + 3, buf_a)
        sel_process(buf_b, j + 1, False)
        sel_scores(j + 4, buf_b)
        sel_process(buf_c, j + 2, False)
        return carry

    n_trios = lax.div(n_full, 3)
    lax.fori_loop(0, n_trios, trio_body, 0)
    j_last = 3 * n_trios
    n_left = n_full - j_last

    @pl.when(n_left == 0)
    def _():
        sel_process(buf_a, j_last, True)

    @pl.when(n_left == 1)
    def _():
        sel_process(buf_a, j_last, False)
        sel_process(buf_b, j_last + 1, True)

    @pl.when(n_left == 2)
    def _():
        sel_scores(j_last + 2, buf_c)
        sel_process(buf_a, j_last, False)
        sel_process(buf_b, j_last + 1, False)
        sel_process(buf_c, j_last + 2, True)

    for h in range(N_KV_HEADS):
        o_s = acc_ref[h, 0:HEAD_DIM, :] / acc_ref[h, HEAD_DIM:HEAD_DIM + 1, :]
        o_w = acc_ref[2 + h, 0:HEAD_DIM, :] / acc_ref[2 + h, HEAD_DIM:HEAD_DIM + 1, :]
        for g in range(GROUP):
            gs = slice(g * Q_TILE, (g + 1) * Q_TILE)
            gr = h * GROUP * 3 + g * 3
            og = (gT_ref[0, gr:gr + 1, :] * o_c[h][:, gs] + gT_ref[0, gr + 1:gr + 2, :] * o_s[:, gs]
                  + gT_ref[0, gr + 2:gr + 3, :] * o_w[:, gs])
            r0 = h * GROUP * HEAD_DIM + g * HEAD_DIM
            outT_ref[r0:r0 + HEAD_DIM, :] = og

    o_ref[0] = outT_ref[...].T.astype(BF)


def _attention_prompt(qT, kaug, vsT, kw, vwT, kc, vcT, gT, n_batch, seq):
    per_b3 = lambda n, i: (n, 0, 0)
    per_b5 = lambda n, i: (n, 0, 0, 0, 0)
    rows = GROUP * Q_TILE
    return pl.pallas_call(
        _attn_kernel,
        out_shape=jax.ShapeDtypeStruct((n_batch, seq, NSA_WIDTH), BF),
        grid=(n_batch, seq // Q_TILE),
        in_specs=[pl.BlockSpec((1, NSA_WIDTH, Q_TILE), lambda n, i: (n, 0, i)),
                  pl.BlockSpec((1, seq, 256), per_b3),
                  pl.BlockSpec((1, seq // SEL_CHUNK, N_KV_HEADS, V_ROWS, SEL_CHUNK), per_b5),
                  pl.BlockSpec((1, seq, KV_HALF), per_b3),
                  pl.BlockSpec((1, seq // WIN_CHUNK, N_KV_HEADS, V_ROWS, WIN_CHUNK), per_b5),
                  pl.BlockSpec((1, seq // BLOCK, KV_HALF), per_b3),
                  pl.BlockSpec((1, KV_HALF, seq // BLOCK), per_b3),
                  pl.BlockSpec((1, 32, Q_TILE), lambda n, i: (n, 0, i))],
        out_specs=pl.BlockSpec((1, Q_TILE, NSA_WIDTH), lambda n, i: (n, i, 0)),
        scratch_shapes=[pltpu.VMEM((N_KV_HEADS, 256, rows), BF),
                        pltpu.VMEM((N_KV_HEADS, SEL_CHUNK, rows), F32),
                        pltpu.VMEM((N_KV_HEADS, SEL_CHUNK, rows), F32),
                        pltpu.VMEM((N_KV_HEADS, SEL_CHUNK, rows), F32),
                        pltpu.VMEM((N_KV_HEADS, 1, rows), F32),
                        pltpu.VMEM((N_KV_HEADS, 1, rows), F32),
                        pltpu.VMEM((N_KV_HEADS, 1, rows), F32),
                        pltpu.VMEM((2 * N_KV_HEADS, 1, rows), F32),
                        pltpu.VMEM((2 * N_KV_HEADS, V_ROWS, rows), F32),
                        pltpu.VMEM((NSA_WIDTH, Q_TILE), F32)],
        compiler_params=_cparams("arbitrary", "arbitrary"),
        name="attention_prompt",
    )(qT, kaug, vsT, kw, vwT, kc, vcT, gT)


def _pool_kernel(u_ref, halo_ref, wp_ref, ps_ref, o_ref, ext_ref, *, tm, tpb):
    t = pl.program_id(0) % tpb
    ext_ref[0:16, :] = jnp.where(t == 0, 0.0, halo_ref[...])
    u = u_ref[...]
    ext_ref[16:16 + tm, :] = u
    pos = t * tm + lax.broadcasted_iota(I32, (tm, 1), 0)
    outs = []
    for gi, w in enumerate(POOL_WINDOWS):
        cs = slice(gi * POOL_GROUP_DIM, (gi + 1) * POOL_GROUP_DIM)
        acc = u[:, cs]
        for k in range(1, w):
            acc = acc + ext_ref[pl.ds(16 - k, tm), cs]
        cnt = jnp.minimum(pos + 1, w).astype(F32)
        pooled = acc / cnt - u[:, cs]
        outs.append(_dot(pooled.astype(BF), wp_ref[gi]))
    o_ref[...] = (jnp.concatenate(outs, axis=1) * ps_ref[...]).astype(BF)


def _pool_prompt(u, w_pool, pool_scale, n_batch, seq):
    tm = TOKEN_TILE
    tpb = seq // tm
    nt = n_batch * seq
    return pl.pallas_call(
        functools.partial(_pool_kernel, tm=tm, tpb=tpb),
        out_shape=jax.ShapeDtypeStruct((nt, POOL_WIDTH), BF),
        grid=(nt // tm,),
        in_specs=[pl.BlockSpec((tm, POOL_WIDTH), lambda t: (t, 0)),
                  pl.BlockSpec((16, POOL_WIDTH), lambda t: (jnp.maximum(t * (tm // 16) - 1, 0), 0)),
                  pl.BlockSpec(w_pool.shape, lambda t: (0, 0, 0)),
                  pl.BlockSpec((1, POOL_WIDTH), lambda t: (0, 0))],
        out_specs=pl.BlockSpec((tm, POOL_WIDTH), lambda t: (t, 0)),
        scratch_shapes=[pltpu.VMEM((tm + 16, POOL_WIDTH), F32)],
        compiler_params=_cparams("arbitrary"),
        name="pool_prompt",
    )(u, u, w_pool, pool_scale)


def _spool_kernel(u_ref, hist_ref, wp_ref, ps_ref, o_ref):
    u = u_ref[...]
    outs = []
    for gi, w in enumerate(POOL_WINDOWS):
        cs = slice(gi * POOL_GROUP_DIM, (gi + 1) * POOL_GROUP_DIM)
        acc = u[:, cs]
        for k in range(1, w):
            acc = acc + hist_ref[POOL_HIST - k, :, cs]
        pooled = acc / float(w) - u[:, cs]
        outs.append(_dot(pooled.astype(BF), wp_ref[gi]))
    o_ref[...] = (jnp.concatenate(outs, axis=1) * ps_ref[...]).astype(BF)


def _pool_sample(u, hist_t, w_pool, pool_scale):
    n = u.shape[0]
    return pl.pallas_call(
        _spool_kernel,
        out_shape=jax.ShapeDtypeStruct((n, POOL_WIDTH), BF),
        name="pool_sample",
    )(u, hist_t, w_pool, pool_scale)


def _merge_kernel(x_ref, sh_ref, sc_ref, gate_ref, gpre_ref, gpost_ref, onsa_ref, opool_ref,
                  wgm_ref, wun_ref, wup_ref, wo_ref, o_ref):
    x = x_ref[...]
    h = _rms(x, gpre_ref[...]) * (1.0 + sc_ref[0]) + sh_ref[0]
    gm = jax.nn.sigmoid(_dot(h.astype(BF), wgm_ref[...]))
    m = (gm[:, :D_MODEL] * _dot(onsa_ref[...], wun_ref[...])
         + gm[:, D_MODEL:] * _dot(opool_ref[...], wup_ref[...]))
    m = _dot(m.astype(BF), wo_ref[...])
    o_ref[...] = x + gate_ref[0] * _rms(m, gpost_ref[...])


def _mlp_kernel(x_ref, sh_ref, sc_ref, gate_ref, gpre_ref, gpost_ref, w1_ref, w2_ref, o_ref):
    x = x_ref[...]
    h = _rms(x, gpre_ref[...]) * (1.0 + sc_ref[0]) + sh_ref[0]
    hb = h.astype(BF)
    f = jnp.zeros(x.shape, F32)
    fc = 1024
    for c in range(D_FF // fc):
        a = jnp.maximum(_dot(hb, w1_ref[:, c * fc:(c + 1) * fc]), 0.0)
        f = f + _dot((a * a).astype(BF), w2_ref[c * fc:(c + 1) * fc, :])
    o_ref[...] = x + gate_ref[0] * _rms(f, gpost_ref[...])


def _token_call(kernel, name, x2, mods, tm, rows_per_mod, extra_tok, consts):
    nt = x2.shape[0]
    r = mods[0].shape[1]
    mod_spec = pl.BlockSpec((1, r, D_MODEL), lambda t: ((t * tm) // rows_per_mod, 0, 0))
    in_specs = [pl.BlockSpec((tm, D_MODEL), lambda t: (t, 0))] + [mod_spec] * len(mods)
    in_specs += [pl.BlockSpec((1, D_MODEL), lambda t: (0, 0))] * 2
    in_specs += [pl.BlockSpec((tm, a.shape[1]), lambda t: (t, 0)) for a in extra_tok]
    in_specs += [pl.BlockSpec(w.shape, lambda t: (0, 0), pipeline_mode=pl.Buffered(1)) for w in consts[2:]]
    return pl.pallas_call(
        kernel,
        out_shape=jax.ShapeDtypeStruct((nt, D_MODEL), F32),
        grid=(nt // tm,),
        in_specs=in_specs,
        out_specs=pl.BlockSpec((tm, D_MODEL), lambda t: (t, 0)),
        compiler_params=_cparams("arbitrary"),
        name=name,
    )(x2, *mods, consts[0], consts[1], *extra_tok, *consts[2:])


def _sproj_kernel(x_ref, sh_ref, sc_ref, g_ref, wn_ref, wt_ref, z_ref, zs_ref, zT_ref):
    x = x_ref[...]
    h = _rms(x, g_ref[...]) * (1.0 + sc_ref[...]) + sh_ref[...]
    hb = h.astype(BF)
    z = _dot(hb, wn_ref[...])
    z_ref[...] = z
    zs_ref[...] = jax.nn.sigmoid(z)
    zT_ref[...] = _dot_nt(wt_ref[...], hb)


def _project_sample(x2, shift, scale, g, wn, wt):
    n = x2.shape[0]
    shp = jax.ShapeDtypeStruct((n, wn.shape[1]), F32)
    return pl.pallas_call(
        _sproj_kernel,
        out_shape=(shp, shp, jax.ShapeDtypeStruct((wt.shape[0], n), F32)),
        name="project_sample",
    )(x2, shift, scale, g, wn, wt)


def _sattn_init(kaug_ref, past_len, seq_step):
    @pl.when(pl.program_id(0) == 0)
    def _():
        blk = lax.broadcasted_iota(I32, (128, past_len), 0)
        key_blk = jnp.right_shift(lax.broadcasted_iota(I32, (128, past_len), 1), 6)
        onehot = jnp.where(blk == key_blk, MASK_BIG, 0.0).astype(BF)
        for q in range(seq_step):
            kaug_ref[q, 128:256, :] = onehot


def _sattn_main(*refs, n_pages, past_len, seq_step):
    pages = refs[:seq_step * n_pages]
    (qb_ref, kvc_ref, win_ref, ksn_ref, kwn_ref, kwnT_ref, g_ref,
     o_ref, nwin_ref, kaug_ref, vall_ref) = refs[seq_step * n_pages:]

    for q in range(seq_step):
        for p in range(n_pages):
            pg = pages[q * n_pages + p][0]
            kaug_ref[q, 0:128, p * PAGE_SIZE:(p + 1) * PAGE_SIZE] = pg[0:KV_HALF, :].astype(BF)
            vall_ref[q, :, p * PAGE_SIZE:(p + 1) * PAGE_SIZE] = pg[KV_HALF:KV_WIDTH, :].astype(BF)

    nb_past = past_len // BLOCK
    win_buf = win_ref.shape[2]
    n_rows = 8 * seq_step
    per_seq = lambda f: jnp.concatenate([f(q) for q in range(seq_step)], axis=0)
    rows_of = lambda x, q: x[8 * q:8 * (q + 1)]
    qb = qb_ref[...].reshape(n_rows, KV_HALF).astype(BF)
    qf = qb.astype(F32)
    row = lax.broadcasted_iota(I32, (n_rows, 128), 0)
    lane = lax.broadcasted_iota(I32, (n_rows, 128), 1)

    def new_key(ref):
        return per_seq(lambda q: jnp.broadcast_to(ref[q], (8, KV_WIDTH))).astype(BF).astype(F32)

    s_c = per_seq(lambda q: _dot_nt(rows_of(qb, q), kvc_ref[q, :, 0:128].astype(BF)))
    cm = lane < nb_past
    s_c = jnp.where(cm, s_c, NEG_INF)
    mx = jnp.max(s_c, axis=1, keepdims=True)
    mx = jnp.where(mx > NEG_INF, mx, 0.0)
    e = jnp.where(cm, jnp.exp(s_c - mx), 0.0)
    p_c = e / jnp.maximum(jnp.sum(e, axis=1, keepdims=True), 1e-30)
    o_c = per_seq(lambda q: _dot(rows_of(p_c, q).astype(BF), kvc_ref[q, :, 128:256].astype(BF)))

    imp = jnp.zeros((n_rows, 128), F32)
    for grp in range(n_rows // GROUP):
        in_grp = jnp.right_shift(row, 2) == grp
        imp = jnp.where(in_grp, jnp.sum(jnp.where(in_grp, p_c, 0.0), axis=0, keepdims=True), imp)
    cur = nb_past
    forced = (lane == 0) | (lane == cur) | (lane == cur - 1)
    score = jnp.where(lane <= cur, imp + FORCE_SCORE * forced.astype(F32), -1.0)
    cnt = jnp.zeros((n_rows, 128), F32)
    for bp in range(nb_past + 1):
        other = score[:, bp:bp + 1]
        ahead = (other > score) | ((other == score) & (bp < lane))
        cnt = cnt + jnp.where(ahead, 1.0, 0.0)
    sel = (cnt < float(N_SELECT)) & (score >= 0.0)
    selm1 = jnp.where(sel, 0.0, -1.0).astype(BF)

    qaug = jnp.concatenate([qb, selm1], axis=1)
    s_s = per_seq(lambda q: _dot(rows_of(qaug, q), kaug_ref[q]))
    kv_n = new_key(ksn_ref)
    s_n = jnp.sum(qf * kv_n[:, 0:128], axis=1, keepdims=True)
    m_s = jnp.maximum(jnp.max(s_s, axis=1, keepdims=True), s_n)
    e_s = jnp.exp(s_s - m_s)
    e_n = jnp.exp(s_n - m_s)
    l_s = jnp.sum(e_s, axis=1, keepdims=True) + e_n
    pv = per_seq(lambda q: _dot_nt(rows_of(e_s, q).astype(BF), vall_ref[q]))
    o_s = (pv + e_n.astype(BF).astype(F32) * kv_n[:, 128:256]) / l_s

    s_w = per_seq(lambda q: _dot(rows_of(qb, q), win_ref[q, 0:KV_HALF, :].astype(BF)))
    lane_w = lax.broadcasted_iota(I32, (n_rows, win_buf), 1)
    s_w = jnp.where(lane_w >= win_buf + 1 - WINDOW, s_w, NEG_INF)
    kv_n = new_key(kwn_ref)
    s_n = jnp.sum(qf * kv_n[:, 0:128], axis=1, keepdims=True)
    m_w = jnp.maximum(jnp.max(s_w, axis=1, keepdims=True), s_n)
    e_w = jnp.exp(s_w - m_w)
    e_n = jnp.exp(s_n - m_w)
    l_w = jnp.sum(e_w, axis=1, keepdims=True) + e_n
    pv = per_seq(lambda q: _dot_nt(rows_of(e_w, q).astype(BF), win_ref[q, KV_HALF:KV_WIDTH, :].astype(BF)))
    o_w = (pv + e_n.astype(BF).astype(F32) * kv_n[:, 128:256]) / l_w

    g = g_ref[...].reshape(n_rows, 3)
    o_ref[...] = (g[:, 0:1] * o_c + g[:, 1:2] * o_s + g[:, 2:3] * o_w).reshape(seq_step, 8, KV_HALF)

    seq_lane = lax.broadcasted_iota(I32, kwnT_ref.shape, 1)
    row_lane = lax.broadcasted_iota(I32, (KV_WIDTH, win_buf), 1)
    for q in range(seq_step):
        n = pl.program_id(0) * seq_step + q
        new_col = jnp.sum(jnp.where(seq_lane == n, kwnT_ref[...], 0.0), axis=1, keepdims=True)
        shifted = pltpu.roll(win_ref[q], win_buf - 1, axis=1)
        nwin_ref[q] = jnp.where(row_lane == win_buf - 1, new_col, shifted)


def _sattn_kernel(pt_ref, *refs, n_pages, past_len, seq_step):
    _sattn_init(refs[-2], past_len, seq_step)
    _sattn_main(*refs, n_pages=n_pages, past_len=past_len, seq_step=seq_step)


def _attention_sample(page_table_flat, cache_t, qblk, kvc_pad, win_t, kvs_new, kvw_new, kvw_new_t, gates8,
                      n_pages, past_len):
    n_seq = qblk.shape[0]
    win_buf = win_t.shape[2]
    g = SAMPLE_SEQ_STEP
    const2 = lambda t, pt: (0, 0)
    per_step = lambda t, pt: (t, 0, 0)

    def page_map(k):
        return lambda t, pt: (pt[t * g * n_pages + k], 0, 0)

    grid_spec = pltpu.PrefetchScalarGridSpec(
        num_scalar_prefetch=1,
        grid=(n_seq // g,),
        in_specs=[pl.BlockSpec((1, KV_WIDTH, PAGE_SIZE), page_map(k)) for k in range(g * n_pages)]
        + [pl.BlockSpec((g, 8, 128), per_step),
           pl.BlockSpec((g, 128, KV_WIDTH), per_step),
           pl.BlockSpec((g, KV_WIDTH, win_buf), per_step),
           pl.BlockSpec((g, 1, KV_WIDTH), per_step),
           pl.BlockSpec((g, 1, KV_WIDTH), per_step),
           pl.BlockSpec(kvw_new_t.shape, const2),
           pl.BlockSpec((g, 8, 3), per_step)],
        out_specs=(pl.BlockSpec((g, 8, 128), per_step),
                   pl.BlockSpec((g, KV_WIDTH, win_buf), per_step)),
        scratch_shapes=[pltpu.VMEM((g, 256, past_len), BF), pltpu.VMEM((g, KV_HALF, past_len), BF)],
    )
    return pl.pallas_call(
        functools.partial(_sattn_kernel, n_pages=n_pages, past_len=past_len, seq_step=g),
        out_shape=(jax.ShapeDtypeStruct((n_seq, 8, 128), F32),
                   jax.ShapeDtypeStruct((n_seq, KV_WIDTH, win_buf), F32)),
        grid_spec=grid_spec,
        compiler_params=_cparams("arbitrary"),
        name="attention_sample",
    )(page_table_flat, *([cache_t] * (g * n_pages)), qblk, kvc_pad, win_t, kvs_new, kvw_new, kvw_new_t, gates8)


def _kv_rows_view(kv_t):
    n, _, t = kv_t.shape
    return jnp.transpose(kv_t.reshape(n, 2, N_KV_HEADS, HEAD_DIM, t), (0, 4, 1, 2, 3))


def _kv_feat_view(kv):
    n, t = kv.shape[:2]
    return jnp.transpose(kv, (0, 2, 3, 4, 1)).reshape(n, KV_WIDTH, t)


def kernel(x_prompt, x_sample, cache_cmp_kv, cache_sel_kv, state_win_kv, state_pool, page_table, c_prompt, c_sample, w_ada, b_ada, g_pre_mix, g_post_mix, g_pre_mlp, g_post_mlp, w_in, w_cmp, pos_cmp, w_pool, pool_scale, w_up_nsa, w_up_pool, w_o, w_ff1, w_ff2):
    n_batch, seq, _ = x_prompt.shape
    n_seq = x_sample.shape[0]
    n_pages = page_table.shape[1]
    past_len = n_pages * PAGE_SIZE
    nb_past = past_len // BLOCK
    assert x_sample.shape[1] == 1 and w_ada.shape[0] == 1 and past_len % BLOCK == 0
    assert seq % TOKEN_TILE == 0 and seq // BLOCK == 128 and state_win_kv.shape[2] == WINDOW
    assert (n_seq * nb_past) % CMP_GROUP == 0 and n_seq == CMP_GROUP
    assert FORCE_SCORE > GROUP

    w_t = w_in[0].T
    wt = jnp.pad(w_t[0:1304], ((0, 8), (0, 0))).astype(BF)
    wn = jnp.concatenate([w_t[768:896], w_t[1024:1152], w_t[1304:1816], w_t[512:768]], axis=0).T.astype(BF)
    ws = jnp.pad(w_t[0:1816], ((0, 104), (0, 0))).T.astype(BF)
    wgm = w_t[1816:3864].T.astype(BF)
    eye = jnp.eye(N_KV_HEADS, dtype=F32)
    wc = w_cmp[0].astype(BF)
    wz = jnp.zeros_like(wc)
    w4 = jnp.concatenate([jnp.concatenate([wc, wz], axis=3), jnp.concatenate([wz, wc], axis=3)], axis=2)
    w4 = w4.reshape(2, CMP_SPLIT, CMP_K // CMP_SPLIT, KV_HALF)
    pos4 = jnp.broadcast_to(jnp.transpose(pos_cmp[0], (1, 0, 2))[:, :, None, :],
                            (2, BLOCK, N_KV_HEADS, HEAD_DIM)).reshape(2, CMP_SPLIT, 1, CMP_K // CMP_SPLIT)
    wp = w_pool[0].astype(BF)
    ps = pool_scale[0].reshape(1, POOL_WIDTH)
    wun, wup, wo = w_up_nsa[0].astype(BF), w_up_pool[0].astype(BF), w_o[0].astype(BF)
    w1, w2 = w_ff1[0].astype(BF), w_ff2[0].astype(BF)
    gpm, gqm = g_pre_mix[0].reshape(1, D_MODEL), g_post_mix[0].reshape(1, D_MODEL)
    gpf, gqf = g_pre_mlp[0].reshape(1, D_MODEL), g_post_mlp[0].reshape(1, D_MODEL)

    n_c = n_batch + n_seq
    c_all = jnp.pad(jnp.concatenate([c_prompt, c_sample], axis=0), ((0, (-n_c) % 8), (0, 0)))
    ada = _adaln(c_all, w_ada[0], b_ada[0].reshape(1, -1))
    ada_p = ada[:n_batch].reshape(n_batch, 6, 1, D_MODEL)
    ada_s = ada[n_batch:n_c].reshape(n_seq, 6, D_MODEL)
    mods_p = [ada_p[:, k] for k in range(6)]
    mods_s = [ada_s[:, k][None] for k in range(6)]

    xp = x_prompt.reshape(n_batch * seq, D_MODEL)
    pt_flat = page_table.reshape(-1)
    (kvcT, kvsT, kvwT, kvc_a, kvc_b, u_p, qT, kaug, vsT, kw, vwT, gT,
     kvc_past) = _project_prompt_compress_cache(xp, mods_p[0], mods_p[1], gpm, wn, wt, n_batch, seq,
                                                _kv_feat_view(cache_cmp_kv[0]), pt_flat, pos4, w4)
    kvc_blk = _compress(kvc_a, kvc_b, pos4, w4).reshape(n_batch, seq // BLOCK, KV_WIDTH)
    kc = kvc_blk[:, :, 0:128].astype(BF)
    vcT = jnp.swapaxes(kvc_blk[:, :, 128:256], 1, 2).astype(BF)
    onsa_p = _attention_prompt(qT, kaug, vsT, kw, vwT, kc, vcT, gT, n_batch, seq)
    opool_p = _pool_prompt(u_p, wp, ps, n_batch, seq)
    x1_p = _token_call(_merge_kernel, "merge_prompt", xp, [mods_p[0], mods_p[1], mods_p[2]], TOKEN_TILE, seq,
                       [onsa_p.reshape(n_batch * seq, NSA_WIDTH), opool_p], [gpm, gqm, wgm, wun, wup, wo])
    y_p = _token_call(_mlp_kernel, "mlp_prompt", x1_p, [mods_p[3], mods_p[4], mods_p[5]], TOKEN_TILE, seq,
                      [], [gpf, gqf, w1, w2])

    xs = x_sample.reshape(n_seq, D_MODEL)
    z, zsig, zT = _project_sample(xs, mods_s[0][0], mods_s[1][0], gpm, ws, wt)
    q_s = z[:, 0:512] * (HEAD_DIM ** -0.5)
    kvc_n, kvs_n, kvw_n = z[:, 512:768], z[:, 768:1024], z[:, 1024:1280]
    gates_s = zsig[:, 1280:1304].reshape(n_seq, N_HEADS, 3)
    u_s = z[:, 1304:1816]
    q5 = q_s.reshape(n_seq, N_KV_HEADS, GROUP, 1, HEAD_DIM) * eye[None, :, None, :, None]
    qblk = q5.reshape(n_seq, N_HEADS, KV_HALF)

    last_a = jnp.pad(kvc_n[:, None, 0:128], ((0, 0), (0, BLOCK - 1), (0, 0))).reshape(n_seq * BLOCK, KV_HALF)
    last_b = jnp.pad(kvc_n[:, None, 128:256], ((0, 0), (0, BLOCK - 1), (0, 0))).reshape(n_seq * BLOCK, KV_HALF)
    kvc_last = _compress(last_a, last_b, pos4, w4)
    kvc_s = jnp.concatenate([kvc_past.reshape(n_seq, nb_past, KV_WIDTH), kvc_last[:, None, :]], axis=1)
    kvc_pad = jnp.pad(kvc_s, ((0, 0), (0, 128 - nb_past - 1), (0, 0)))
    o8, new_win_t = _attention_sample(
        pt_flat, _kv_feat_view(cache_sel_kv[0]), qblk, kvc_pad, _kv_feat_view(state_win_kv[0]),
        kvs_n[:, None, :], kvw_n[:, None, :], zT[1024:1280], gates_s, n_pages, past_len)
    o5 = o8.reshape(n_seq, N_KV_HEADS, GROUP, N_KV_HEADS, HEAD_DIM)
    onsa_s = jnp.concatenate([o5[:, 0, :, 0, :], o5[:, 1, :, 1, :]], axis=1).reshape(n_seq, NSA_WIDTH).astype(BF)
    opool_s = _pool_sample(u_s, jnp.swapaxes(state_pool[0], 0, 1), wp, ps)
    x1_s = _token_call(_merge_kernel, "merge_sample", xs, [mods_s[0], mods_s[1], mods_s[2]], n_seq, n_seq,
                       [onsa_s, opool_s], [gpm, gqm, wgm, wun, wup, wo])
    y_s = _token_call(_mlp_kernel, "mlp_sample", x1_s, [mods_s[3], mods_s[4], mods_s[5]], n_seq, n_seq,
                      [], [gpf, gqf, w1, w2])

    win_p = min(WINDOW, seq)
    new_kv_s = lambda rows: _kv_rows_view(rows.reshape(1, KV_WIDTH, n_seq))[0][None, :, None]
    return (
        y_p.reshape(n_batch, seq, D_MODEL),
        y_s.reshape(n_seq, 1, D_MODEL),
        _kv_rows_view(kvcT)[None],
        _kv_rows_view(kvsT)[None],
        _kv_rows_view(kvwT[:, :, seq - win_p:])[None],
        u_p.reshape(n_batch, seq, POOL_WIDTH)[None, :, seq - POOL_HIST:],
        new_kv_s(zT[512:768]),
        new_kv_s(zT[768:1024]),
        _kv_rows_view(new_win_t)[None],
        jnp.concatenate([state_pool[0][:, 1:], u_s[:, None, :]], axis=1)[None],
    )
```

```python
import functools
import math

import jax
import jax.numpy as jnp
from jax import lax
from jax.experimental import pallas as pl
from jax.experimental.pallas import tpu as pltpu

D_MODEL = 1024
N_HEADS = 8
HEAD_DIM = 64
N_KV_HEADS = 2
GROUP = N_HEADS // N_KV_HEADS
BLOCK = 64
N_SELECT = 16
WINDOW = 512
Q_TILE = 256
WIN_CHUNK = 128
NSA_WIDTH = N_HEADS * HEAD_DIM
KV_WIDTH = 2 * N_KV_HEADS * HEAD_DIM
KV_HALF = N_KV_HEADS * HEAD_DIM
FORCE_SCORE = 16.0
N_FORCED = 3
POOL_WINDOWS = (2, 4, 8, 16)
POOL_WIDTH = 512
POOL_GROUP_DIM = 128
POOL_HIST = 15
D_FF = 4 * D_MODEL
EPS = 1e-6
PAGE_SIZE = 128
V_ROWS = HEAD_DIM + 16
CMP_K = BLOCK * KV_HALF
CMP_SPLIT = 4
CMP_GROUP = 128
CMP_PITCH = BLOCK + 4

BF = jnp.bfloat16
F32 = jnp.float32
I32 = jnp.int32
MASK_BIG = 2.0 ** 100
NEG_INF = float("-inf")
LOG2E = math.log2(math.e)

TOKEN_TILE = 512
PROJ_TILE = 256
SEL_CHUNK = 512
SAMPLE_SEQ_STEP = 4
VMEM_LIMIT = 56 * 1024 * 1024


def _cparams(*sem):
    return pltpu.CompilerParams(dimension_semantics=sem, vmem_limit_bytes=VMEM_LIMIT)


def _rms(x, g):
    return x * lax.rsqrt(jnp.mean(x * x, axis=-1, keepdims=True) + EPS) * g


def _dot(a, b):
    return jnp.dot(a, b, preferred_element_type=F32)


def _dot_nt(a, b):
    return lax.dot_general(a, b, (((1,), (1,)), ((), ())), preferred_element_type=F32)


def _ada_kernel(c_ref, w_ref, b_ref, o_ref):
    c = c_ref[...]
    a = (c * jax.nn.sigmoid(c)).astype(BF)
    o_ref[...] = _dot(a, w_ref[...].astype(BF)) + b_ref[...]


def _adaln(c_all, w_ada, b_ada):
    rows = c_all.shape[0]
    n_out = w_ada.shape[1]
    tn = 512
    return pl.pallas_call(
        _ada_kernel,
        out_shape=jax.ShapeDtypeStruct((rows, n_out), F32),
        grid=(n_out // tn,),
        in_specs=[pl.BlockSpec((rows, D_MODEL), lambda j: (0, 0)),
                  pl.BlockSpec((D_MODEL, tn), lambda j: (0, j)),
                  pl.BlockSpec((1, tn), lambda j: (0, j))],
        out_specs=pl.BlockSpec((rows, tn), lambda j: (0, j)),
        compiler_params=_cparams("arbitrary"),
        name="adaln",
    )(c_all, w_ada, b_ada)


def _proj_kernel(x_ref, sh_ref, sc_ref, g_ref, wn_ref, wt_ref,
                 kvcT_ref, kvsT_ref, kvwT_ref, ca_ref, cb_ref, u_ref,
                 qT_ref, kaug_ref, vsT_ref, kw_ref, vwT_ref, gT_ref, *, tm, tpb):
    x = x_ref[...]
    h = _rms(x, g_ref[...]) * (1.0 + sc_ref[0]) + sh_ref[0]
    hb = h.astype(BF)
    zn = _dot(hb, wn_ref[...])
    zt = _dot_nt(wt_ref[...], hb)
    kvcT_ref[0] = zt[512:768]
    kvsT_ref[0] = zt[768:1024]
    kvwT_ref[0] = zt[1024:1280]
    ca_ref[...] = zn[:, 768:896]
    cb_ref[...] = zn[:, 896:1024]
    u_ref[...] = zn[:, 256:768]
    qT_ref[0] = (zt[0:512] * (HEAD_DIM ** -0.5 * LOG2E)).astype(BF)
    ones = jnp.ones((V_ROWS - HEAD_DIM, tm), BF)
    for h in range(N_KV_HEADS):
        vs_h = zt[896 + h * HEAD_DIM:896 + (h + 1) * HEAD_DIM].astype(BF)
        vw_h = zt[1152 + h * HEAD_DIM:1152 + (h + 1) * HEAD_DIM].astype(BF)
        vsT_ref[0, 0, h, 0:HEAD_DIM, :] = vs_h
        vsT_ref[0, 0, h, HEAD_DIM:V_ROWS, :] = ones
        for c in range(tm // WIN_CHUNK):
            cs = slice(c * WIN_CHUNK, (c + 1) * WIN_CHUNK)
            vwT_ref[0, c, h, 0:HEAD_DIM, :] = vw_h[:, cs]
            vwT_ref[0, c, h, HEAD_DIM:V_ROWS, :] = ones[:, cs]
    gT_ref[0] = jax.nn.sigmoid(zt[1280:1312])
    t0 = (pl.program_id(0) % tpb) * tm
    blk = jnp.right_shift(t0 + lax.broadcasted_iota(I32, (tm, 128), 0), 6)
    lane = lax.broadcasted_iota(I32, (tm, 128), 1)
    kaug_ref[0, :, 0:128] = zn[:, 0:128].astype(BF)
    kaug_ref[0, :, 128:256] = jnp.where(blk == lane, MASK_BIG, 0.0).astype(BF)
    kw_ref[0] = zn[:, 128:256].astype(BF)


def _proj_cmp_kernel(pt_ref, *refs, n_pages_step, tm, tpb):
    proj_in = refs[:6]
    pages = refs[6:6 + n_pages_step]
    pos_ref, w_ref = refs[6 + n_pages_step:8 + n_pages_step]
    proj_out = refs[8 + n_pages_step:20 + n_pages_step]
    o_ref, sa_ref, sb_ref = refs[20 + n_pages_step:]
    _proj_kernel(*proj_in, *proj_out, tm=tm, tpb=tpb)
    blocks_per_page = PAGE_SIZE // BLOCK
    for k in range(n_pages_step):
        pg = pages[k][0]
        for s, dst in enumerate((sa_ref, sb_ref)):
            rows = pg[s * KV_HALF:(s + 1) * KV_HALF, :].T
            for b in range(blocks_per_page):
                m = k * blocks_per_page + b
                dst[m * CMP_PITCH:m * CMP_PITCH + BLOCK, :] = rows[b * BLOCK:(b + 1) * BLOCK, :]
    _compress_rows((sa_ref, sb_ref), pos_ref, w_ref, o_ref, n_pages_step * blocks_per_page, CMP_PITCH)


def _project_prompt_compress_cache(x2, shift, scale, g, wn, wt, n_batch, seq,
                                   cache_t, page_table_flat, pos4, w4):
    tm = PROJ_TILE
    tpb = seq // tm
    nt = n_batch * seq
    n_steps = nt // tm
    blocks_per_page = PAGE_SIZE // BLOCK
    n_pages_all = page_table_flat.shape[0]
    n_pages_step = n_pages_all // n_steps
    assert n_pages_step * n_steps == n_pages_all
    blocks_step = n_pages_step * blocks_per_page
    chunk_steps = SEL_CHUNK // tm
    tok = lambda t, pt: (t, 0)
    per_b = lambda t, pt: (t // tpb, 0, 0)
    featT = lambda t, pt: (t // tpb, 0, t % tpb)
    rows3 = lambda t, pt: (t // tpb, t % tpb, 0)
    rows5 = lambda t, pt: (t // tpb, t % tpb, 0, 0, 0)
    chunk5 = lambda t, pt: (t // tpb, (t % tpb) // chunk_steps, 0, 0, t % chunk_steps)
    const2 = lambda t, pt: (0, 0)
    const4 = lambda t, pt: (0, 0, 0, 0)

    def page_map(k):
        return lambda t, pt: (pt[t * n_pages_step + k], 0, 0)

    kvT = jax.ShapeDtypeStruct((n_batch, KV_WIDTH, seq), F32)
    out_shape = (
        kvT, kvT, kvT,
        jax.ShapeDtypeStruct((nt, KV_HALF), F32),
        jax.ShapeDtypeStruct((nt, KV_HALF), F32),
        jax.ShapeDtypeStruct((nt, POOL_WIDTH), F32),
        jax.ShapeDtypeStruct((n_batch, NSA_WIDTH, seq), BF),
        jax.ShapeDtypeStruct((n_batch, seq, 256), BF),
        jax.ShapeDtypeStruct((n_batch, seq // SEL_CHUNK, N_KV_HEADS, V_ROWS, SEL_CHUNK), BF),
        jax.ShapeDtypeStruct((n_batch, seq, KV_HALF), BF),
        jax.ShapeDtypeStruct((n_batch, seq // WIN_CHUNK, N_KV_HEADS, V_ROWS, WIN_CHUNK), BF),
        jax.ShapeDtypeStruct((n_batch, 32, seq), F32),
        jax.ShapeDtypeStruct((n_pages_all * blocks_per_page, KV_WIDTH), F32),
    )
    out_specs = (
        pl.BlockSpec((1, KV_WIDTH, tm), featT),
        pl.BlockSpec((1, KV_WIDTH, tm), featT),
        pl.BlockSpec((1, KV_WIDTH, tm), featT),
        pl.BlockSpec((tm, KV_HALF), tok),
        pl.BlockSpec((tm, KV_HALF), tok),
        pl.BlockSpec((tm, POOL_WIDTH), tok),
        pl.BlockSpec((1, NSA_WIDTH, tm), featT),
        pl.BlockSpec((1, tm, 256), rows3),
        pl.BlockSpec((1, 1, N_KV_HEADS, V_ROWS, tm), chunk5),
        pl.BlockSpec((1, tm, KV_HALF), rows3),
        pl.BlockSpec((1, tm // WIN_CHUNK, N_KV_HEADS, V_ROWS, WIN_CHUNK), rows5),
        pl.BlockSpec((1, 32, tm), featT),
        pl.BlockSpec((blocks_step, KV_WIDTH), tok),
    )
    grid_spec = pltpu.PrefetchScalarGridSpec(
        num_scalar_prefetch=1,
        grid=(n_steps,),
        in_specs=[pl.BlockSpec((tm, D_MODEL), tok),
                  pl.BlockSpec((1, 1, D_MODEL), per_b),
                  pl.BlockSpec((1, 1, D_MODEL), per_b),
                  pl.BlockSpec((1, D_MODEL), const2),
                  pl.BlockSpec(wn.shape, const2),
                  pl.BlockSpec(wt.shape, const2)]
        + [pl.BlockSpec((1, KV_WIDTH, PAGE_SIZE), page_map(k)) for k in range(n_pages_step)]
        + [pl.BlockSpec(pos4.shape, const4), pl.BlockSpec(w4.shape, const4)],
        out_specs=out_specs,
        scratch_shapes=[pltpu.VMEM((blocks_step * CMP_PITCH, KV_HALF), F32),
                        pltpu.VMEM((blocks_step * CMP_PITCH, KV_HALF), F32)],
    )
    return pl.pallas_call(
        functools.partial(_proj_cmp_kernel, n_pages_step=n_pages_step, tm=tm, tpb=tpb),
        out_shape=out_shape,
        grid_spec=grid_spec,
        compiler_params=_cparams("arbitrary"),
        name="project_prompt_compress_cache",
    )(page_table_flat, x2, shift, scale, g, wn, wt, *([cache_t] * n_pages_step), pos4, w4)


def _compress_rows(src_refs, pos_ref, w_ref, o_ref, n_blocks, pitch):
    rows_per_slice = BLOCK // CMP_SPLIT
    for s in range(2):
        acc = jnp.zeros((n_blocks, KV_HALF), F32)
        for c in range(CMP_SPLIT):
            xc = jnp.concatenate(
                [src_refs[s][pl.ds(c * rows_per_slice + r, n_blocks, stride=pitch), :]
                 for r in range(rows_per_slice)], axis=1) + pos_ref[s, c]
            acc = acc + _dot(xc.astype(BF), w_ref[s, c])
        o_ref[:, s * KV_HALF:(s + 1) * KV_HALF] = acc


def _cmp_kernel(xa_ref, xb_ref, pos_ref, w_ref, o_ref):
    _compress_rows((xa_ref, xb_ref), pos_ref, w_ref, o_ref, CMP_GROUP, BLOCK)


def _compress(xa, xb, pos4, w4):
    m = xa.shape[0] // BLOCK
    rows = CMP_GROUP * BLOCK
    return pl.pallas_call(
        _cmp_kernel,
        out_shape=jax.ShapeDtypeStruct((m, KV_WIDTH), F32),
        grid=(m // CMP_GROUP,),
        in_specs=[pl.BlockSpec((rows, KV_HALF), lambda i: (i, 0)),
                  pl.BlockSpec((rows, KV_HALF), lambda i: (i, 0)),
                  pl.BlockSpec(pos4.shape, lambda i: (0, 0, 0, 0)),
                  pl.BlockSpec(w4.shape, lambda i: (0, 0, 0, 0))],
        out_specs=pl.BlockSpec((CMP_GROUP, KV_WIDTH), lambda i: (i, 0)),
        compiler_params=_cparams("arbitrary"),
        name="compress_blocks",
    )(xa, xb, pos4, w4)


def _attn_kernel(qT_ref, kaug_ref, vsT_ref, kw_ref, vwT_ref, kc_ref, vcT_ref, gT_ref, o_ref,
                 qa_ref, sa_ref, sb_ref, sc_ref, ma_ref, mb_ref, mc_ref, m_ref, acc_ref, outT_ref):
    i = pl.program_id(1)
    q0 = i * Q_TILE
    n_full = lax.div(q0, SEL_CHUNK)
    n_blk = kc_ref.shape[1]
    row = lax.broadcasted_iota(I32, (n_blk, Q_TILE), 0)
    tok = lax.broadcasted_iota(I32, (n_blk, Q_TILE), 1)
    qpos = q0 + tok
    cur = jnp.right_shift(qpos, 6)
    cmask = (row + 1) * BLOCK - 1 <= qpos
    valid = row <= cur
    forced = (row == 0) | (row == cur) | (row == cur - 1)
    key_c = lax.broadcasted_iota(I32, (SEL_CHUNK, Q_TILE), 0)
    qpos_c = q0 + lax.broadcasted_iota(I32, (SEL_CHUNK, Q_TILE), 1)
    hs = [slice(h * HEAD_DIM, (h + 1) * HEAD_DIM) for h in range(N_KV_HEADS)]

    def bias4(keep):
        b = jnp.where(keep, 0.0, -MASK_BIG)
        return jnp.concatenate([b] * GROUP, axis=1)

    def online_step(st, s, v_t, s_max=None):
        m_old = m_ref[st]
        m_new = jnp.maximum(m_old, jnp.max(s, axis=0, keepdims=True) if s_max is None else s_max)
        alpha = jnp.exp2(m_old - m_new)
        p = jnp.exp2(s - m_new)
        acc_ref[st] = alpha * acc_ref[st] + _dot(v_t, p.astype(BF))
        m_ref[st] = m_new

    m_ref[...] = jnp.full(m_ref.shape, NEG_INF, F32)
    acc_ref[...] = jnp.zeros(acc_ref.shape, F32)

    o_c = []
    scores = []
    for h in range(N_KV_HEADS):
        qa_ref[h] = jnp.zeros(qa_ref.shape[1:], BF)
        for g in range(GROUP):
            r0 = h * GROUP * HEAD_DIM + g * HEAD_DIM
            qa_ref[h, hs[h], g * Q_TILE:(g + 1) * Q_TILE] = qT_ref[0, r0:r0 + HEAD_DIM, :]
        sc = _dot(kc_ref[0], qa_ref[h, 0:KV_HALF, :])
        imp = jnp.zeros((n_blk, Q_TILE), F32)
        p_parts = []
        for g in range(GROUP):
            s = jnp.where(cmask, sc[:, g * Q_TILE:(g + 1) * Q_TILE], NEG_INF)
            mx = jnp.max(s, axis=0, keepdims=True)
            mx = jnp.where(mx > NEG_INF, mx, 0.0)
            e = jnp.where(cmask, jnp.exp2(s - mx), 0.0)
            p = e / jnp.maximum(jnp.sum(e, axis=0, keepdims=True), 1e-30)
            imp = imp + p
            p_parts.append(p)
        o_c.append(_dot(vcT_ref[0, hs[h], :], jnp.concatenate(p_parts, axis=1).astype(BF)))
        scores.append(jnp.where(valid, jnp.where(forced, -2.0, imp), -1.0))

    blk_f = lax.broadcasted_iota(I32, (n_blk, N_KV_HEADS * Q_TILE), 0).astype(F32)

    def pick_body(k, work):
        best = jnp.max(work, axis=0, keepdims=True)
        first = jnp.min(jnp.where(work == best, blk_f, float(n_blk)), axis=0, keepdims=True)
        return jnp.where((blk_f == first) & (best >= 0.0), -2.0, work)

    work = lax.fori_loop(0, N_SELECT - N_FORCED, pick_body, jnp.concatenate(scores, axis=1), unroll=True)
    selm1 = jnp.where(work == -2.0, 0.0, -1.0).astype(BF)
    for h in range(N_KV_HEADS):
        for g in range(GROUP):
            qa_ref[h, KV_HALF:KV_HALF + n_blk, g * Q_TILE:(g + 1) * Q_TILE] = selm1[:, h * Q_TILE:(h + 1) * Q_TILE]

    last_chunk = kaug_ref.shape[1] // SEL_CHUNK - 1

    def sel_scores(j, buf):
        dst_ref, max_ref = buf
        j = jnp.minimum(j, last_chunk)
        kt = kaug_ref[0, pl.ds(pl.multiple_of(j * SEL_CHUNK, SEL_CHUNK), SEL_CHUNK), :]
        for h in range(N_KV_HEADS):
            s = _dot(kt, qa_ref[h])
            dst_ref[h] = s
            max_ref[h] = jnp.max(s, axis=0, keepdims=True)

    def sel_process(buf, j, causal):
        src_ref, max_ref = buf
        if causal:
            cbias = bias4(j * SEL_CHUNK + key_c <= qpos_c)
        for h in range(N_KV_HEADS):
            if causal:
                online_step(h, src_ref[h] + cbias, vsT_ref[0, j, h])
            else:
                online_step(h, src_ref[h], vsT_ref[0, j, h], max_ref[h])

    buf_a, buf_b, buf_c = (sa_ref, ma_ref), (sb_ref, mb_ref), (sc_ref, mc_ref)
    sel_scores(0, buf_a)
    sel_scores(1, buf_b)

    w0 = jnp.maximum(q0 - WINDOW, 0)
    j0 = lax.div(w0, WIN_CHUNK)
    jq = lax.div(q0, WIN_CHUNK)
    delta = qpos_c - (w0 + key_c)
    wbias = bias4((delta >= 0) & (delta < WINDOW))
    key_d = lax.broadcasted_iota(I32, (Q_TILE, Q_TILE), 0)
    tok_d = lax.broadcasted_iota(I32, (Q_TILE, Q_TILE), 1)
    dbias = bias4((key_d <= tok_d) & (q0 >= WINDOW))
    kw_a = kw_ref[0, pl.ds(pl.multiple_of(w0, WIN_CHUNK), WINDOW), :]
    kw_b = kw_ref[0, pl.ds(pl.multiple_of(q0, Q_TILE), Q_TILE), :]
    for h in range(N_KV_HEADS):
        qf = qa_ref[h, 0:KV_HALF, :]
        v_a = jnp.concatenate([vwT_ref[0, j0 + c, h] for c in range(WINDOW // WIN_CHUNK)], axis=1)
        v_b = jnp.concatenate([vwT_ref[0, jq + c, h] for c in range(Q_TILE // WIN_CHUNK)], axis=1)
        online_step(2 + h, _dot(kw_a, qf) + wbias, v_a)
        online_step(2 + h, _dot(kw_b, qf) + dbias, v_b)

    def trio_body(t, carry):
        j = 3 * t
        sel_scores(j + 2, buf_c)
        sel_process(buf_a, j, False)
        sel_scores(j + 3, buf_a)
        sel_process(buf_b, j + 1, False)
        sel_scores(j + 4, buf_b)
        sel_process(buf_c, j + 2, False)
        return carry

    n_trios = lax.div(n_full, 3)
    lax.fori_loop(0, n_trios, trio_body, 0)
    j_last = 3 * n_trios
    n_left = n_full - j_last

    @pl.when(n_left == 0)
    def _():
        sel_process(buf_a, j_last, True)

    @pl.when(n_left == 1)
    def _():
        sel_process(buf_a, j_last, False)
        sel_process(buf_b, j_last + 1, True)

    @pl.when(n_left == 2)
    def _():
        sel_scores(j_last + 2, buf_c)
        sel_process(buf_a, j_last, False)
        sel_process(buf_b, j_last + 1, False)
        sel_process(buf_c, j_last + 2, True)

    for h in range(N_KV_HEADS):
        o_s = acc_ref[h, 0:HEAD_DIM, :] / acc_ref[h, HEAD_DIM:HEAD_DIM + 1, :]
        o_w = acc_ref[2 + h, 0:HEAD_DIM, :] / acc_ref[2 + h, HEAD_DIM:HEAD_DIM + 1, :]
        for g in range(GROUP):
            gs = slice(g * Q_TILE, (g + 1) * Q_TILE)
            gr = h * GROUP * 3 + g * 3
            og = (gT_ref[0, gr:gr + 1, :] * o_c[h][:, gs] + gT_ref[0, gr + 1:gr + 2, :] * o_s[:, gs]
                  + gT_ref[0, gr + 2:gr + 3, :] * o_w[:, gs])
            r0 = h * GROUP * HEAD_DIM + g * HEAD_DIM
            outT_ref[r0:r0 + HEAD_DIM, :] = og

    o_ref[0] = outT_ref[...].T.astype(BF)


def _attention_prompt(qT, kaug, vsT, kw, vwT, kc, vcT, gT, n_batch, seq):
    per_b3 = lambda n, i: (n, 0, 0)
    per_b5 = lambda n, i: (n, 0, 0, 0, 0)
    rows = GROUP * Q_TILE
    return pl.pallas_call(
        _attn_kernel,
        out_shape=jax.ShapeDtypeStruct((n_batch, seq, NSA_WIDTH), BF),
        grid=(n_batch, seq // Q_TILE),
        in_specs=[pl.BlockSpec((1, NSA_WIDTH, Q_TILE), lambda n, i: (n, 0, i)),
                  pl.BlockSpec((1, seq, 256), per_b3),
                  pl.BlockSpec((1, seq // SEL_CHUNK, N_KV_HEADS, V_ROWS, SEL_CHUNK), per_b5),
                  pl.BlockSpec((1, seq, KV_HALF), per_b3),
                  pl.BlockSpec((1, seq // WIN_CHUNK, N_KV_HEADS, V_ROWS, WIN_CHUNK), per_b5),
                  pl.BlockSpec((1, seq // BLOCK, KV_HALF), per_b3),
                  pl.BlockSpec((1, KV_HALF, seq // BLOCK), per_b3),
                  pl.BlockSpec((1, 32, Q_TILE), lambda n, i: (n, 0, i))],
        out_specs=pl.BlockSpec((1, Q_TILE, NSA_WIDTH), lambda n, i: (n, i, 0)),
        scratch_shapes=[pltpu.VMEM((N_KV_HEADS, 256, rows), BF),
                        pltpu.VMEM((N_KV_HEADS, SEL_CHUNK, rows), F32),
                        pltpu.VMEM((N_KV_HEADS, SEL_CHUNK, rows), F32),
                        pltpu.VMEM((N_KV_HEADS, SEL_CHUNK, rows), F32),
                        pltpu.VMEM((N_KV_HEADS, 1, rows), F32),
                        pltpu.VMEM((N_KV_HEADS, 1, rows), F32),
                        pltpu.VMEM((N_KV_HEADS, 1, rows), F32),
                        pltpu.VMEM((2 * N_KV_HEADS, 1, rows), F32),
                        pltpu.VMEM((2 * N_KV_HEADS, V_ROWS, rows), F32),
                        pltpu.VMEM((NSA_WIDTH, Q_TILE), F32)],
        compiler_params=_cparams("arbitrary", "arbitrary"),
        name="attention_prompt",
    )(qT, kaug, vsT, kw, vwT, kc, vcT, gT)


def _pool_kernel(u_ref, halo_ref, wp_ref, ps_ref, o_ref, ext_ref, lvl_ref, *, tm, tpb):
    t = pl.program_id(0) % tpb
    ext_ref[0:16, :] = jnp.where(t == 0, 0.0, halo_ref[...])
    u = u_ref[...]
    ext_ref[16:16 + tm, :] = u
    pos = t * tm + lax.broadcasted_iota(I32, (tm, 1), 0)
    end = 16 + tm
    outs = []
    for gi, w in enumerate(POOL_WINDOWS):
        cs = slice(gi * POOL_GROUP_DIM, (gi + 1) * POOL_GROUP_DIM)
        lo = 16 - (w - 2)
        acc = ext_ref[lo:end, cs] + ext_ref[lo - 1:end - 1, cs]
        d = 2
        while d < w:
            lvl_ref[lo:end, :] = acc
            lo += d
            acc = lvl_ref[lo:end, :] + lvl_ref[lo - d:end - d, :]
            d *= 2
        cnt = jnp.minimum(pos + 1, w).astype(F32)
        pooled = acc / cnt - u[:, cs]
        outs.append(_dot(pooled.astype(BF), wp_ref[gi]))
    o_ref[...] = (jnp.concatenate(outs, axis=1) * ps_ref[...]).astype(BF)


def _pool_prompt(u, w_pool, pool_scale, n_batch, seq):
    tm = TOKEN_TILE
    tpb = seq // tm
    nt = n_batch * seq
    return pl.pallas_call(
        functools.partial(_pool_kernel, tm=tm, tpb=tpb),
        out_shape=jax.ShapeDtypeStruct((nt, POOL_WIDTH), BF),
        grid=(nt // tm,),
        in_specs=[pl.BlockSpec((tm, POOL_WIDTH), lambda t: (t, 0)),
                  pl.BlockSpec((16, POOL_WIDTH), lambda t: (jnp.maximum(t * (tm // 16) - 1, 0), 0)),
                  pl.BlockSpec(w_pool.shape, lambda t: (0, 0, 0)),
                  pl.BlockSpec((1, POOL_WIDTH), lambda t: (0, 0))],
        out_specs=pl.BlockSpec((tm, POOL_WIDTH), lambda t: (t, 0)),
        scratch_shapes=[pltpu.VMEM((tm + 16, POOL_WIDTH), F32), pltpu.VMEM((tm + 16, POOL_GROUP_DIM), F32)],
        compiler_params=_cparams("arbitrary"),
        name="pool_prompt",
    )(u, u, w_pool, pool_scale)


def _spool_kernel(u_ref, hist_ref, wp_ref, ps_ref, o_ref):
    u = u_ref[...]
    outs = []
    for gi, w in enumerate(POOL_WINDOWS):
        cs = slice(gi * POOL_GROUP_DIM, (gi + 1) * POOL_GROUP_DIM)
        acc = u[:, cs]
        for k in range(1, w):
            acc = acc + hist_ref[POOL_HIST - k, :, cs]
        pooled = acc / float(w) - u[:, cs]
        outs.append(_dot(pooled.astype(BF), wp_ref[gi]))
    o_ref[...] = (jnp.concatenate(outs, axis=1) * ps_ref[...]).astype(BF)


def _pool_sample(u, hist_t, w_pool, pool_scale):
    n = u.shape[0]
    return pl.pallas_call(
        _spool_kernel,
        out_shape=jax.ShapeDtypeStruct((n, POOL_WIDTH), BF),
        name="pool_sample",
    )(u, hist_t, w_pool, pool_scale)


def _merge_kernel(x_ref, sh_ref, sc_ref, gate_ref, gpre_ref, gpost_ref, onsa_ref, opool_ref,
                  wgm_ref, wun_ref, wup_ref, wo_ref, o_ref):
    x = x_ref[...]
    h = _rms(x, gpre_ref[...]) * (1.0 + sc_ref[0]) + sh_ref[0]
    gm = jax.nn.sigmoid(_dot(h.astype(BF), wgm_ref[...]))
    m = (gm[:, :D_MODEL] * _dot(onsa_ref[...], wun_ref[...])
         + gm[:, D_MODEL:] * _dot(opool_ref[...], wup_ref[...]))
    m = _dot(m.astype(BF), wo_ref[...])
    o_ref[...] = x + gate_ref[0] * _rms(m, gpost_ref[...])


def _mlp_kernel(x_ref, sh_ref, sc_ref, gate_ref, gpre_ref, gpost_ref, w1_ref, w2_ref, o_ref):
    x = x_ref[...]
    h = _rms(x, gpre_ref[...]) * (1.0 + sc_ref[0]) + sh_ref[0]
    hb = h.astype(BF)
    f = jnp.zeros(x.shape, F32)
    fc = 1024
    for c in range(D_FF // fc):
        a = jnp.maximum(_dot(hb, w1_ref[:, c * fc:(c + 1) * fc]), 0.0)
        f = f + _dot((a * a).astype(BF), w2_ref[c * fc:(c + 1) * fc, :])
    o_ref[...] = x + gate_ref[0] * _rms(f, gpost_ref[...])


def _token_call(kernel, name, x2, mods, tm, rows_per_mod, extra_tok, consts):
    nt = x2.shape[0]
    r = mods[0].shape[1]
    mod_spec = pl.BlockSpec((1, r, D_MODEL), lambda t: ((t * tm) // rows_per_mod, 0, 0))
    in_specs = [pl.BlockSpec((tm, D_MODEL), lambda t: (t, 0))] + [mod_spec] * len(mods)
    in_specs += [pl.BlockSpec((1, D_MODEL), lambda t: (0, 0))] * 2
    in_specs += [pl.BlockSpec((tm, a.shape[1]), lambda t: (t, 0)) for a in extra_tok]
    in_specs += [pl.BlockSpec(w.shape, lambda t: (0, 0), pipeline_mode=pl.Buffered(1)) for w in consts[2:]]
    return pl.pallas_call(
        kernel,
        out_shape=jax.ShapeDtypeStruct((nt, D_MODEL), F32),
        grid=(nt // tm,),
        in_specs=in_specs,
        out_specs=pl.BlockSpec((tm, D_MODEL), lambda t: (t, 0)),
        compiler_params=_cparams("arbitrary"),
        name=name,
    )(x2, *mods, consts[0], consts[1], *extra_tok, *consts[2:])


def _sproj_kernel(x_ref, sh_ref, sc_ref, g_ref, wn_ref, wt_ref, z_ref, zs_ref, zT_ref):
    x = x_ref[...]
    h = _rms(x, g_ref[...]) * (1.0 + sc_ref[...]) + sh_ref[...]
    hb = h.astype(BF)
    z = _dot(hb, wn_ref[...])
    z_ref[...] = z
    zs_ref[...] = jax.nn.sigmoid(z)
    zT_ref[...] = _dot_nt(wt_ref[...], hb)


def _project_sample(x2, shift, scale, g, wn, wt):
    n = x2.shape[0]
    shp = jax.ShapeDtypeStruct((n, wn.shape[1]), F32)
    return pl.pallas_call(
        _sproj_kernel,
        out_shape=(shp, shp, jax.ShapeDtypeStruct((wt.shape[0], n), F32)),
        name="project_sample",
    )(x2, shift, scale, g, wn, wt)


def _sattn_init(kaug_ref, past_len, seq_step):
    @pl.when(pl.program_id(0) == 0)
    def _():
        blk = lax.broadcasted_iota(I32, (128, past_len), 0)
        key_blk = jnp.right_shift(lax.broadcasted_iota(I32, (128, past_len), 1), 6)
        onehot = jnp.where(blk == key_blk, MASK_BIG, 0.0).astype(BF)
        for q in range(seq_step):
            kaug_ref[q, 128:256, :] = onehot


def _sattn_main(*refs, n_pages, past_len, seq_step):
    pages = refs[:seq_step * n_pages]
    (qb_ref, kvc_ref, win_ref, ksn_ref, kwn_ref, kwnT_ref, g_ref,
     o_ref, nwin_ref, kaug_ref, vall_ref) = refs[seq_step * n_pages:]

    for q in range(seq_step):
        for p in range(n_pages):
            pg = pages[q * n_pages + p][0]
            kaug_ref[q, 0:128, p * PAGE_SIZE:(p + 1) * PAGE_SIZE] = pg[0:KV_HALF, :].astype(BF)
            vall_ref[q, :, p * PAGE_SIZE:(p + 1) * PAGE_SIZE] = pg[KV_HALF:KV_WIDTH, :].astype(BF)

    nb_past = past_len // BLOCK
    win_buf = win_ref.shape[2]
    n_rows = 8 * seq_step
    per_seq = lambda f: jnp.concatenate([f(q) for q in range(seq_step)], axis=0)
    rows_of = lambda x, q: x[8 * q:8 * (q + 1)]
    qb = qb_ref[...].reshape(n_rows, KV_HALF).astype(BF)
    qf = qb.astype(F32)
    row = lax.broadcasted_iota(I32, (n_rows, 128), 0)
    lane = lax.broadcasted_iota(I32, (n_rows, 128), 1)

    def new_key(ref):
        return per_seq(lambda q: jnp.broadcast_to(ref[q], (8, KV_WIDTH))).astype(BF).astype(F32)

    s_c = per_seq(lambda q: _dot_nt(rows_of(qb, q), kvc_ref[q, :, 0:128].astype(BF)))
    cm = lane < nb_past
    s_c = jnp.where(cm, s_c, NEG_INF)
    mx = jnp.max(s_c, axis=1, keepdims=True)
    mx = jnp.where(mx > NEG_INF, mx, 0.0)
    e = jnp.where(cm, jnp.exp(s_c - mx), 0.0)
    p_c = e / jnp.maximum(jnp.sum(e, axis=1, keepdims=True), 1e-30)
    o_c = per_seq(lambda q: _dot(rows_of(p_c, q).astype(BF), kvc_ref[q, :, 128:256].astype(BF)))

    imp = jnp.zeros((n_rows, 128), F32)
    for grp in range(n_rows // GROUP):
        in_grp = jnp.right_shift(row, 2) == grp
        imp = jnp.where(in_grp, jnp.sum(jnp.where(in_grp, p_c, 0.0), axis=0, keepdims=True), imp)
    cur = nb_past
    forced = (lane == 0) | (lane == cur) | (lane == cur - 1)
    score = jnp.where(lane <= cur, imp + FORCE_SCORE * forced.astype(F32), -1.0)
    cnt = jnp.zeros((n_rows, 128), F32)
    for bp in range(nb_past + 1):
        other = score[:, bp:bp + 1]
        ahead = (other > score) | ((other == score) & (bp < lane))
        cnt = cnt + jnp.where(ahead, 1.0, 0.0)
    sel = (cnt < float(N_SELECT)) & (score >= 0.0)
    selm1 = jnp.where(sel, 0.0, -1.0).astype(BF)

    qaug = jnp.concatenate([qb, selm1], axis=1)
    s_s = per_seq(lambda q: _dot(rows_of(qaug, q), kaug_ref[q]))
    kv_n = new_key(ksn_ref)
    s_n = jnp.sum(qf * kv_n[:, 0:128], axis=1, keepdims=True)
    m_s = jnp.maximum(jnp.max(s_s, axis=1, keepdims=True), s_n)
    e_s = jnp.exp(s_s - m_s)
    e_n = jnp.exp(s_n - m_s)
    l_s = jnp.sum(e_s, axis=1, keepdims=True) + e_n
    pv = per_seq(lambda q: _dot_nt(rows_of(e_s, q).astype(BF), vall_ref[q]))
    o_s = (pv + e_n.astype(BF).astype(F32) * kv_n[:, 128:256]) / l_s

    s_w = per_seq(lambda q: _dot(rows_of(qb, q), win_ref[q, 0:KV_HALF, :].astype(BF)))
    lane_w = lax.broadcasted_iota(I32, (n_rows, win_buf), 1)
    s_w = jnp.where(lane_w >= win_buf + 1 - WINDOW, s_w, NEG_INF)
    kv_n = new_key(kwn_ref)
    s_n = jnp.sum(qf * kv_n[:, 0:128], axis=1, keepdims=True)
    m_w = jnp.maximum(jnp.max(s_w, axis=1, keepdims=True), s_n)
    e_w = jnp.exp(s_w - m_w)
    e_n = jnp.exp(s_n - m_w)
    l_w = jnp.sum(e_w, axis=1, keepdims=True) + e_n
    pv = per_seq(lambda q: _dot_nt(rows_of(e_w, q).astype(BF), win_ref[q, KV_HALF:KV_WIDTH, :].astype(BF)))
    o_w = (pv + e_n.astype(BF).astype(F32) * kv_n[:, 128:256]) / l_w

    g = g_ref[...].reshape(n_rows, 3)
    o_ref[...] = (g[:, 0:1] * o_c + g[:, 1:2] * o_s + g[:, 2:3] * o_w).reshape(seq_step, 8, KV_HALF)

    seq_lane = lax.broadcasted_iota(I32, kwnT_ref.shape, 1)
    row_lane = lax.broadcasted_iota(I32, (KV_WIDTH, win_buf), 1)
    for q in range(seq_step):
        n = pl.program_id(0) * seq_step + q
        new_col = jnp.sum(jnp.where(seq_lane == n, kwnT_ref[...], 0.0), axis=1, keepdims=True)
        shifted = pltpu.roll(win_ref[q], win_buf - 1, axis=1)
        nwin_ref[q] = jnp.where(row_lane == win_buf - 1, new_col, shifted)


def _sattn_kernel(pt_ref, *refs, n_pages, past_len, seq_step):
    _sattn_init(refs[-2], past_len, seq_step)
    _sattn_main(*refs, n_pages=n_pages, past_len=past_len, seq_step=seq_step)


def _attention_sample(page_table_flat, cache_t, qblk, kvc_pad, win_t, kvs_new, kvw_new, kvw_new_t, gates8,
                      n_pages, past_len):
    n_seq = qblk.shape[0]
    win_buf = win_t.shape[2]
    g = SAMPLE_SEQ_STEP
    const2 = lambda t, pt: (0, 0)
    per_step = lambda t, pt: (t, 0, 0)

    def page_map(k):
        return lambda t, pt: (pt[t * g * n_pages + k], 0, 0)

    grid_spec = pltpu.PrefetchScalarGridSpec(
        num_scalar_prefetch=1,
        grid=(n_seq // g,),
        in_specs=[pl.BlockSpec((1, KV_WIDTH, PAGE_SIZE), page_map(k)) for k in range(g * n_pages)]
        + [pl.BlockSpec((g, 8, 128), per_step),
           pl.BlockSpec((g, 128, KV_WIDTH), per_step),
           pl.BlockSpec((g, KV_WIDTH, win_buf), per_step),
           pl.BlockSpec((g, 1, KV_WIDTH), per_step),
           pl.BlockSpec((g, 1, KV_WIDTH), per_step),
           pl.BlockSpec(kvw_new_t.shape, const2),
           pl.BlockSpec((g, 8, 3), per_step)],
        out_specs=(pl.BlockSpec((g, 8, 128), per_step),
                   pl.BlockSpec((g, KV_WIDTH, win_buf), per_step)),
        scratch_shapes=[pltpu.VMEM((g, 256, past_len), BF), pltpu.VMEM((g, KV_HALF, past_len), BF)],
    )
    return pl.pallas_call(
        functools.partial(_sattn_kernel, n_pages=n_pages, past_len=past_len, seq_step=g),
        out_shape=(jax.ShapeDtypeStruct((n_seq, 8, 128), F32),
                   jax.ShapeDtypeStruct((n_seq, KV_WIDTH, win_buf), F32)),
        grid_spec=grid_spec,
        compiler_params=_cparams("arbitrary"),
        name="attention_sample",
    )(page_table_flat, *([cache_t] * (g * n_pages)), qblk, kvc_pad, win_t, kvs_new, kvw_new, kvw_new_t, gates8)


def _kv_rows_view(kv_t):
    n, _, t = kv_t.shape
    return jnp.transpose(kv_t.reshape(n, 2, N_KV_HEADS, HEAD_DIM, t), (0, 4, 1, 2, 3))


def _kv_feat_view(kv):
    n, t = kv.shape[:2]
    return jnp.transpose(kv, (0, 2, 3, 4, 1)).reshape(n, KV_WIDTH, t)


def kernel(x_prompt, x_sample, cache_cmp_kv, cache_sel_kv, state_win_kv, state_pool, page_table, c_prompt, c_sample, w_ada, b_ada, g_pre_mix, g_post_mix, g_pre_mlp, g_post_mlp, w_in, w_cmp, pos_cmp, w_pool, pool_scale, w_up_nsa, w_up_pool, w_o, w_ff1, w_ff2):
    n_batch, seq, _ = x_prompt.shape
    n_seq = x_sample.shape[0]
    n_pages = page_table.shape[1]
    past_len = n_pages * PAGE_SIZE
    nb_past = past_len // BLOCK
    assert x_sample.shape[1] == 1 and w_ada.shape[0] == 1 and past_len % BLOCK == 0
    assert seq % TOKEN_TILE == 0 and seq // BLOCK == 128 and state_win_kv.shape[2] == WINDOW
    assert (n_seq * nb_past) % CMP_GROUP == 0 and n_seq == CMP_GROUP
    assert FORCE_SCORE > GROUP

    w_t = w_in[0].T
    wt = jnp.pad(w_t[0:1304], ((0, 8), (0, 0))).astype(BF)
    wn = jnp.concatenate([w_t[768:896], w_t[1024:1152], w_t[1304:1816], w_t[512:768]], axis=0).T.astype(BF)
    ws = jnp.pad(w_t[0:1816], ((0, 104), (0, 0))).T.astype(BF)
    wgm = w_t[1816:3864].T.astype(BF)
    eye = jnp.eye(N_KV_HEADS, dtype=F32)
    wc = w_cmp[0].astype(BF)
    wz = jnp.zeros_like(wc)
    w4 = jnp.concatenate([jnp.concatenate([wc, wz], axis=3), jnp.concatenate([wz, wc], axis=3)], axis=2)
    w4 = w4.reshape(2, CMP_SPLIT, CMP_K // CMP_SPLIT, KV_HALF)
    pos4 = jnp.broadcast_to(jnp.transpose(pos_cmp[0], (1, 0, 2))[:, :, None, :],
                            (2, BLOCK, N_KV_HEADS, HEAD_DIM)).reshape(2, CMP_SPLIT, 1, CMP_K // CMP_SPLIT)
    wp = w_pool[0].astype(BF)
    ps = pool_scale[0].reshape(1, POOL_WIDTH)
    wun, wup, wo = w_up_nsa[0].astype(BF), w_up_pool[0].astype(BF), w_o[0].astype(BF)
    w1, w2 = w_ff1[0].astype(BF), w_ff2[0].astype(BF)
    gpm, gqm = g_pre_mix[0].reshape(1, D_MODEL), g_post_mix[0].reshape(1, D_MODEL)
    gpf, gqf = g_pre_mlp[0].reshape(1, D_MODEL), g_post_mlp[0].reshape(1, D_MODEL)

    n_c = n_batch + n_seq
    c_all = jnp.pad(jnp.concatenate([c_prompt, c_sample], axis=0), ((0, (-n_c) % 8), (0, 0)))
    ada = _adaln(c_all, w_ada[0], b_ada[0].reshape(1, -1))
    ada_p = ada[:n_batch].reshape(n_batch, 6, 1, D_MODEL)
    ada_s = ada[n_batch:n_c].reshape(n_seq, 6, D_MODEL)
    mods_p = [ada_p[:, k] for k in range(6)]
    mods_s = [ada_s[:, k][None] for k in range(6)]

    xp = x_prompt.reshape(n_batch * seq, D_MODEL)
    pt_flat = page_table.reshape(-1)
    (kvcT, kvsT, kvwT, kvc_a, kvc_b, u_p, qT, kaug, vsT, kw, vwT, gT,
     kvc_past) = _project_prompt_compress_cache(xp, mods_p[0], mods_p[1], gpm, wn, wt, n_batch, seq,
                                                _kv_feat_view(cache_cmp_kv[0]), pt_flat, pos4, w4)
    kvc_blk = _compress(kvc_a, kvc_b, pos4, w4).reshape(n_batch, seq // BLOCK, KV_WIDTH)
    kc = kvc_blk[:, :, 0:128].astype(BF)
    vcT = jnp.swapaxes(kvc_blk[:, :, 128:256], 1, 2).astype(BF)
    onsa_p = _attention_prompt(qT, kaug, vsT, kw, vwT, kc, vcT, gT, n_batch, seq)
    opool_p = _pool_prompt(u_p, wp, ps, n_batch, seq)
    x1_p = _token_call(_merge_kernel, "merge_prompt", xp, [mods_p[0], mods_p[1], mods_p[2]], TOKEN_TILE, seq,
                       [onsa_p.reshape(n_batch * seq, NSA_WIDTH), opool_p], [gpm, gqm, wgm, wun, wup, wo])
    y_p = _token_call(_mlp_kernel, "mlp_prompt", x1_p, [mods_p[3], mods_p[4], mods_p[5]], TOKEN_TILE, seq,
                      [], [gpf, gqf, w1, w2])

    xs = x_sample.reshape(n_seq, D_MODEL)
    z, zsig, zT = _project_sample(xs, mods_s[0][0], mods_s[1][0], gpm, ws, wt)
    q_s = z[:, 0:512] * (HEAD_DIM ** -0.5)
    kvc_n, kvs_n, kvw_n = z[:, 512:768], z[:, 768:1024], z[:, 1024:1280]
    gates_s = zsig[:, 1280:1304].reshape(n_seq, N_HEADS, 3)
    u_s = z[:, 1304:1816]
    q5 = q_s.reshape(n_seq, N_KV_HEADS, GROUP, 1, HEAD_DIM) * eye[None, :, None, :, None]
    qblk = q5.reshape(n_seq, N_HEADS, KV_HALF)

    last_a = jnp.pad(kvc_n[:, None, 0:128], ((0, 0), (0, BLOCK - 1), (0, 0))).reshape(n_seq * BLOCK, KV_HALF)
    last_b = jnp.pad(kvc_n[:, None, 128:256], ((0, 0), (0, BLOCK - 1), (0, 0))).reshape(n_seq * BLOCK, KV_HALF)
    kvc_last = _compress(last_a, last_b, pos4, w4)
    kvc_s = jnp.concatenate([kvc_past.reshape(n_seq, nb_past, KV_WIDTH), kvc_last[:, None, :]], axis=1)
    kvc_pad = jnp.pad(kvc_s, ((0, 0), (0, 128 - nb_past - 1), (0, 0)))
    o8, new_win_t = _attention_sample(
        pt_flat, _kv_feat_view(cache_sel_kv[0]), qblk, kvc_pad, _kv_feat_view(state_win_kv[0]),
        kvs_n[:, None, :], kvw_n[:, None, :], zT[1024:1280], gates_s, n_pages, past_len)
    o5 = o8.reshape(n_seq, N_KV_HEADS, GROUP, N_KV_HEADS, HEAD_DIM)
    onsa_s = jnp.concatenate([o5[:, 0, :, 0, :], o5[:, 1, :, 1, :]], axis=1).reshape(n_seq, NSA_WIDTH).astype(BF)
    opool_s = _pool_sample(u_s, jnp.swapaxes(state_pool[0], 0, 1), wp, ps)
    x1_s = _token_call(_merge_kernel, "merge_sample", xs, [mods_s[0], mods_s[1], mods_s[2]], n_seq, n_seq,
                       [onsa_s, opool_s], [gpm, gqm, wgm, wun, wup, wo])
    y_s = _token_call(_mlp_kernel, "mlp_sample", x1_s, [mods_s[3], mods_s[4], mods_s[5]], n_seq, n_seq,
                      [], [gpf, gqf, w1, w2])

    win_p = min(WINDOW, seq)
    new_kv_s = lambda rows: _kv_rows_view(rows.reshape(1, KV_WIDTH, n_seq))[0][None, :, None]
    return (
        y_p.reshape(n_batch, seq, D_MODEL),
        y_s.reshape(n_seq, 1, D_MODEL),
        _kv_rows_view(kvcT)[None],
        _kv_rows_view(kvsT)[None],
        _kv_rows_view(kvwT[:, :, seq - win_p:])[None],
        u_p.reshape(n_batch, seq, POOL_WIDTH)[None, :, seq - POOL_HIST:],
        new_kv_s(zT[512:768]),
        new_kv_s(zT[768:1024]),
        _kv_rows_view(new_win_t)[None],
        jnp.concatenate([state_pool[0][:, 1:], u_s[:, None, :]], axis=1)[None],
    )
```

```python
import functools
import math

import jax
import jax.numpy as jnp
from jax import lax
from jax.experimental import pallas as pl
from jax.experimental.pallas import tpu as pltpu

D_MODEL = 1024
N_HEADS = 8
HEAD_DIM = 64
N_KV_HEADS = 2
GROUP = N_HEADS // N_KV_HEADS
BLOCK = 64
N_SELECT = 16
WINDOW = 512
Q_TILE = 256
WIN_CHUNK = 128
NSA_WIDTH = N_HEADS * HEAD_DIM
KV_WIDTH = 2 * N_KV_HEADS * HEAD_DIM
KV_HALF = N_KV_HEADS * HEAD_DIM
FORCE_SCORE = 16.0
N_FORCED = 3
POOL_WINDOWS = (2, 4, 8, 16)
POOL_WIDTH = 512
POOL_GROUP_DIM = 128
POOL_HIST = 15
D_FF = 4 * D_MODEL
EPS = 1e-6
PAGE_SIZE = 128
V_ROWS = HEAD_DIM + 16
CMP_K = BLOCK * KV_HALF
CMP_SPLIT = 4
CMP_GROUP = 128
CMP_PITCH = BLOCK + 4

BF = jnp.bfloat16
F32 = jnp.float32
I32 = jnp.int32
MASK_BIG = 2.0 ** 100
NEG_INF = float("-inf")
LOG2E = math.log2(math.e)

TOKEN_TILE = 512
PROJ_TILE = 256
SEL_CHUNK = 512
SAMPLE_SEQ_STEP = 4
VMEM_LIMIT = 56 * 1024 * 1024


def _cparams(*sem):
    return pltpu.CompilerParams(dimension_semantics=sem, vmem_limit_bytes=VMEM_LIMIT)


def _rms(x, g):
    return x * lax.rsqrt(jnp.mean(x * x, axis=-1, keepdims=True) + EPS) * g


def _dot(a, b):
    return jnp.dot(a, b, preferred_element_type=F32)


def _dot_nt(a, b):
    return lax.dot_general(a, b, (((1,), (1,)), ((), ())), preferred_element_type=F32)


def _ada_kernel(c_ref, w_ref, b_ref, o_ref):
    c = c_ref[...]
    a = (c * jax.nn.sigmoid(c)).astype(BF)
    o_ref[...] = _dot(a, w_ref[...].astype(BF)) + b_ref[...]


def _adaln(c_all, w_ada, b_ada):
    rows = c_all.shape[0]
    n_out = w_ada.shape[1]
    tn = 512
    return pl.pallas_call(
        _ada_kernel,
        out_shape=jax.ShapeDtypeStruct((rows, n_out), F32),
        grid=(n_out // tn,),
        in_specs=[pl.BlockSpec((rows, D_MODEL), lambda j: (0, 0)),
                  pl.BlockSpec((D_MODEL, tn), lambda j: (0, j)),
                  pl.BlockSpec((1, tn), lambda j: (0, j))],
        out_specs=pl.BlockSpec((rows, tn), lambda j: (0, j)),
        compiler_params=_cparams("arbitrary"),
        name="adaln",
    )(c_all, w_ada, b_ada)


def _proj_kernel(x_ref, sh_ref, sc_ref, g_ref, wn_ref, wt_ref,
                 kvcT_ref, kvsT_ref, kvwT_ref, ca_ref, cb_ref, u_ref,
                 qT_ref, kaug_ref, vsT_ref, kw_ref, vwT_ref, gT_ref, *, tm, tpb):
    x = x_ref[...]
    h = _rms(x, g_ref[...]) * (1.0 + sc_ref[0]) + sh_ref[0]
    hb = h.astype(BF)
    zn = _dot(hb, wn_ref[...])
    zt = _dot_nt(wt_ref[...], hb)
    kvcT_ref[0] = zt[512:768]
    kvsT_ref[0] = zt[768:1024]
    kvwT_ref[0] = zt[1024:1280]
    ca_ref[...] = zn[:, 768:896]
    cb_ref[...] = zn[:, 896:1024]
    u_ref[...] = zn[:, 256:768]
    qT_ref[0] = (zt[0:512] * (HEAD_DIM ** -0.5 * LOG2E)).astype(BF)
    ones = jnp.ones((V_ROWS - HEAD_DIM, tm), BF)
    for h in range(N_KV_HEADS):
        vs_h = zt[896 + h * HEAD_DIM:896 + (h + 1) * HEAD_DIM].astype(BF)
        vw_h = zt[1152 + h * HEAD_DIM:1152 + (h + 1) * HEAD_DIM].astype(BF)
        vsT_ref[0, 0, h, 0:HEAD_DIM, :] = vs_h
        vsT_ref[0, 0, h, HEAD_DIM:V_ROWS, :] = ones
        for c in range(tm // WIN_CHUNK):
            cs = slice(c * WIN_CHUNK, (c + 1) * WIN_CHUNK)
            vwT_ref[0, c, h, 0:HEAD_DIM, :] = vw_h[:, cs]
            vwT_ref[0, c, h, HEAD_DIM:V_ROWS, :] = ones[:, cs]
    gT_ref[0] = jax.nn.sigmoid(zt[1280:1312])
    t0 = (pl.program_id(0) % tpb) * tm
    blk = jnp.right_shift(t0 + lax.broadcasted_iota(I32, (tm, 128), 0), 6)
    lane = lax.broadcasted_iota(I32, (tm, 128), 1)
    kaug_ref[0, :, 0:128] = zn[:, 0:128].astype(BF)
    kaug_ref[0, :, 128:256] = jnp.where(blk == lane, MASK_BIG, 0.0).astype(BF)
    kw_ref[0] = zn[:, 128:256].astype(BF)


def _proj_cmp_kernel(pt_ref, *refs, n_pages_step, tm, tpb):
    proj_in = refs[:6]
    pages = refs[6:6 + n_pages_step]
    pos_ref, w_ref = refs[6 + n_pages_step:8 + n_pages_step]
    proj_out = refs[8 + n_pages_step:20 + n_pages_step]
    o_ref, sa_ref, sb_ref = refs[20 + n_pages_step:]
    _proj_kernel(*proj_in, *proj_out, tm=tm, tpb=tpb)
    blocks_per_page = PAGE_SIZE // BLOCK
    for k in range(n_pages_step):
        pg = pages[k][0]
        for s, dst in enumerate((sa_ref, sb_ref)):
            rows = pg[s * KV_HALF:(s + 1) * KV_HALF, :].T
            for b in range(blocks_per_page):
                m = k * blocks_per_page + b
                dst[m * CMP_PITCH:m * CMP_PITCH + BLOCK, :] = rows[b * BLOCK:(b + 1) * BLOCK, :]
    _compress_rows((sa_ref, sb_ref), pos_ref, w_ref, o_ref, n_pages_step * blocks_per_page, CMP_PITCH)


def _project_prompt_compress_cache(x2, shift, scale, g, wn, wt, n_batch, seq,
                                   cache_t, page_table_flat, pos4, w4):
    tm = PROJ_TILE
    tpb = seq // tm
    nt = n_batch * seq
    n_steps = nt // tm
    blocks_per_page = PAGE_SIZE // BLOCK
    n_pages_all = page_table_flat.shape[0]
    n_pages_step = n_pages_all // n_steps
    assert n_pages_step * n_steps == n_pages_all
    blocks_step = n_pages_step * blocks_per_page
    chunk_steps = SEL_CHUNK // tm
    tok = lambda t, pt: (t, 0)
    per_b = lambda t, pt: (t // tpb, 0, 0)
    featT = lambda t, pt: (t // tpb, 0, t % tpb)
    rows3 = lambda t, pt: (t // tpb, t % tpb, 0)
    rows5 = lambda t, pt: (t // tpb, t % tpb, 0, 0, 0)
    chunk5 = lambda t, pt: (t // tpb, (t % tpb) // chunk_steps, 0, 0, t % chunk_steps)
    const2 = lambda t, pt: (0, 0)
    const4 = lambda t, pt: (0, 0, 0, 0)

    def page_map(k):
        return lambda t, pt: (pt[t * n_pages_step + k], 0, 0)

    kvT = jax.ShapeDtypeStruct((n_batch, KV_WIDTH, seq), F32)
    out_shape = (
        kvT, kvT, kvT,
        jax.ShapeDtypeStruct((nt, KV_HALF), F32),
        jax.ShapeDtypeStruct((nt, KV_HALF), F32),
        jax.ShapeDtypeStruct((nt, POOL_WIDTH), F32),
        jax.ShapeDtypeStruct((n_batch, NSA_WIDTH, seq), BF),
        jax.ShapeDtypeStruct((n_batch, seq, 256), BF),
        jax.ShapeDtypeStruct((n_batch, seq // SEL_CHUNK, N_KV_HEADS, V_ROWS, SEL_CHUNK), BF),
        jax.ShapeDtypeStruct((n_batch, seq, KV_HALF), BF),
        jax.ShapeDtypeStruct((n_batch, seq // WIN_CHUNK, N_KV_HEADS, V_ROWS, WIN_CHUNK), BF),
        jax.ShapeDtypeStruct((n_batch, 32, seq), F32),
        jax.ShapeDtypeStruct((n_pages_all * blocks_per_page, KV_WIDTH), F32),
    )
    out_specs = (
        pl.BlockSpec((1, KV_WIDTH, tm), featT),
        pl.BlockSpec((1, KV_WIDTH, tm), featT),
        pl.BlockSpec((1, KV_WIDTH, tm), featT),
        pl.BlockSpec((tm, KV_HALF), tok),
        pl.BlockSpec((tm, KV_HALF), tok),
        pl.BlockSpec((tm, POOL_WIDTH), tok),
        pl.BlockSpec((1, NSA_WIDTH, tm), featT),
        pl.BlockSpec((1, tm, 256), rows3),
        pl.BlockSpec((1, 1, N_KV_HEADS, V_ROWS, tm), chunk5),
        pl.BlockSpec((1, tm, KV_HALF), rows3),
        pl.BlockSpec((1, tm // WIN_CHUNK, N_KV_HEADS, V_ROWS, WIN_CHUNK), rows5),
        pl.BlockSpec((1, 32, tm), featT),
        pl.BlockSpec((blocks_step, KV_WIDTH), tok),
    )
    grid_spec = pltpu.PrefetchScalarGridSpec(
        num_scalar_prefetch=1,
        grid=(n_steps,),
        in_specs=[pl.BlockSpec((tm, D_MODEL), tok),
                  pl.BlockSpec((1, 1, D_MODEL), per_b),
                  pl.BlockSpec((1, 1, D_MODEL), per_b),
                  pl.BlockSpec((1, D_MODEL), const2),
                  pl.BlockSpec(wn.shape, const2),
                  pl.BlockSpec(wt.shape, const2)]
        + [pl.BlockSpec((1, KV_WIDTH, PAGE_SIZE), page_map(k)) for k in range(n_pages_step)]
        + [pl.BlockSpec(pos4.shape, const4), pl.BlockSpec(w4.shape, const4)],
        out_specs=out_specs,
        scratch_shapes=[pltpu.VMEM((blocks_step * CMP_PITCH, KV_HALF), F32),
                        pltpu.VMEM((blocks_step * CMP_PITCH, KV_HALF), F32)],
    )
    return pl.pallas_call(
        functools.partial(_proj_cmp_kernel, n_pages_step=n_pages_step, tm=tm, tpb=tpb),
        out_shape=out_shape,
        grid_spec=grid_spec,
        compiler_params=_cparams("arbitrary"),
        name="project_prompt_compress_cache",
    )(page_table_flat, x2, shift, scale, g, wn, wt, *([cache_t] * n_pages_step), pos4, w4)


def _compress_rows(src_refs, pos_ref, w_ref, o_ref, n_blocks, pitch):
    rows_per_slice = BLOCK // CMP_SPLIT
    for s in range(2):
        acc = jnp.zeros((n_blocks, KV_HALF), F32)
        for c in range(CMP_SPLIT):
            xc = jnp.concatenate(
                [src_refs[s][pl.ds(c * rows_per_slice + r, n_blocks, stride=pitch), :]
                 for r in range(rows_per_slice)], axis=1) + pos_ref[s, c]
            acc = acc + _dot(xc.astype(BF), w_ref[s, c])
        o_ref[:, s * KV_HALF:(s + 1) * KV_HALF] = acc


def _cmp_kernel(xa_ref, xb_ref, pos_ref, w_ref, o_ref):
    _compress_rows((xa_ref, xb_ref), pos_ref, w_ref, o_ref, CMP_GROUP, BLOCK)


def _compress(xa, xb, pos4, w4):
    m = xa.shape[0] // BLOCK
    rows = CMP_GROUP * BLOCK
    return pl.pallas_call(
        _cmp_kernel,
        out_shape=jax.ShapeDtypeStruct((m, KV_WIDTH), F32),
        grid=(m // CMP_GROUP,),
        in_specs=[pl.BlockSpec((rows, KV_HALF), lambda i: (i, 0)),
                  pl.BlockSpec((rows, KV_HALF), lambda i: (i, 0)),
                  pl.BlockSpec(pos4.shape, lambda i: (0, 0, 0, 0)),
                  pl.BlockSpec(w4.shape, lambda i: (0, 0, 0, 0))],
        out_specs=pl.BlockSpec((CMP_GROUP, KV_WIDTH), lambda i: (i, 0)),
        compiler_params=_cparams("arbitrary"),
        name="compress_blocks",
    )(xa, xb, pos4, w4)


def _attn_kernel(qT_ref, kaug_ref, vsT_ref, kw_ref, vwT_ref, kc_ref, vcT_ref, gT_ref, o_ref,
                 qa_ref, sa_ref, sb_ref, sc_ref, swb_ref, ma_ref, mb_ref, mc_ref, m_ref, acc_ref, outT_ref):
    i = pl.program_id(1)
    q0 = i * Q_TILE
    n_full = lax.div(q0, SEL_CHUNK)
    n_blk = kc_ref.shape[1]
    row = lax.broadcasted_iota(I32, (n_blk, Q_TILE), 0)
    tok = lax.broadcasted_iota(I32, (n_blk, Q_TILE), 1)
    qpos = q0 + tok
    cur = jnp.right_shift(qpos, 6)
    cmask = (row + 1) * BLOCK - 1 <= qpos
    valid = row <= cur
    forced = (row == 0) | (row == cur) | (row == cur - 1)
    key_c = lax.broadcasted_iota(I32, (SEL_CHUNK, Q_TILE), 0)
    qpos_c = q0 + lax.broadcasted_iota(I32, (SEL_CHUNK, Q_TILE), 1)
    hs = [slice(h * HEAD_DIM, (h + 1) * HEAD_DIM) for h in range(N_KV_HEADS)]

    def bias4(keep):
        b = jnp.where(keep, 0.0, -MASK_BIG)
        return jnp.concatenate([b] * GROUP, axis=1)

    def online_step(st, s, v_t, s_max=None):
        m_old = m_ref[st]
        m_new = jnp.maximum(m_old, jnp.max(s, axis=0, keepdims=True) if s_max is None else s_max)
        alpha = jnp.exp2(m_old - m_new)
        p = jnp.exp2(s - m_new)
        acc_ref[st] = alpha * acc_ref[st] + _dot(v_t, p.astype(BF))
        m_ref[st] = m_new

    m_ref[...] = jnp.full(m_ref.shape, NEG_INF, F32)
    acc_ref[...] = jnp.zeros(acc_ref.shape, F32)

    for h in range(N_KV_HEADS):
        qa_ref[h] = jnp.zeros(qa_ref.shape[1:], BF)
        for g in range(GROUP):
            r0 = h * GROUP * HEAD_DIM + g * HEAD_DIM
            qa_ref[h, hs[h], g * Q_TILE:(g + 1) * Q_TILE] = qT_ref[0, r0:r0 + HEAD_DIM, :]

    w0 = jnp.maximum(q0 - WINDOW, 0)
    kw_a = kw_ref[0, pl.ds(pl.multiple_of(w0, WIN_CHUNK), WINDOW), :]
    kw_b = kw_ref[0, pl.ds(pl.multiple_of(q0, Q_TILE), Q_TILE), :]
    for h in range(N_KV_HEADS):
        sc_ref[h] = _dot(kw_a, qa_ref[h, 0:KV_HALF, :])
        swb_ref[h] = _dot(kw_b, qa_ref[h, 0:KV_HALF, :])

    o_c = []
    scores = []
    for h in range(N_KV_HEADS):
        sc = _dot(kc_ref[0], qa_ref[h, 0:KV_HALF, :])
        imp = jnp.zeros((n_blk, Q_TILE), F32)
        p_parts = []
        for g in range(GROUP):
            s = jnp.where(cmask, sc[:, g * Q_TILE:(g + 1) * Q_TILE], NEG_INF)
            mx = jnp.max(s, axis=0, keepdims=True)
            mx = jnp.where(mx > NEG_INF, mx, 0.0)
            e = jnp.where(cmask, jnp.exp2(s - mx), 0.0)
            p = e / jnp.maximum(jnp.sum(e, axis=0, keepdims=True), 1e-30)
            imp = imp + p
            p_parts.append(p)
        o_c.append(_dot(vcT_ref[0, hs[h], :], jnp.concatenate(p_parts, axis=1).astype(BF)))
        scores.append(jnp.where(valid, jnp.where(forced, -2.0, imp), -1.0))

    blk_f = row.astype(F32)

    def pick(work):
        best = jnp.max(work, axis=0, keepdims=True)
        first = jnp.min(jnp.where(work == best, blk_f, float(n_blk)), axis=0, keepdims=True)
        return jnp.where((blk_f == first) & (best >= 0.0), -2.0, work)

    for _ in range(N_SELECT - N_FORCED):
        scores = [pick(w) for w in scores]
    for h in range(N_KV_HEADS):
        selm1 = jnp.where(scores[h] == -2.0, 0.0, -1.0).astype(BF)
        for g in range(GROUP):
            qa_ref[h, KV_HALF:KV_HALF + n_blk, g * Q_TILE:(g + 1) * Q_TILE] = selm1

    last_chunk = kaug_ref.shape[1] // SEL_CHUNK - 1

    def sel_scores(j, buf):
        dst_ref, max_ref = buf
        j = jnp.minimum(j, last_chunk)
        kt = kaug_ref[0, pl.ds(pl.multiple_of(j * SEL_CHUNK, SEL_CHUNK), SEL_CHUNK), :]
        for h in range(N_KV_HEADS):
            s = _dot(kt, qa_ref[h])
            dst_ref[h] = s
            max_ref[h] = jnp.max(s, axis=0, keepdims=True)

    def sel_process(buf, j, causal):
        src_ref, max_ref = buf
        if causal:
            cbias = bias4(j * SEL_CHUNK + key_c <= qpos_c)
        for h in range(N_KV_HEADS):
            if causal:
                online_step(h, src_ref[h] + cbias, vsT_ref[0, j, h])
            else:
                online_step(h, src_ref[h], vsT_ref[0, j, h], max_ref[h])

    buf_a, buf_b, buf_c = (sa_ref, ma_ref), (sb_ref, mb_ref), (sc_ref, mc_ref)
    sel_scores(0, buf_a)
    sel_scores(1, buf_b)

    j0 = lax.div(w0, WIN_CHUNK)
    jq = lax.div(q0, WIN_CHUNK)
    delta = qpos_c - (w0 + key_c)
    wbias = bias4((delta >= 0) & (delta < WINDOW))
    key_d = lax.broadcasted_iota(I32, (Q_TILE, Q_TILE), 0)
    tok_d = lax.broadcasted_iota(I32, (Q_TILE, Q_TILE), 1)
    dbias = bias4((key_d <= tok_d) & (q0 >= WINDOW))
    for h in range(N_KV_HEADS):
        v_a = jnp.concatenate([vwT_ref[0, j0 + c, h] for c in range(WINDOW // WIN_CHUNK)], axis=1)
        v_b = jnp.concatenate([vwT_ref[0, jq + c, h] for c in range(Q_TILE // WIN_CHUNK)], axis=1)
        online_step(2 + h, sc_ref[h] + wbias, v_a)
        online_step(2 + h, swb_ref[h] + dbias, v_b)

    def trio_body(t, carry):
        j = 3 * t
        sel_scores(j + 2, buf_c)
        sel_process(buf_a, j, False)
        sel_scores(j + 3, buf_a)
        sel_process(buf_b, j + 1, False)
        sel_scores(j + 4, buf_b)
        sel_process(buf_c, j + 2, False)
        return carry

    n_trios = lax.div(n_full, 3)
    lax.fori_loop(0, n_trios, trio_body, 0)
    j_last = 3 * n_trios
    n_left = n_full - j_last

    @pl.when(n_left == 0)
    def _():
        sel_process(buf_a, j_last, True)

    @pl.when(n_left == 1)
    def _():
        sel_process(buf_a, j_last, False)
        sel_process(buf_b, j_last + 1, True)

    @pl.when(n_left == 2)
    def _():
        sel_scores(j_last + 2, buf_c)
        sel_process(buf_a, j_last, False)
        sel_process(buf_b, j_last + 1, False)
        sel_process(buf_c, j_last + 2, True)

    for h in range(N_KV_HEADS):
        o_s = acc_ref[h, 0:HEAD_DIM, :] / acc_ref[h, HEAD_DIM:HEAD_DIM + 1, :]
        o_w = acc_ref[2 + h, 0:HEAD_DIM, :] / acc_ref[2 + h, HEAD_DIM:HEAD_DIM + 1, :]
        for g in range(GROUP):
            gs = slice(g * Q_TILE, (g + 1) * Q_TILE)
            gr = h * GROUP * 3 + g * 3
            og = (gT_ref[0, gr:gr + 1, :] * o_c[h][:, gs] + gT_ref[0, gr + 1:gr + 2, :] * o_s[:, gs]
                  + gT_ref[0, gr + 2:gr + 3, :] * o_w[:, gs])
            r0 = h * GROUP * HEAD_DIM + g * HEAD_DIM
            outT_ref[r0:r0 + HEAD_DIM, :] = og

    o_ref[0] = outT_ref[...].T.astype(BF)


def _attention_prompt(qT, kaug, vsT, kw, vwT, kc, vcT, gT, n_batch, seq):
    per_b3 = lambda n, i: (n, 0, 0)
    per_b5 = lambda n, i: (n, 0, 0, 0, 0)
    rows = GROUP * Q_TILE
    return pl.pallas_call(
        _attn_kernel,
        out_shape=jax.ShapeDtypeStruct((n_batch, seq, NSA_WIDTH), BF),
        grid=(n_batch, seq // Q_TILE),
        in_specs=[pl.BlockSpec((1, NSA_WIDTH, Q_TILE), lambda n, i: (n, 0, i)),
                  pl.BlockSpec((1, seq, 256), per_b3),
                  pl.BlockSpec((1, seq // SEL_CHUNK, N_KV_HEADS, V_ROWS, SEL_CHUNK), per_b5),
                  pl.BlockSpec((1, seq, KV_HALF), per_b3),
                  pl.BlockSpec((1, seq // WIN_CHUNK, N_KV_HEADS, V_ROWS, WIN_CHUNK), per_b5),
                  pl.BlockSpec((1, seq // BLOCK, KV_HALF), per_b3),
                  pl.BlockSpec((1, KV_HALF, seq // BLOCK), per_b3),
                  pl.BlockSpec((1, 32, Q_TILE), lambda n, i: (n, 0, i))],
        out_specs=pl.BlockSpec((1, Q_TILE, NSA_WIDTH), lambda n, i: (n, i, 0)),
        scratch_shapes=[pltpu.VMEM((N_KV_HEADS, 256, rows), BF),
                        pltpu.VMEM((N_KV_HEADS, SEL_CHUNK, rows), F32),
                        pltpu.VMEM((N_KV_HEADS, SEL_CHUNK, rows), F32),
                        pltpu.VMEM((N_KV_HEADS, SEL_CHUNK, rows), F32),
                        pltpu.VMEM((N_KV_HEADS, Q_TILE, rows), F32),
                        pltpu.VMEM((N_KV_HEADS, 1, rows), F32),
                        pltpu.VMEM((N_KV_HEADS, 1, rows), F32),
                        pltpu.VMEM((N_KV_HEADS, 1, rows), F32),
                        pltpu.VMEM((2 * N_KV_HEADS, 1, rows), F32),
                        pltpu.VMEM((2 * N_KV_HEADS, V_ROWS, rows), F32),
                        pltpu.VMEM((NSA_WIDTH, Q_TILE), F32)],
        compiler_params=_cparams("arbitrary", "arbitrary"),
        name="attention_prompt",
    )(qT, kaug, vsT, kw, vwT, kc, vcT, gT)


def _pool_kernel(u_ref, halo_ref, wp_ref, ps_ref, o_ref, ext_ref, lvl_ref, *, tm, tpb):
    t = pl.program_id(0) % tpb
    ext_ref[0:16, :] = jnp.where(t == 0, 0.0, halo_ref[...])
    u = u_ref[...]
    ext_ref[16:16 + tm, :] = u
    pos = t * tm + lax.broadcasted_iota(I32, (tm, 1), 0)
    end = 16 + tm
    outs = []
    for gi, w in enumerate(POOL_WINDOWS):
        cs = slice(gi * POOL_GROUP_DIM, (gi + 1) * POOL_GROUP_DIM)
        lo = 16 - (w - 2)
        acc = ext_ref[lo:end, cs] + ext_ref[lo - 1:end - 1, cs]
        d = 2
        while d < w:
            lvl_ref[lo:end, :] = acc
            lo += d
            acc = lvl_ref[lo:end, :] + lvl_ref[lo - d:end - d, :]
            d *= 2
        cnt = jnp.minimum(pos + 1, w).astype(F32)
        pooled = acc / cnt - u[:, cs]
        outs.append(_dot(pooled.astype(BF), wp_ref[gi]))
    o_ref[...] = (jnp.concatenate(outs, axis=1) * ps_ref[...]).astype(BF)


def _pool_prompt(u, w_pool, pool_scale, n_batch, seq):
    tm = TOKEN_TILE
    tpb = seq // tm
    nt = n_batch * seq
    return pl.pallas_call(
        functools.partial(_pool_kernel, tm=tm, tpb=tpb),
        out_shape=jax.ShapeDtypeStruct((nt, POOL_WIDTH), BF),
        grid=(nt // tm,),
        in_specs=[pl.BlockSpec((tm, POOL_WIDTH), lambda t: (t, 0)),
                  pl.BlockSpec((16, POOL_WIDTH), lambda t: (jnp.maximum(t * (tm // 16) - 1, 0), 0)),
                  pl.BlockSpec(w_pool.shape, lambda t: (0, 0, 0)),
                  pl.BlockSpec((1, POOL_WIDTH), lambda t: (0, 0))],
        out_specs=pl.BlockSpec((tm, POOL_WIDTH), lambda t: (t, 0)),
        scratch_shapes=[pltpu.VMEM((tm + 16, POOL_WIDTH), F32), pltpu.VMEM((tm + 16, POOL_GROUP_DIM), F32)],
        compiler_params=_cparams("arbitrary"),
        name="pool_prompt",
    )(u, u, w_pool, pool_scale)


def _spool_kernel(u_ref, hist_ref, wp_ref, ps_ref, o_ref):
    u = u_ref[...]
    outs = []
    for gi, w in enumerate(POOL_WINDOWS):
        cs = slice(gi * POOL_GROUP_DIM, (gi + 1) * POOL_GROUP_DIM)
        acc = u[:, cs]
        for k in range(1, w):
            acc = acc + hist_ref[POOL_HIST - k, :, cs]
        pooled = acc / float(w) - u[:, cs]
        outs.append(_dot(pooled.astype(BF), wp_ref[gi]))
    o_ref[...] = (jnp.concatenate(outs, axis=1) * ps_ref[...]).astype(BF)


def _pool_sample(u, hist_t, w_pool, pool_scale):
    n = u.shape[0]
    return pl.pallas_call(
        _spool_kernel,
        out_shape=jax.ShapeDtypeStruct((n, POOL_WIDTH), BF),
        name="pool_sample",
    )(u, hist_t, w_pool, pool_scale)


def _merge_kernel(x_ref, sh_ref, sc_ref, gate_ref, gpre_ref, gpost_ref, onsa_ref, opool_ref,
                  wgm_ref, wun_ref, wup_ref, wo_ref, o_ref):
    x = x_ref[...]
    h = _rms(x, gpre_ref[...]) * (1.0 + sc_ref[0]) + sh_ref[0]
    gm = jax.nn.sigmoid(_dot(h.astype(BF), wgm_ref[...]))
    m = (gm[:, :D_MODEL] * _dot(onsa_ref[...], wun_ref[...])
         + gm[:, D_MODEL:] * _dot(opool_ref[...], wup_ref[...]))
    m = _dot(m.astype(BF), wo_ref[...])
    o_ref[...] = x + gate_ref[0] * _rms(m, gpost_ref[...])


def _mlp_kernel(x_ref, sh_ref, sc_ref, gate_ref, gpre_ref, gpost_ref, w1_ref, w2_ref, o_ref):
    x = x_ref[...]
    h = _rms(x, gpre_ref[...]) * (1.0 + sc_ref[0]) + sh_ref[0]
    hb = h.astype(BF)
    f = jnp.zeros(x.shape, F32)
    fc = 1024
    for c in range(D_FF // fc):
        a = jnp.maximum(_dot(hb, w1_ref[:, c * fc:(c + 1) * fc]), 0.0)
        f = f + _dot((a * a).astype(BF), w2_ref[c * fc:(c + 1) * fc, :])
    o_ref[...] = x + gate_ref[0] * _rms(f, gpost_ref[...])


def _token_call(kernel, name, x2, mods, tm, rows_per_mod, extra_tok, consts):
    nt = x2.shape[0]
    r = mods[0].shape[1]
    mod_spec = pl.BlockSpec((1, r, D_MODEL), lambda t: ((t * tm) // rows_per_mod, 0, 0))
    in_specs = [pl.BlockSpec((tm, D_MODEL), lambda t: (t, 0))] + [mod_spec] * len(mods)
    in_specs += [pl.BlockSpec((1, D_MODEL), lambda t: (0, 0))] * 2
    in_specs += [pl.BlockSpec((tm, a.shape[1]), lambda t: (t, 0)) for a in extra_tok]
    in_specs += [pl.BlockSpec(w.shape, lambda t: (0, 0), pipeline_mode=pl.Buffered(1)) for w in consts[2:]]
    return pl.pallas_call(
        kernel,
        out_shape=jax.ShapeDtypeStruct((nt, D_MODEL), F32),
        grid=(nt // tm,),
        in_specs=in_specs,
        out_specs=pl.BlockSpec((tm, D_MODEL), lambda t: (t, 0)),
        compiler_params=_cparams("arbitrary"),
        name=name,
    )(x2, *mods, consts[0], consts[1], *extra_tok, *consts[2:])


def _sproj_kernel(x_ref, sh_ref, sc_ref, g_ref, wn_ref, wt_ref, z_ref, zs_ref, zT_ref):
    x = x_ref[...]
    h = _rms(x, g_ref[...]) * (1.0 + sc_ref[...]) + sh_ref[...]
    hb = h.astype(BF)
    z = _dot(hb, wn_ref[...])
    z_ref[...] = z
    zs_ref[...] = jax.nn.sigmoid(z)
    zT_ref[...] = _dot_nt(wt_ref[...], hb)


def _project_sample(x2, shift, scale, g, wn, wt):
    n = x2.shape[0]
    shp = jax.ShapeDtypeStruct((n, wn.shape[1]), F32)
    return pl.pallas_call(
        _sproj_kernel,
        out_shape=(shp, shp, jax.ShapeDtypeStruct((wt.shape[0], n), F32)),
        name="project_sample",
    )(x2, shift, scale, g, wn, wt)


def _sattn_init(kaug_ref, past_len, seq_step):
    @pl.when(pl.program_id(0) == 0)
    def _():
        blk = lax.broadcasted_iota(I32, (128, past_len), 0)
        key_blk = jnp.right_shift(lax.broadcasted_iota(I32, (128, past_len), 1), 6)
        onehot = jnp.where(blk == key_blk, MASK_BIG, 0.0).astype(BF)
        for q in range(seq_step):
            kaug_ref[q, 128:256, :] = onehot


def _sattn_main(*refs, n_pages, past_len, seq_step):
    pages = refs[:seq_step * n_pages]
    (qb_ref, kvc_ref, win_ref, ksn_ref, kwn_ref, kwnT_ref, g_ref,
     o_ref, nwin_ref, kaug_ref, vall_ref) = refs[seq_step * n_pages:]

    for q in range(seq_step):
        for p in range(n_pages):
            pg = pages[q * n_pages + p][0]
            kaug_ref[q, 0:128, p * PAGE_SIZE:(p + 1) * PAGE_SIZE] = pg[0:KV_HALF, :].astype(BF)
            vall_ref[q, :, p * PAGE_SIZE:(p + 1) * PAGE_SIZE] = pg[KV_HALF:KV_WIDTH, :].astype(BF)

    nb_past = past_len // BLOCK
    win_buf = win_ref.shape[2]
    n_rows = 8 * seq_step
    per_seq = lambda f: jnp.concatenate([f(q) for q in range(seq_step)], axis=0)
    rows_of = lambda x, q: x[8 * q:8 * (q + 1)]
    qb = qb_ref[...].reshape(n_rows, KV_HALF).astype(BF)
    qf = qb.astype(F32)
    row = lax.broadcasted_iota(I32, (n_rows, 128), 0)
    lane = lax.broadcasted_iota(I32, (n_rows, 128), 1)

    def new_key(ref):
        return per_seq(lambda q: jnp.broadcast_to(ref[q], (8, KV_WIDTH))).astype(BF).astype(F32)

    s_c = per_seq(lambda q: _dot_nt(rows_of(qb, q), kvc_ref[q, :, 0:128].astype(BF)))
    cm = lane < nb_past
    s_c = jnp.where(cm, s_c, NEG_INF)
    mx = jnp.max(s_c, axis=1, keepdims=True)
    mx = jnp.where(mx > NEG_INF, mx, 0.0)
    e = jnp.where(cm, jnp.exp(s_c - mx), 0.0)
    p_c = e / jnp.maximum(jnp.sum(e, axis=1, keepdims=True), 1e-30)
    o_c = per_seq(lambda q: _dot(rows_of(p_c, q).astype(BF), kvc_ref[q, :, 128:256].astype(BF)))

    imp = jnp.zeros((n_rows, 128), F32)
    for grp in range(n_rows // GROUP):
        in_grp = jnp.right_shift(row, 2) == grp
        imp = jnp.where(in_grp, jnp.sum(jnp.where(in_grp, p_c, 0.0), axis=0, keepdims=True), imp)
    cur = nb_past
    forced = (lane == 0) | (lane == cur) | (lane == cur - 1)
    score = jnp.where(lane <= cur, imp + FORCE_SCORE * forced.astype(F32), -1.0)
    cnt = jnp.zeros((n_rows, 128), F32)
    for bp in range(nb_past + 1):
        other = score[:, bp:bp + 1]
        ahead = (other > score) | ((other == score) & (bp < lane))
        cnt = cnt + jnp.where(ahead, 1.0, 0.0)
    sel = (cnt < float(N_SELECT)) & (score >= 0.0)
    selm1 = jnp.where(sel, 0.0, -1.0).astype(BF)

    qaug = jnp.concatenate([qb, selm1], axis=1)
    s_s = per_seq(lambda q: _dot(rows_of(qaug, q), kaug_ref[q]))
    kv_n = new_key(ksn_ref)
    s_n = jnp.sum(qf * kv_n[:, 0:128], axis=1, keepdims=True)
    m_s = jnp.maximum(jnp.max(s_s, axis=1, keepdims=True), s_n)
    e_s = jnp.exp(s_s - m_s)
    e_n = jnp.exp(s_n - m_s)
    l_s = jnp.sum(e_s, axis=1, keepdims=True) + e_n
    pv = per_seq(lambda q: _dot_nt(rows_of(e_s, q).astype(BF), vall_ref[q]))
    o_s = (pv + e_n.astype(BF).astype(F32) * kv_n[:, 128:256]) / l_s

    s_w = per_seq(lambda q: _dot(rows_of(qb, q), win_ref[q, 0:KV_HALF, :].astype(BF)))
    lane_w = lax.broadcasted_iota(I32, (n_rows, win_buf), 1)
    s_w = jnp.where(lane_w >= win_buf + 1 - WINDOW, s_w, NEG_INF)
    kv_n = new_key(kwn_ref)
    s_n = jnp.sum(qf * kv_n[:, 0:128], axis=1, keepdims=True)
    m_w = jnp.maximum(jnp.max(s_w, axis=1, keepdims=True), s_n)
    e_w = jnp.exp(s_w - m_w)
    e_n = jnp.exp(s_n - m_w)
    l_w = jnp.sum(e_w, axis=1, keepdims=True) + e_n
    pv = per_seq(lambda q: _dot_nt(rows_of(e_w, q).astype(BF), win_ref[q, KV_HALF:KV_WIDTH, :].astype(BF)))
    o_w = (pv + e_n.astype(BF).astype(F32) * kv_n[:, 128:256]) / l_w

    g = g_ref[...].reshape(n_rows, 3)
    o_ref[...] = (g[:, 0:1] * o_c + g[:, 1:2] * o_s + g[:, 2:3] * o_w).reshape(seq_step, 8, KV_HALF)

    seq_lane = lax.broadcasted_iota(I32, kwnT_ref.shape, 1)
    row_lane = lax.broadcasted_iota(I32, (KV_WIDTH, win_buf), 1)
    for q in range(seq_step):
        n = pl.program_id(0) * seq_step + q
        new_col = jnp.sum(jnp.where(seq_lane == n, kwnT_ref[...], 0.0), axis=1, keepdims=True)
        shifted = pltpu.roll(win_ref[q], win_buf - 1, axis=1)
        nwin_ref[q] = jnp.where(row_lane == win_buf - 1, new_col, shifted)


def _sattn_kernel(pt_ref, *refs, n_pages, past_len, seq_step):
    _sattn_init(refs[-2], past_len, seq_step)
    _sattn_main(*refs, n_pages=n_pages, past_len=past_len, seq_step=seq_step)


def _attention_sample(page_table_flat, cache_t, qblk, kvc_pad, win_t, kvs_new, kvw_new, kvw_new_t, gates8,
                      n_pages, past_len):
    n_seq = qblk.shape[0]
    win_buf = win_t.shape[2]
    g = SAMPLE_SEQ_STEP
    const2 = lambda t, pt: (0, 0)
    per_step = lambda t, pt: (t, 0, 0)

    def page_map(k):
        return lambda t, pt: (pt[t * g * n_pages + k], 0, 0)

    grid_spec = pltpu.PrefetchScalarGridSpec(
        num_scalar_prefetch=1,
        grid=(n_seq // g,),
        in_specs=[pl.BlockSpec((1, KV_WIDTH, PAGE_SIZE), page_map(k)) for k in range(g * n_pages)]
        + [pl.BlockSpec((g, 8, 128), per_step),
           pl.BlockSpec((g, 128, KV_WIDTH), per_step),
           pl.BlockSpec((g, KV_WIDTH, win_buf), per_step),
           pl.BlockSpec((g, 1, KV_WIDTH), per_step),
           pl.BlockSpec((g, 1, KV_WIDTH), per_step),
           pl.BlockSpec(kvw_new_t.shape, const2),
           pl.BlockSpec((g, 8, 3), per_step)],
        out_specs=(pl.BlockSpec((g, 8, 128), per_step),
                   pl.BlockSpec((g, KV_WIDTH, win_buf), per_step)),
        scratch_shapes=[pltpu.VMEM((g, 256, past_len), BF), pltpu.VMEM((g, KV_HALF, past_len), BF)],
    )
    return pl.pallas_call(
        functools.partial(_sattn_kernel, n_pages=n_pages, past_len=past_len, seq_step=g),
        out_shape=(jax.ShapeDtypeStruct((n_seq, 8, 128), F32),
                   jax.ShapeDtypeStruct((n_seq, KV_WIDTH, win_buf), F32)),
        grid_spec=grid_spec,
        compiler_params=_cparams("arbitrary"),
        name="attention_sample",
    )(page_table_flat, *([cache_t] * (g * n_pages)), qblk, kvc_pad, win_t, kvs_new, kvw_new, kvw_new_t, gates8)


def _kv_rows_view(kv_t):
    n, _, t = kv_t.shape
    return jnp.transpose(kv_t.reshape(n, 2, N_KV_HEADS, HEAD_DIM, t), (0, 4, 1, 2, 3))


def _kv_feat_view(kv):
    n, t = kv.shape[:2]
    return jnp.transpose(kv, (0, 2, 3, 4, 1)).reshape(n, KV_WIDTH, t)


def kernel(x_prompt, x_sample, cache_cmp_kv, cache_sel_kv, state_win_kv, state_pool, page_table, c_prompt, c_sample, w_ada, b_ada, g_pre_mix, g_post_mix, g_pre_mlp, g_post_mlp, w_in, w_cmp, pos_cmp, w_pool, pool_scale, w_up_nsa, w_up_pool, w_o, w_ff1, w_ff2):
    n_batch, seq, _ = x_prompt.shape
    n_seq = x_sample.shape[0]
    n_pages = page_table.shape[1]
    past_len = n_pages * PAGE_SIZE
    nb_past = past_len // BLOCK
    assert x_sample.shape[1] == 1 and w_ada.shape[0] == 1 and past_len % BLOCK == 0
    assert seq % TOKEN_TILE == 0 and seq // BLOCK == 128 and state_win_kv.shape[2] == WINDOW
    assert (n_seq * nb_past) % CMP_GROUP == 0 and n_seq == CMP_GROUP
    assert FORCE_SCORE > GROUP

    w_t = w_in[0].T
    wt = jnp.pad(w_t[0:1304], ((0, 8), (0, 0))).astype(BF)
    wn = jnp.concatenate([w_t[768:896], w_t[1024:1152], w_t[1304:1816], w_t[512:768]], axis=0).T.astype(BF)
    ws = jnp.pad(w_t[0:1816], ((0, 104), (0, 0))).T.astype(BF)
    wgm = w_t[1816:3864].T.astype(BF)
    eye = jnp.eye(N_KV_HEADS, dtype=F32)
    wc = w_cmp[0].astype(BF)
    wz = jnp.zeros_like(wc)
    w4 = jnp.concatenate([jnp.concatenate([wc, wz], axis=3), jnp.concatenate([wz, wc], axis=3)], axis=2)
    w4 = w4.reshape(2, CMP_SPLIT, CMP_K // CMP_SPLIT, KV_HALF)
    pos4 = jnp.broadcast_to(jnp.transpose(pos_cmp[0], (1, 0, 2))[:, :, None, :],
                            (2, BLOCK, N_KV_HEADS, HEAD_DIM)).reshape(2, CMP_SPLIT, 1, CMP_K // CMP_SPLIT)
    wp = w_pool[0].astype(BF)
    ps = pool_scale[0].reshape(1, POOL_WIDTH)
    wun, wup, wo = w_up_nsa[0].astype(BF), w_up_pool[0].astype(BF), w_o[0].astype(BF)
    w1, w2 = w_ff1[0].astype(BF), w_ff2[0].astype(BF)
    gpm, gqm = g_pre_mix[0].reshape(1, D_MODEL), g_post_mix[0].reshape(1, D_MODEL)
    gpf, gqf = g_pre_mlp[0].reshape(1, D_MODEL), g_post_mlp[0].reshape(1, D_MODEL)

    n_c = n_batch + n_seq
    c_all = jnp.pad(jnp.concatenate([c_prompt, c_sample], axis=0), ((0, (-n_c) % 8), (0, 0)))
    ada = _adaln(c_all, w_ada[0], b_ada[0].reshape(1, -1))
    ada_p = ada[:n_batch].reshape(n_batch, 6, 1, D_MODEL)
    ada_s = ada[n_batch:n_c].reshape(n_seq, 6, D_MODEL)
    mods_p = [ada_p[:, k] for k in range(6)]
    mods_s = [ada_s[:, k][None] for k in range(6)]

    xp = x_prompt.reshape(n_batch * seq, D_MODEL)
    pt_flat = page_table.reshape(-1)
    (kvcT, kvsT, kvwT, kvc_a, kvc_b, u_p, qT, kaug, vsT, kw, vwT, gT,
     kvc_past) = _project_prompt_compress_cache(xp, mods_p[0], mods_p[1], gpm, wn, wt, n_batch, seq,
                                                _kv_feat_view(cache_cmp_kv[0]), pt_flat, pos4, w4)
    kvc_blk = _compress(kvc_a, kvc_b, pos4, w4).reshape(n_batch, seq // BLOCK, KV_WIDTH)
    kc = kvc_blk[:, :, 0:128].astype(BF)
    vcT = jnp.swapaxes(kvc_blk[:, :, 128:256], 1, 2).astype(BF)
    onsa_p = _attention_prompt(qT, kaug, vsT, kw, vwT, kc, vcT, gT, n_batch, seq)
    opool_p = _pool_prompt(u_p, wp, ps, n_batch, seq)
    x1_p = _token_call(_merge_kernel, "merge_prompt", xp, [mods_p[0], mods_p[1], mods_p[2]], TOKEN_TILE, seq,
                       [onsa_p.reshape(n_batch * seq, NSA_WIDTH), opool_p], [gpm, gqm, wgm, wun, wup, wo])
    y_p = _token_call(_mlp_kernel, "mlp_prompt", x1_p, [mods_p[3], mods_p[4], mods_p[5]], TOKEN_TILE, seq,
                      [], [gpf, gqf, w1, w2])

    xs = x_sample.reshape(n_seq, D_MODEL)
    z, zsig, zT = _project_sample(xs, mods_s[0][0], mods_s[1][0], gpm, ws, wt)
    q_s = z[:, 0:512] * (HEAD_DIM ** -0.5)
    kvc_n, kvs_n, kvw_n = z[:, 512:768], z[:, 768:1024], z[:, 1024:1280]
    gates_s = zsig[:, 1280:1304].reshape(n_seq, N_HEADS, 3)
    u_s = z[:, 1304:1816]
    q5 = q_s.reshape(n_seq, N_KV_HEADS, GROUP, 1, HEAD_DIM) * eye[None, :, None, :, None]
    qblk = q5.reshape(n_seq, N_HEADS, KV_HALF)

    last_a = jnp.pad(kvc_n[:, None, 0:128], ((0, 0), (0, BLOCK - 1), (0, 0))).reshape(n_seq * BLOCK, KV_HALF)
    last_b = jnp.pad(kvc_n[:, None, 128:256], ((0, 0), (0, BLOCK - 1), (0, 0))).reshape(n_seq * BLOCK, KV_HALF)
    kvc_last = _compress(last_a, last_b, pos4, w4)
    kvc_s = jnp.concatenate([kvc_past.reshape(n_seq, nb_past, KV_WIDTH), kvc_last[:, None, :]], axis=1)
    kvc_pad = jnp.pad(kvc_s, ((0, 0), (0, 128 - nb_past - 1), (0, 0)))
    o8, new_win_t = _attention_sample(
        pt_flat, _kv_feat_view(cache_sel_kv[0]), qblk, kvc_pad, _kv_feat_view(state_win_kv[0]),
        kvs_n[:, None, :], kvw_n[:, None, :], zT[1024:1280], gates_s, n_pages, past_len)
    o5 = o8.reshape(n_seq, N_KV_HEADS, GROUP, N_KV_HEADS, HEAD_DIM)
    onsa_s = jnp.concatenate([o5[:, 0, :, 0, :], o5[:, 1, :, 1, :]], axis=1).reshape(n_seq, NSA_WIDTH).astype(BF)
    opool_s = _pool_sample(u_s, jnp.swapaxes(state_pool[0], 0, 1), wp, ps)
    x1_s = _token_call(_merge_kernel, "merge_sample", xs, [mods_s[0], mods_s[1], mods_s[2]], n_seq, n_seq,
                       [onsa_s, opool_s], [gpm, gqm, wgm, wun, wup, wo])
    y_s = _token_call(_mlp_kernel, "mlp_sample", x1_s, [mods_s[3], mods_s[4], mods_s[5]], n_seq, n_seq,
                      [], [gpf, gqf, w1, w2])

    win_p = min(WINDOW, seq)
    new_kv_s = lambda rows: _kv_rows_view(rows.reshape(1, KV_WIDTH, n_seq))[0][None, :, None]
    return (
        y_p.reshape(n_batch, seq, D_MODEL),
        y_s.reshape(n_seq, 1, D_MODEL),
        _kv_rows_view(kvcT)[None],
        _kv_rows_view(kvsT)[None],
        _kv_rows_view(kvwT[:, :, seq - win_p:])[None],
        u_p.reshape(n_batch, seq, POOL_WIDTH)[None, :, seq - POOL_HIST:],
        new_kv_s(zT[512:768]),
        new_kv_s(zT[768:1024]),
        _kv_rows_view(new_win_t)[None],
        jnp.concatenate([state_pool[0][:, 1:], u_s[:, None, :]], axis=1)[None],
    )
```

```python
import functools
import math

import jax
import jax.numpy as jnp
from jax import lax
from jax.experimental import pallas as pl
from jax.experimental.pallas import tpu as pltpu

D_MODEL = 1024
N_HEADS = 8
HEAD_DIM = 64
N_KV_HEADS = 2
GROUP = N_HEADS // N_KV_HEADS
BLOCK = 64
N_SELECT = 16
WINDOW = 512
Q_TILE = 256
WIN_CHUNK = 128
NSA_WIDTH = N_HEADS * HEAD_DIM
KV_WIDTH = 2 * N_KV_HEADS * HEAD_DIM
KV_HALF = N_KV_HEADS * HEAD_DIM
FORCE_SCORE = 16.0
N_FORCED = 3
POOL_WINDOWS = (2, 4, 8, 16)
POOL_WIDTH = 512
POOL_GROUP_DIM = 128
POOL_HIST = 15
D_FF = 4 * D_MODEL
EPS = 1e-6
PAGE_SIZE = 128
V_ROWS = HEAD_DIM + 16
CMP_K = BLOCK * KV_HALF
CMP_SPLIT = 4
CMP_GROUP = 128
CMP_PITCH = BLOCK + 4

BF = jnp.bfloat16
F32 = jnp.float32
I32 = jnp.int32
MASK_BIG = 2.0 ** 100
NEG_INF = float("-inf")
LOG2E = math.log2(math.e)

TOKEN_TILE = 512
PROJ_TILE = 256
SEL_CHUNK = 512
SAMPLE_SEQ_STEP = 4
VMEM_LIMIT = 56 * 1024 * 1024


def _cparams(*sem):
    return pltpu.CompilerParams(dimension_semantics=sem, vmem_limit_bytes=VMEM_LIMIT)


def _rms(x, g):
    return x * lax.rsqrt(jnp.mean(x * x, axis=-1, keepdims=True) + EPS) * g


def _dot(a, b):
    return jnp.dot(a, b, preferred_element_type=F32)


def _dot_nt(a, b):
    return lax.dot_general(a, b, (((1,), (1,)), ((), ())), preferred_element_type=F32)


def _ada_kernel(c_ref, w_ref, b_ref, o_ref):
    c = c_ref[...]
    a = (c * jax.nn.sigmoid(c)).astype(BF)
    o_ref[...] = _dot(a, w_ref[...].astype(BF)) + b_ref[...]


def _adaln(c_all, w_ada, b_ada):
    rows = c_all.shape[0]
    n_out = w_ada.shape[1]
    tn = 512
    return pl.pallas_call(
        _ada_kernel,
        out_shape=jax.ShapeDtypeStruct((rows, n_out), F32),
        grid=(n_out // tn,),
        in_specs=[pl.BlockSpec((rows, D_MODEL), lambda j: (0, 0)),
                  pl.BlockSpec((D_MODEL, tn), lambda j: (0, j)),
                  pl.BlockSpec((1, tn), lambda j: (0, j))],
        out_specs=pl.BlockSpec((rows, tn), lambda j: (0, j)),
        compiler_params=_cparams("arbitrary"),
        name="adaln",
    )(c_all, w_ada, b_ada)


def _proj_kernel(x_ref, sh_ref, sc_ref, g_ref, wn_ref, wt_ref,
                 kvcT_ref, kvsT_ref, kvwT_ref, ca_ref, cb_ref, u_ref,
                 qT_ref, kaug_ref, vsT_ref, kw_ref, vwT_ref, gT_ref, *, tm, tpb):
    x = x_ref[...]
    h = _rms(x, g_ref[...]) * (1.0 + sc_ref[0]) + sh_ref[0]
    hb = h.astype(BF)
    zn = _dot(hb, wn_ref[...])
    zt = _dot_nt(wt_ref[...], hb)
    kvcT_ref[0] = zt[512:768]
    kvsT_ref[0] = zt[768:1024]
    kvwT_ref[0] = zt[1024:1280]
    ca_ref[...] = zn[:, 768:896]
    cb_ref[...] = zn[:, 896:1024]
    u_ref[...] = zn[:, 256:768]
    qT_ref[0] = (zt[0:512] * (HEAD_DIM ** -0.5 * LOG2E)).astype(BF)
    ones = jnp.ones((V_ROWS - HEAD_DIM, tm), BF)
    for h in range(N_KV_HEADS):
        vs_h = zt[896 + h * HEAD_DIM:896 + (h + 1) * HEAD_DIM].astype(BF)
        vw_h = zt[1152 + h * HEAD_DIM:1152 + (h + 1) * HEAD_DIM].astype(BF)
        vsT_ref[0, 0, h, 0:HEAD_DIM, :] = vs_h
        vsT_ref[0, 0, h, HEAD_DIM:V_ROWS, :] = ones
        for c in range(tm // WIN_CHUNK):
            cs = slice(c * WIN_CHUNK, (c + 1) * WIN_CHUNK)
            vwT_ref[0, c, h, 0:HEAD_DIM, :] = vw_h[:, cs]
            vwT_ref[0, c, h, HEAD_DIM:V_ROWS, :] = ones[:, cs]
    gT_ref[0] = jax.nn.sigmoid(zt[1280:1312])
    t0 = (pl.program_id(0) % tpb) * tm
    blk = jnp.right_shift(t0 + lax.broadcasted_iota(I32, (tm, 128), 0), 6)
    lane = lax.broadcasted_iota(I32, (tm, 128), 1)
    kaug_ref[0, :, 0:128] = zn[:, 0:128].astype(BF)
    kaug_ref[0, :, 128:256] = jnp.where(blk == lane, MASK_BIG, 0.0).astype(BF)
    kw_ref[0] = zn[:, 128:256].astype(BF)


def _proj_cmp_kernel(pt_ref, *refs, n_pages_step, tm, tpb):
    proj_in = refs[:6]
    pages = refs[6:6 + n_pages_step]
    pos_ref, w_ref = refs[6 + n_pages_step:8 + n_pages_step]
    proj_out = refs[8 + n_pages_step:20 + n_pages_step]
    o_ref, sa_ref, sb_ref = refs[20 + n_pages_step:]
    _proj_kernel(*proj_in, *proj_out, tm=tm, tpb=tpb)
    blocks_per_page = PAGE_SIZE // BLOCK
    for k in range(n_pages_step):
        pg = pages[k][0]
        for s, dst in enumerate((sa_ref, sb_ref)):
            rows = pg[s * KV_HALF:(s + 1) * KV_HALF, :].T
            for b in range(blocks_per_page):
                m = k * blocks_per_page + b
                dst[m * CMP_PITCH:m * CMP_PITCH + BLOCK, :] = rows[b * BLOCK:(b + 1) * BLOCK, :]
    _compress_rows((sa_ref, sb_ref), pos_ref, w_ref, o_ref, n_pages_step * blocks_per_page, CMP_PITCH)


def _project_prompt_compress_cache(x2, shift, scale, g, wn, wt, n_batch, seq,
                                   cache_t, page_table_flat, pos4, w4):
    tm = PROJ_TILE
    tpb = seq // tm
    nt = n_batch * seq
    n_steps = nt // tm
    blocks_per_page = PAGE_SIZE // BLOCK
    n_pages_all = page_table_flat.shape[0]
    n_pages_step = n_pages_all // n_steps
    assert n_pages_step * n_steps == n_pages_all
    blocks_step = n_pages_step * blocks_per_page
    chunk_steps = SEL_CHUNK // tm
    tok = lambda t, pt: (t, 0)
    per_b = lambda t, pt: (t // tpb, 0, 0)
    featT = lambda t, pt: (t // tpb, 0, t % tpb)
    rows3 = lambda t, pt: (t // tpb, t % tpb, 0)
    rows5 = lambda t, pt: (t // tpb, t % tpb, 0, 0, 0)
    chunk5 = lambda t, pt: (t // tpb, (t % tpb) // chunk_steps, 0, 0, t % chunk_steps)
    const2 = lambda t, pt: (0, 0)
    const4 = lambda t, pt: (0, 0, 0, 0)

    def page_map(k):
        return lambda t, pt: (pt[t * n_pages_step + k], 0, 0)

    kvT = jax.ShapeDtypeStruct((n_batch, KV_WIDTH, seq), F32)
    out_shape = (
        kvT, kvT, kvT,
        jax.ShapeDtypeStruct((nt, KV_HALF), F32),
        jax.ShapeDtypeStruct((nt, KV_HALF), F32),
        jax.ShapeDtypeStruct((nt, POOL_WIDTH), F32),
        jax.ShapeDtypeStruct((n_batch, NSA_WIDTH, seq), BF),
        jax.ShapeDtypeStruct((n_batch, seq, 256), BF),
        jax.ShapeDtypeStruct((n_batch, seq // SEL_CHUNK, N_KV_HEADS, V_ROWS, SEL_CHUNK), BF),
        jax.ShapeDtypeStruct((n_batch, seq, KV_HALF), BF),
        jax.ShapeDtypeStruct((n_batch, seq // WIN_CHUNK, N_KV_HEADS, V_ROWS, WIN_CHUNK), BF),
        jax.ShapeDtypeStruct((n_batch, 32, seq), F32),
        jax.ShapeDtypeStruct((n_pages_all * blocks_per_page, KV_WIDTH), F32),
    )
    out_specs = (
        pl.BlockSpec((1, KV_WIDTH, tm), featT),
        pl.BlockSpec((1, KV_WIDTH, tm), featT),
        pl.BlockSpec((1, KV_WIDTH, tm), featT),
        pl.BlockSpec((tm, KV_HALF), tok),
        pl.BlockSpec((tm, KV_HALF), tok),
        pl.BlockSpec((tm, POOL_WIDTH), tok),
        pl.BlockSpec((1, NSA_WIDTH, tm), featT),
        pl.BlockSpec((1, tm, 256), rows3),
        pl.BlockSpec((1, 1, N_KV_HEADS, V_ROWS, tm), chunk5),
        pl.BlockSpec((1, tm, KV_HALF), rows3),
        pl.BlockSpec((1, tm // WIN_CHUNK, N_KV_HEADS, V_ROWS, WIN_CHUNK), rows5),
        pl.BlockSpec((1, 32, tm), featT),
        pl.BlockSpec((blocks_step, KV_WIDTH), tok),
    )
    grid_spec = pltpu.PrefetchScalarGridSpec(
        num_scalar_prefetch=1,
        grid=(n_steps,),
        in_specs=[pl.BlockSpec((tm, D_MODEL), tok),
                  pl.BlockSpec((1, 1, D_MODEL), per_b),
                  pl.BlockSpec((1, 1, D_MODEL), per_b),
                  pl.BlockSpec((1, D_MODEL), const2),
                  pl.BlockSpec(wn.shape, const2),
                  pl.BlockSpec(wt.shape, const2)]
        + [pl.BlockSpec((1, KV_WIDTH, PAGE_SIZE), page_map(k)) for k in range(n_pages_step)]
        + [pl.BlockSpec(pos4.shape, const4), pl.BlockSpec(w4.shape, const4)],
        out_specs=out_specs,
        scratch_shapes=[pltpu.VMEM((blocks_step * CMP_PITCH, KV_HALF), F32),
                        pltpu.VMEM((blocks_step * CMP_PITCH, KV_HALF), F32)],
    )
    return pl.pallas_call(
        functools.partial(_proj_cmp_kernel, n_pages_step=n_pages_step, tm=tm, tpb=tpb),
        out_shape=out_shape,
        grid_spec=grid_spec,
        compiler_params=_cparams("arbitrary"),
        name="project_prompt_compress_cache",
    )(page_table_flat, x2, shift, scale, g, wn, wt, *([cache_t] * n_pages_step), pos4, w4)


def _compress_rows(src_refs, pos_ref, w_ref, o_ref, n_blocks, pitch):
    rows_per_slice = BLOCK // CMP_SPLIT
    for s in range(2):
        acc = jnp.zeros((n_blocks, KV_HALF), F32)
        for c in range(CMP_SPLIT):
            xc = jnp.concatenate(
                [src_refs[s][pl.ds(c * rows_per_slice + r, n_blocks, stride=pitch), :]
                 for r in range(rows_per_slice)], axis=1) + pos_ref[s, c]
            acc = acc + _dot(xc.astype(BF), w_ref[s, c])
        o_ref[:, s * KV_HALF:(s + 1) * KV_HALF] = acc


def _cmp_kernel(xa_ref, xb_ref, pos_ref, w_ref, o_ref):
    _compress_rows((xa_ref, xb_ref), pos_ref, w_ref, o_ref, CMP_GROUP, BLOCK)


def _compress(xa, xb, pos4, w4):
    m = xa.shape[0] // BLOCK
    rows = CMP_GROUP * BLOCK
    return pl.pallas_call(
        _cmp_kernel,
        out_shape=jax.ShapeDtypeStruct((m, KV_WIDTH), F32),
        grid=(m // CMP_GROUP,),
        in_specs=[pl.BlockSpec((rows, KV_HALF), lambda i: (i, 0)),
                  pl.BlockSpec((rows, KV_HALF), lambda i: (i, 0)),
                  pl.BlockSpec(pos4.shape, lambda i: (0, 0, 0, 0)),
                  pl.BlockSpec(w4.shape, lambda i: (0, 0, 0, 0))],
        out_specs=pl.BlockSpec((CMP_GROUP, KV_WIDTH), lambda i: (i, 0)),
        compiler_params=_cparams("arbitrary"),
        name="compress_blocks",
    )(xa, xb, pos4, w4)


def _attn_kernel(qT_ref, kaug_ref, vsT_ref, kw_ref, vwT_ref, kc_ref, vcT_ref, gT_ref, o_ref,
                 qa_ref, sa_ref, sb_ref, sc_ref, swb_ref, ma_ref, mb_ref, mc_ref, m_ref, acc_ref, outT_ref):
    i = pl.program_id(1)
    q0 = i * Q_TILE
    n_full = lax.div(q0, SEL_CHUNK)
    n_blk = kc_ref.shape[1]
    row = lax.broadcasted_iota(I32, (n_blk, Q_TILE), 0)
    tok = lax.broadcasted_iota(I32, (n_blk, Q_TILE), 1)
    qpos = q0 + tok
    cur = jnp.right_shift(qpos, 6)
    cmp_bias = jnp.where((row + 1) * BLOCK - 1 <= qpos, 0.0, -MASK_BIG)
    valid = row <= cur
    forced = (row == 0) | (row == cur) | (row == cur - 1)
    key_c = lax.broadcasted_iota(I32, (SEL_CHUNK, Q_TILE), 0)
    qpos_c = q0 + lax.broadcasted_iota(I32, (SEL_CHUNK, Q_TILE), 1)
    hs = [slice(h * HEAD_DIM, (h + 1) * HEAD_DIM) for h in range(N_KV_HEADS)]
    tri_bias = jnp.where(lax.broadcasted_iota(I32, (128, 128), 0) <= lax.broadcasted_iota(I32, (128, 128), 1),
                         0.0, -MASK_BIG)

    def bias4(keep):
        b = jnp.where(keep, 0.0, -MASK_BIG)
        return jnp.concatenate([b] * GROUP, axis=1)

    def online_step(st, s, v_t, s_max=None):
        m_old = m_ref[st]
        m_new = jnp.maximum(m_old, jnp.max(s, axis=0, keepdims=True) if s_max is None else s_max)
        alpha = jnp.exp2(m_old - m_new)
        p = jnp.exp2(s - m_new)
        acc_ref[st] = alpha * acc_ref[st] + _dot(v_t, p.astype(BF))
        m_ref[st] = m_new

    m_ref[...] = jnp.full(m_ref.shape, NEG_INF, F32)
    acc_ref[...] = jnp.zeros(acc_ref.shape, F32)

    for h in range(N_KV_HEADS):
        qa_ref[h] = jnp.zeros(qa_ref.shape[1:], BF)
        for g in range(GROUP):
            r0 = h * GROUP * HEAD_DIM + g * HEAD_DIM
            qa_ref[h, hs[h], g * Q_TILE:(g + 1) * Q_TILE] = qT_ref[0, r0:r0 + HEAD_DIM, :]

    w0 = jnp.maximum(q0 - WINDOW, 0)
    kw_a = kw_ref[0, pl.ds(pl.multiple_of(w0, WIN_CHUNK), WINDOW), :]
    kw_b = kw_ref[0, pl.ds(pl.multiple_of(q0, Q_TILE), Q_TILE), :]
    for h in range(N_KV_HEADS):
        sc_ref[h] = _dot(kw_a, qa_ref[h, 0:KV_HALF, :])
        swb_ref[h] = _dot(kw_b, qa_ref[h, 0:KV_HALF, :])

    o_c = []
    scores = []
    for h in range(N_KV_HEADS):
        sc = _dot(kc_ref[0], qa_ref[h, 0:KV_HALF, :])
        imp = jnp.zeros((n_blk, Q_TILE), F32)
        p_parts = []
        for g in range(GROUP):
            s = sc[:, g * Q_TILE:(g + 1) * Q_TILE] + cmp_bias
            mx = jnp.max(s, axis=0, keepdims=True)
            e = jnp.exp2(s - mx)
            inv = jnp.where(mx > -0.5 * MASK_BIG, 1.0 / jnp.sum(e, axis=0, keepdims=True), 0.0)
            p = e * inv
            imp = imp + p
            p_parts.append(p)
        o_c.append(_dot(vcT_ref[0, hs[h], :], jnp.concatenate(p_parts, axis=1).astype(BF)))
        scores.append(jnp.where(valid, jnp.where(forced, -2.0, imp), -1.0))

    blk_f = row.astype(F32)

    def pick(work):
        best = jnp.max(work, axis=0, keepdims=True)
        first = jnp.min(jnp.where(work == best, blk_f, float(n_blk)), axis=0, keepdims=True)
        return jnp.where((blk_f == first) & (best >= 0.0), -2.0, work)

    for _ in range(N_SELECT - N_FORCED):
        scores = [pick(w) for w in scores]
    for h in range(N_KV_HEADS):
        selm1 = jnp.where(scores[h] == -2.0, 0.0, -1.0).astype(BF)
        for g in range(GROUP):
            qa_ref[h, KV_HALF:KV_HALF + n_blk, g * Q_TILE:(g + 1) * Q_TILE] = selm1

    last_chunk = kaug_ref.shape[1] // SEL_CHUNK - 1

    def sel_scores(j, buf):
        dst_ref, max_ref = buf
        j = jnp.minimum(j, last_chunk)
        kt = kaug_ref[0, pl.ds(pl.multiple_of(j * SEL_CHUNK, SEL_CHUNK), SEL_CHUNK), :]
        for h in range(N_KV_HEADS):
            s = _dot(kt, qa_ref[h])
            dst_ref[h] = s
            max_ref[h] = jnp.max(s, axis=0, keepdims=True)

    def sel_process(buf, j, causal):
        src_ref, max_ref = buf
        for h in range(N_KV_HEADS):
            if causal:
                r0 = q0 - j * SEL_CHUNK
                for p in range(Q_TILE // 128):
                    rows = pl.ds(pl.multiple_of(r0 + p * 128, 128), 128)
                    for g in range(GROUP):
                        lanes = slice(g * Q_TILE + p * 128, g * Q_TILE + (p + 1) * 128)
                        src_ref[h, rows, lanes] = src_ref[h, rows, lanes] + tri_bias
                online_step(h, src_ref[h], vsT_ref[0, j, h])
            else:
                online_step(h, src_ref[h], vsT_ref[0, j, h], max_ref[h])

    buf_a, buf_b, buf_c = (sa_ref, ma_ref), (sb_ref, mb_ref), (sc_ref, mc_ref)
    sel_scores(0, buf_a)
    sel_scores(1, buf_b)

    j0 = lax.div(w0, WIN_CHUNK)
    jq = lax.div(q0, WIN_CHUNK)
    delta = qpos_c - (w0 + key_c)
    wbias = bias4((delta >= 0) & (delta < WINDOW))
    key_d = lax.broadcasted_iota(I32, (Q_TILE, Q_TILE), 0)
    tok_d = lax.broadcasted_iota(I32, (Q_TILE, Q_TILE), 1)
    dbias = bias4((key_d <= tok_d) & (q0 >= WINDOW))
    for h in range(N_KV_HEADS):
        v_a = jnp.concatenate([vwT_ref[0, j0 + c, h] for c in range(WINDOW // WIN_CHUNK)], axis=1)
        v_b = jnp.concatenate([vwT_ref[0, jq + c, h] for c in range(Q_TILE // WIN_CHUNK)], axis=1)
        online_step(2 + h, sc_ref[h] + wbias, v_a)
        online_step(2 + h, swb_ref[h] + dbias, v_b)

    def trio_body(t, carry):
        j = 3 * t
        sel_scores(j + 2, buf_c)
        sel_process(buf_a, j, False)
        sel_scores(j + 3, buf_a)
        sel_process(buf_b, j + 1, False)
        sel_scores(j + 4, buf_b)
        sel_process(buf_c, j + 2, False)
        return carry

    n_trios = lax.div(n_full, 3)
    lax.fori_loop(0, n_trios, trio_body, 0)
    j_last = 3 * n_trios
    n_left = n_full - j_last

    @pl.when(n_left == 0)
    def _():
        sel_process(buf_a, j_last, True)

    @pl.when(n_left == 1)
    def _():
        sel_process(buf_a, j_last, False)
        sel_process(buf_b, j_last + 1, True)

    @pl.when(n_left == 2)
    def _():
        sel_scores(j_last + 2, buf_c)
        sel_process(buf_a, j_last, False)
        sel_process(buf_b, j_last + 1, False)
        sel_process(buf_c, j_last + 2, True)

    for h in range(N_KV_HEADS):
        o_s = acc_ref[h, 0:HEAD_DIM, :] / acc_ref[h, HEAD_DIM:HEAD_DIM + 1, :]
        o_w = acc_ref[2 + h, 0:HEAD_DIM, :] / acc_ref[2 + h, HEAD_DIM:HEAD_DIM + 1, :]
        for g in range(GROUP):
            gs = slice(g * Q_TILE, (g + 1) * Q_TILE)
            gr = h * GROUP * 3 + g * 3
            og = (gT_ref[0, gr:gr + 1, :] * o_c[h][:, gs] + gT_ref[0, gr + 1:gr + 2, :] * o_s[:, gs]
                  + gT_ref[0, gr + 2:gr + 3, :] * o_w[:, gs])
            r0 = h * GROUP * HEAD_DIM + g * HEAD_DIM
            outT_ref[r0:r0 + HEAD_DIM, :] = og

    o_ref[0] = outT_ref[...].T.astype(BF)


def _attention_prompt(qT, kaug, vsT, kw, vwT, kc, vcT, gT, n_batch, seq):
    per_b3 = lambda n, i: (n, 0, 0)
    per_b5 = lambda n, i: (n, 0, 0, 0, 0)
    rows = GROUP * Q_TILE
    return pl.pallas_call(
        _attn_kernel,
        out_shape=jax.ShapeDtypeStruct((n_batch, seq, NSA_WIDTH), BF),
        grid=(n_batch, seq // Q_TILE),
        in_specs=[pl.BlockSpec((1, NSA_WIDTH, Q_TILE), lambda n, i: (n, 0, i)),
                  pl.BlockSpec((1, seq, 256), per_b3),
                  pl.BlockSpec((1, seq // SEL_CHUNK, N_KV_HEADS, V_ROWS, SEL_CHUNK), per_b5),
                  pl.BlockSpec((1, seq, KV_HALF), per_b3),
                  pl.BlockSpec((1, seq // WIN_CHUNK, N_KV_HEADS, V_ROWS, WIN_CHUNK), per_b5),
                  pl.BlockSpec((1, seq // BLOCK, KV_HALF), per_b3),
                  pl.BlockSpec((1, KV_HALF, seq // BLOCK), per_b3),
                  pl.BlockSpec((1, 32, Q_TILE), lambda n, i: (n, 0, i))],
        out_specs=pl.BlockSpec((1, Q_TILE, NSA_WIDTH), lambda n, i: (n, i, 0)),
        scratch_shapes=[pltpu.VMEM((N_KV_HEADS, 256, rows), BF),
                        pltpu.VMEM((N_KV_HEADS, SEL_CHUNK, rows), F32),
                        pltpu.VMEM((N_KV_HEADS, SEL_CHUNK, rows), F32),
                        pltpu.VMEM((N_KV_HEADS, SEL_CHUNK, rows), F32),
                        pltpu.VMEM((N_KV_HEADS, Q_TILE, rows), F32),
                        pltpu.VMEM((N_KV_HEADS, 1, rows), F32),
                        pltpu.VMEM((N_KV_HEADS, 1, rows), F32),
                        pltpu.VMEM((N_KV_HEADS, 1, rows), F32),
                        pltpu.VMEM((2 * N_KV_HEADS, 1, rows), F32),
                        pltpu.VMEM((2 * N_KV_HEADS, V_ROWS, rows), F32),
                        pltpu.VMEM((NSA_WIDTH, Q_TILE), F32)],
        compiler_params=_cparams("arbitrary", "arbitrary"),
        name="attention_prompt",
    )(qT, kaug, vsT, kw, vwT, kc, vcT, gT)


def _pool_kernel(u_ref, halo_ref, wp_ref, ps_ref, o_ref, ext_ref, lvl_ref, *, tm, tpb):
    t = pl.program_id(0) % tpb
    ext_ref[0:16, :] = jnp.where(t == 0, 0.0, halo_ref[...])
    u = u_ref[...]
    ext_ref[16:16 + tm, :] = u
    pos = t * tm + lax.broadcasted_iota(I32, (tm, 1), 0)
    end = 16 + tm
    outs = []
    for gi, w in enumerate(POOL_WINDOWS):
        cs = slice(gi * POOL_GROUP_DIM, (gi + 1) * POOL_GROUP_DIM)
        lo = 16 - (w - 2)
        acc = ext_ref[lo:end, cs] + ext_ref[lo - 1:end - 1, cs]
        d = 2
        while d < w:
            lvl_ref[lo:end, :] = acc
            lo += d
            acc = lvl_ref[lo:end, :] + lvl_ref[lo - d:end - d, :]
            d *= 2
        cnt = jnp.minimum(pos + 1, w).astype(F32)
        pooled = acc / cnt - u[:, cs]
        outs.append(_dot(pooled.astype(BF), wp_ref[gi]))
    o_ref[...] = (jnp.concatenate(outs, axis=1) * ps_ref[...]).astype(BF)


def _pool_prompt(u, w_pool, pool_scale, n_batch, seq):
    tm = TOKEN_TILE
    tpb = seq // tm
    nt = n_batch * seq
    return pl.pallas_call(
        functools.partial(_pool_kernel, tm=tm, tpb=tpb),
        out_shape=jax.ShapeDtypeStruct((nt, POOL_WIDTH), BF),
        grid=(nt // tm,),
        in_specs=[pl.BlockSpec((tm, POOL_WIDTH), lambda t: (t, 0)),
                  pl.BlockSpec((16, POOL_WIDTH), lambda t: (jnp.maximum(t * (tm // 16) - 1, 0), 0)),
                  pl.BlockSpec(w_pool.shape, lambda t: (0, 0, 0)),
                  pl.BlockSpec((1, POOL_WIDTH), lambda t: (0, 0))],
        out_specs=pl.BlockSpec((tm, POOL_WIDTH), lambda t: (t, 0)),
        scratch_shapes=[pltpu.VMEM((tm + 16, POOL_WIDTH), F32), pltpu.VMEM((tm + 16, POOL_GROUP_DIM), F32)],
        compiler_params=_cparams("arbitrary"),
        name="pool_prompt",
    )(u, u, w_pool, pool_scale)


def _spool_kernel(u_ref, hist_ref, wp_ref, ps_ref, o_ref):
    u = u_ref[...]
    outs = []
    for gi, w in enumerate(POOL_WINDOWS):
        cs = slice(gi * POOL_GROUP_DIM, (gi + 1) * POOL_GROUP_DIM)
        acc = u[:, cs]
        for k in range(1, w):
            acc = acc + hist_ref[POOL_HIST - k, :, cs]
        pooled = acc / float(w) - u[:, cs]
        outs.append(_dot(pooled.astype(BF), wp_ref[gi]))
    o_ref[...] = (jnp.concatenate(outs, axis=1) * ps_ref[...]).astype(BF)


def _pool_sample(u, hist_t, w_pool, pool_scale):
    n = u.shape[0]
    return pl.pallas_call(
        _spool_kernel,
        out_shape=jax.ShapeDtypeStruct((n, POOL_WIDTH), BF),
        name="pool_sample",
    )(u, hist_t, w_pool, pool_scale)


def _merge_kernel(x_ref, sh_ref, sc_ref, gate_ref, gpre_ref, gpost_ref, onsa_ref, opool_ref,
                  wgm_ref, wun_ref, wup_ref, wo_ref, o_ref):
    x = x_ref[...]
    h = _rms(x, gpre_ref[...]) * (1.0 + sc_ref[0]) + sh_ref[0]
    gm = jax.nn.sigmoid(_dot(h.astype(BF), wgm_ref[...]))
    m = (gm[:, :D_MODEL] * _dot(onsa_ref[...], wun_ref[...])
         + gm[:, D_MODEL:] * _dot(opool_ref[...], wup_ref[...]))
    m = _dot(m.astype(BF), wo_ref[...])
    o_ref[...] = x + gate_ref[0] * _rms(m, gpost_ref[...])


def _mlp_kernel(x_ref, sh_ref, sc_ref, gate_ref, gpre_ref, gpost_ref, w1_ref, w2_ref, o_ref):
    x = x_ref[...]
    h = _rms(x, gpre_ref[...]) * (1.0 + sc_ref[0]) + sh_ref[0]
    hb = h.astype(BF)
    f = jnp.zeros(x.shape, F32)
    fc = 1024
    for c in range(D_FF // fc):
        a = jnp.maximum(_dot(hb, w1_ref[:, c * fc:(c + 1) * fc]), 0.0)
        f = f + _dot((a * a).astype(BF), w2_ref[c * fc:(c + 1) * fc, :])
    o_ref[...] = x + gate_ref[0] * _rms(f, gpost_ref[...])


def _token_call(kernel, name, x2, mods, tm, rows_per_mod, extra_tok, consts):
    nt = x2.shape[0]
    r = mods[0].shape[1]
    mod_spec = pl.BlockSpec((1, r, D_MODEL), lambda t: ((t * tm) // rows_per_mod, 0, 0))
    in_specs = [pl.BlockSpec((tm, D_MODEL), lambda t: (t, 0))] + [mod_spec] * len(mods)
    in_specs += [pl.BlockSpec((1, D_MODEL), lambda t: (0, 0))] * 2
    in_specs += [pl.BlockSpec((tm, a.shape[1]), lambda t: (t, 0)) for a in extra_tok]
    in_specs += [pl.BlockSpec(w.shape, lambda t: (0, 0), pipeline_mode=pl.Buffered(1)) for w in consts[2:]]
    return pl.pallas_call(
        kernel,
        out_shape=jax.ShapeDtypeStruct((nt, D_MODEL), F32),
        grid=(nt // tm,),
        in_specs=in_specs,
        out_specs=pl.BlockSpec((tm, D_MODEL), lambda t: (t, 0)),
        compiler_params=_cparams("arbitrary"),
        name=name,
    )(x2, *mods, consts[0], consts[1], *extra_tok, *consts[2:])


def _sproj_kernel(x_ref, sh_ref, sc_ref, g_ref, wn_ref, wt_ref, z_ref, zs_ref, zT_ref):
    x = x_ref[...]
    h = _rms(x, g_ref[...]) * (1.0 + sc_ref[...]) + sh_ref[...]
    hb = h.astype(BF)
    z = _dot(hb, wn_ref[...])
    z_ref[...] = z
    zs_ref[...] = jax.nn.sigmoid(z)
    zT_ref[...] = _dot_nt(wt_ref[...], hb)


def _project_sample(x2, shift, scale, g, wn, wt):
    n = x2.shape[0]
    shp = jax.ShapeDtypeStruct((n, wn.shape[1]), F32)
    return pl.pallas_call(
        _sproj_kernel,
        out_shape=(shp, shp, jax.ShapeDtypeStruct((wt.shape[0], n), F32)),
        name="project_sample",
    )(x2, shift, scale, g, wn, wt)


def _sattn_init(kaug_ref, past_len, seq_step):
    @pl.when(pl.program_id(0) == 0)
    def _():
        blk = lax.broadcasted_iota(I32, (128, past_len), 0)
        key_blk = jnp.right_shift(lax.broadcasted_iota(I32, (128, past_len), 1), 6)
        onehot = jnp.where(blk == key_blk, MASK_BIG, 0.0).astype(BF)
        for q in range(seq_step):
            kaug_ref[q, 128:256, :] = onehot


def _sattn_main(*refs, n_pages, past_len, seq_step):
    pages = refs[:seq_step * n_pages]
    (qb_ref, kvc_ref, win_ref, ksn_ref, kwn_ref, kwnT_ref, g_ref,
     o_ref, nwin_ref, kaug_ref, vall_ref) = refs[seq_step * n_pages:]

    for q in range(seq_step):
        for p in range(n_pages):
            pg = pages[q * n_pages + p][0]
            kaug_ref[q, 0:128, p * PAGE_SIZE:(p + 1) * PAGE_SIZE] = pg[0:KV_HALF, :].astype(BF)
            vall_ref[q, :, p * PAGE_SIZE:(p + 1) * PAGE_SIZE] = pg[KV_HALF:KV_WIDTH, :].astype(BF)

    nb_past = past_len // BLOCK
    win_buf = win_ref.shape[2]
    n_rows = 8 * seq_step
    per_seq = lambda f: jnp.concatenate([f(q) for q in range(seq_step)], axis=0)
    rows_of = lambda x, q: x[8 * q:8 * (q + 1)]
    qb = qb_ref[...].reshape(n_rows, KV_HALF).astype(BF)
    qf = qb.astype(F32)
    row = lax.broadcasted_iota(I32, (n_rows, 128), 0)
    lane = lax.broadcasted_iota(I32, (n_rows, 128), 1)

    def new_key(ref):
        return per_seq(lambda q: jnp.broadcast_to(ref[q], (8, KV_WIDTH))).astype(BF).astype(F32)

    s_c = per_seq(lambda q: _dot_nt(rows_of(qb, q), kvc_ref[q, :, 0:128].astype(BF)))
    cm = lane < nb_past
    s_c = jnp.where(cm, s_c, NEG_INF)
    mx = jnp.max(s_c, axis=1, keepdims=True)
    mx = jnp.where(mx > NEG_INF, mx, 0.0)
    e = jnp.where(cm, jnp.exp(s_c - mx), 0.0)
    p_c = e / jnp.maximum(jnp.sum(e, axis=1, keepdims=True), 1e-30)
    o_c = per_seq(lambda q: _dot(rows_of(p_c, q).astype(BF), kvc_ref[q, :, 128:256].astype(BF)))

    imp = jnp.zeros((n_rows, 128), F32)
    for grp in range(n_rows // GROUP):
        in_grp = jnp.right_shift(row, 2) == grp
        imp = jnp.where(in_grp, jnp.sum(jnp.where(in_grp, p_c, 0.0), axis=0, keepdims=True), imp)
    cur = nb_past
    forced = (lane == 0) | (lane == cur) | (lane == cur - 1)
    score = jnp.where(lane <= cur, imp + FORCE_SCORE * forced.astype(F32), -1.0)
    cnt = jnp.zeros((n_rows, 128), F32)
    for bp in range(nb_past + 1):
        other = score[:, bp:bp + 1]
        ahead = (other > score) | ((other == score) & (bp < lane))
        cnt = cnt + jnp.where(ahead, 1.0, 0.0)
    sel = (cnt < float(N_SELECT)) & (score >= 0.0)
    selm1 = jnp.where(sel, 0.0, -1.0).astype(BF)

    qaug = jnp.concatenate([qb, selm1], axis=1)
    s_s = per_seq(lambda q: _dot(rows_of(qaug, q), kaug_ref[q]))
    kv_n = new_key(ksn_ref)
    s_n = jnp.sum(qf * kv_n[:, 0:128], axis=1, keepdims=True)
    m_s = jnp.maximum(jnp.max(s_s, axis=1, keepdims=True), s_n)
    e_s = jnp.exp(s_s - m_s)
    e_n = jnp.exp(s_n - m_s)
    l_s = jnp.sum(e_s, axis=1, keepdims=True) + e_n
    pv = per_seq(lambda q: _dot_nt(rows_of(e_s, q).astype(BF), vall_ref[q]))
    o_s = (pv + e_n.astype(BF).astype(F32) * kv_n[:, 128:256]) / l_s

    s_w = per_seq(lambda q: _dot(rows_of(qb, q), win_ref[q, 0:KV_HALF, :].astype(BF)))
    lane_w = lax.broadcasted_iota(I32, (n_rows, win_buf), 1)
    s_w = jnp.where(lane_w >= win_buf + 1 - WINDOW, s_w, NEG_INF)
    kv_n = new_key(kwn_ref)
    s_n = jnp.sum(qf * kv_n[:, 0:128], axis=1, keepdims=True)
    m_w = jnp.maximum(jnp.max(s_w, axis=1, keepdims=True), s_n)
    e_w = jnp.exp(s_w - m_w)
    e_n = jnp.exp(s_n - m_w)
    l_w = jnp.sum(e_w, axis=1, keepdims=True) + e_n
    pv = per_seq(lambda q: _dot_nt(rows_of(e_w, q).astype(BF), win_ref[q, KV_HALF:KV_WIDTH, :].astype(BF)))
    o_w = (pv + e_n.astype(BF).astype(F32) * kv_n[:, 128:256]) / l_w

    g = g_ref[...].reshape(n_rows, 3)
    o_ref[...] = (g[:, 0:1] * o_c + g[:, 1:2] * o_s + g[:, 2:3] * o_w).reshape(seq_step, 8, KV_HALF)

    seq_lane = lax.broadcasted_iota(I32, kwnT_ref.shape, 1)
    row_lane = lax.broadcasted_iota(I32, (KV_WIDTH, win_buf), 1)
    for q in range(seq_step):
        n = pl.program_id(0) * seq_step + q
        new_col = jnp.sum(jnp.where(seq_lane == n, kwnT_ref[...], 0.0), axis=1, keepdims=True)
        shifted = pltpu.roll(win_ref[q], win_buf - 1, axis=1)
        nwin_ref[q] = jnp.where(row_lane == win_buf - 1, new_col, shifted)


def _sattn_kernel(pt_ref, *refs, n_pages, past_len, seq_step):
    _sattn_init(refs[-2], past_len, seq_step)
    _sattn_main(*refs, n_pages=n_pages, past_len=past_len, seq_step=seq_step)


def _attention_sample(page_table_flat, cache_t, qblk, kvc_pad, win_t, kvs_new, kvw_new, kvw_new_t, gates8,
                      n_pages, past_len):
    n_seq = qblk.shape[0]
    win_buf = win_t.shape[2]
    g = SAMPLE_SEQ_STEP
    const2 = lambda t, pt: (0, 0)
    per_step = lambda t, pt: (t, 0, 0)

    def page_map(k):
        return lambda t, pt: (pt[t * g * n_pages + k], 0, 0)

    grid_spec = pltpu.PrefetchScalarGridSpec(
        num_scalar_prefetch=1,
        grid=(n_seq // g,),
        in_specs=[pl.BlockSpec((1, KV_WIDTH, PAGE_SIZE), page_map(k)) for k in range(g * n_pages)]
        + [pl.BlockSpec((g, 8, 128), per_step),
           pl.BlockSpec((g, 128, KV_WIDTH), per_step),
           pl.BlockSpec((g, KV_WIDTH, win_buf), per_step),
           pl.BlockSpec((g, 1, KV_WIDTH), per_step),
           pl.BlockSpec((g, 1, KV_WIDTH), per_step),
           pl.BlockSpec(kvw_new_t.shape, const2),
           pl.BlockSpec((g, 8, 3), per_step)],
        out_specs=(pl.BlockSpec((g, 8, 128), per_step),
                   pl.BlockSpec((g, KV_WIDTH, win_buf), per_step)),
        scratch_shapes=[pltpu.VMEM((g, 256, past_len), BF), pltpu.VMEM((g, KV_HALF, past_len), BF)],
    )
    return pl.pallas_call(
        functools.partial(_sattn_kernel, n_pages=n_pages, past_len=past_len, seq_step=g),
        out_shape=(jax.ShapeDtypeStruct((n_seq, 8, 128), F32),
                   jax.ShapeDtypeStruct((n_seq, KV_WIDTH, win_buf), F32)),
        grid_spec=grid_spec,
        compiler_params=_cparams("arbitrary"),
        name="attention_sample",
    )(page_table_flat, *([cache_t] * (g * n_pages)), qblk, kvc_pad, win_t, kvs_new, kvw_new, kvw_new_t, gates8)


def _kv_rows_view(kv_t):
    n, _, t = kv_t.shape
    return jnp.transpose(kv_t.reshape(n, 2, N_KV_HEADS, HEAD_DIM, t), (0, 4, 1, 2, 3))


def _kv_feat_view(kv):
    n, t = kv.shape[:2]
    return jnp.transpose(kv, (0, 2, 3, 4, 1)).reshape(n, KV_WIDTH, t)


def kernel(x_prompt, x_sample, cache_cmp_kv, cache_sel_kv, state_win_kv, state_pool, page_table, c_prompt, c_sample, w_ada, b_ada, g_pre_mix, g_post_mix, g_pre_mlp, g_post_mlp, w_in, w_cmp, pos_cmp, w_pool, pool_scale, w_up_nsa, w_up_pool, w_o, w_ff1, w_ff2):
    n_batch, seq, _ = x_prompt.shape
    n_seq = x_sample.shape[0]
    n_pages = page_table.shape[1]
    past_len = n_pages * PAGE_SIZE
    nb_past = past_len // BLOCK
    assert x_sample.shape[1] == 1 and w_ada.shape[0] == 1 and past_len % BLOCK == 0
    assert seq % TOKEN_TILE == 0 and seq // BLOCK == 128 and state_win_kv.shape[2] == WINDOW
    assert (n_seq * nb_past) % CMP_GROUP == 0 and n_seq == CMP_GROUP
    assert FORCE_SCORE > GROUP

    w_t = w_in[0].T
    wt = jnp.pad(w_t[0:1304], ((0, 8), (0, 0))).astype(BF)
    wn = jnp.concatenate([w_t[768:896], w_t[1024:1152], w_t[1304:1816], w_t[512:768]], axis=0).T.astype(BF)
    ws = jnp.pad(w_t[0:1816], ((0, 104), (0, 0))).T.astype(BF)
    wgm = w_t[1816:3864].T.astype(BF)
    eye = jnp.eye(N_KV_HEADS, dtype=F32)
    wc = w_cmp[0].astype(BF)
    wz = jnp.zeros_like(wc)
    w4 = jnp.concatenate([jnp.concatenate([wc, wz], axis=3), jnp.concatenate([wz, wc], axis=3)], axis=2)
    w4 = w4.reshape(2, CMP_SPLIT, CMP_K // CMP_SPLIT, KV_HALF)
    pos4 = jnp.broadcast_to(jnp.transpose(pos_cmp[0], (1, 0, 2))[:, :, None, :],
                            (2, BLOCK, N_KV_HEADS, HEAD_DIM)).reshape(2, CMP_SPLIT, 1, CMP_K // CMP_SPLIT)
    wp = w_pool[0].astype(BF)
    ps = pool_scale[0].reshape(1, POOL_WIDTH)
    wun, wup, wo = w_up_nsa[0].astype(BF), w_up_pool[0].astype(BF), w_o[0].astype(BF)
    w1, w2 = w_ff1[0].astype(BF), w_ff2[0].astype(BF)
    gpm, gqm = g_pre_mix[0].reshape(1, D_MODEL), g_post_mix[0].reshape(1, D_MODEL)
    gpf, gqf = g_pre_mlp[0].reshape(1, D_MODEL), g_post_mlp[0].reshape(1, D_MODEL)

    n_c = n_batch + n_seq
    c_all = jnp.pad(jnp.concatenate([c_prompt, c_sample], axis=0), ((0, (-n_c) % 8), (0, 0)))
    ada = _adaln(c_all, w_ada[0], b_ada[0].reshape(1, -1))
    ada_p = ada[:n_batch].reshape(n_batch, 6, 1, D_MODEL)
    ada_s = ada[n_batch:n_c].reshape(n_seq, 6, D_MODEL)
    mods_p = [ada_p[:, k] for k in range(6)]
    mods_s = [ada_s[:, k][None] for k in range(6)]

    xp = x_prompt.reshape(n_batch * seq, D_MODEL)
    pt_flat = page_table.reshape(-1)
    (kvcT, kvsT, kvwT, kvc_a, kvc_b, u_p, qT, kaug, vsT, kw, vwT, gT,
     kvc_past) = _project_prompt_compress_cache(xp, mods_p[0], mods_p[1], gpm, wn, wt, n_batch, seq,
                                                _kv_feat_view(cache_cmp_kv[0]), pt_flat, pos4, w4)
    kvc_blk = _compress(kvc_a, kvc_b, pos4, w4).reshape(n_batch, seq // BLOCK, KV_WIDTH)
    kc = kvc_blk[:, :, 0:128].astype(BF)
    vcT = jnp.swapaxes(kvc_blk[:, :, 128:256], 1, 2).astype(BF)
    onsa_p = _attention_prompt(qT, kaug, vsT, kw, vwT, kc, vcT, gT, n_batch, seq)
    opool_p = _pool_prompt(u_p, wp, ps, n_batch, seq)
    x1_p = _token_call(_merge_kernel, "merge_prompt", xp, [mods_p[0], mods_p[1], mods_p[2]], TOKEN_TILE, seq,
                       [onsa_p.reshape(n_batch * seq, NSA_WIDTH), opool_p], [gpm, gqm, wgm, wun, wup, wo])
    y_p = _token_call(_mlp_kernel, "mlp_prompt", x1_p, [mods_p[3], mods_p[4], mods_p[5]], TOKEN_TILE, seq,
                      [], [gpf, gqf, w1, w2])

    xs = x_sample.reshape(n_seq, D_MODEL)
    z, zsig, zT = _project_sample(xs, mods_s[0][0], mods_s[1][0], gpm, ws, wt)
    q_s = z[:, 0:512] * (HEAD_DIM ** -0.5)
    kvc_n, kvs_n, kvw_n = z[:, 512:768], z[:, 768:1024], z[:, 1024:1280]
    gates_s = zsig[:, 1280:1304].reshape(n_seq, N_HEADS, 3)
    u_s = z[:, 1304:1816]
    q5 = q_s.reshape(n_seq, N_KV_HEADS, GROUP, 1, HEAD_DIM) * eye[None, :, None, :, None]
    qblk = q5.reshape(n_seq, N_HEADS, KV_HALF)

    last_a = jnp.pad(kvc_n[:, None, 0:128], ((0, 0), (0, BLOCK - 1), (0, 0))).reshape(n_seq * BLOCK, KV_HALF)
    last_b = jnp.pad(kvc_n[:, None, 128:256], ((0, 0), (0, BLOCK - 1), (0, 0))).reshape(n_seq * BLOCK, KV_HALF)
    kvc_last = _compress(last_a, last_b, pos4, w4)
    kvc_s = jnp.concatenate([kvc_past.reshape(n_seq, nb_past, KV_WIDTH), kvc_last[:, None, :]], axis=1)
    kvc_pad = jnp.pad(kvc_s, ((0, 0), (0, 128 - nb_past - 1), (0, 0)))
    o8, new_win_t = _attention_sample(
        pt_flat, _kv_feat_view(cache_sel_kv[0]), qblk, kvc_pad, _kv_feat_view(state_win_kv[0]),
        kvs_n[:, None, :], kvw_n[:, None, :], zT[1024:1280], gates_s, n_pages, past_len)
    o5 = o8.reshape(n_seq, N_KV_HEADS, GROUP, N_KV_HEADS, HEAD_DIM)
    onsa_s = jnp.concatenate([o5[:, 0, :, 0, :], o5[:, 1, :, 1, :]], axis=1).reshape(n_seq, NSA_WIDTH).astype(BF)
    opool_s = _pool_sample(u_s, jnp.swapaxes(state_pool[0], 0, 1), wp, ps)
    x1_s = _token_call(_merge_kernel, "merge_sample", xs, [mods_s[0], mods_s[1], mods_s[2]], n_seq, n_seq,
                       [onsa_s, opool_s], [gpm, gqm, wgm, wun, wup, wo])
    y_s = _token_call(_mlp_kernel, "mlp_sample", x1_s, [mods_s[3], mods_s[4], mods_s[5]], n_seq, n_seq,
                      [], [gpf, gqf, w1, w2])

    win_p = min(WINDOW, seq)
    new_kv_s = lambda rows: _kv_rows_view(rows.reshape(1, KV_WIDTH, n_seq))[0][None, :, None]
    return (
        y_p.reshape(n_batch, seq, D_MODEL),
        y_s.reshape(n_seq, 1, D_MODEL),
        _kv_rows_view(kvcT)[None],
        _kv_rows_view(kvsT)[None],
        _kv_rows_view(kvwT[:, :, seq - win_p:])[None],
        u_p.reshape(n_batch, seq, POOL_WIDTH)[None, :, seq - POOL_HIST:],
        new_kv_s(zT[512:768]),
        new_kv_s(zT[768:1024]),
        _kv_rows_view(new_win_t)[None],
        jnp.concatenate([state_pool[0][:, 1:], u_s[:, None, :]], axis=1)[None],
    )
```

```python
import functools
import math

import jax
import jax.numpy as jnp
from jax import lax
from jax.experimental import pallas as pl
from jax.experimental.pallas import tpu as pltpu

D_MODEL = 1024
N_HEADS = 8
HEAD_DIM = 64
N_KV_HEADS = 2
GROUP = N_HEADS // N_KV_HEADS
BLOCK = 64
N_SELECT = 16
WINDOW = 512
Q_TILE = 256
WIN_CHUNK = 128
NSA_WIDTH = N_HEADS * HEAD_DIM
KV_WIDTH = 2 * N_KV_HEADS * HEAD_DIM
KV_HALF = N_KV_HEADS * HEAD_DIM
FORCE_SCORE = 16.0
N_FORCED = 3
POOL_WINDOWS = (2, 4, 8, 16)
POOL_WIDTH = 512
POOL_GROUP_DIM = 128
POOL_HIST = 15
D_FF = 4 * D_MODEL
EPS = 1e-6
PAGE_SIZE = 128
V_ROWS = HEAD_DIM + 16
CMP_K = BLOCK * KV_HALF
CMP_SPLIT = 4
CMP_GROUP = 128
CMP_PITCH = BLOCK + 4

BF = jnp.bfloat16
F32 = jnp.float32
I32 = jnp.int32
MASK_BIG = 2.0 ** 100
NEG_INF = float("-inf")
LOG2E = math.log2(math.e)

TOKEN_TILE = 512
PROJ_TILE = 256
SEL_CHUNK = 512
SAMPLE_SEQ_STEP = 4
VMEM_LIMIT = 56 * 1024 * 1024


def _cparams(*sem):
    return pltpu.CompilerParams(dimension_semantics=sem, vmem_limit_bytes=VMEM_LIMIT)


def _rms(x, g):
    return x * lax.rsqrt(jnp.mean(x * x, axis=-1, keepdims=True) + EPS) * g


def _dot(a, b):
    return jnp.dot(a, b, preferred_element_type=F32)


def _dot_nt(a, b):
    return lax.dot_general(a, b, (((1,), (1,)), ((), ())), preferred_element_type=F32)


def _ada_kernel(c_ref, w_ref, b_ref, o_ref):
    c = c_ref[...]
    a = (c * jax.nn.sigmoid(c)).astype(BF)
    o_ref[...] = _dot(a, w_ref[...].astype(BF)) + b_ref[...]


def _adaln(c_all, w_ada, b_ada):
    rows = c_all.shape[0]
    n_out = w_ada.shape[1]
    tn = 1024
    return pl.pallas_call(
        _ada_kernel,
        out_shape=jax.ShapeDtypeStruct((rows, n_out), F32),
        grid=(n_out // tn,),
        in_specs=[pl.BlockSpec((rows, D_MODEL), lambda j: (0, 0)),
                  pl.BlockSpec((D_MODEL, tn), lambda j: (0, j)),
                  pl.BlockSpec((1, tn), lambda j: (0, j))],
        out_specs=pl.BlockSpec((rows, tn), lambda j: (0, j)),
        compiler_params=_cparams("arbitrary"),
        name="adaln",
    )(c_all, w_ada, b_ada)


def _proj_kernel(x_ref, sh_ref, sc_ref, g_ref, wn_ref, wt_ref,
                 kvcT_ref, kvsT_ref, kvwT_ref, ca_ref, cb_ref, u_ref,
                 qT_ref, kaug_ref, vsT_ref, kw_ref, vwT_ref, gT_ref, *, tm, tpb):
    x = x_ref[...]
    h = _rms(x, g_ref[...]) * (1.0 + sc_ref[0]) + sh_ref[0]
    hb = h.astype(BF)
    zn = _dot(hb, wn_ref[...])
    zt = _dot_nt(wt_ref[...], hb)
    kvcT_ref[0] = zt[512:768]
    kvsT_ref[0] = zt[768:1024]
    kvwT_ref[0] = zt[1024:1280]
    pad_rows = jnp.zeros((CMP_PITCH - BLOCK, KV_HALF), F32)
    for b in range(tm // BLOCK):
        for dst, c0 in ((ca_ref, 768), (cb_ref, 896)):
            dst[b * CMP_PITCH:b * CMP_PITCH + BLOCK, :] = zn[b * BLOCK:(b + 1) * BLOCK, c0:c0 + KV_HALF]
            dst[b * CMP_PITCH + BLOCK:(b + 1) * CMP_PITCH, :] = pad_rows
    u_ref[...] = zn[:, 256:768]
    qT_ref[0] = (zt[0:512] * (HEAD_DIM ** -0.5 * LOG2E)).astype(BF)
    ones = jnp.ones((V_ROWS - HEAD_DIM, tm), BF)
    for h in range(N_KV_HEADS):
        vs_h = zt[896 + h * HEAD_DIM:896 + (h + 1) * HEAD_DIM].astype(BF)
        vw_h = zt[1152 + h * HEAD_DIM:1152 + (h + 1) * HEAD_DIM].astype(BF)
        vsT_ref[0, 0, h, 0:HEAD_DIM, :] = vs_h
        vsT_ref[0, 0, h, HEAD_DIM:V_ROWS, :] = ones
        for c in range(tm // WIN_CHUNK):
            cs = slice(c * WIN_CHUNK, (c + 1) * WIN_CHUNK)
            vwT_ref[0, c, h, 0:HEAD_DIM, :] = vw_h[:, cs]
            vwT_ref[0, c, h, HEAD_DIM:V_ROWS, :] = ones[:, cs]
    gT_ref[0] = jax.nn.sigmoid(zt[1280:1312])
    t0 = (pl.program_id(0) % tpb) * tm
    blk = jnp.right_shift(t0 + lax.broadcasted_iota(I32, (tm, 128), 0), 6)
    lane = lax.broadcasted_iota(I32, (tm, 128), 1)
    kaug_ref[0, :, 0:128] = zn[:, 0:128].astype(BF)
    kaug_ref[0, :, 128:256] = jnp.where(blk == lane, MASK_BIG, 0.0).astype(BF)
    kw_ref[0] = zn[:, 128:256].astype(BF)


def _proj_cmp_kernel(pt_ref, *refs, n_pages_step, tm, tpb):
    proj_in = refs[:6]
    pages = refs[6:6 + n_pages_step]
    pos_ref, w_ref = refs[6 + n_pages_step:8 + n_pages_step]
    proj_out = refs[8 + n_pages_step:20 + n_pages_step]
    o_ref, sa_ref, sb_ref = refs[20 + n_pages_step:]
    _proj_kernel(*proj_in, *proj_out, tm=tm, tpb=tpb)
    blocks_per_page = PAGE_SIZE // BLOCK
    for k in range(n_pages_step):
        pg = pages[k][0]
        for s, dst in enumerate((sa_ref, sb_ref)):
            rows = pg[s * KV_HALF:(s + 1) * KV_HALF, :].T
            for b in range(blocks_per_page):
                m = k * blocks_per_page + b
                dst[m * CMP_PITCH:m * CMP_PITCH + BLOCK, :] = rows[b * BLOCK:(b + 1) * BLOCK, :]
    _compress_rows((sa_ref, sb_ref), pos_ref, w_ref, o_ref, n_pages_step * blocks_per_page, CMP_PITCH)


def _project_prompt_compress_cache(x2, shift, scale, g, wn, wt, n_batch, seq,
                                   cache_t, page_table_flat, pos4, w4):
    tm = PROJ_TILE
    tpb = seq // tm
    nt = n_batch * seq
    n_steps = nt // tm
    blocks_per_page = PAGE_SIZE // BLOCK
    n_pages_all = page_table_flat.shape[0]
    n_pages_step = n_pages_all // n_steps
    assert n_pages_step * n_steps == n_pages_all
    blocks_step = n_pages_step * blocks_per_page
    chunk_steps = SEL_CHUNK // tm
    tok = lambda t, pt: (t, 0)
    per_b = lambda t, pt: (t // tpb, 0, 0)
    featT = lambda t, pt: (t // tpb, 0, t % tpb)
    rows3 = lambda t, pt: (t // tpb, t % tpb, 0)
    rows5 = lambda t, pt: (t // tpb, t % tpb, 0, 0, 0)
    chunk5 = lambda t, pt: (t // tpb, (t % tpb) // chunk_steps, 0, 0, t % chunk_steps)
    const2 = lambda t, pt: (0, 0)
    const4 = lambda t, pt: (0, 0, 0, 0)

    def page_map(k):
        return lambda t, pt: (pt[t * n_pages_step + k], 0, 0)

    kvT = jax.ShapeDtypeStruct((n_batch, KV_WIDTH, seq), F32)
    out_shape = (
        kvT, kvT, kvT,
        jax.ShapeDtypeStruct((nt // BLOCK * CMP_PITCH, KV_HALF), F32),
        jax.ShapeDtypeStruct((nt // BLOCK * CMP_PITCH, KV_HALF), F32),
        jax.ShapeDtypeStruct((nt, POOL_WIDTH), F32),
        jax.ShapeDtypeStruct((n_batch, NSA_WIDTH, seq), BF),
        jax.ShapeDtypeStruct((n_batch, seq, 256), BF),
        jax.ShapeDtypeStruct((n_batch, seq // SEL_CHUNK, N_KV_HEADS, V_ROWS, SEL_CHUNK), BF),
        jax.ShapeDtypeStruct((n_batch, seq, KV_HALF), BF),
        jax.ShapeDtypeStruct((n_batch, seq // WIN_CHUNK, N_KV_HEADS, V_ROWS, WIN_CHUNK), BF),
        jax.ShapeDtypeStruct((n_batch, 32, seq), F32),
        jax.ShapeDtypeStruct((n_pages_all * blocks_per_page, KV_WIDTH), F32),
    )
    out_specs = (
        pl.BlockSpec((1, KV_WIDTH, tm), featT),
        pl.BlockSpec((1, KV_WIDTH, tm), featT),
        pl.BlockSpec((1, KV_WIDTH, tm), featT),
        pl.BlockSpec((tm // BLOCK * CMP_PITCH, KV_HALF), tok),
        pl.BlockSpec((tm // BLOCK * CMP_PITCH, KV_HALF), tok),
        pl.BlockSpec((tm, POOL_WIDTH), tok),
        pl.BlockSpec((1, NSA_WIDTH, tm), featT),
        pl.BlockSpec((1, tm, 256), rows3),
        pl.BlockSpec((1, 1, N_KV_HEADS, V_ROWS, tm), chunk5),
        pl.BlockSpec((1, tm, KV_HALF), rows3),
        pl.BlockSpec((1, tm // WIN_CHUNK, N_KV_HEADS, V_ROWS, WIN_CHUNK), rows5),
        pl.BlockSpec((1, 32, tm), featT),
        pl.BlockSpec((blocks_step, KV_WIDTH), tok),
    )
    grid_spec = pltpu.PrefetchScalarGridSpec(
        num_scalar_prefetch=1,
        grid=(n_steps,),
        in_specs=[pl.BlockSpec((tm, D_MODEL), tok),
                  pl.BlockSpec((1, 1, D_MODEL), per_b),
                  pl.BlockSpec((1, 1, D_MODEL), per_b),
                  pl.BlockSpec((1, D_MODEL), const2),
                  pl.BlockSpec(wn.shape, const2),
                  pl.BlockSpec(wt.shape, const2)]
        + [pl.BlockSpec((1, KV_WIDTH, PAGE_SIZE), page_map(k)) for k in range(n_pages_step)]
        + [pl.BlockSpec(pos4.shape, const4), pl.BlockSpec(w4.shape, const4)],
        out_specs=out_specs,
        scratch_shapes=[pltpu.VMEM((blocks_step * CMP_PITCH, KV_HALF), F32),
                        pltpu.VMEM((blocks_step * CMP_PITCH, KV_HALF), F32)],
    )
    return pl.pallas_call(
        functools.partial(_proj_cmp_kernel, n_pages_step=n_pages_step, tm=tm, tpb=tpb),
        out_shape=out_shape,
        grid_spec=grid_spec,
        compiler_params=_cparams("arbitrary"),
        name="project_prompt_compress_cache",
    )(page_table_flat, x2, shift, scale, g, wn, wt, *([cache_t] * n_pages_step), pos4, w4)


def _compress_rows(src_refs, pos_ref, w_ref, o_ref, n_blocks, pitch):
    rows_per_slice = BLOCK // CMP_SPLIT
    for s in range(2):
        acc = jnp.zeros((n_blocks, KV_HALF), F32)
        for c in range(CMP_SPLIT):
            xc = jnp.concatenate(
                [src_refs[s][pl.ds(c * rows_per_slice + r, n_blocks, stride=pitch), :]
                 for r in range(rows_per_slice)], axis=1) + pos_ref[s, c]
            acc = acc + _dot(xc.astype(BF), w_ref[s, c])
        o_ref[:, s * KV_HALF:(s + 1) * KV_HALF] = acc


def _cmp_kernel(xa_ref, xb_ref, pos_ref, w_ref, o_ref):
    _compress_rows((xa_ref, xb_ref), pos_ref, w_ref, o_ref, CMP_GROUP, CMP_PITCH)


def _compress(xa, xb, pos4, w4):
    m = xa.shape[0] // CMP_PITCH
    rows = CMP_GROUP * CMP_PITCH
    return pl.pallas_call(
        _cmp_kernel,
        out_shape=jax.ShapeDtypeStruct((m, KV_WIDTH), F32),
        grid=(m // CMP_GROUP,),
        in_specs=[pl.BlockSpec((rows, KV_HALF), lambda i: (i, 0)),
                  pl.BlockSpec((rows, KV_HALF), lambda i: (i, 0)),
                  pl.BlockSpec(pos4.shape, lambda i: (0, 0, 0, 0)),
                  pl.BlockSpec(w4.shape, lambda i: (0, 0, 0, 0))],
        out_specs=pl.BlockSpec((CMP_GROUP, KV_WIDTH), lambda i: (i, 0)),
        compiler_params=_cparams("arbitrary"),
        name="compress_blocks",
    )(xa, xb, pos4, w4)


def _attn_kernel(qT_ref, kaug_ref, vsT_ref, kw_ref, vwT_ref, kc_ref, vcT_ref, gT_ref, o_ref,
                 qa_ref, sa_ref, sb_ref, sc_ref, swb_ref, ma_ref, mb_ref, mc_ref, m_ref, acc_ref, outT_ref):
    i = pl.program_id(1)
    q0 = i * Q_TILE
    n_full = lax.div(q0, SEL_CHUNK)
    n_blk = kc_ref.shape[1]
    row = lax.broadcasted_iota(I32, (n_blk, Q_TILE), 0)
    tok = lax.broadcasted_iota(I32, (n_blk, Q_TILE), 1)
    qpos = q0 + tok
    cur = jnp.right_shift(qpos, 6)
    cmp_bias = jnp.where((row + 1) * BLOCK - 1 <= qpos, 0.0, -MASK_BIG)
    valid = row <= cur
    forced = (row == 0) | (row == cur) | (row == cur - 1)
    key_c = lax.broadcasted_iota(I32, (SEL_CHUNK, Q_TILE), 0)
    qpos_c = q0 + lax.broadcasted_iota(I32, (SEL_CHUNK, Q_TILE), 1)
    hs = [slice(h * HEAD_DIM, (h + 1) * HEAD_DIM) for h in range(N_KV_HEADS)]
    tri_bias = jnp.where(lax.broadcasted_iota(I32, (128, 128), 0) <= lax.broadcasted_iota(I32, (128, 128), 1),
                         0.0, -MASK_BIG)

    def bias4(keep):
        b = jnp.where(keep, 0.0, -MASK_BIG)
        return jnp.concatenate([b] * GROUP, axis=1)

    def online_step(st, s, v_t, s_max=None):
        m_old = m_ref[st]
        m_new = jnp.maximum(m_old, jnp.max(s, axis=0, keepdims=True) if s_max is None else s_max)
        alpha = jnp.exp2(m_old - m_new)
        p = jnp.exp2(s - m_new)
        acc_ref[st] = alpha * acc_ref[st] + _dot(v_t, p.astype(BF))
        m_ref[st] = m_new

    m_ref[...] = jnp.full(m_ref.shape, NEG_INF, F32)
    acc_ref[...] = jnp.zeros(acc_ref.shape, F32)

    for h in range(N_KV_HEADS):
        qa_ref[h] = jnp.zeros(qa_ref.shape[1:], BF)
        for g in range(GROUP):
            r0 = h * GROUP * HEAD_DIM + g * HEAD_DIM
            qa_ref[h, hs[h], g * Q_TILE:(g + 1) * Q_TILE] = qT_ref[0, r0:r0 + HEAD_DIM, :]

    w0 = jnp.maximum(q0 - WINDOW, 0)
    kw_a = kw_ref[0, pl.ds(pl.multiple_of(w0, WIN_CHUNK), WINDOW), :]
    kw_b = kw_ref[0, pl.ds(pl.multiple_of(q0, Q_TILE), Q_TILE), :]
    for h in range(N_KV_HEADS):
        sc_ref[h] = _dot(kw_a, qa_ref[h, 0:KV_HALF, :])
        swb_ref[h] = _dot(kw_b, qa_ref[h, 0:KV_HALF, :])

    o_c = []
    scores = []
    for h in range(N_KV_HEADS):
        sc = _dot(kc_ref[0], qa_ref[h, 0:KV_HALF, :])
        imp = jnp.zeros((n_blk, Q_TILE), F32)
        p_parts = []
        for g in range(GROUP):
            s = sc[:, g * Q_TILE:(g + 1) * Q_TILE] + cmp_bias
            mx = jnp.max(s, axis=0, keepdims=True)
            e = jnp.exp2(s - mx)
            inv = jnp.where(mx > -0.5 * MASK_BIG, 1.0 / jnp.sum(e, axis=0, keepdims=True), 0.0)
            p = e * inv
            imp = imp + p
            p_parts.append(p)
        o_c.append(_dot(vcT_ref[0, hs[h], :], jnp.concatenate(p_parts, axis=1).astype(BF)))
        scores.append(jnp.where(valid, jnp.where(forced, -2.0, imp), -1.0))

    blk_f = row.astype(F32)

    def pick(work):
        best = jnp.max(work, axis=0, keepdims=True)
        first = jnp.min(jnp.where(work == best, blk_f, float(n_blk)), axis=0, keepdims=True)
        return jnp.where((blk_f == first) & (best >= 0.0), -2.0, work)

    for _ in range(N_SELECT - N_FORCED):
        scores = [pick(w) for w in scores]
    for h in range(N_KV_HEADS):
        selm1 = jnp.where(scores[h] == -2.0, 0.0, -1.0).astype(BF)
        for g in range(GROUP):
            qa_ref[h, KV_HALF:KV_HALF + n_blk, g * Q_TILE:(g + 1) * Q_TILE] = selm1

    last_chunk = kaug_ref.shape[1] // SEL_CHUNK - 1

    def sel_scores(j, buf):
        dst_ref, max_ref = buf
        j = jnp.minimum(j, last_chunk)
        kt = kaug_ref[0, pl.ds(pl.multiple_of(j * SEL_CHUNK, SEL_CHUNK), SEL_CHUNK), :]
        for h in range(N_KV_HEADS):
            s = _dot(kt, qa_ref[h])
            dst_ref[h] = s
            max_ref[h] = jnp.max(s, axis=0, keepdims=True)

    def sel_process(buf, j, causal):
        src_ref, max_ref = buf
        for h in range(N_KV_HEADS):
            if causal:
                r0 = q0 - j * SEL_CHUNK
                for p in range(Q_TILE // 128):
                    rows = pl.ds(pl.multiple_of(r0 + p * 128, 128), 128)
                    for g in range(GROUP):
                        lanes = slice(g * Q_TILE + p * 128, g * Q_TILE + (p + 1) * 128)
                        src_ref[h, rows, lanes] = src_ref[h, rows, lanes] + tri_bias
                online_step(h, src_ref[h], vsT_ref[0, j, h])
            else:
                online_step(h, src_ref[h], vsT_ref[0, j, h], max_ref[h])

    buf_a, buf_b, buf_c = (sa_ref, ma_ref), (sb_ref, mb_ref), (sc_ref, mc_ref)
    sel_scores(0, buf_a)
    sel_scores(1, buf_b)

    j0 = lax.div(w0, WIN_CHUNK)
    jq = lax.div(q0, WIN_CHUNK)
    delta = qpos_c - (w0 + key_c)
    wbias = bias4((delta >= 0) & (delta < WINDOW))
    key_d = lax.broadcasted_iota(I32, (Q_TILE, Q_TILE), 0)
    tok_d = lax.broadcasted_iota(I32, (Q_TILE, Q_TILE), 1)
    dbias = bias4((key_d <= tok_d) & (q0 >= WINDOW))
    for h in range(N_KV_HEADS):
        v_a = jnp.concatenate([vwT_ref[0, j0 + c, h] for c in range(WINDOW // WIN_CHUNK)], axis=1)
        v_b = jnp.concatenate([vwT_ref[0, jq + c, h] for c in range(Q_TILE // WIN_CHUNK)], axis=1)
        online_step(2 + h, sc_ref[h] + wbias, v_a)
        online_step(2 + h, swb_ref[h] + dbias, v_b)

    def trio_body(t, carry):
        j = 3 * t
        sel_scores(j + 2, buf_c)
        sel_process(buf_a, j, False)
        sel_scores(j + 3, buf_a)
        sel_process(buf_b, j + 1, False)
        sel_scores(j + 4, buf_b)
        sel_process(buf_c, j + 2, False)
        return carry

    n_trios = lax.div(n_full, 3)
    lax.fori_loop(0, n_trios, trio_body, 0)
    j_last = 3 * n_trios
    n_left = n_full - j_last

    @pl.when(n_left == 0)
    def _():
        sel_process(buf_a, j_last, True)

    @pl.when(n_left == 1)
    def _():
        sel_process(buf_a, j_last, False)
        sel_process(buf_b, j_last + 1, True)

    @pl.when(n_left == 2)
    def _():
        sel_scores(j_last + 2, buf_c)
        sel_process(buf_a, j_last, False)
        sel_process(buf_b, j_last + 1, False)
        sel_process(buf_c, j_last + 2, True)

    for h in range(N_KV_HEADS):
        o_s = acc_ref[h, 0:HEAD_DIM, :] * (1.0 / acc_ref[h, HEAD_DIM:HEAD_DIM + 1, :])
        o_w = acc_ref[2 + h, 0:HEAD_DIM, :] * (1.0 / acc_ref[2 + h, HEAD_DIM:HEAD_DIM + 1, :])
        for g in range(GROUP):
            gs = slice(g * Q_TILE, (g + 1) * Q_TILE)
            gr = h * GROUP * 3 + g * 3
            og = (gT_ref[0, gr:gr + 1, :] * o_c[h][:, gs] + gT_ref[0, gr + 1:gr + 2, :] * o_s[:, gs]
                  + gT_ref[0, gr + 2:gr + 3, :] * o_w[:, gs])
            r0 = h * GROUP * HEAD_DIM + g * HEAD_DIM
            outT_ref[r0:r0 + HEAD_DIM, :] = og

    o_ref[0] = outT_ref[...].T.astype(BF)


def _attention_prompt(qT, kaug, vsT, kw, vwT, kc, vcT, gT, n_batch, seq):
    per_b3 = lambda n, i: (n, 0, 0)
    per_b5 = lambda n, i: (n, 0, 0, 0, 0)
    rows = GROUP * Q_TILE
    return pl.pallas_call(
        _attn_kernel,
        out_shape=jax.ShapeDtypeStruct((n_batch, seq, NSA_WIDTH), BF),
        grid=(n_batch, seq // Q_TILE),
        in_specs=[pl.BlockSpec((1, NSA_WIDTH, Q_TILE), lambda n, i: (n, 0, i)),
                  pl.BlockSpec((1, seq, 256), per_b3),
                  pl.BlockSpec((1, seq // SEL_CHUNK, N_KV_HEADS, V_ROWS, SEL_CHUNK), per_b5),
                  pl.BlockSpec((1, seq, KV_HALF), per_b3),
                  pl.BlockSpec((1, seq // WIN_CHUNK, N_KV_HEADS, V_ROWS, WIN_CHUNK), per_b5),
                  pl.BlockSpec((1, seq // BLOCK, KV_HALF), per_b3),
                  pl.BlockSpec((1, KV_HALF, seq // BLOCK), per_b3),
                  pl.BlockSpec((1, 32, Q_TILE), lambda n, i: (n, 0, i))],
        out_specs=pl.BlockSpec((1, Q_TILE, NSA_WIDTH), lambda n, i: (n, i, 0)),
        scratch_shapes=[pltpu.VMEM((N_KV_HEADS, 256, rows), BF),
                        pltpu.VMEM((N_KV_HEADS, SEL_CHUNK, rows), F32),
                        pltpu.VMEM((N_KV_HEADS, SEL_CHUNK, rows), F32),
                        pltpu.VMEM((N_KV_HEADS, SEL_CHUNK, rows), F32),
                        pltpu.VMEM((N_KV_HEADS, Q_TILE, rows), F32),
                        pltpu.VMEM((N_KV_HEADS, 1, rows), F32),
                        pltpu.VMEM((N_KV_HEADS, 1, rows), F32),
                        pltpu.VMEM((N_KV_HEADS, 1, rows), F32),
                        pltpu.VMEM((2 * N_KV_HEADS, 1, rows), F32),
                        pltpu.VMEM((2 * N_KV_HEADS, V_ROWS, rows), F32),
                        pltpu.VMEM((NSA_WIDTH, Q_TILE), F32)],
        compiler_params=_cparams("arbitrary", "arbitrary"),
        name="attention_prompt",
    )(qT, kaug, vsT, kw, vwT, kc, vcT, gT)


def _pool_kernel(u_ref, halo_ref, wp_ref, ps_ref, o_ref, ext_ref, lvl_ref, *, tm, tpb):
    t = pl.program_id(0) % tpb
    ext_ref[0:16, :] = jnp.where(t == 0, 0.0, halo_ref[...])
    u = u_ref[...]
    ext_ref[16:16 + tm, :] = u
    pos = t * tm + lax.broadcasted_iota(I32, (tm, 1), 0)
    end = 16 + tm
    outs = []
    for gi, w in enumerate(POOL_WINDOWS):
        cs = slice(gi * POOL_GROUP_DIM, (gi + 1) * POOL_GROUP_DIM)
        lo = 16 - (w - 2)
        acc = ext_ref[lo:end, cs] + ext_ref[lo - 1:end - 1, cs]
        d = 2
        while d < w:
            lvl_ref[lo:end, :] = acc
            lo += d
            acc = lvl_ref[lo:end, :] + lvl_ref[lo - d:end - d, :]
            d *= 2
        cnt = jnp.minimum(pos + 1, w).astype(F32)
        pooled = acc / cnt - u[:, cs]
        outs.append(_dot(pooled.astype(BF), wp_ref[gi]))
    o_ref[...] = (jnp.concatenate(outs, axis=1) * ps_ref[...]).astype(BF)


def _pool_prompt(u, w_pool, pool_scale, n_batch, seq):
    tm = TOKEN_TILE
    tpb = seq // tm
    nt = n_batch * seq
    return pl.pallas_call(
        functools.partial(_pool_kernel, tm=tm, tpb=tpb),
        out_shape=jax.ShapeDtypeStruct((nt, POOL_WIDTH), BF),
        grid=(nt // tm,),
        in_specs=[pl.BlockSpec((tm, POOL_WIDTH), lambda t: (t, 0)),
                  pl.BlockSpec((16, POOL_WIDTH), lambda t: (jnp.maximum(t * (tm // 16) - 1, 0), 0)),
                  pl.BlockSpec(w_pool.shape, lambda t: (0, 0, 0)),
                  pl.BlockSpec((1, POOL_WIDTH), lambda t: (0, 0))],
        out_specs=pl.BlockSpec((tm, POOL_WIDTH), lambda t: (t, 0)),
        scratch_shapes=[pltpu.VMEM((tm + 16, POOL_WIDTH), F32), pltpu.VMEM((tm + 16, POOL_GROUP_DIM), F32)],
        compiler_params=_cparams("arbitrary"),
        name="pool_prompt",
    )(u, u, w_pool, pool_scale)


def _spool_kernel(u_ref, hist_ref, wp_ref, ps_ref, o_ref):
    u = u_ref[...]
    outs = []
    for gi, w in enumerate(POOL_WINDOWS):
        cs = slice(gi * POOL_GROUP_DIM, (gi + 1) * POOL_GROUP_DIM)
        acc = u[:, cs]
        for k in range(1, w):
            acc = acc + hist_ref[POOL_HIST - k, :, cs]
        pooled = acc / float(w) - u[:, cs]
        outs.append(_dot(pooled.astype(BF), wp_ref[gi]))
    o_ref[...] = (jnp.concatenate(outs, axis=1) * ps_ref[...]).astype(BF)


def _pool_sample(u, hist_t, w_pool, pool_scale):
    n = u.shape[0]
    return pl.pallas_call(
        _spool_kernel,
        out_shape=jax.ShapeDtypeStruct((n, POOL_WIDTH), BF),
        name="pool_sample",
    )(u, hist_t, w_pool, pool_scale)


def _merge_kernel(x_ref, sh_ref, sc_ref, gate_ref, gpre_ref, gpost_ref, onsa_ref, opool_ref,
                  wgm_ref, wun_ref, wup_ref, wo_ref, o_ref):
    x = x_ref[...]
    h = _rms(x, gpre_ref[...]) * (1.0 + sc_ref[0]) + sh_ref[0]
    gm = jax.nn.sigmoid(_dot(h.astype(BF), wgm_ref[...]))
    m = (gm[:, :D_MODEL] * _dot(onsa_ref[...], wun_ref[...])
         + gm[:, D_MODEL:] * _dot(opool_ref[...], wup_ref[...]))
    m = _dot(m.astype(BF), wo_ref[...])
    o_ref[...] = x + gate_ref[0] * _rms(m, gpost_ref[...])


def _mlp_kernel(x_ref, sh_ref, sc_ref, gate_ref, gpre_ref, gpost_ref, w1_ref, w2_ref, o_ref):
    x = x_ref[...]
    h = _rms(x, gpre_ref[...]) * (1.0 + sc_ref[0]) + sh_ref[0]
    hb = h.astype(BF)
    f = jnp.zeros(x.shape, F32)
    fc = 1024
    for c in range(D_FF // fc):
        a = jnp.maximum(_dot(hb, w1_ref[:, c * fc:(c + 1) * fc]), 0.0)
        f = f + _dot((a * a).astype(BF), w2_ref[c * fc:(c + 1) * fc, :])
    o_ref[...] = x + gate_ref[0] * _rms(f, gpost_ref[...])


def _token_call(kernel, name, x2, mods, tm, rows_per_mod, extra_tok, consts):
    nt = x2.shape[0]
    r = mods[0].shape[1]
    mod_spec = pl.BlockSpec((1, r, D_MODEL), lambda t: ((t * tm) // rows_per_mod, 0, 0))
    in_specs = [pl.BlockSpec((tm, D_MODEL), lambda t: (t, 0))] + [mod_spec] * len(mods)
    in_specs += [pl.BlockSpec((1, D_MODEL), lambda t: (0, 0))] * 2
    in_specs += [pl.BlockSpec((tm, a.shape[1]), lambda t: (t, 0)) for a in extra_tok]
    in_specs += [pl.BlockSpec(w.shape, lambda t: (0, 0), pipeline_mode=pl.Buffered(1)) for w in consts[2:]]
    return pl.pallas_call(
        kernel,
        out_shape=jax.ShapeDtypeStruct((nt, D_MODEL), F32),
        grid=(nt // tm,),
        in_specs=in_specs,
        out_specs=pl.BlockSpec((tm, D_MODEL), lambda t: (t, 0)),
        compiler_params=_cparams("arbitrary"),
        name=name,
    )(x2, *mods, consts[0], consts[1], *extra_tok, *consts[2:])


def _sproj_kernel(x_ref, sh_ref, sc_ref, g_ref, wn_ref, wt_ref, z_ref, zs_ref, zT_ref):
    x = x_ref[...]
    h = _rms(x, g_ref[...]) * (1.0 + sc_ref[...]) + sh_ref[...]
    hb = h.astype(BF)
    z = _dot(hb, wn_ref[...])
    z_ref[...] = z
    zs_ref[...] = jax.nn.sigmoid(z)
    zT_ref[...] = _dot_nt(wt_ref[...], hb)


def _project_sample(x2, shift, scale, g, wn, wt):
    n = x2.shape[0]
    shp = jax.ShapeDtypeStruct((n, wn.shape[1]), F32)
    return pl.pallas_call(
        _sproj_kernel,
        out_shape=(shp, shp, jax.ShapeDtypeStruct((wt.shape[0], n), F32)),
        name="project_sample",
    )(x2, shift, scale, g, wn, wt)


def _sattn_init(kaug_ref, past_len, seq_step):
    @pl.when(pl.program_id(0) == 0)
    def _():
        blk = lax.broadcasted_iota(I32, (128, past_len), 0)
        key_blk = jnp.right_shift(lax.broadcasted_iota(I32, (128, past_len), 1), 6)
        onehot = jnp.where(blk == key_blk, MASK_BIG, 0.0).astype(BF)
        for q in range(seq_step):
            kaug_ref[q, 128:256, :] = onehot


def _sattn_main(*refs, n_pages, past_len, seq_step):
    pages = refs[:seq_step * n_pages]
    (qb_ref, kvc_ref, win_ref, ksn_ref, kwn_ref, kwnT_ref, g_ref,
     o_ref, nwin_ref, kaug_ref, vall_ref) = refs[seq_step * n_pages:]

    for q in range(seq_step):
        for p in range(n_pages):
            pg = pages[q * n_pages + p][0]
            kaug_ref[q, 0:128, p * PAGE_SIZE:(p + 1) * PAGE_SIZE] = pg[0:KV_HALF, :].astype(BF)
            vall_ref[q, :, p * PAGE_SIZE:(p + 1) * PAGE_SIZE] = pg[KV_HALF:KV_WIDTH, :].astype(BF)

    nb_past = past_len // BLOCK
    win_buf = win_ref.shape[2]
    n_rows = 8 * seq_step
    per_seq = lambda f: jnp.concatenate([f(q) for q in range(seq_step)], axis=0)
    rows_of = lambda x, q: x[8 * q:8 * (q + 1)]
    qb = qb_ref[...].reshape(n_rows, KV_HALF).astype(BF)
    qf = qb.astype(F32)
    row = lax.broadcasted_iota(I32, (n_rows, 128), 0)
    lane = lax.broadcasted_iota(I32, (n_rows, 128), 1)

    def new_key(ref):
        return per_seq(lambda q: jnp.broadcast_to(ref[q], (8, KV_WIDTH))).astype(BF).astype(F32)

    s_c = per_seq(lambda q: _dot_nt(rows_of(qb, q), kvc_ref[q, :, 0:128].astype(BF)))
    cm = lane < nb_past
    s_c = jnp.where(cm, s_c, NEG_INF)
    mx = jnp.max(s_c, axis=1, keepdims=True)
    mx = jnp.where(mx > NEG_INF, mx, 0.0)
    e = jnp.where(cm, jnp.exp(s_c - mx), 0.0)
    p_c = e / jnp.maximum(jnp.sum(e, axis=1, keepdims=True), 1e-30)
    o_c = per_seq(lambda q: _dot(rows_of(p_c, q).astype(BF), kvc_ref[q, :, 128:256].astype(BF)))

    imp = jnp.zeros((n_rows, 128), F32)
    for grp in range(n_rows // GROUP):
        in_grp = jnp.right_shift(row, 2) == grp
        imp = jnp.where(in_grp, jnp.sum(jnp.where(in_grp, p_c, 0.0), axis=0, keepdims=True), imp)
    cur = nb_past
    forced = (lane == 0) | (lane == cur) | (lane == cur - 1)
    score = jnp.where(lane <= cur, imp + FORCE_SCORE * forced.astype(F32), -1.0)
    cnt = jnp.zeros((n_rows, 128), F32)
    for bp in range(nb_past + 1):
        other = score[:, bp:bp + 1]
        ahead = (other > score) | ((other == score) & (bp < lane))
        cnt = cnt + jnp.where(ahead, 1.0, 0.0)
    sel = (cnt < float(N_SELECT)) & (score >= 0.0)
    selm1 = jnp.where(sel, 0.0, -1.0).astype(BF)

    qaug = jnp.concatenate([qb, selm1], axis=1)
    s_s = per_seq(lambda q: _dot(rows_of(qaug, q), kaug_ref[q]))
    kv_n = new_key(ksn_ref)
    s_n = jnp.sum(qf * kv_n[:, 0:128], axis=1, keepdims=True)
    m_s = jnp.maximum(jnp.max(s_s, axis=1, keepdims=True), s_n)
    e_s = jnp.exp(s_s - m_s)
    e_n = jnp.exp(s_n - m_s)
    l_s = jnp.sum(e_s, axis=1, keepdims=True) + e_n
    pv = per_seq(lambda q: _dot_nt(rows_of(e_s, q).astype(BF), vall_ref[q]))
    o_s = (pv + e_n.astype(BF).astype(F32) * kv_n[:, 128:256]) / l_s

    s_w = per_seq(lambda q: _dot(rows_of(qb, q), win_ref[q, 0:KV_HALF, :].astype(BF)))
    lane_w = lax.broadcasted_iota(I32, (n_rows, win_buf), 1)
    s_w = jnp.where(lane_w >= win_buf + 1 - WINDOW, s_w, NEG_INF)
    kv_n = new_key(kwn_ref)
    s_n = jnp.sum(qf * kv_n[:, 0:128], axis=1, keepdims=True)
    m_w = jnp.maximum(jnp.max(s_w, axis=1, keepdims=True), s_n)
    e_w = jnp.exp(s_w - m_w)
    e_n = jnp.exp(s_n - m_w)
    l_w = jnp.sum(e_w, axis=1, keepdims=True) + e_n
    pv = per_seq(lambda q: _dot_nt(rows_of(e_w, q).astype(BF), win_ref[q, KV_HALF:KV_WIDTH, :].astype(BF)))
    o_w = (pv + e_n.astype(BF).astype(F32) * kv_n[:, 128:256]) / l_w

    g = g_ref[...].reshape(n_rows, 3)
    o_ref[...] = (g[:, 0:1] * o_c + g[:, 1:2] * o_s + g[:, 2:3] * o_w).reshape(seq_step, 8, KV_HALF)

    seq_lane = lax.broadcasted_iota(I32, kwnT_ref.shape, 1)
    row_lane = lax.broadcasted_iota(I32, (KV_WIDTH, win_buf), 1)
    for q in range(seq_step):
        n = pl.program_id(0) * seq_step + q
        new_col = jnp.sum(jnp.where(seq_lane == n, kwnT_ref[...], 0.0), axis=1, keepdims=True)
        shifted = pltpu.roll(win_ref[q], win_buf - 1, axis=1)
        nwin_ref[q] = jnp.where(row_lane == win_buf - 1, new_col, shifted)


def _sattn_kernel(pt_ref, *refs, n_pages, past_len, seq_step):
    _sattn_init(refs[-2], past_len, seq_step)
    _sattn_main(*refs, n_pages=n_pages, past_len=past_len, seq_step=seq_step)


def _attention_sample(page_table_flat, cache_t, qblk, kvc_pad, win_t, kvs_new, kvw_new, kvw_new_t, gates8,
                      n_pages, past_len):
    n_seq = qblk.shape[0]
    win_buf = win_t.shape[2]
    g = SAMPLE_SEQ_STEP
    const2 = lambda t, pt: (0, 0)
    per_step = lambda t, pt: (t, 0, 0)

    def page_map(k):
        return lambda t, pt: (pt[t * g * n_pages + k], 0, 0)

    grid_spec = pltpu.PrefetchScalarGridSpec(
        num_scalar_prefetch=1,
        grid=(n_seq // g,),
        in_specs=[pl.BlockSpec((1, KV_WIDTH, PAGE_SIZE), page_map(k)) for k in range(g * n_pages)]
        + [pl.BlockSpec((g, 8, 128), per_step),
           pl.BlockSpec((g, 128, KV_WIDTH), per_step),
           pl.BlockSpec((g, KV_WIDTH, win_buf), per_step),
           pl.BlockSpec((g, 1, KV_WIDTH), per_step),
           pl.BlockSpec((g, 1, KV_WIDTH), per_step),
           pl.BlockSpec(kvw_new_t.shape, const2),
           pl.BlockSpec((g, 8, 3), per_step)],
        out_specs=(pl.BlockSpec((g, 8, 128), per_step),
                   pl.BlockSpec((g, KV_WIDTH, win_buf), per_step)),
        scratch_shapes=[pltpu.VMEM((g, 256, past_len), BF), pltpu.VMEM((g, KV_HALF, past_len), BF)],
    )
    return pl.pallas_call(
        functools.partial(_sattn_kernel, n_pages=n_pages, past_len=past_len, seq_step=g),
        out_shape=(jax.ShapeDtypeStruct((n_seq, 8, 128), F32),
                   jax.ShapeDtypeStruct((n_seq, KV_WIDTH, win_buf), F32)),
        grid_spec=grid_spec,
        compiler_params=_cparams("arbitrary"),
        name="attention_sample",
    )(page_table_flat, *([cache_t] * (g * n_pages)), qblk, kvc_pad, win_t, kvs_new, kvw_new, kvw_new_t, gates8)


def _kv_rows_view(kv_t):
    n, _, t = kv_t.shape
    return jnp.transpose(kv_t.reshape(n, 2, N_KV_HEADS, HEAD_DIM, t), (0, 4, 1, 2, 3))


def _kv_feat_view(kv):
    n, t = kv.shape[:2]
    return jnp.transpose(kv, (0, 2, 3, 4, 1)).reshape(n, KV_WIDTH, t)


def kernel(x_prompt, x_sample, cache_cmp_kv, cache_sel_kv, state_win_kv, state_pool, page_table, c_prompt, c_sample, w_ada, b_ada, g_pre_mix, g_post_mix, g_pre_mlp, g_post_mlp, w_in, w_cmp, pos_cmp, w_pool, pool_scale, w_up_nsa, w_up_pool, w_o, w_ff1, w_ff2):
    n_batch, seq, _ = x_prompt.shape
    n_seq = x_sample.shape[0]
    n_pages = page_table.shape[1]
    past_len = n_pages * PAGE_SIZE
    nb_past = past_len // BLOCK
    assert x_sample.shape[1] == 1 and w_ada.shape[0] == 1 and past_len % BLOCK == 0
    assert seq % TOKEN_TILE == 0 and seq // BLOCK == 128 and state_win_kv.shape[2] == WINDOW
    assert (n_seq * nb_past) % CMP_GROUP == 0 and n_seq == CMP_GROUP
    assert FORCE_SCORE > GROUP

    w_t = w_in[0].T
    wt = jnp.pad(w_t[0:1304], ((0, 8), (0, 0))).astype(BF)
    wn = jnp.concatenate([w_t[768:896], w_t[1024:1152], w_t[1304:1816], w_t[512:768]], axis=0).T.astype(BF)
    ws = jnp.pad(w_t[0:1816], ((0, 104), (0, 0))).T.astype(BF)
    wgm = w_t[1816:3864].T.astype(BF)
    eye = jnp.eye(N_KV_HEADS, dtype=F32)
    wc = w_cmp[0].astype(BF)
    wz = jnp.zeros_like(wc)
    w4 = jnp.concatenate([jnp.concatenate([wc, wz], axis=3), jnp.concatenate([wz, wc], axis=3)], axis=2)
    w4 = w4.reshape(2, CMP_SPLIT, CMP_K // CMP_SPLIT, KV_HALF)
    pos4 = jnp.broadcast_to(jnp.transpose(pos_cmp[0], (1, 0, 2))[:, :, None, :],
                            (2, BLOCK, N_KV_HEADS, HEAD_DIM)).reshape(2, CMP_SPLIT, 1, CMP_K // CMP_SPLIT)
    wp = w_pool[0].astype(BF)
    ps = pool_scale[0].reshape(1, POOL_WIDTH)
    wun, wup, wo = w_up_nsa[0].astype(BF), w_up_pool[0].astype(BF), w_o[0].astype(BF)
    w1, w2 = w_ff1[0].astype(BF), w_ff2[0].astype(BF)
    gpm, gqm = g_pre_mix[0].reshape(1, D_MODEL), g_post_mix[0].reshape(1, D_MODEL)
    gpf, gqf = g_pre_mlp[0].reshape(1, D_MODEL), g_post_mlp[0].reshape(1, D_MODEL)

    n_c = n_batch + n_seq
    c_all = jnp.pad(jnp.concatenate([c_prompt, c_sample], axis=0), ((0, (-n_c) % 8), (0, 0)))
    ada = _adaln(c_all, w_ada[0], b_ada[0].reshape(1, -1))
    ada_p = ada[:n_batch].reshape(n_batch, 6, 1, D_MODEL)
    ada_s = ada[n_batch:n_c].reshape(n_seq, 6, D_MODEL)
    mods_p = [ada_p[:, k] for k in range(6)]
    mods_s = [ada_s[:, k][None] for k in range(6)]

    xp = x_prompt.reshape(n_batch * seq, D_MODEL)
    pt_flat = page_table.reshape(-1)
    (kvcT, kvsT, kvwT, kvc_a, kvc_b, u_p, qT, kaug, vsT, kw, vwT, gT,
     kvc_past) = _project_prompt_compress_cache(xp, mods_p[0], mods_p[1], gpm, wn, wt, n_batch, seq,
                                                _kv_feat_view(cache_cmp_kv[0]), pt_flat, pos4, w4)
    kvc_blk = _compress(kvc_a, kvc_b, pos4, w4).reshape(n_batch, seq // BLOCK, KV_WIDTH)
    kc = kvc_blk[:, :, 0:128].astype(BF)
    vcT = jnp.swapaxes(kvc_blk[:, :, 128:256], 1, 2).astype(BF)
    onsa_p = _attention_prompt(qT, kaug, vsT, kw, vwT, kc, vcT, gT, n_batch, seq)
    opool_p = _pool_prompt(u_p, wp, ps, n_batch, seq)
    x1_p = _token_call(_merge_kernel, "merge_prompt", xp, [mods_p[0], mods_p[1], mods_p[2]], TOKEN_TILE, seq,
                       [onsa_p.reshape(n_batch * seq, NSA_WIDTH), opool_p], [gpm, gqm, wgm, wun, wup, wo])
    y_p = _token_call(_mlp_kernel, "mlp_prompt", x1_p, [mods_p[3], mods_p[4], mods_p[5]], TOKEN_TILE, seq,
                      [], [gpf, gqf, w1, w2])

    xs = x_sample.reshape(n_seq, D_MODEL)
    z, zsig, zT = _project_sample(xs, mods_s[0][0], mods_s[1][0], gpm, ws, wt)
    q_s = z[:, 0:512] * (HEAD_DIM ** -0.5)
    kvc_n, kvs_n, kvw_n = z[:, 512:768], z[:, 768:1024], z[:, 1024:1280]
    gates_s = zsig[:, 1280:1304].reshape(n_seq, N_HEADS, 3)
    u_s = z[:, 1304:1816]
    q5 = q_s.reshape(n_seq, N_KV_HEADS, GROUP, 1, HEAD_DIM) * eye[None, :, None, :, None]
    qblk = q5.reshape(n_seq, N_HEADS, KV_HALF)

    last_a = jnp.pad(kvc_n[:, None, 0:128], ((0, 0), (0, CMP_PITCH - 1), (0, 0))).reshape(-1, KV_HALF)
    last_b = jnp.pad(kvc_n[:, None, 128:256], ((0, 0), (0, CMP_PITCH - 1), (0, 0))).reshape(-1, KV_HALF)
    kvc_last = _compress(last_a, last_b, pos4, w4)
    kvc_s = jnp.concatenate([kvc_past.reshape(n_seq, nb_past, KV_WIDTH), kvc_last[:, None, :]], axis=1)
    kvc_pad = jnp.pad(kvc_s, ((0, 0), (0, 128 - nb_past - 1), (0, 0)))
    o8, new_win_t = _attention_sample(
        pt_flat, _kv_feat_view(cache_sel_kv[0]), qblk, kvc_pad, _kv_feat_view(state_win_kv[0]),
        kvs_n[:, None, :], kvw_n[:, None, :], zT[1024:1280], gates_s, n_pages, past_len)
    o5 = o8.reshape(n_seq, N_KV_HEADS, GROUP, N_KV_HEADS, HEAD_DIM)
    onsa_s = jnp.concatenate([o5[:, 0, :, 0, :], o5[:, 1, :, 1, :]], axis=1).reshape(n_seq, NSA_WIDTH).astype(BF)
    opool_s = _pool_sample(u_s, jnp.swapaxes(state_pool[0], 0, 1), wp, ps)
    x1_s = _token_call(_merge_kernel, "merge_sample", xs, [mods_s[0], mods_s[1], mods_s[2]], n_seq, n_seq,
                       [onsa_s, opool_s], [gpm, gqm, wgm, wun, wup, wo])
    y_s = _token_call(_mlp_kernel, "mlp_sample", x1_s, [mods_s[3], mods_s[4], mods_s[5]], n_seq, n_seq,
                      [], [gpf, gqf, w1, w2])

    win_p = min(WINDOW, seq)
    new_kv_s = lambda rows: _kv_rows_view(rows.reshape(1, KV_WIDTH, n_seq))[0][None, :, None]
    return (
        y_p.reshape(n_batch, seq, D_MODEL),
        y_s.reshape(n_seq, 1, D_MODEL),
        _kv_rows_view(kvcT)[None],
        _kv_rows_view(kvsT)[None],
        _kv_rows_view(kvwT[:, :, seq - win_p:])[None],
        u_p.reshape(n_batch, seq, POOL_WIDTH)[None, :, seq - POOL_HIST:],
        new_kv_s(zT[512:768]),
        new_kv_s(zT[768:1024]),
        _kv_rows_view(new_win_t)[None],
        jnp.concatenate([state_pool[0][:, 1:], u_s[:, None, :]], axis=1)[None],
    )
```

```python
import functools
import math

import jax
import jax.numpy as jnp
from jax import lax
from jax.experimental import pallas as pl
from jax.experimental.pallas import tpu as pltpu

D_MODEL = 1024
N_HEADS = 8
HEAD_DIM = 64
N_KV_HEADS = 2
GROUP = N_HEADS // N_KV_HEADS
BLOCK = 64
N_SELECT = 16
WINDOW = 512
Q_TILE = 256
WIN_CHUNK = 128
NSA_WIDTH = N_HEADS * HEAD_DIM
KV_WIDTH = 2 * N_KV_HEADS * HEAD_DIM
KV_HALF = N_KV_HEADS * HEAD_DIM
FORCE_SCORE = 16.0
N_FORCED = 3
POOL_WINDOWS = (2, 4, 8, 16)
POOL_WIDTH = 512
POOL_GROUP_DIM = 128
POOL_HIST = 15
D_FF = 4 * D_MODEL
EPS = 1e-6
PAGE_SIZE = 128
V_ROWS = HEAD_DIM + 16
CMP_K = BLOCK * KV_HALF
CMP_SPLIT = 4
CMP_GROUP = 128
CMP_PITCH = BLOCK + 4

BF = jnp.bfloat16
F32 = jnp.float32
I32 = jnp.int32
MASK_BIG = 2.0 ** 100
NEG_INF = float("-inf")
LOG2E = math.log2(math.e)

TOKEN_TILE = 512
PROJ_TILE = 256
SEL_CHUNK = 512
SAMPLE_SEQ_STEP = 4
VMEM_LIMIT = 56 * 1024 * 1024


def _cparams(*sem):
    return pltpu.CompilerParams(dimension_semantics=sem, vmem_limit_bytes=VMEM_LIMIT)


def _rms(x, g):
    return x * lax.rsqrt(jnp.mean(x * x, axis=-1, keepdims=True) + EPS) * g


def _dot(a, b):
    return jnp.dot(a, b, preferred_element_type=F32)


def _dot_nt(a, b):
    return lax.dot_general(a, b, (((1,), (1,)), ((), ())), preferred_element_type=F32)


def _ada_kernel(c_ref, w_ref, b_ref, o_ref):
    c = c_ref[...]
    a = (c * jax.nn.sigmoid(c)).astype(BF)
    o_ref[...] = _dot(a, w_ref[...].astype(BF)) + b_ref[...]


def _adaln(c_all, w_ada, b_ada):
    rows = c_all.shape[0]
    n_out = w_ada.shape[1]
    tn = 1024
    return pl.pallas_call(
        _ada_kernel,
        out_shape=jax.ShapeDtypeStruct((rows, n_out), F32),
        grid=(n_out // tn,),
        in_specs=[pl.BlockSpec((rows, D_MODEL), lambda j: (0, 0)),
                  pl.BlockSpec((D_MODEL, tn), lambda j: (0, j)),
                  pl.BlockSpec((1, tn), lambda j: (0, j))],
        out_specs=pl.BlockSpec((rows, tn), lambda j: (0, j)),
        compiler_params=_cparams("arbitrary"),
        name="adaln",
    )(c_all, w_ada, b_ada)


def _proj_kernel(x_ref, sh_ref, sc_ref, g_ref, wn_ref, wt_ref,
                 kvcT_ref, kvsT_ref, kvwT_ref, ca_ref, cb_ref, u_ref,
                 qT_ref, kaug_ref, vsT_ref, kw_ref, vwT_ref, gT_ref, *, tm, tpb):
    x = x_ref[...]
    h = _rms(x, g_ref[...]) * (1.0 + sc_ref[0]) + sh_ref[0]
    hb = h.astype(BF)
    zn = _dot(hb, wn_ref[...])
    zt = _dot_nt(wt_ref[...], hb)
    kvcT_ref[0] = zt[512:768]
    kvsT_ref[0] = zt[768:1024]
    kvwT_ref[0] = zt[1024:1280]
    pad_rows = jnp.zeros((CMP_PITCH - BLOCK, KV_HALF), F32)
    for b in range(tm // BLOCK):
        for dst, c0 in ((ca_ref, 768), (cb_ref, 896)):
            dst[b * CMP_PITCH:b * CMP_PITCH + BLOCK, :] = zn[b * BLOCK:(b + 1) * BLOCK, c0:c0 + KV_HALF]
            dst[b * CMP_PITCH + BLOCK:(b + 1) * CMP_PITCH, :] = pad_rows
    u_ref[...] = zn[:, 256:768]
    qT_ref[0] = (zt[0:512] * (HEAD_DIM ** -0.5 * LOG2E)).astype(BF)
    ones = jnp.ones((V_ROWS - HEAD_DIM, tm), BF)
    for h in range(N_KV_HEADS):
        vs_h = zt[896 + h * HEAD_DIM:896 + (h + 1) * HEAD_DIM].astype(BF)
        vw_h = zt[1152 + h * HEAD_DIM:1152 + (h + 1) * HEAD_DIM].astype(BF)
        vsT_ref[0, 0, h, 0:HEAD_DIM, :] = vs_h
        vsT_ref[0, 0, h, HEAD_DIM:V_ROWS, :] = ones
        for c in range(tm // WIN_CHUNK):
            cs = slice(c * WIN_CHUNK, (c + 1) * WIN_CHUNK)
            vwT_ref[0, c, h, 0:HEAD_DIM, :] = vw_h[:, cs]
            vwT_ref[0, c, h, HEAD_DIM:V_ROWS, :] = ones[:, cs]
    gT_ref[0] = jax.nn.sigmoid(zt[1280:1312])
    t0 = (pl.program_id(0) % tpb) * tm
    blk = jnp.right_shift(t0 + lax.broadcasted_iota(I32, (tm, 128), 0), 6)
    lane = lax.broadcasted_iota(I32, (tm, 128), 1)
    kaug_ref[0, :, 0:128] = zn[:, 0:128].astype(BF)
    kaug_ref[0, :, 128:256] = jnp.where(blk == lane, MASK_BIG, 0.0).astype(BF)
    kw_ref[0] = zn[:, 128:256].astype(BF)


def _proj_cmp_kernel(pt_ref, *refs, n_pages_step, tm, tpb):
    proj_in = refs[:6]
    pages = refs[6:6 + n_pages_step]
    pos_ref, w_ref = refs[6 + n_pages_step:8 + n_pages_step]
    proj_out = refs[8 + n_pages_step:20 + n_pages_step]
    o_ref, sa_ref, sb_ref = refs[20 + n_pages_step:]
    _proj_kernel(*proj_in, *proj_out, tm=tm, tpb=tpb)
    blocks_per_page = PAGE_SIZE // BLOCK
    for k in range(n_pages_step):
        pg = pages[k][0]
        for s, dst in enumerate((sa_ref, sb_ref)):
            rows = pg[s * KV_HALF:(s + 1) * KV_HALF, :].T
            for b in range(blocks_per_page):
                m = k * blocks_per_page + b
                dst[m * CMP_PITCH:m * CMP_PITCH + BLOCK, :] = rows[b * BLOCK:(b + 1) * BLOCK, :]
    _compress_rows((sa_ref, sb_ref), pos_ref, w_ref, o_ref, n_pages_step * blocks_per_page, CMP_PITCH)


def _project_prompt_compress_cache(x2, shift, scale, g, wn, wt, n_batch, seq,
                                   cache_t, page_table_flat, pos4, w4):
    tm = PROJ_TILE
    tpb = seq // tm
    nt = n_batch * seq
    n_steps = nt // tm
    blocks_per_page = PAGE_SIZE // BLOCK
    n_pages_all = page_table_flat.shape[0]
    n_pages_step = n_pages_all // n_steps
    assert n_pages_step * n_steps == n_pages_all
    blocks_step = n_pages_step * blocks_per_page
    chunk_steps = SEL_CHUNK // tm
    tok = lambda t, pt: (t, 0)
    per_b = lambda t, pt: (t // tpb, 0, 0)
    featT = lambda t, pt: (t // tpb, 0, t % tpb)
    rows3 = lambda t, pt: (t // tpb, t % tpb, 0)
    rows5 = lambda t, pt: (t // tpb, t % tpb, 0, 0, 0)
    chunk5 = lambda t, pt: (t // tpb, (t % tpb) // chunk_steps, 0, 0, t % chunk_steps)
    const2 = lambda t, pt: (0, 0)
    const4 = lambda t, pt: (0, 0, 0, 0)

    def page_map(k):
        return lambda t, pt: (pt[t * n_pages_step + k], 0, 0)

    kvT = jax.ShapeDtypeStruct((n_batch, KV_WIDTH, seq), F32)
    out_shape = (
        kvT, kvT, kvT,
        jax.ShapeDtypeStruct((nt // BLOCK * CMP_PITCH, KV_HALF), F32),
        jax.ShapeDtypeStruct((nt // BLOCK * CMP_PITCH, KV_HALF), F32),
        jax.ShapeDtypeStruct((nt, POOL_WIDTH), F32),
        jax.ShapeDtypeStruct((n_batch, NSA_WIDTH, seq), BF),
        jax.ShapeDtypeStruct((n_batch, seq, 256), BF),
        jax.ShapeDtypeStruct((n_batch, seq // SEL_CHUNK, N_KV_HEADS, V_ROWS, SEL_CHUNK), BF),
        jax.ShapeDtypeStruct((n_batch, seq, KV_HALF), BF),
        jax.ShapeDtypeStruct((n_batch, seq // WIN_CHUNK, N_KV_HEADS, V_ROWS, WIN_CHUNK), BF),
        jax.ShapeDtypeStruct((n_batch, 32, seq), F32),
        jax.ShapeDtypeStruct((n_pages_all * blocks_per_page, KV_WIDTH), F32),
    )
    out_specs = (
        pl.BlockSpec((1, KV_WIDTH, tm), featT),
        pl.BlockSpec((1, KV_WIDTH, tm), featT),
        pl.BlockSpec((1, KV_WIDTH, tm), featT),
        pl.BlockSpec((tm // BLOCK * CMP_PITCH, KV_HALF), tok),
        pl.BlockSpec((tm // BLOCK * CMP_PITCH, KV_HALF), tok),
        pl.BlockSpec((tm, POOL_WIDTH), tok),
        pl.BlockSpec((1, NSA_WIDTH, tm), featT),
        pl.BlockSpec((1, tm, 256), rows3),
        pl.BlockSpec((1, 1, N_KV_HEADS, V_ROWS, tm), chunk5),
        pl.BlockSpec((1, tm, KV_HALF), rows3),
        pl.BlockSpec((1, tm // WIN_CHUNK, N_KV_HEADS, V_ROWS, WIN_CHUNK), rows5),
        pl.BlockSpec((1, 32, tm), featT),
        pl.BlockSpec((blocks_step, KV_WIDTH), tok),
    )
    grid_spec = pltpu.PrefetchScalarGridSpec(
        num_scalar_prefetch=1,
        grid=(n_steps,),
        in_specs=[pl.BlockSpec((tm, D_MODEL), tok),
                  pl.BlockSpec((1, 1, D_MODEL), per_b),
                  pl.BlockSpec((1, 1, D_MODEL), per_b),
                  pl.BlockSpec((1, D_MODEL), const2),
                  pl.BlockSpec(wn.shape, const2),
                  pl.BlockSpec(wt.shape, const2)]
        + [pl.BlockSpec((1, KV_WIDTH, PAGE_SIZE), page_map(k)) for k in range(n_pages_step)]
        + [pl.BlockSpec(pos4.shape, const4), pl.BlockSpec(w4.shape, const4)],
        out_specs=out_specs,
        scratch_shapes=[pltpu.VMEM((blocks_step * CMP_PITCH, KV_HALF), F32),
                        pltpu.VMEM((blocks_step * CMP_PITCH, KV_HALF), F32)],
    )
    return pl.pallas_call(
        functools.partial(_proj_cmp_kernel, n_pages_step=n_pages_step, tm=tm, tpb=tpb),
        out_shape=out_shape,
        grid_spec=grid_spec,
        compiler_params=_cparams("arbitrary"),
        name="project_prompt_compress_cache",
    )(page_table_flat, x2, shift, scale, g, wn, wt, *([cache_t] * n_pages_step), pos4, w4)


def _compress_rows(src_refs, pos_ref, w_ref, o_ref, n_blocks, pitch):
    rows_per_slice = BLOCK // CMP_SPLIT
    for s in range(2):
        acc = jnp.zeros((n_blocks, KV_HALF), F32)
        for c in range(CMP_SPLIT):
            xc = jnp.concatenate(
                [src_refs[s][pl.ds(c * rows_per_slice + r, n_blocks, stride=pitch), :]
                 for r in range(rows_per_slice)], axis=1) + pos_ref[s, c]
            acc = acc + _dot(xc.astype(BF), w_ref[s, c])
        o_ref[:, s * KV_HALF:(s + 1) * KV_HALF] = acc


def _cmp_kernel(xa_ref, xb_ref, pos_ref, w_ref, o_ref):
    _compress_rows((xa_ref, xb_ref), pos_ref, w_ref, o_ref, CMP_GROUP, CMP_PITCH)


def _compress(xa, xb, pos4, w4):
    m = xa.shape[0] // CMP_PITCH
    rows = CMP_GROUP * CMP_PITCH
    return pl.pallas_call(
        _cmp_kernel,
        out_shape=jax.ShapeDtypeStruct((m, KV_WIDTH), F32),
        grid=(m // CMP_GROUP,),
        in_specs=[pl.BlockSpec((rows, KV_HALF), lambda i: (i, 0)),
                  pl.BlockSpec((rows, KV_HALF), lambda i: (i, 0)),
                  pl.BlockSpec(pos4.shape, lambda i: (0, 0, 0, 0)),
                  pl.BlockSpec(w4.shape, lambda i: (0, 0, 0, 0))],
        out_specs=pl.BlockSpec((CMP_GROUP, KV_WIDTH), lambda i: (i, 0)),
        compiler_params=_cparams("arbitrary"),
        name="compress_blocks",
    )(xa, xb, pos4, w4)


def _cmp_last_kernel(x_ref, pos_ref, w_ref, o_ref):
    lane = lax.broadcasted_iota(I32, (8, CMP_K // CMP_SPLIT), 1)
    for s in range(2):
        x0 = x_ref[:, s * KV_HALF:(s + 1) * KV_HALF] + pos_ref[s, 0][:, 0:KV_HALF]
        acc = _dot(x0.astype(BF), w_ref[s, 0, 0:KV_HALF, :])
        pad_term = jnp.zeros((8, KV_HALF), F32)
        for c in range(CMP_SPLIT):
            pc = jnp.broadcast_to(pos_ref[s, c], (8, CMP_K // CMP_SPLIT))
            if c == 0:
                pc = jnp.where(lane >= KV_HALF, pc, 0.0)
            pad_term = pad_term + _dot(pc.astype(BF), w_ref[s, c])
        o_ref[:, s * KV_HALF:(s + 1) * KV_HALF] = acc + pad_term[0:1, :]


def _compress_last(x_new, pos4, w4):
    return pl.pallas_call(
        _cmp_last_kernel,
        out_shape=jax.ShapeDtypeStruct(x_new.shape, F32),
        compiler_params=pltpu.CompilerParams(vmem_limit_bytes=VMEM_LIMIT),
        name="compress_last_block",
    )(x_new, pos4, w4)


def _attn_kernel(qT_ref, kaug_ref, vsT_ref, kw_ref, vwT_ref, kc_ref, vcT_ref, gT_ref, o_ref,
                 qa_ref, sa_ref, sb_ref, sc_ref, swb_ref, ma_ref, mb_ref, mc_ref, m_ref, acc_ref, outT_ref):
    i = pl.program_id(1)
    q0 = i * Q_TILE
    n_full = lax.div(q0, SEL_CHUNK)
    n_blk = kc_ref.shape[1]
    row = lax.broadcasted_iota(I32, (n_blk, Q_TILE), 0)
    tok = lax.broadcasted_iota(I32, (n_blk, Q_TILE), 1)
    qpos = q0 + tok
    cur = jnp.right_shift(qpos, 6)
    cmp_bias = jnp.where((row + 1) * BLOCK - 1 <= qpos, 0.0, -MASK_BIG)
    valid = row <= cur
    forced = (row == 0) | (row == cur) | (row == cur - 1)
    key_c = lax.broadcasted_iota(I32, (SEL_CHUNK, Q_TILE), 0)
    qpos_c = q0 + lax.broadcasted_iota(I32, (SEL_CHUNK, Q_TILE), 1)
    hs = [slice(h * HEAD_DIM, (h + 1) * HEAD_DIM) for h in range(N_KV_HEADS)]
    tri_bias = jnp.where(lax.broadcasted_iota(I32, (128, 128), 0) <= lax.broadcasted_iota(I32, (128, 128), 1),
                         0.0, -MASK_BIG)

    def bias4(keep):
        b = jnp.where(keep, 0.0, -MASK_BIG)
        return jnp.concatenate([b] * GROUP, axis=1)

    def online_step(st, s, v_t, s_max=None):
        m_old = m_ref[st]
        m_new = jnp.maximum(m_old, jnp.max(s, axis=0, keepdims=True) if s_max is None else s_max)
        alpha = jnp.exp2(m_old - m_new)
        p = jnp.exp2(s - m_new)
        acc_ref[st] = alpha * acc_ref[st] + _dot(v_t, p.astype(BF))
        m_ref[st] = m_new

    m_ref[...] = jnp.full(m_ref.shape, NEG_INF, F32)
    acc_ref[...] = jnp.zeros(acc_ref.shape, F32)

    for h in range(N_KV_HEADS):
        qa_ref[h] = jnp.zeros(qa_ref.shape[1:], BF)
        for g in range(GROUP):
            r0 = h * GROUP * HEAD_DIM + g * HEAD_DIM
            qa_ref[h, hs[h], g * Q_TILE:(g + 1) * Q_TILE] = qT_ref[0, r0:r0 + HEAD_DIM, :]

    w0 = jnp.maximum(q0 - WINDOW, 0)
    kw_a = kw_ref[0, pl.ds(pl.multiple_of(w0, WIN_CHUNK), WINDOW), :]
    kw_b = kw_ref[0, pl.ds(pl.multiple_of(q0, Q_TILE), Q_TILE), :]
    for h in range(N_KV_HEADS):
        sc_ref[h] = _dot(kw_a, qa_ref[h, 0:KV_HALF, :])
        swb_ref[h] = _dot(kw_b, qa_ref[h, 0:KV_HALF, :])

    o_c = []
    scores = []
    for h in range(N_KV_HEADS):
        sc = _dot(kc_ref[0], qa_ref[h, 0:KV_HALF, :])
        imp = jnp.zeros((n_blk, Q_TILE), F32)
        p_parts = []
        for g in range(GROUP):
            s = sc[:, g * Q_TILE:(g + 1) * Q_TILE] + cmp_bias
            mx = jnp.max(s, axis=0, keepdims=True)
            e = jnp.exp2(s - mx)
            inv = jnp.where(mx > -0.5 * MASK_BIG, 1.0 / jnp.sum(e, axis=0, keepdims=True), 0.0)
            p = e * inv
            imp = imp + p
            p_parts.append(p)
        o_c.append(_dot(vcT_ref[0, hs[h], :], jnp.concatenate(p_parts, axis=1).astype(BF)))
        scores.append(jnp.where(valid, jnp.where(forced, -2.0, imp), -1.0))

    blk_f = row.astype(F32)

    def pick(work):
        best = jnp.max(work, axis=0, keepdims=True)
        first = jnp.min(jnp.where(work == best, blk_f, float(n_blk)), axis=0, keepdims=True)
        return jnp.where((blk_f == first) & (best >= 0.0), -2.0, work)

    for _ in range(N_SELECT - N_FORCED):
        scores = [pick(w) for w in scores]
    for h in range(N_KV_HEADS):
        selm1 = jnp.where(scores[h] == -2.0, 0.0, -1.0).astype(BF)
        for g in range(GROUP):
            qa_ref[h, KV_HALF:KV_HALF + n_blk, g * Q_TILE:(g + 1) * Q_TILE] = selm1

    last_chunk = kaug_ref.shape[1] // SEL_CHUNK - 1

    def sel_scores(j, buf):
        dst_ref, max_ref = buf
        j = jnp.minimum(j, last_chunk)
        kt = kaug_ref[0, pl.ds(pl.multiple_of(j * SEL_CHUNK, SEL_CHUNK), SEL_CHUNK), :]
        for h in range(N_KV_HEADS):
            s = _dot(kt, qa_ref[h])
            dst_ref[h] = s
            max_ref[h] = jnp.max(s, axis=0, keepdims=True)

    def sel_process(buf, j, causal):
        src_ref, max_ref = buf
        for h in range(N_KV_HEADS):
            if causal:
                r0 = q0 - j * SEL_CHUNK
                for p in range(Q_TILE // 128):
                    rows = pl.ds(pl.multiple_of(r0 + p * 128, 128), 128)
                    for g in range(GROUP):
                        lanes = slice(g * Q_TILE + p * 128, g * Q_TILE + (p + 1) * 128)
                        src_ref[h, rows, lanes] = src_ref[h, rows, lanes] + tri_bias
                online_step(h, src_ref[h], vsT_ref[0, j, h])
            else:
                online_step(h, src_ref[h], vsT_ref[0, j, h], max_ref[h])

    buf_a, buf_b, buf_c = (sa_ref, ma_ref), (sb_ref, mb_ref), (sc_ref, mc_ref)
    sel_scores(0, buf_a)
    sel_scores(1, buf_b)

    j0 = lax.div(w0, WIN_CHUNK)
    jq = lax.div(q0, WIN_CHUNK)
    delta = qpos_c - (w0 + key_c)
    wbias = bias4((delta >= 0) & (delta < WINDOW))
    key_d = lax.broadcasted_iota(I32, (Q_TILE, Q_TILE), 0)
    tok_d = lax.broadcasted_iota(I32, (Q_TILE, Q_TILE), 1)
    dbias = bias4((key_d <= tok_d) & (q0 >= WINDOW))
    for h in range(N_KV_HEADS):
        v_a = jnp.concatenate([vwT_ref[0, j0 + c, h] for c in range(WINDOW // WIN_CHUNK)], axis=1)
        v_b = jnp.concatenate([vwT_ref[0, jq + c, h] for c in range(Q_TILE // WIN_CHUNK)], axis=1)
        online_step(2 + h, sc_ref[h] + wbias, v_a)
        online_step(2 + h, swb_ref[h] + dbias, v_b)

    def trio_body(t, carry):
        j = 3 * t
        sel_scores(j + 2, buf_c)
        sel_process(buf_a, j, False)
        sel_scores(j + 3, buf_a)
        sel_process(buf_b, j + 1, False)
        sel_scores(j + 4, buf_b)
        sel_process(buf_c, j + 2, False)
        return carry

    n_trios = lax.div(n_full, 3)
    lax.fori_loop(0, n_trios, trio_body, 0)
    j_last = 3 * n_trios
    n_left = n_full - j_last

    @pl.when(n_left == 0)
    def _():
        sel_process(buf_a, j_last, True)

    @pl.when(n_left == 1)
    def _():
        sel_process(buf_a, j_last, False)
        sel_process(buf_b, j_last + 1, True)

    @pl.when(n_left == 2)
    def _():
        sel_scores(j_last + 2, buf_c)
        sel_process(buf_a, j_last, False)
        sel_process(buf_b, j_last + 1, False)
        sel_process(buf_c, j_last + 2, True)

    for h in range(N_KV_HEADS):
        o_s = acc_ref[h, 0:HEAD_DIM, :] * (1.0 / acc_ref[h, HEAD_DIM:HEAD_DIM + 1, :])
        o_w = acc_ref[2 + h, 0:HEAD_DIM, :] * (1.0 / acc_ref[2 + h, HEAD_DIM:HEAD_DIM + 1, :])
        for g in range(GROUP):
            gs = slice(g * Q_TILE, (g + 1) * Q_TILE)
            gr = h * GROUP * 3 + g * 3
            og = (gT_ref[0, gr:gr + 1, :] * o_c[h][:, gs] + gT_ref[0, gr + 1:gr + 2, :] * o_s[:, gs]
                  + gT_ref[0, gr + 2:gr + 3, :] * o_w[:, gs])
            r0 = h * GROUP * HEAD_DIM + g * HEAD_DIM
            outT_ref[r0:r0 + HEAD_DIM, :] = og

    o_ref[0] = outT_ref[...].T.astype(BF)


def _attention_prompt(qT, kaug, vsT, kw, vwT, kc, vcT, gT, n_batch, seq):
    per_b3 = lambda n, i: (n, 0, 0)
    per_b5 = lambda n, i: (n, 0, 0, 0, 0)
    rows = GROUP * Q_TILE
    return pl.pallas_call(
        _attn_kernel,
        out_shape=jax.ShapeDtypeStruct((n_batch, seq, NSA_WIDTH), BF),
        grid=(n_batch, seq // Q_TILE),
        in_specs=[pl.BlockSpec((1, NSA_WIDTH, Q_TILE), lambda n, i: (n, 0, i)),
                  pl.BlockSpec((1, seq, 256), per_b3),
                  pl.BlockSpec((1, seq // SEL_CHUNK, N_KV_HEADS, V_ROWS, SEL_CHUNK), per_b5),
                  pl.BlockSpec((1, seq, KV_HALF), per_b3),
                  pl.BlockSpec((1, seq // WIN_CHUNK, N_KV_HEADS, V_ROWS, WIN_CHUNK), per_b5),
                  pl.BlockSpec((1, seq // BLOCK, KV_HALF), per_b3),
                  pl.BlockSpec((1, KV_HALF, seq // BLOCK), per_b3),
                  pl.BlockSpec((1, 32, Q_TILE), lambda n, i: (n, 0, i))],
        out_specs=pl.BlockSpec((1, Q_TILE, NSA_WIDTH), lambda n, i: (n, i, 0)),
        scratch_shapes=[pltpu.VMEM((N_KV_HEADS, 256, rows), BF),
                        pltpu.VMEM((N_KV_HEADS, SEL_CHUNK, rows), F32),
                        pltpu.VMEM((N_KV_HEADS, SEL_CHUNK, rows), F32),
                        pltpu.VMEM((N_KV_HEADS, SEL_CHUNK, rows), F32),
                        pltpu.VMEM((N_KV_HEADS, Q_TILE, rows), F32),
                        pltpu.VMEM((N_KV_HEADS, 1, rows), F32),
                        pltpu.VMEM((N_KV_HEADS, 1, rows), F32),
                        pltpu.VMEM((N_KV_HEADS, 1, rows), F32),
                        pltpu.VMEM((2 * N_KV_HEADS, 1, rows), F32),
                        pltpu.VMEM((2 * N_KV_HEADS, V_ROWS, rows), F32),
                        pltpu.VMEM((NSA_WIDTH, Q_TILE), F32)],
        compiler_params=_cparams("arbitrary", "arbitrary"),
        name="attention_prompt",
    )(qT, kaug, vsT, kw, vwT, kc, vcT, gT)


def _pool_kernel(u_ref, halo_ref, wp_ref, ps_ref, o_ref, ext_ref, lvl_ref, *, tm, tpb):
    t = pl.program_id(0) % tpb
    ext_ref[0:16, :] = jnp.where(t == 0, 0.0, halo_ref[...])
    u = u_ref[...]
    ext_ref[16:16 + tm, :] = u
    pos = t * tm + lax.broadcasted_iota(I32, (tm, 1), 0)
    end = 16 + tm
    outs = []
    for gi, w in enumerate(POOL_WINDOWS):
        cs = slice(gi * POOL_GROUP_DIM, (gi + 1) * POOL_GROUP_DIM)
        lo = 16 - (w - 2)
        acc = ext_ref[lo:end, cs] + ext_ref[lo - 1:end - 1, cs]
        d = 2
        while d < w:
            lvl_ref[lo:end, :] = acc
            lo += d
            acc = lvl_ref[lo:end, :] + lvl_ref[lo - d:end - d, :]
            d *= 2
        cnt = jnp.minimum(pos + 1, w).astype(F32)
        pooled = acc / cnt - u[:, cs]
        outs.append(_dot(pooled.astype(BF), wp_ref[gi]))
    o_ref[...] = (jnp.concatenate(outs, axis=1) * ps_ref[...]).astype(BF)


def _pool_prompt(u, w_pool, pool_scale, n_batch, seq):
    tm = TOKEN_TILE
    tpb = seq // tm
    nt = n_batch * seq
    return pl.pallas_call(
        functools.partial(_pool_kernel, tm=tm, tpb=tpb),
        out_shape=jax.ShapeDtypeStruct((nt, POOL_WIDTH), BF),
        grid=(nt // tm,),
        in_specs=[pl.BlockSpec((tm, POOL_WIDTH), lambda t: (t, 0)),
                  pl.BlockSpec((16, POOL_WIDTH), lambda t: (jnp.maximum(t * (tm // 16) - 1, 0), 0)),
                  pl.BlockSpec(w_pool.shape, lambda t: (0, 0, 0)),
                  pl.BlockSpec((1, POOL_WIDTH), lambda t: (0, 0))],
        out_specs=pl.BlockSpec((tm, POOL_WIDTH), lambda t: (t, 0)),
        scratch_shapes=[pltpu.VMEM((tm + 16, POOL_WIDTH), F32), pltpu.VMEM((tm + 16, POOL_GROUP_DIM), F32)],
        compiler_params=_cparams("arbitrary"),
        name="pool_prompt",
    )(u, u, w_pool, pool_scale)


def _spool_kernel(u_ref, hist_ref, wp_ref, ps_ref, o_ref):
    u = u_ref[...]
    outs = []
    for gi, w in enumerate(POOL_WINDOWS):
        cs = slice(gi * POOL_GROUP_DIM, (gi + 1) * POOL_GROUP_DIM)
        acc = u[:, cs]
        for k in range(1, w):
            acc = acc + hist_ref[POOL_HIST - k, :, cs]
        pooled = acc / float(w) - u[:, cs]
        outs.append(_dot(pooled.astype(BF), wp_ref[gi]))
    o_ref[...] = (jnp.concatenate(outs, axis=1) * ps_ref[...]).astype(BF)


def _pool_sample(u, hist_t, w_pool, pool_scale):
    n = u.shape[0]
    return pl.pallas_call(
        _spool_kernel,
        out_shape=jax.ShapeDtypeStruct((n, POOL_WIDTH), BF),
        name="pool_sample",
    )(u, hist_t, w_pool, pool_scale)


def _merge_kernel(x_ref, sh_ref, sc_ref, gate_ref, gpre_ref, gpost_ref, onsa_ref, opool_ref,
                  wgm_ref, wun_ref, wup_ref, wo_ref, o_ref):
    x = x_ref[...]
    h = _rms(x, gpre_ref[...]) * (1.0 + sc_ref[0]) + sh_ref[0]
    gm = jax.nn.sigmoid(_dot(h.astype(BF), wgm_ref[...]))
    m = (gm[:, :D_MODEL] * _dot(onsa_ref[...], wun_ref[...])
         + gm[:, D_MODEL:] * _dot(opool_ref[...], wup_ref[...]))
    m = _dot(m.astype(BF), wo_ref[...])
    o_ref[...] = x + gate_ref[0] * _rms(m, gpost_ref[...])


def _mlp_kernel(x_ref, sh_ref, sc_ref, gate_ref, gpre_ref, gpost_ref, w1_ref, w2_ref, o_ref):
    x = x_ref[...]
    h = _rms(x, gpre_ref[...]) * (1.0 + sc_ref[0]) + sh_ref[0]
    hb = h.astype(BF)
    f = jnp.zeros(x.shape, F32)
    fc = 1024
    for c in range(D_FF // fc):
        a = jnp.maximum(_dot(hb, w1_ref[:, c * fc:(c + 1) * fc]), 0.0)
        f = f + _dot((a * a).astype(BF), w2_ref[c * fc:(c + 1) * fc, :])
    o_ref[...] = x + gate_ref[0] * _rms(f, gpost_ref[...])


def _token_call(kernel, name, x2, mods, tm, rows_per_mod, extra_tok, consts):
    nt = x2.shape[0]
    r = mods[0].shape[1]
    mod_spec = pl.BlockSpec((1, r, D_MODEL), lambda t: ((t * tm) // rows_per_mod, 0, 0))
    in_specs = [pl.BlockSpec((tm, D_MODEL), lambda t: (t, 0))] + [mod_spec] * len(mods)
    in_specs += [pl.BlockSpec((1, D_MODEL), lambda t: (0, 0))] * 2
    in_specs += [pl.BlockSpec((tm, a.shape[1]), lambda t: (t, 0)) for a in extra_tok]
    in_specs += [pl.BlockSpec(w.shape, lambda t: (0, 0), pipeline_mode=pl.Buffered(1)) for w in consts[2:]]
    return pl.pallas_call(
        kernel,
        out_shape=jax.ShapeDtypeStruct((nt, D_MODEL), F32),
        grid=(nt // tm,),
        in_specs=in_specs,
        out_specs=pl.BlockSpec((tm, D_MODEL), lambda t: (t, 0)),
        compiler_params=_cparams("arbitrary"),
        name=name,
    )(x2, *mods, consts[0], consts[1], *extra_tok, *consts[2:])


def _sproj_kernel(x_ref, sh_ref, sc_ref, g_ref, wn_ref, wt_ref, z_ref, zs_ref, zT_ref):
    x = x_ref[...]
    h = _rms(x, g_ref[...]) * (1.0 + sc_ref[...]) + sh_ref[...]
    hb = h.astype(BF)
    z = _dot(hb, wn_ref[...])
    z_ref[...] = z
    zs_ref[...] = jax.nn.sigmoid(z)
    zT_ref[...] = _dot_nt(wt_ref[...], hb)


def _project_sample(x2, shift, scale, g, wn, wt):
    n = x2.shape[0]
    shp = jax.ShapeDtypeStruct((n, wn.shape[1]), F32)
    return pl.pallas_call(
        _sproj_kernel,
        out_shape=(shp, shp, jax.ShapeDtypeStruct((wt.shape[0], n), F32)),
        name="project_sample",
    )(x2, shift, scale, g, wn, wt)


def _sattn_init(kaug_ref, past_len, seq_step):
    @pl.when(pl.program_id(0) == 0)
    def _():
        blk = lax.broadcasted_iota(I32, (128, past_len), 0)
        key_blk = jnp.right_shift(lax.broadcasted_iota(I32, (128, past_len), 1), 6)
        onehot = jnp.where(blk == key_blk, MASK_BIG, 0.0).astype(BF)
        for q in range(seq_step):
            kaug_ref[q, 128:256, :] = onehot


def _sattn_main(*refs, n_pages, past_len, seq_step):
    pages = refs[:seq_step * n_pages]
    (qb_ref, kvc_ref, win_ref, ksn_ref, kwn_ref, kwnT_ref, g_ref,
     o_ref, nwin_ref, kaug_ref, vall_ref) = refs[seq_step * n_pages:]

    for q in range(seq_step):
        for p in range(n_pages):
            pg = pages[q * n_pages + p][0]
            kaug_ref[q, 0:128, p * PAGE_SIZE:(p + 1) * PAGE_SIZE] = pg[0:KV_HALF, :].astype(BF)
            vall_ref[q, :, p * PAGE_SIZE:(p + 1) * PAGE_SIZE] = pg[KV_HALF:KV_WIDTH, :].astype(BF)

    nb_past = past_len // BLOCK
    win_buf = win_ref.shape[2]
    n_rows = 8 * seq_step
    per_seq = lambda f: jnp.concatenate([f(q) for q in range(seq_step)], axis=0)
    rows_of = lambda x, q: x[8 * q:8 * (q + 1)]
    qb = qb_ref[...].reshape(n_rows, KV_HALF).astype(BF)
    qf = qb.astype(F32)
    row = lax.broadcasted_iota(I32, (n_rows, 128), 0)
    lane = lax.broadcasted_iota(I32, (n_rows, 128), 1)

    def new_key(ref):
        return per_seq(lambda q: jnp.broadcast_to(ref[q], (8, KV_WIDTH))).astype(BF).astype(F32)

    s_c = per_seq(lambda q: _dot_nt(rows_of(qb, q), kvc_ref[q, :, 0:128].astype(BF)))
    cm = lane < nb_past
    s_c = jnp.where(cm, s_c, NEG_INF)
    mx = jnp.max(s_c, axis=1, keepdims=True)
    mx = jnp.where(mx > NEG_INF, mx, 0.0)
    e = jnp.where(cm, jnp.exp(s_c - mx), 0.0)
    p_c = e / jnp.maximum(jnp.sum(e, axis=1, keepdims=True), 1e-30)
    o_c = per_seq(lambda q: _dot(rows_of(p_c, q).astype(BF), kvc_ref[q, :, 128:256].astype(BF)))

    imp = jnp.zeros((n_rows, 128), F32)
    for grp in range(n_rows // GROUP):
        in_grp = jnp.right_shift(row, 2) == grp
        imp = jnp.where(in_grp, jnp.sum(jnp.where(in_grp, p_c, 0.0), axis=0, keepdims=True), imp)
    cur = nb_past
    forced = (lane == 0) | (lane == cur) | (lane == cur - 1)
    score = jnp.where(lane <= cur, imp + FORCE_SCORE * forced.astype(F32), -1.0)
    cnt = jnp.zeros((n_rows, 128), F32)
    for bp in range(nb_past + 1):
        other = score[:, bp:bp + 1]
        ahead = (other > score) | ((other == score) & (bp < lane))
        cnt = cnt + jnp.where(ahead, 1.0, 0.0)
    sel = (cnt < float(N_SELECT)) & (score >= 0.0)
    selm1 = jnp.where(sel, 0.0, -1.0).astype(BF)

    qaug = jnp.concatenate([qb, selm1], axis=1)
    s_s = per_seq(lambda q: _dot(rows_of(qaug, q), kaug_ref[q]))
    kv_n = new_key(ksn_ref)
    s_n = jnp.sum(qf * kv_n[:, 0:128], axis=1, keepdims=True)
    m_s = jnp.maximum(jnp.max(s_s, axis=1, keepdims=True), s_n)
    e_s = jnp.exp(s_s - m_s)
    e_n = jnp.exp(s_n - m_s)
    l_s = jnp.sum(e_s, axis=1, keepdims=True) + e_n
    pv = per_seq(lambda q: _dot_nt(rows_of(e_s, q).astype(BF), vall_ref[q]))
    o_s = (pv + e_n.astype(BF).astype(F32) * kv_n[:, 128:256]) / l_s

    s_w = per_seq(lambda q: _dot(rows_of(qb, q), win_ref[q, 0:KV_HALF, :].astype(BF)))
    lane_w = lax.broadcasted_iota(I32, (n_rows, win_buf), 1)
    s_w = jnp.where(lane_w >= win_buf + 1 - WINDOW, s_w, NEG_INF)
    kv_n = new_key(kwn_ref)
    s_n = jnp.sum(qf * kv_n[:, 0:128], axis=1, keepdims=True)
    m_w = jnp.maximum(jnp.max(s_w, axis=1, keepdims=True), s_n)
    e_w = jnp.exp(s_w - m_w)
    e_n = jnp.exp(s_n - m_w)
    l_w = jnp.sum(e_w, axis=1, keepdims=True) + e_n
    pv = per_seq(lambda q: _dot_nt(rows_of(e_w, q).astype(BF), win_ref[q, KV_HALF:KV_WIDTH, :].astype(BF)))
    o_w = (pv + e_n.astype(BF).astype(F32) * kv_n[:, 128:256]) / l_w

    g = g_ref[...].reshape(n_rows, 3)
    o_ref[...] = (g[:, 0:1] * o_c + g[:, 1:2] * o_s + g[:, 2:3] * o_w).reshape(seq_step, 8, KV_HALF)

    seq_lane = lax.broadcasted_iota(I32, kwnT_ref.shape, 1)
    row_lane = lax.broadcasted_iota(I32, (KV_WIDTH, win_buf), 1)
    for q in range(seq_step):
        n = pl.program_id(0) * seq_step + q
        new_col = jnp.sum(jnp.where(seq_lane == n, kwnT_ref[...], 0.0), axis=1, keepdims=True)
        shifted = pltpu.roll(win_ref[q], win_buf - 1, axis=1)
        nwin_ref[q] = jnp.where(row_lane == win_buf - 1, new_col, shifted)


def _sattn_kernel(pt_ref, *refs, n_pages, past_len, seq_step):
    _sattn_init(refs[-2], past_len, seq_step)
    _sattn_main(*refs, n_pages=n_pages, past_len=past_len, seq_step=seq_step)


def _attention_sample(page_table_flat, cache_t, qblk, kvc_pad, win_t, kvs_new, kvw_new, kvw_new_t, gates8,
                      n_pages, past_len):
    n_seq = qblk.shape[0]
    win_buf = win_t.shape[2]
    g = SAMPLE_SEQ_STEP
    const2 = lambda t, pt: (0, 0)
    per_step = lambda t, pt: (t, 0, 0)

    def page_map(k):
        return lambda t, pt: (pt[t * g * n_pages + k], 0, 0)

    grid_spec = pltpu.PrefetchScalarGridSpec(
        num_scalar_prefetch=1,
        grid=(n_seq // g,),
        in_specs=[pl.BlockSpec((1, KV_WIDTH, PAGE_SIZE), page_map(k)) for k in range(g * n_pages)]
        + [pl.BlockSpec((g, 8, 128), per_step),
           pl.BlockSpec((g, 128, KV_WIDTH), per_step),
           pl.BlockSpec((g, KV_WIDTH, win_buf), per_step),
           pl.BlockSpec((g, 1, KV_WIDTH), per_step),
           pl.BlockSpec((g, 1, KV_WIDTH), per_step),
           pl.BlockSpec(kvw_new_t.shape, const2),
           pl.BlockSpec((g, 8, 3), per_step)],
        out_specs=(pl.BlockSpec((g, 8, 128), per_step),
                   pl.BlockSpec((g, KV_WIDTH, win_buf), per_step)),
        scratch_shapes=[pltpu.VMEM((g, 256, past_len), BF), pltpu.VMEM((g, KV_HALF, past_len), BF)],
    )
    return pl.pallas_call(
        functools.partial(_sattn_kernel, n_pages=n_pages, past_len=past_len, seq_step=g),
        out_shape=(jax.ShapeDtypeStruct((n_seq, 8, 128), F32),
                   jax.ShapeDtypeStruct((n_seq, KV_WIDTH, win_buf), F32)),
        grid_spec=grid_spec,
        compiler_params=_cparams("arbitrary"),
        name="attention_sample",
    )(page_table_flat, *([cache_t] * (g * n_pages)), qblk, kvc_pad, win_t, kvs_new, kvw_new, kvw_new_t, gates8)


def _kv_rows_view(kv_t):
    n, _, t = kv_t.shape
    return jnp.transpose(kv_t.reshape(n, 2, N_KV_HEADS, HEAD_DIM, t), (0, 4, 1, 2, 3))


def _kv_feat_view(kv):
    n, t = kv.shape[:2]
    return jnp.transpose(kv, (0, 2, 3, 4, 1)).reshape(n, KV_WIDTH, t)


def kernel(x_prompt, x_sample, cache_cmp_kv, cache_sel_kv, state_win_kv, state_pool, page_table, c_prompt, c_sample, w_ada, b_ada, g_pre_mix, g_post_mix, g_pre_mlp, g_post_mlp, w_in, w_cmp, pos_cmp, w_pool, pool_scale, w_up_nsa, w_up_pool, w_o, w_ff1, w_ff2):
    n_batch, seq, _ = x_prompt.shape
    n_seq = x_sample.shape[0]
    n_pages = page_table.shape[1]
    past_len = n_pages * PAGE_SIZE
    nb_past = past_len // BLOCK
    assert x_sample.shape[1] == 1 and w_ada.shape[0] == 1 and past_len % BLOCK == 0
    assert seq % TOKEN_TILE == 0 and seq // BLOCK == 128 and state_win_kv.shape[2] == WINDOW
    assert (n_seq * nb_past) % CMP_GROUP == 0 and n_seq == CMP_GROUP
    assert FORCE_SCORE > GROUP

    w_t = w_in[0].T
    wt = jnp.pad(w_t[0:1304], ((0, 8), (0, 0))).astype(BF)
    wn = jnp.concatenate([w_t[768:896], w_t[1024:1152], w_t[1304:1816], w_t[512:768]], axis=0).T.astype(BF)
    ws = jnp.pad(w_t[0:1816], ((0, 104), (0, 0))).T.astype(BF)
    wgm = w_t[1816:3864].T.astype(BF)
    eye = jnp.eye(N_KV_HEADS, dtype=F32)
    wc = w_cmp[0].astype(BF)
    wz = jnp.zeros_like(wc)
    w4 = jnp.concatenate([jnp.concatenate([wc, wz], axis=3), jnp.concatenate([wz, wc], axis=3)], axis=2)
    w4 = w4.reshape(2, CMP_SPLIT, CMP_K // CMP_SPLIT, KV_HALF)
    pos4 = jnp.broadcast_to(jnp.transpose(pos_cmp[0], (1, 0, 2))[:, :, None, :],
                            (2, BLOCK, N_KV_HEADS, HEAD_DIM)).reshape(2, CMP_SPLIT, 1, CMP_K // CMP_SPLIT)
    wp = w_pool[0].astype(BF)
    ps = pool_scale[0].reshape(1, POOL_WIDTH)
    wun, wup, wo = w_up_nsa[0].astype(BF), w_up_pool[0].astype(BF), w_o[0].astype(BF)
    w1, w2 = w_ff1[0].astype(BF), w_ff2[0].astype(BF)
    gpm, gqm = g_pre_mix[0].reshape(1, D_MODEL), g_post_mix[0].reshape(1, D_MODEL)
    gpf, gqf = g_pre_mlp[0].reshape(1, D_MODEL), g_post_mlp[0].reshape(1, D_MODEL)

    n_c = n_batch + n_seq
    c_all = jnp.pad(jnp.concatenate([c_prompt, c_sample], axis=0), ((0, (-n_c) % 8), (0, 0)))
    ada = _adaln(c_all, w_ada[0], b_ada[0].reshape(1, -1))
    ada_p = ada[:n_batch].reshape(n_batch, 6, 1, D_MODEL)
    ada_s = ada[n_batch:n_c].reshape(n_seq, 6, D_MODEL)
    mods_p = [ada_p[:, k] for k in range(6)]
    mods_s = [ada_s[:, k][None] for k in range(6)]

    xp = x_prompt.reshape(n_batch * seq, D_MODEL)
    pt_flat = page_table.reshape(-1)
    (kvcT, kvsT, kvwT, kvc_a, kvc_b, u_p, qT, kaug, vsT, kw, vwT, gT,
     kvc_past) = _project_prompt_compress_cache(xp, mods_p[0], mods_p[1], gpm, wn, wt, n_batch, seq,
                                                _kv_feat_view(cache_cmp_kv[0]), pt_flat, pos4, w4)
    kvc_blk = _compress(kvc_a, kvc_b, pos4, w4).reshape(n_batch, seq // BLOCK, KV_WIDTH)
    kc = kvc_blk[:, :, 0:128].astype(BF)
    vcT = jnp.swapaxes(kvc_blk[:, :, 128:256], 1, 2).astype(BF)
    onsa_p = _attention_prompt(qT, kaug, vsT, kw, vwT, kc, vcT, gT, n_batch, seq)
    opool_p = _pool_prompt(u_p, wp, ps, n_batch, seq)
    x1_p = _token_call(_merge_kernel, "merge_prompt", xp, [mods_p[0], mods_p[1], mods_p[2]], TOKEN_TILE, seq,
                       [onsa_p.reshape(n_batch * seq, NSA_WIDTH), opool_p], [gpm, gqm, wgm, wun, wup, wo])
    y_p = _token_call(_mlp_kernel, "mlp_prompt", x1_p, [mods_p[3], mods_p[4], mods_p[5]], TOKEN_TILE, seq,
                      [], [gpf, gqf, w1, w2])

    xs = x_sample.reshape(n_seq, D_MODEL)
    z, zsig, zT = _project_sample(xs, mods_s[0][0], mods_s[1][0], gpm, ws, wt)
    q_s = z[:, 0:512] * (HEAD_DIM ** -0.5)
    kvc_n, kvs_n, kvw_n = z[:, 512:768], z[:, 768:1024], z[:, 1024:1280]
    gates_s = zsig[:, 1280:1304].reshape(n_seq, N_HEADS, 3)
    u_s = z[:, 1304:1816]
    q5 = q_s.reshape(n_seq, N_KV_HEADS, GROUP, 1, HEAD_DIM) * eye[None, :, None, :, None]
    qblk = q5.reshape(n_seq, N_HEADS, KV_HALF)

    kvc_last = _compress_last(kvc_n, pos4, w4)
    kvc_s = jnp.concatenate([kvc_past.reshape(n_seq, nb_past, KV_WIDTH), kvc_last[:, None, :]], axis=1)
    kvc_pad = jnp.pad(kvc_s, ((0, 0), (0, 128 - nb_past - 1), (0, 0)))
    o8, new_win_t = _attention_sample(
        pt_flat, _kv_feat_view(cache_sel_kv[0]), qblk, kvc_pad, _kv_feat_view(state_win_kv[0]),
        kvs_n[:, None, :], kvw_n[:, None, :], zT[1024:1280], gates_s, n_pages, past_len)
    o5 = o8.reshape(n_seq, N_KV_HEADS, GROUP, N_KV_HEADS, HEAD_DIM)
    onsa_s = jnp.concatenate([o5[:, 0, :, 0, :], o5[:, 1, :, 1, :]], axis=1).reshape(n_seq, NSA_WIDTH).astype(BF)
    opool_s = _pool_sample(u_s, jnp.swapaxes(state_pool[0], 0, 1), wp, ps)
    x1_s = _token_call(_merge_kernel, "merge_sample", xs, [mods_s[0], mods_s[1], mods_s[2]], n_seq, n_seq,
                       [onsa_s, opool_s], [gpm, gqm, wgm, wun, wup, wo])
    y_s = _token_call(_mlp_kernel, "mlp_sample", x1_s, [mods_s[3], mods_s[4], mods_s[5]], n_seq, n_seq,
                      [], [gpf, gqf, w1, w2])

    win_p = min(WINDOW, seq)
    new_kv_s = lambda rows: _kv_rows_view(rows.reshape(1, KV_WIDTH, n_seq))[0][None, :, None]
    return (
        y_p.reshape(n_batch, seq, D_MODEL),
        y_s.reshape(n_seq, 1, D_MODEL),
        _kv_rows_view(kvcT)[None],
        _kv_rows_view(kvsT)[None],
        _kv_rows_view(kvwT[:, :, seq - win_p:])[None],
        u_p.reshape(n_batch, seq, POOL_WIDTH)[None, :, seq - POOL_HIST:],
        new_kv_s(zT[512:768]),
        new_kv_s(zT[768:1024]),
        _kv_rows_view(new_win_t)[None],
        jnp.concatenate([state_pool[0][:, 1:], u_s[:, None, :]], axis=1)[None],
    )
```

```python
import functools
import math

import jax
import jax.numpy as jnp
from jax import lax
from jax.experimental import pallas as pl
from jax.experimental.pallas import tpu as pltpu

D_MODEL = 1024
N_HEADS = 8
HEAD_DIM = 64
N_KV_HEADS = 2
GROUP = N_HEADS // N_KV_HEADS
BLOCK = 64
N_SELECT = 16
WINDOW = 512
Q_TILE = 256
WIN_CHUNK = 128
NSA_WIDTH = N_HEADS * HEAD_DIM
KV_WIDTH = 2 * N_KV_HEADS * HEAD_DIM
KV_HALF = N_KV_HEADS * HEAD_DIM
FORCE_SCORE = 16.0
N_FORCED = 3
POOL_WINDOWS = (2, 4, 8, 16)
POOL_WIDTH = 512
POOL_GROUP_DIM = 128
POOL_HIST = 15
D_FF = 4 * D_MODEL
EPS = 1e-6
PAGE_SIZE = 128
V_ROWS = HEAD_DIM + 16
CMP_K = BLOCK * KV_HALF
CMP_SPLIT = 4
CMP_GROUP = 128
CMP_PITCH = BLOCK + 4

BF = jnp.bfloat16
F32 = jnp.float32
I32 = jnp.int32
MASK_BIG = 2.0 ** 100
NEG_INF = float("-inf")
LOG2E = math.log2(math.e)

TOKEN_TILE = 512
PROJ_TILE = 256
SEL_CHUNK = 512
SAMPLE_SEQ_STEP = 4
VMEM_LIMIT = 56 * 1024 * 1024


def _cparams(*sem):
    return pltpu.CompilerParams(dimension_semantics=sem, vmem_limit_bytes=VMEM_LIMIT)


def _rms(x, g):
    return x * lax.rsqrt(jnp.mean(x * x, axis=-1, keepdims=True) + EPS) * g


def _dot(a, b):
    return jnp.dot(a, b, preferred_element_type=F32)


def _dot_nt(a, b):
    return lax.dot_general(a, b, (((1,), (1,)), ((), ())), preferred_element_type=F32)


def _ada_kernel(c_ref, w_ref, b_ref, o_ref):
    c = c_ref[...]
    a = (c * jax.nn.sigmoid(c)).astype(BF)
    o_ref[...] = _dot(a, w_ref[...].astype(BF)) + b_ref[...]


def _adaln(c_all, w_ada, b_ada):
    rows = c_all.shape[0]
    n_out = w_ada.shape[1]
    tn = 1024
    return pl.pallas_call(
        _ada_kernel,
        out_shape=jax.ShapeDtypeStruct((rows, n_out), F32),
        grid=(n_out // tn,),
        in_specs=[pl.BlockSpec((rows, D_MODEL), lambda j: (0, 0)),
                  pl.BlockSpec((D_MODEL, tn), lambda j: (0, j)),
                  pl.BlockSpec((1, tn), lambda j: (0, j))],
        out_specs=pl.BlockSpec((rows, tn), lambda j: (0, j)),
        compiler_params=_cparams("arbitrary"),
        name="adaln",
    )(c_all, w_ada, b_ada)


def _proj_kernel(x_ref, sh_ref, sc_ref, g_ref, wn_ref, wt_ref,
                 kvcT_ref, kvsT_ref, kvwT_ref, ca_ref, cb_ref, u_ref,
                 qT_ref, kaug_ref, vsT_ref, kw_ref, vwT_ref, gT_ref, *, tm, tpb):
    x = x_ref[...]
    h = _rms(x, g_ref[...]) * (1.0 + sc_ref[0]) + sh_ref[0]
    hb = h.astype(BF)
    zn = _dot_nt(hb, wn_ref[...])
    zt = _dot_nt(wt_ref[...], hb)
    kvcT_ref[0] = zt[512:768]
    kvsT_ref[0] = zt[768:1024]
    kvwT_ref[0] = zt[1024:1280]
    pad_rows = jnp.zeros((CMP_PITCH - BLOCK, KV_HALF), F32)
    for b in range(tm // BLOCK):
        for dst, c0 in ((ca_ref, 768), (cb_ref, 896)):
            dst[b * CMP_PITCH:b * CMP_PITCH + BLOCK, :] = zn[b * BLOCK:(b + 1) * BLOCK, c0:c0 + KV_HALF]
            dst[b * CMP_PITCH + BLOCK:(b + 1) * CMP_PITCH, :] = pad_rows
    u_ref[...] = zn[:, 256:768]
    qT_ref[0] = (zt[0:512] * (HEAD_DIM ** -0.5 * LOG2E)).astype(BF)
    ones = jnp.ones((V_ROWS - HEAD_DIM, tm), BF)
    for h in range(N_KV_HEADS):
        vs_h = zt[896 + h * HEAD_DIM:896 + (h + 1) * HEAD_DIM].astype(BF)
        vw_h = zt[1152 + h * HEAD_DIM:1152 + (h + 1) * HEAD_DIM].astype(BF)
        vsT_ref[0, 0, h, 0:HEAD_DIM, :] = vs_h
        vsT_ref[0, 0, h, HEAD_DIM:V_ROWS, :] = ones
        for c in range(tm // WIN_CHUNK):
            cs = slice(c * WIN_CHUNK, (c + 1) * WIN_CHUNK)
            vwT_ref[0, c, h, 0:HEAD_DIM, :] = vw_h[:, cs]
            vwT_ref[0, c, h, HEAD_DIM:V_ROWS, :] = ones[:, cs]
    gT_ref[0] = jax.nn.sigmoid(zt[1280:1312])
    t0 = (pl.program_id(0) % tpb) * tm
    blk = jnp.right_shift(t0 + lax.broadcasted_iota(I32, (tm, 128), 0), 6)
    lane = lax.broadcasted_iota(I32, (tm, 128), 1)
    kaug_ref[0, :, 0:128] = zn[:, 0:128].astype(BF)
    kaug_ref[0, :, 128:256] = jnp.where(blk == lane, MASK_BIG, 0.0).astype(BF)
    kw_ref[0] = zn[:, 128:256].astype(BF)


def _proj_cmp_kernel(pt_ref, *refs, n_pages_step, tm, tpb):
    proj_in = refs[:6]
    pages = refs[6:6 + n_pages_step]
    pos_ref, w_ref = refs[6 + n_pages_step:8 + n_pages_step]
    proj_out = refs[8 + n_pages_step:20 + n_pages_step]
    o_ref, sa_ref, sb_ref = refs[20 + n_pages_step:]
    _proj_kernel(*proj_in, *proj_out, tm=tm, tpb=tpb)
    blocks_per_page = PAGE_SIZE // BLOCK
    for k in range(n_pages_step):
        pg = pages[k][0]
        for s, dst in enumerate((sa_ref, sb_ref)):
            rows = pg[s * KV_HALF:(s + 1) * KV_HALF, :].T
            for b in range(blocks_per_page):
                m = k * blocks_per_page + b
                dst[m * CMP_PITCH:m * CMP_PITCH + BLOCK, :] = rows[b * BLOCK:(b + 1) * BLOCK, :]
    _compress_rows((sa_ref, sb_ref), pos_ref, w_ref, o_ref, n_pages_step * blocks_per_page, CMP_PITCH)


def _project_prompt_compress_cache(x2, shift, scale, g, wn, wt, n_batch, seq,
                                   cache_t, page_table_flat, pos4, w4):
    tm = PROJ_TILE
    tpb = seq // tm
    nt = n_batch * seq
    n_steps = nt // tm
    blocks_per_page = PAGE_SIZE // BLOCK
    n_pages_all = page_table_flat.shape[0]
    n_pages_step = n_pages_all // n_steps
    assert n_pages_step * n_steps == n_pages_all
    blocks_step = n_pages_step * blocks_per_page
    chunk_steps = SEL_CHUNK // tm
    tok = lambda t, pt: (t, 0)
    per_b = lambda t, pt: (t // tpb, 0, 0)
    featT = lambda t, pt: (t // tpb, 0, t % tpb)
    rows3 = lambda t, pt: (t // tpb, t % tpb, 0)
    rows5 = lambda t, pt: (t // tpb, t % tpb, 0, 0, 0)
    chunk5 = lambda t, pt: (t // tpb, (t % tpb) // chunk_steps, 0, 0, t % chunk_steps)
    const2 = lambda t, pt: (0, 0)
    const4 = lambda t, pt: (0, 0, 0, 0)

    def page_map(k):
        return lambda t, pt: (pt[t * n_pages_step + k], 0, 0)

    kvT = jax.ShapeDtypeStruct((n_batch, KV_WIDTH, seq), F32)
    out_shape = (
        kvT, kvT, kvT,
        jax.ShapeDtypeStruct((nt // BLOCK * CMP_PITCH, KV_HALF), F32),
        jax.ShapeDtypeStruct((nt // BLOCK * CMP_PITCH, KV_HALF), F32),
        jax.ShapeDtypeStruct((nt, POOL_WIDTH), F32),
        jax.ShapeDtypeStruct((n_batch, NSA_WIDTH, seq), BF),
        jax.ShapeDtypeStruct((n_batch, seq, 256), BF),
        jax.ShapeDtypeStruct((n_batch, seq // SEL_CHUNK, N_KV_HEADS, V_ROWS, SEL_CHUNK), BF),
        jax.ShapeDtypeStruct((n_batch, seq, KV_HALF), BF),
        jax.ShapeDtypeStruct((n_batch, seq // WIN_CHUNK, N_KV_HEADS, V_ROWS, WIN_CHUNK), BF),
        jax.ShapeDtypeStruct((n_batch, 32, seq), F32),
        jax.ShapeDtypeStruct((n_pages_all * blocks_per_page, KV_WIDTH), F32),
    )
    out_specs = (
        pl.BlockSpec((1, KV_WIDTH, tm), featT),
        pl.BlockSpec((1, KV_WIDTH, tm), featT),
        pl.BlockSpec((1, KV_WIDTH, tm), featT),
        pl.BlockSpec((tm // BLOCK * CMP_PITCH, KV_HALF), tok),
        pl.BlockSpec((tm // BLOCK * CMP_PITCH, KV_HALF), tok),
        pl.BlockSpec((tm, POOL_WIDTH), tok),
        pl.BlockSpec((1, NSA_WIDTH, tm), featT),
        pl.BlockSpec((1, tm, 256), rows3),
        pl.BlockSpec((1, 1, N_KV_HEADS, V_ROWS, tm), chunk5),
        pl.BlockSpec((1, tm, KV_HALF), rows3),
        pl.BlockSpec((1, tm // WIN_CHUNK, N_KV_HEADS, V_ROWS, WIN_CHUNK), rows5),
        pl.BlockSpec((1, 32, tm), featT),
        pl.BlockSpec((blocks_step, KV_WIDTH), tok),
    )
    grid_spec = pltpu.PrefetchScalarGridSpec(
        num_scalar_prefetch=1,
        grid=(n_steps,),
        in_specs=[pl.BlockSpec((tm, D_MODEL), tok),
                  pl.BlockSpec((1, 1, D_MODEL), per_b),
                  pl.BlockSpec((1, 1, D_MODEL), per_b),
                  pl.BlockSpec((1, D_MODEL), const2),
                  pl.BlockSpec(wn.shape, const2),
                  pl.BlockSpec(wt.shape, const2)]
        + [pl.BlockSpec((1, KV_WIDTH, PAGE_SIZE), page_map(k)) for k in range(n_pages_step)]
        + [pl.BlockSpec(pos4.shape, const4), pl.BlockSpec(w4.shape, const4)],
        out_specs=out_specs,
        scratch_shapes=[pltpu.VMEM((blocks_step * CMP_PITCH, KV_HALF), F32),
                        pltpu.VMEM((blocks_step * CMP_PITCH, KV_HALF), F32)],
    )
    return pl.pallas_call(
        functools.partial(_proj_cmp_kernel, n_pages_step=n_pages_step, tm=tm, tpb=tpb),
        out_shape=out_shape,
        grid_spec=grid_spec,
        compiler_params=_cparams("arbitrary"),
        name="project_prompt_compress_cache",
    )(page_table_flat, x2, shift, scale, g, wn, wt, *([cache_t] * n_pages_step), pos4, w4)


def _compress_rows(src_refs, pos_ref, w_ref, o_ref, n_blocks, pitch):
    rows_per_slice = BLOCK // CMP_SPLIT
    for s in range(2):
        acc = jnp.zeros((n_blocks, KV_HALF), F32)
        for c in range(CMP_SPLIT):
            xc = jnp.concatenate(
                [src_refs[s][pl.ds(c * rows_per_slice + r, n_blocks, stride=pitch), :]
                 for r in range(rows_per_slice)], axis=1) + pos_ref[s, c]
            acc = acc + _dot(xc.astype(BF), w_ref[s, c])
        o_ref[:, s * KV_HALF:(s + 1) * KV_HALF] = acc


def _cmp_kernel(xa_ref, xb_ref, pos_ref, w_ref, o_ref):
    _compress_rows((xa_ref, xb_ref), pos_ref, w_ref, o_ref, CMP_GROUP, CMP_PITCH)


def _compress(xa, xb, pos4, w4):
    m = xa.shape[0] // CMP_PITCH
    rows = CMP_GROUP * CMP_PITCH
    return pl.pallas_call(
        _cmp_kernel,
        out_shape=jax.ShapeDtypeStruct((m, KV_WIDTH), F32),
        grid=(m // CMP_GROUP,),
        in_specs=[pl.BlockSpec((rows, KV_HALF), lambda i: (i, 0)),
                  pl.BlockSpec((rows, KV_HALF), lambda i: (i, 0)),
                  pl.BlockSpec(pos4.shape, lambda i: (0, 0, 0, 0)),
                  pl.BlockSpec(w4.shape, lambda i: (0, 0, 0, 0))],
        out_specs=pl.BlockSpec((CMP_GROUP, KV_WIDTH), lambda i: (i, 0)),
        compiler_params=_cparams("arbitrary"),
        name="compress_blocks",
    )(xa, xb, pos4, w4)


def _cmp_last_kernel(x_ref, pos_ref, w_ref, o_ref):
    lane = lax.broadcasted_iota(I32, (8, CMP_K // CMP_SPLIT), 1)
    for s in range(2):
        x0 = x_ref[:, s * KV_HALF:(s + 1) * KV_HALF] + pos_ref[s, 0][:, 0:KV_HALF]
        acc = _dot(x0.astype(BF), w_ref[s, 0, 0:KV_HALF, :])
        pad_term = jnp.zeros((8, KV_HALF), F32)
        for c in range(CMP_SPLIT):
            pc = jnp.broadcast_to(pos_ref[s, c], (8, CMP_K // CMP_SPLIT))
            if c == 0:
                pc = jnp.where(lane >= KV_HALF, pc, 0.0)
            pad_term = pad_term + _dot(pc.astype(BF), w_ref[s, c])
        o_ref[:, s * KV_HALF:(s + 1) * KV_HALF] = acc + pad_term[0:1, :]


def _compress_last(x_new, pos4, w4):
    return pl.pallas_call(
        _cmp_last_kernel,
        out_shape=jax.ShapeDtypeStruct(x_new.shape, F32),
        compiler_params=pltpu.CompilerParams(vmem_limit_bytes=VMEM_LIMIT),
        name="compress_last_block",
    )(x_new, pos4, w4)


def _attn_kernel(qT_ref, kaug_ref, vsT_ref, kw_ref, vwT_ref, kc_ref, vcT_ref, gT_ref, o_ref,
                 qa_ref, sa_ref, sb_ref, sc_ref, swb_ref, ma_ref, mb_ref, mc_ref, m_ref, acc_ref, outT_ref):
    i = pl.program_id(1)
    q0 = i * Q_TILE
    n_full = lax.div(q0, SEL_CHUNK)
    n_blk = kc_ref.shape[1]
    row = lax.broadcasted_iota(I32, (n_blk, Q_TILE), 0)
    tok = lax.broadcasted_iota(I32, (n_blk, Q_TILE), 1)
    qpos = q0 + tok
    cur = jnp.right_shift(qpos, 6)
    cmp_bias = jnp.where((row + 1) * BLOCK - 1 <= qpos, 0.0, -MASK_BIG)
    valid = row <= cur
    forced = (row == 0) | (row == cur) | (row == cur - 1)
    key_c = lax.broadcasted_iota(I32, (SEL_CHUNK, Q_TILE), 0)
    qpos_c = q0 + lax.broadcasted_iota(I32, (SEL_CHUNK, Q_TILE), 1)
    hs = [slice(h * HEAD_DIM, (h + 1) * HEAD_DIM) for h in range(N_KV_HEADS)]
    tri_bias = jnp.where(lax.broadcasted_iota(I32, (128, 128), 0) <= lax.broadcasted_iota(I32, (128, 128), 1),
                         0.0, -MASK_BIG)

    def bias4(keep):
        b = jnp.where(keep, 0.0, -MASK_BIG)
        return jnp.concatenate([b] * GROUP, axis=1)

    def online_step(st, s, v_t, s_max=None):
        m_old = m_ref[st]
        m_new = jnp.maximum(m_old, jnp.max(s, axis=0, keepdims=True) if s_max is None else s_max)
        alpha = jnp.exp2(m_old - m_new)
        p = jnp.exp2(s - m_new)
        acc_ref[st] = alpha * acc_ref[st] + _dot(v_t, p.astype(BF))
        m_ref[st] = m_new

    m_ref[...] = jnp.full(m_ref.shape, NEG_INF, F32)
    acc_ref[...] = jnp.zeros(acc_ref.shape, F32)

    for h in range(N_KV_HEADS):
        qa_ref[h] = jnp.zeros(qa_ref.shape[1:], BF)
        for g in range(GROUP):
            r0 = h * GROUP * HEAD_DIM + g * HEAD_DIM
            qa_ref[h, hs[h], g * Q_TILE:(g + 1) * Q_TILE] = qT_ref[0, r0:r0 + HEAD_DIM, :]

    w0 = jnp.maximum(q0 - WINDOW, 0)
    kw_a = kw_ref[0, pl.ds(pl.multiple_of(w0, WIN_CHUNK), WINDOW), :]
    kw_b = kw_ref[0, pl.ds(pl.multiple_of(q0, Q_TILE), Q_TILE), :]
    for h in range(N_KV_HEADS):
        sc_ref[h] = _dot(kw_a, qa_ref[h, 0:KV_HALF, :])
        swb_ref[h] = _dot(kw_b, qa_ref[h, 0:KV_HALF, :])

    o_c = []
    scores = []
    for h in range(N_KV_HEADS):
        sc = _dot(kc_ref[0], qa_ref[h, 0:KV_HALF, :])
        imp = jnp.zeros((n_blk, Q_TILE), F32)
        p_parts = []
        for g in range(GROUP):
            s = sc[:, g * Q_TILE:(g + 1) * Q_TILE] + cmp_bias
            mx = jnp.max(s, axis=0, keepdims=True)
            e = jnp.exp2(s - mx)
            inv = jnp.where(mx > -0.5 * MASK_BIG, 1.0 / jnp.sum(e, axis=0, keepdims=True), 0.0)
            p = e * inv
            imp = imp + p
            p_parts.append(p)
        o_c.append(_dot(vcT_ref[0, hs[h], :], jnp.concatenate(p_parts, axis=1).astype(BF)))
        scores.append(jnp.where(valid, jnp.where(forced, -2.0, imp), -1.0))

    blk_f = row.astype(F32)

    def pick(work):
        best = jnp.max(work, axis=0, keepdims=True)
        first = jnp.min(jnp.where(work == best, blk_f, float(n_blk)), axis=0, keepdims=True)
        return jnp.where((blk_f == first) & (best >= 0.0), -2.0, work)

    for _ in range(N_SELECT - N_FORCED):
        scores = [pick(w) for w in scores]
    for h in range(N_KV_HEADS):
        selm1 = jnp.where(scores[h] == -2.0, 0.0, -1.0).astype(BF)
        for g in range(GROUP):
            qa_ref[h, KV_HALF:KV_HALF + n_blk, g * Q_TILE:(g + 1) * Q_TILE] = selm1

    last_chunk = kaug_ref.shape[1] // SEL_CHUNK - 1

    def sel_scores(j, buf):
        dst_ref, max_ref = buf
        j = jnp.minimum(j, last_chunk)
        kt = kaug_ref[0, pl.ds(pl.multiple_of(j * SEL_CHUNK, SEL_CHUNK), SEL_CHUNK), :]
        for h in range(N_KV_HEADS):
            s = _dot(kt, qa_ref[h])
            dst_ref[h] = s
            max_ref[h] = jnp.max(s, axis=0, keepdims=True)

    def sel_process(buf, j, causal):
        src_ref, max_ref = buf
        for h in range(N_KV_HEADS):
            if causal:
                r0 = q0 - j * SEL_CHUNK
                for p in range(Q_TILE // 128):
                    rows = pl.ds(pl.multiple_of(r0 + p * 128, 128), 128)
                    for g in range(GROUP):
                        lanes = slice(g * Q_TILE + p * 128, g * Q_TILE + (p + 1) * 128)
                        src_ref[h, rows, lanes] = src_ref[h, rows, lanes] + tri_bias
                online_step(h, src_ref[h], vsT_ref[0, j, h])
            else:
                online_step(h, src_ref[h], vsT_ref[0, j, h], max_ref[h])

    buf_a, buf_b, buf_c = (sa_ref, ma_ref), (sb_ref, mb_ref), (sc_ref, mc_ref)
    sel_scores(0, buf_a)
    sel_scores(1, buf_b)

    j0 = lax.div(w0, WIN_CHUNK)
    jq = lax.div(q0, WIN_CHUNK)
    delta = qpos_c - (w0 + key_c)
    wbias = bias4((delta >= 0) & (delta < WINDOW))
    key_d = lax.broadcasted_iota(I32, (Q_TILE, Q_TILE), 0)
    tok_d = lax.broadcasted_iota(I32, (Q_TILE, Q_TILE), 1)
    dbias = bias4((key_d <= tok_d) & (q0 >= WINDOW))
    for h in range(N_KV_HEADS):
        v_a = jnp.concatenate([vwT_ref[0, j0 + c, h] for c in range(WINDOW // WIN_CHUNK)], axis=1)
        v_b = jnp.concatenate([vwT_ref[0, jq + c, h] for c in range(Q_TILE // WIN_CHUNK)], axis=1)
        online_step(2 + h, sc_ref[h] + wbias, v_a)
        online_step(2 + h, swb_ref[h] + dbias, v_b)

    def trio_body(t, carry):
        j = 3 * t
        sel_scores(j + 2, buf_c)
        sel_process(buf_a, j, False)
        sel_scores(j + 3, buf_a)
        sel_process(buf_b, j + 1, False)
        sel_scores(j + 4, buf_b)
        sel_process(buf_c, j + 2, False)
        return carry

    n_trios = lax.div(n_full, 3)
    lax.fori_loop(0, n_trios, trio_body, 0)
    j_last = 3 * n_trios
    n_left = n_full - j_last

    @pl.when(n_left == 0)
    def _():
        sel_process(buf_a, j_last, True)

    @pl.when(n_left == 1)
    def _():
        sel_process(buf_a, j_last, False)
        sel_process(buf_b, j_last + 1, True)

    @pl.when(n_left == 2)
    def _():
        sel_scores(j_last + 2, buf_c)
        sel_process(buf_a, j_last, False)
        sel_process(buf_b, j_last + 1, False)
        sel_process(buf_c, j_last + 2, True)

    for h in range(N_KV_HEADS):
        o_s = acc_ref[h, 0:HEAD_DIM, :] * (1.0 / acc_ref[h, HEAD_DIM:HEAD_DIM + 1, :])
        o_w = acc_ref[2 + h, 0:HEAD_DIM, :] * (1.0 / acc_ref[2 + h, HEAD_DIM:HEAD_DIM + 1, :])
        for g in range(GROUP):
            gs = slice(g * Q_TILE, (g + 1) * Q_TILE)
            gr = h * GROUP * 3 + g * 3
            og = (gT_ref[0, gr:gr + 1, :] * o_c[h][:, gs] + gT_ref[0, gr + 1:gr + 2, :] * o_s[:, gs]
                  + gT_ref[0, gr + 2:gr + 3, :] * o_w[:, gs])
            r0 = h * GROUP * HEAD_DIM + g * HEAD_DIM
            outT_ref[r0:r0 + HEAD_DIM, :] = og

    o_ref[0] = outT_ref[...].T.astype(BF)


def _attention_prompt(qT, kaug, vsT, kw, vwT, kc, vcT, gT, n_batch, seq):
    per_b3 = lambda n, i: (n, 0, 0)
    per_b5 = lambda n, i: (n, 0, 0, 0, 0)
    rows = GROUP * Q_TILE
    return pl.pallas_call(
        _attn_kernel,
        out_shape=jax.ShapeDtypeStruct((n_batch, seq, NSA_WIDTH), BF),
        grid=(n_batch, seq // Q_TILE),
        in_specs=[pl.BlockSpec((1, NSA_WIDTH, Q_TILE), lambda n, i: (n, 0, i)),
                  pl.BlockSpec((1, seq, 256), per_b3),
                  pl.BlockSpec((1, seq // SEL_CHUNK, N_KV_HEADS, V_ROWS, SEL_CHUNK), per_b5),
                  pl.BlockSpec((1, seq, KV_HALF), per_b3),
                  pl.BlockSpec((1, seq // WIN_CHUNK, N_KV_HEADS, V_ROWS, WIN_CHUNK), per_b5),
                  pl.BlockSpec((1, seq // BLOCK, KV_HALF), per_b3),
                  pl.BlockSpec((1, KV_HALF, seq // BLOCK), per_b3),
                  pl.BlockSpec((1, 32, Q_TILE), lambda n, i: (n, 0, i))],
        out_specs=pl.BlockSpec((1, Q_TILE, NSA_WIDTH), lambda n, i: (n, i, 0)),
        scratch_shapes=[pltpu.VMEM((N_KV_HEADS, 256, rows), BF),
                        pltpu.VMEM((N_KV_HEADS, SEL_CHUNK, rows), F32),
                        pltpu.VMEM((N_KV_HEADS, SEL_CHUNK, rows), F32),
                        pltpu.VMEM((N_KV_HEADS, SEL_CHUNK, rows), F32),
                        pltpu.VMEM((N_KV_HEADS, Q_TILE, rows), F32),
                        pltpu.VMEM((N_KV_HEADS, 1, rows), F32),
                        pltpu.VMEM((N_KV_HEADS, 1, rows), F32),
                        pltpu.VMEM((N_KV_HEADS, 1, rows), F32),
                        pltpu.VMEM((2 * N_KV_HEADS, 1, rows), F32),
                        pltpu.VMEM((2 * N_KV_HEADS, V_ROWS, rows), F32),
                        pltpu.VMEM((NSA_WIDTH, Q_TILE), F32)],
        compiler_params=_cparams("arbitrary", "arbitrary"),
        name="attention_prompt",
    )(qT, kaug, vsT, kw, vwT, kc, vcT, gT)


def _pool_kernel(u_ref, halo_ref, wp_ref, ps_ref, o_ref, ext_ref, lvl_ref, *, tm, tpb):
    t = pl.program_id(0) % tpb
    ext_ref[0:16, :] = jnp.where(t == 0, 0.0, halo_ref[...])
    u = u_ref[...]
    ext_ref[16:16 + tm, :] = u
    pos = t * tm + lax.broadcasted_iota(I32, (tm, 1), 0)
    end = 16 + tm
    outs = []
    for gi, w in enumerate(POOL_WINDOWS):
        cs = slice(gi * POOL_GROUP_DIM, (gi + 1) * POOL_GROUP_DIM)
        lo = 16 - (w - 2)
        acc = ext_ref[lo:end, cs] + ext_ref[lo - 1:end - 1, cs]
        d = 2
        while d < w:
            lvl_ref[lo:end, :] = acc
            lo += d
            acc = lvl_ref[lo:end, :] + lvl_ref[lo - d:end - d, :]
            d *= 2
        cnt = jnp.minimum(pos + 1, w).astype(F32)
        pooled = acc / cnt - u[:, cs]
        outs.append(_dot(pooled.astype(BF), wp_ref[gi]))
    o_ref[...] = (jnp.concatenate(outs, axis=1) * ps_ref[...]).astype(BF)


def _pool_prompt(u, w_pool, pool_scale, n_batch, seq):
    tm = TOKEN_TILE
    tpb = seq // tm
    nt = n_batch * seq
    return pl.pallas_call(
        functools.partial(_pool_kernel, tm=tm, tpb=tpb),
        out_shape=jax.ShapeDtypeStruct((nt, POOL_WIDTH), BF),
        grid=(nt // tm,),
        in_specs=[pl.BlockSpec((tm, POOL_WIDTH), lambda t: (t, 0)),
                  pl.BlockSpec((16, POOL_WIDTH), lambda t: (jnp.maximum(t * (tm // 16) - 1, 0), 0)),
                  pl.BlockSpec(w_pool.shape, lambda t: (0, 0, 0)),
                  pl.BlockSpec((1, POOL_WIDTH), lambda t: (0, 0))],
        out_specs=pl.BlockSpec((tm, POOL_WIDTH), lambda t: (t, 0)),
        scratch_shapes=[pltpu.VMEM((tm + 16, POOL_WIDTH), F32), pltpu.VMEM((tm + 16, POOL_GROUP_DIM), F32)],
        compiler_params=_cparams("arbitrary"),
        name="pool_prompt",
    )(u, u, w_pool, pool_scale)


def _spool_kernel(u_ref, hist_ref, wp_ref, ps_ref, o_ref):
    u = u_ref[...]
    outs = []
    for gi, w in enumerate(POOL_WINDOWS):
        cs = slice(gi * POOL_GROUP_DIM, (gi + 1) * POOL_GROUP_DIM)
        acc = u[:, cs]
        for k in range(1, w):
            acc = acc + hist_ref[POOL_HIST - k, :, cs]
        pooled = acc / float(w) - u[:, cs]
        outs.append(_dot(pooled.astype(BF), wp_ref[gi]))
    o_ref[...] = (jnp.concatenate(outs, axis=1) * ps_ref[...]).astype(BF)


def _pool_sample(u, hist_t, w_pool, pool_scale):
    n = u.shape[0]
    return pl.pallas_call(
        _spool_kernel,
        out_shape=jax.ShapeDtypeStruct((n, POOL_WIDTH), BF),
        name="pool_sample",
    )(u, hist_t, w_pool, pool_scale)


def _merge_kernel(x_ref, sh_ref, sc_ref, gate_ref, gpre_ref, gpost_ref, onsa_ref, opool_ref,
                  wgm_ref, wun_ref, wup_ref, wo_ref, o_ref):
    x = x_ref[...]
    h = _rms(x, gpre_ref[...]) * (1.0 + sc_ref[0]) + sh_ref[0]
    gm = jax.nn.sigmoid(_dot_nt(h.astype(BF), wgm_ref[...]))
    m = (gm[:, :D_MODEL] * _dot(onsa_ref[...], wun_ref[...])
         + gm[:, D_MODEL:] * _dot(opool_ref[...], wup_ref[...]))
    m = _dot(m.astype(BF), wo_ref[...])
    o_ref[...] = x + gate_ref[0] * _rms(m, gpost_ref[...])


def _mlp_kernel(x_ref, sh_ref, sc_ref, gate_ref, gpre_ref, gpost_ref, w1_ref, w2_ref, o_ref):
    x = x_ref[...]
    h = _rms(x, gpre_ref[...]) * (1.0 + sc_ref[0]) + sh_ref[0]
    hb = h.astype(BF)
    f = jnp.zeros(x.shape, F32)
    fc = 1024
    for c in range(D_FF // fc):
        a = jnp.maximum(_dot(hb, w1_ref[:, c * fc:(c + 1) * fc]), 0.0)
        f = f + _dot((a * a).astype(BF), w2_ref[c * fc:(c + 1) * fc, :])
    o_ref[...] = x + gate_ref[0] * _rms(f, gpost_ref[...])


def _token_call(kernel, name, x2, mods, tm, rows_per_mod, extra_tok, consts):
    nt = x2.shape[0]
    r = mods[0].shape[1]
    mod_spec = pl.BlockSpec((1, r, D_MODEL), lambda t: ((t * tm) // rows_per_mod, 0, 0))
    in_specs = [pl.BlockSpec((tm, D_MODEL), lambda t: (t, 0))] + [mod_spec] * len(mods)
    in_specs += [pl.BlockSpec((1, D_MODEL), lambda t: (0, 0))] * 2
    in_specs += [pl.BlockSpec((tm, a.shape[1]), lambda t: (t, 0)) for a in extra_tok]
    in_specs += [pl.BlockSpec(w.shape, lambda t: (0, 0), pipeline_mode=pl.Buffered(1)) for w in consts[2:]]
    return pl.pallas_call(
        kernel,
        out_shape=jax.ShapeDtypeStruct((nt, D_MODEL), F32),
        grid=(nt // tm,),
        in_specs=in_specs,
        out_specs=pl.BlockSpec((tm, D_MODEL), lambda t: (t, 0)),
        compiler_params=_cparams("arbitrary"),
        name=name,
    )(x2, *mods, consts[0], consts[1], *extra_tok, *consts[2:])


def _sproj_kernel(x_ref, sh_ref, sc_ref, g_ref, wn_ref, wt_ref, z_ref, zs_ref, zT_ref):
    x = x_ref[...]
    h = _rms(x, g_ref[...]) * (1.0 + sc_ref[...]) + sh_ref[...]
    hb = h.astype(BF)
    z = _dot_nt(hb, wn_ref[...])
    z_ref[...] = z
    zs_ref[...] = jax.nn.sigmoid(z)
    zT_ref[...] = _dot_nt(wt_ref[...], hb)


def _project_sample(x2, shift, scale, g, wn, wt):
    n = x2.shape[0]
    shp = jax.ShapeDtypeStruct((n, wn.shape[0]), F32)
    return pl.pallas_call(
        _sproj_kernel,
        out_shape=(shp, shp, jax.ShapeDtypeStruct((wt.shape[0], n), F32)),
        name="project_sample",
    )(x2, shift, scale, g, wn, wt)


def _sattn_init(kaug_ref, past_len, seq_step):
    @pl.when(pl.program_id(0) == 0)
    def _():
        blk = lax.broadcasted_iota(I32, (128, past_len), 0)
        key_blk = jnp.right_shift(lax.broadcasted_iota(I32, (128, past_len), 1), 6)
        onehot = jnp.where(blk == key_blk, MASK_BIG, 0.0).astype(BF)
        for q in range(seq_step):
            kaug_ref[q, 128:256, :] = onehot


def _sattn_main(*refs, n_pages, past_len, seq_step):
    pages = refs[:seq_step * n_pages]
    (qb_ref, kvc_ref, win_ref, ksn_ref, kwn_ref, kwnT_ref, g_ref,
     o_ref, nwin_ref, kaug_ref, vall_ref) = refs[seq_step * n_pages:]

    for q in range(seq_step):
        for p in range(n_pages):
            pg = pages[q * n_pages + p][0]
            kaug_ref[q, 0:128, p * PAGE_SIZE:(p + 1) * PAGE_SIZE] = pg[0:KV_HALF, :].astype(BF)
            vall_ref[q, :, p * PAGE_SIZE:(p + 1) * PAGE_SIZE] = pg[KV_HALF:KV_WIDTH, :].astype(BF)

    nb_past = past_len // BLOCK
    win_buf = win_ref.shape[2]
    n_rows = 8 * seq_step
    per_seq = lambda f: jnp.concatenate([f(q) for q in range(seq_step)], axis=0)
    rows_of = lambda x, q: x[8 * q:8 * (q + 1)]
    qb = qb_ref[...].reshape(n_rows, KV_HALF).astype(BF)
    qf = qb.astype(F32)
    row = lax.broadcasted_iota(I32, (n_rows, 128), 0)
    lane = lax.broadcasted_iota(I32, (n_rows, 128), 1)

    def new_key(ref):
        return per_seq(lambda q: jnp.broadcast_to(ref[q], (8, KV_WIDTH))).astype(BF).astype(F32)

    s_c = per_seq(lambda q: _dot_nt(rows_of(qb, q), kvc_ref[q, :, 0:128].astype(BF)))
    cm = lane < nb_past
    s_c = jnp.where(cm, s_c, NEG_INF)
    mx = jnp.max(s_c, axis=1, keepdims=True)
    mx = jnp.where(mx > NEG_INF, mx, 0.0)
    e = jnp.where(cm, jnp.exp(s_c - mx), 0.0)
    p_c = e / jnp.maximum(jnp.sum(e, axis=1, keepdims=True), 1e-30)
    o_c = per_seq(lambda q: _dot(rows_of(p_c, q).astype(BF), kvc_ref[q, :, 128:256].astype(BF)))

    imp = jnp.zeros((n_rows, 128), F32)
    for grp in range(n_rows // GROUP):
        in_grp = jnp.right_shift(row, 2) == grp
        imp = jnp.where(in_grp, jnp.sum(jnp.where(in_grp, p_c, 0.0), axis=0, keepdims=True), imp)
    cur = nb_past
    forced = (lane == 0) | (lane == cur) | (lane == cur - 1)
    score = jnp.where(lane <= cur, imp + FORCE_SCORE * forced.astype(F32), -1.0)
    cnt = jnp.zeros((n_rows, 128), F32)
    for bp in range(nb_past + 1):
        other = score[:, bp:bp + 1]
        ahead = (other > score) | ((other == score) & (bp < lane))
        cnt = cnt + jnp.where(ahead, 1.0, 0.0)
    sel = (cnt < float(N_SELECT)) & (score >= 0.0)
    selm1 = jnp.where(sel, 0.0, -1.0).astype(BF)

    qaug = jnp.concatenate([qb, selm1], axis=1)
    s_s = per_seq(lambda q: _dot(rows_of(qaug, q), kaug_ref[q]))
    kv_n = new_key(ksn_ref)
    s_n = jnp.sum(qf * kv_n[:, 0:128], axis=1, keepdims=True)
    m_s = jnp.maximum(jnp.max(s_s, axis=1, keepdims=True), s_n)
    e_s = jnp.exp(s_s - m_s)
    e_n = jnp.exp(s_n - m_s)
    l_s = jnp.sum(e_s, axis=1, keepdims=True) + e_n
    pv = per_seq(lambda q: _dot_nt(rows_of(e_s, q).astype(BF), vall_ref[q]))
    o_s = (pv + e_n.astype(BF).astype(F32) * kv_n[:, 128:256]) / l_s

    s_w = per_seq(lambda q: _dot(rows_of(qb, q), win_ref[q, 0:KV_HALF, :].astype(BF)))
    lane_w = lax.broadcasted_iota(I32, (n_rows, win_buf), 1)
    s_w = jnp.where(lane_w >= win_buf + 1 - WINDOW, s_w, NEG_INF)
    kv_n = new_key(kwn_ref)
    s_n = jnp.sum(qf * kv_n[:, 0:128], axis=1, keepdims=True)
    m_w = jnp.maximum(jnp.max(s_w, axis=1, keepdims=True), s_n)
    e_w = jnp.exp(s_w - m_w)
    e_n = jnp.exp(s_n - m_w)
    l_w = jnp.sum(e_w, axis=1, keepdims=True) + e_n
    pv = per_seq(lambda q: _dot_nt(rows_of(e_w, q).astype(BF), win_ref[q, KV_HALF:KV_WIDTH, :].astype(BF)))
    o_w = (pv + e_n.astype(BF).astype(F32) * kv_n[:, 128:256]) / l_w

    g = g_ref[...].reshape(n_rows, 3)
    o_ref[...] = (g[:, 0:1] * o_c + g[:, 1:2] * o_s + g[:, 2:3] * o_w).reshape(seq_step, 8, KV_HALF)

    seq_lane = lax.broadcasted_iota(I32, kwnT_ref.shape, 1)
    row_lane = lax.broadcasted_iota(I32, (KV_WIDTH, win_buf), 1)
    for q in range(seq_step):
        n = pl.program_id(0) * seq_step + q
        new_col = jnp.sum(jnp.where(seq_lane == n, kwnT_ref[...], 0.0), axis=1, keepdims=True)
        shifted = pltpu.roll(win_ref[q], win_buf - 1, axis=1)
        nwin_ref[q] = jnp.where(row_lane == win_buf - 1, new_col, shifted)


def _sattn_kernel(pt_ref, *refs, n_pages, past_len, seq_step):
    _sattn_init(refs[-2], past_len, seq_step)
    _sattn_main(*refs, n_pages=n_pages, past_len=past_len, seq_step=seq_step)


def _attention_sample(page_table_flat, cache_t, qblk, kvc_pad, win_t, kvs_new, kvw_new, kvw_new_t, gates8,
                      n_pages, past_len):
    n_seq = qblk.shape[0]
    win_buf = win_t.shape[2]
    g = SAMPLE_SEQ_STEP
    const2 = lambda t, pt: (0, 0)
    per_step = lambda t, pt: (t, 0, 0)

    def page_map(k):
        return lambda t, pt: (pt[t * g * n_pages + k], 0, 0)

    grid_spec = pltpu.PrefetchScalarGridSpec(
        num_scalar_prefetch=1,
        grid=(n_seq // g,),
        in_specs=[pl.BlockSpec((1, KV_WIDTH, PAGE_SIZE), page_map(k)) for k in range(g * n_pages)]
        + [pl.BlockSpec((g, 8, 128), per_step),
           pl.BlockSpec((g, 128, KV_WIDTH), per_step),
           pl.BlockSpec((g, KV_WIDTH, win_buf), per_step),
           pl.BlockSpec((g, 1, KV_WIDTH), per_step),
           pl.BlockSpec((g, 1, KV_WIDTH), per_step),
           pl.BlockSpec(kvw_new_t.shape, const2),
           pl.BlockSpec((g, 8, 3), per_step)],
        out_specs=(pl.BlockSpec((g, 8, 128), per_step),
                   pl.BlockSpec((g, KV_WIDTH, win_buf), per_step)),
        scratch_shapes=[pltpu.VMEM((g, 256, past_len), BF), pltpu.VMEM((g, KV_HALF, past_len), BF)],
    )
    return pl.pallas_call(
        functools.partial(_sattn_kernel, n_pages=n_pages, past_len=past_len, seq_step=g),
        out_shape=(jax.ShapeDtypeStruct((n_seq, 8, 128), F32),
                   jax.ShapeDtypeStruct((n_seq, KV_WIDTH, win_buf), F32)),
        grid_spec=grid_spec,
        compiler_params=_cparams("arbitrary"),
        name="attention_sample",
    )(page_table_flat, *([cache_t] * (g * n_pages)), qblk, kvc_pad, win_t, kvs_new, kvw_new, kvw_new_t, gates8)


def _kv_rows_view(kv_t):
    n, _, t = kv_t.shape
    return jnp.transpose(kv_t.reshape(n, 2, N_KV_HEADS, HEAD_DIM, t), (0, 4, 1, 2, 3))


def _kv_feat_view(kv):
    n, t = kv.shape[:2]
    return jnp.transpose(kv, (0, 2, 3, 4, 1)).reshape(n, KV_WIDTH, t)


def kernel(x_prompt, x_sample, cache_cmp_kv, cache_sel_kv, state_win_kv, state_pool, page_table, c_prompt, c_sample, w_ada, b_ada, g_pre_mix, g_post_mix, g_pre_mlp, g_post_mlp, w_in, w_cmp, pos_cmp, w_pool, pool_scale, w_up_nsa, w_up_pool, w_o, w_ff1, w_ff2):
    n_batch, seq, _ = x_prompt.shape
    n_seq = x_sample.shape[0]
    n_pages = page_table.shape[1]
    past_len = n_pages * PAGE_SIZE
    nb_past = past_len // BLOCK
    assert x_sample.shape[1] == 1 and w_ada.shape[0] == 1 and past_len % BLOCK == 0
    assert seq % TOKEN_TILE == 0 and seq // BLOCK == 128 and state_win_kv.shape[2] == WINDOW
    assert (n_seq * nb_past) % CMP_GROUP == 0 and n_seq == CMP_GROUP
    assert FORCE_SCORE > GROUP

    w_t = w_in[0].T
    wt = jnp.pad(w_t[0:1304], ((0, 8), (0, 0))).astype(BF)
    wn = jnp.concatenate([w_t[768:896], w_t[1024:1152], w_t[1304:1816], w_t[512:768]], axis=0).astype(BF)
    ws = jnp.pad(w_t[0:1816], ((0, 104), (0, 0))).astype(BF)
    wgm = w_t[1816:3864].astype(BF)
    eye = jnp.eye(N_KV_HEADS, dtype=F32)
    wc = w_cmp[0].astype(BF)
    wz = jnp.zeros_like(wc)
    w4 = jnp.concatenate([jnp.concatenate([wc, wz], axis=3), jnp.concatenate([wz, wc], axis=3)], axis=2)
    w4 = w4.reshape(2, CMP_SPLIT, CMP_K // CMP_SPLIT, KV_HALF)
    pos4 = jnp.broadcast_to(jnp.transpose(pos_cmp[0], (1, 0, 2))[:, :, None, :],
                            (2, BLOCK, N_KV_HEADS, HEAD_DIM)).reshape(2, CMP_SPLIT, 1, CMP_K // CMP_SPLIT)
    wp = w_pool[0].astype(BF)
    ps = pool_scale[0].reshape(1, POOL_WIDTH)
    wun, wup, wo = w_up_nsa[0].astype(BF), w_up_pool[0].astype(BF), w_o[0].astype(BF)
    w1, w2 = w_ff1[0].astype(BF), w_ff2[0].astype(BF)
    gpm, gqm = g_pre_mix[0].reshape(1, D_MODEL), g_post_mix[0].reshape(1, D_MODEL)
    gpf, gqf = g_pre_mlp[0].reshape(1, D_MODEL), g_post_mlp[0].reshape(1, D_MODEL)

    n_c = n_batch + n_seq
    c_all = jnp.pad(jnp.concatenate([c_prompt, c_sample], axis=0), ((0, (-n_c) % 8), (0, 0)))
    ada = _adaln(c_all, w_ada[0], b_ada[0].reshape(1, -1))
    ada_p = ada[:n_batch].reshape(n_batch, 6, 1, D_MODEL)
    ada_s = ada[n_batch:n_c].reshape(n_seq, 6, D_MODEL)
    mods_p = [ada_p[:, k] for k in range(6)]
    mods_s = [ada_s[:, k][None] for k in range(6)]

    xp = x_prompt.reshape(n_batch * seq, D_MODEL)
    pt_flat = page_table.reshape(-1)
    (kvcT, kvsT, kvwT, kvc_a, kvc_b, u_p, qT, kaug, vsT, kw, vwT, gT,
     kvc_past) = _project_prompt_compress_cache(xp, mods_p[0], mods_p[1], gpm, wn, wt, n_batch, seq,
                                                _kv_feat_view(cache_cmp_kv[0]), pt_flat, pos4, w4)
    kvc_blk = _compress(kvc_a, kvc_b, pos4, w4).reshape(n_batch, seq // BLOCK, KV_WIDTH)
    kc = kvc_blk[:, :, 0:128].astype(BF)
    vcT = jnp.swapaxes(kvc_blk[:, :, 128:256], 1, 2).astype(BF)
    onsa_p = _attention_prompt(qT, kaug, vsT, kw, vwT, kc, vcT, gT, n_batch, seq)
    opool_p = _pool_prompt(u_p, wp, ps, n_batch, seq)
    x1_p = _token_call(_merge_kernel, "merge_prompt", xp, [mods_p[0], mods_p[1], mods_p[2]], TOKEN_TILE, seq,
                       [onsa_p.reshape(n_batch * seq, NSA_WIDTH), opool_p], [gpm, gqm, wgm, wun, wup, wo])
    y_p = _token_call(_mlp_kernel, "mlp_prompt", x1_p, [mods_p[3], mods_p[4], mods_p[5]], TOKEN_TILE, seq,
                      [], [gpf, gqf, w1, w2])

    xs = x_sample.reshape(n_seq, D_MODEL)
    z, zsig, zT = _project_sample(xs, mods_s[0][0], mods_s[1][0], gpm, ws, wt)
    q_s = z[:, 0:512] * (HEAD_DIM ** -0.5)
    kvc_n, kvs_n, kvw_n = z[:, 512:768], z[:, 768:1024], z[:, 1024:1280]
    gates_s = zsig[:, 1280:1304].reshape(n_seq, N_HEADS, 3)
    u_s = z[:, 1304:1816]
    q5 = q_s.reshape(n_seq, N_KV_HEADS, GROUP, 1, HEAD_DIM) * eye[None, :, None, :, None]
    qblk = q5.reshape(n_seq, N_HEADS, KV_HALF)

    kvc_last = _compress_last(kvc_n, pos4, w4)
    kvc_s = jnp.concatenate([kvc_past.reshape(n_seq, nb_past, KV_WIDTH), kvc_last[:, None, :]], axis=1)
    kvc_pad = jnp.pad(kvc_s, ((0, 0), (0, 128 - nb_past - 1), (0, 0)))
    o8, new_win_t = _attention_sample(
        pt_flat, _kv_feat_view(cache_sel_kv[0]), qblk, kvc_pad, _kv_feat_view(state_win_kv[0]),
        kvs_n[:, None, :], kvw_n[:, None, :], zT[1024:1280], gates_s, n_pages, past_len)
    o5 = o8.reshape(n_seq, N_KV_HEADS, GROUP, N_KV_HEADS, HEAD_DIM)
    onsa_s = jnp.concatenate([o5[:, 0, :, 0, :], o5[:, 1, :, 1, :]], axis=1).reshape(n_seq, NSA_WIDTH).astype(BF)
    opool_s = _pool_sample(u_s, jnp.swapaxes(state_pool[0], 0, 1), wp, ps)
    x1_s = _token_call(_merge_kernel, "merge_sample", xs, [mods_s[0], mods_s[1], mods_s[2]], n_seq, n_seq,
                       [onsa_s, opool_s], [gpm, gqm, wgm, wun, wup, wo])
    y_s = _token_call(_mlp_kernel, "mlp_sample", x1_s, [mods_s[3], mods_s[4], mods_s[5]], n_seq, n_seq,
                      [], [gpf, gqf, w1, w2])

    win_p = min(WINDOW, seq)
    new_kv_s = lambda rows: _kv_rows_view(rows.reshape(1, KV_WIDTH, n_seq))[0][None, :, None]
    return (
        y_p.reshape(n_batch, seq, D_MODEL),
        y_s.reshape(n_seq, 1, D_MODEL),
        _kv_rows_view(kvcT)[None],
        _kv_rows_view(kvsT)[None],
        _kv_rows_view(kvwT[:, :, seq - win_p:])[None],
        u_p.reshape(n_batch, seq, POOL_WIDTH)[None, :, seq - POOL_HIST:],
        new_kv_s(zT[512:768]),
        new_kv_s(zT[768:1024]),
        _kv_rows_view(new_win_t)[None],
        jnp.concatenate([state_pool[0][:, 1:], u_s[:, None, :]], axis=1)[None],
    )
```
